```python
import jax, jax.numpy as jnp
from jax import lax
import numpy as np

D_MODEL = 1024
BATCH = 8
SEQ = 2048
DEPTH = 2

CHUNK = 64
N_EVEN = (DEPTH + 1) // 2
N_ODD = DEPTH // 2
GROUP_WIDTH = D_MODEL // 2
D_FF = 4 * D_MODEL
NORM_EPS = 1e-6

GLA_HEADS = 4
GLA_DV = GROUP_WIDTH // GLA_HEADS
GLA_DK = GLA_DV // 2
GLA_KW = GLA_HEADS * GLA_DK
GLA_RANK = 16
GLA_GATE_TAU = 16.0

FOX_HEAD_DIM = 64
FOX_HEADS = GROUP_WIDTH // FOX_HEAD_DIM
FOX_BLOCK = 128

CA_HEAD_DIM = 64
CA_HEADS = GROUP_WIDTH // CA_HEAD_DIM
CA_LEFT_CHUNKS = 8
CA_BAND = (CA_LEFT_CHUNKS + 1) * CHUNK
REL_CLIP = 128

LRU_WIDTH = GROUP_WIDTH
LRU_BLOCKS = 8
LRU_BLOCK_DIM = LRU_WIDTH // LRU_BLOCKS
CONV_WIDTH = 4
LRU_C = 8.0

EVEN_SIZES = (GLA_KW, GLA_KW, GROUP_WIDTH, GROUP_WIDTH, GLA_RANK,
              GROUP_WIDTH, GROUP_WIDTH, GROUP_WIDTH, FOX_HEADS)
EVEN_IN = sum(EVEN_SIZES)
ODD_SIZES = (GROUP_WIDTH, GROUP_WIDTH, GROUP_WIDTH, LRU_WIDTH, LRU_WIDTH)
ODD_IN = sum(ODD_SIZES)

kernel_name = 'hybrid_gla_fox_chunkattn_rglru_trunk'


def _split(a, sizes):
    return jnp.split(a, [int(s) for s in np.cumsum(sizes)[:-1]], axis=-1)


def rmsnorm(x, w):
    xf = x.astype(jnp.float32)
    y = xf * lax.rsqrt(jnp.mean(xf * xf, axis=-1, keepdims=True) + NORM_EPS)
    return (y * w.astype(jnp.float32)).astype(x.dtype)


def gla_mixer(q, k, v, r, a_low, w_a_up, b_a, norm_w):
    B, T, _ = q.shape
    nc = T // CHUNK
    f32 = jnp.float32
    qc = q.reshape(B, nc, CHUNK, GLA_HEADS, GLA_DK).astype(f32) * (GLA_DK ** -0.5)
    kc = k.reshape(B, nc, CHUNK, GLA_HEADS, GLA_DK).astype(f32)
    vc = v.reshape(B, nc, CHUNK, GLA_HEADS, GLA_DV).astype(f32)
    log_a = jax.nn.log_sigmoid((a_low @ w_a_up + b_a).astype(f32)) / GLA_GATE_TAU
    log_a = log_a.reshape(B, nc, CHUNK, GLA_HEADS, GLA_DK)
    cum = jnp.cumsum(log_a, axis=2)
    total = cum[:, :, -1]
    k_dec = kc * jnp.exp(total[:, :, None] - cum)
    inc = jnp.einsum('bcshk,bcshv->bchkv', k_dec, vc)

    def step(state, inp):
        dec, add = inp
        state = dec[..., None] * state + add
        return state, state

    init = jnp.zeros((B, GLA_HEADS, GLA_DK, GLA_DV), f32)
    _, states = lax.scan(step, init, (jnp.moveaxis(jnp.exp(total), 1, 0), jnp.moveaxis(inc, 1, 0)))
    states = jnp.moveaxis(states, 0, 1)
    o = jnp.einsum('bcthk,bchkv->bcthv', qc, states).reshape(B, T, GLA_HEADS, GLA_DV)
    o = o * lax.rsqrt(jnp.mean(o * o, axis=-1, keepdims=True) + NORM_EPS)
    o = o.reshape(B, T, GROUP_WIDTH) * norm_w.astype(f32)
    return (o * jax.nn.silu(r.astype(f32))).astype(q.dtype)


def fox_mixer(q, k, v, f_logit):
    B, T, _ = q.shape
    f32 = jnp.float32
    qh = q.reshape(B, T, FOX_HEADS, FOX_HEAD_DIM)
    kh = k.reshape(B, T, FOX_HEADS, FOX_HEAD_DIM)
    vh = v.reshape(B, T, FOX_HEADS, FOX_HEAD_DIM)
    cum = jnp.cumsum(jax.nn.log_sigmoid(f_logit.astype(f32)), axis=1).transpose(0, 2, 1)
    scale = FOX_HEAD_DIM ** -0.5
    neg = jnp.finfo(f32).min
    outs = []
    for blk in range(T // FOX_BLOCK):
        q0 = blk * FOX_BLOCK
        q1 = q0 + FOX_BLOCK
        s = jnp.einsum('bqhd,bkhd->bhqk', qh[:, q0:q1], kh[:, :q1]).astype(f32) * scale
        s = s + (cum[:, :, q0:q1, None] - cum[:, :, None, :q1])
        mask = (q0 + jnp.arange(FOX_BLOCK))[:, None] >= jnp.arange(q1)[None, :]
        p = jax.nn.softmax(jnp.where(mask, s, neg), axis=-1)
        outs.append(jnp.einsum('bhqk,bkhd->bqhd', p.astype(v.dtype), vh[:, :q1]))
    return jnp.concatenate(outs, axis=1).reshape(B, T, GROUP_WIDTH)


def chunk_rel_attention(q, k, v, rel_bias):
    B, T, _ = q.shape
    nc = T // CHUNK
    f32 = jnp.float32
    pad = CA_LEFT_CHUNKS * CHUNK
    qc = q.reshape(B, nc, CHUNK, CA_HEADS, CA_HEAD_DIM)
    kp = jnp.pad(k, ((0, 0), (pad, 0), (0, 0))).reshape(B, nc + CA_LEFT_CHUNKS, CHUNK, CA_HEADS, CA_HEAD_DIM)
    vp = jnp.pad(v, ((0, 0), (pad, 0), (0, 0))).reshape(B, nc + CA_LEFT_CHUNKS, CHUNK, CA_HEADS, CA_HEAD_DIM)
    k_band = jnp.concatenate([kp[:, j:j + nc] for j in range(CA_LEFT_CHUNKS + 1)], axis=2)
    v_band = jnp.concatenate([vp[:, j:j + nc] for j in range(CA_LEFT_CHUNKS + 1)], axis=2)
    s = jnp.einsum('bcqhd,bckhd->bchqk', qc, k_band).astype(f32) * (CA_HEAD_DIM ** -0.5)
    qi = jnp.arange(CHUNK)
    kj = jnp.arange(CA_BAND)
    rel = jnp.clip(pad + qi[:, None] - kj[None, :], -REL_CLIP, REL_CLIP) + REL_CLIP
    bias = rel_bias.astype(f32)[:, rel]
    key_pos = jnp.arange(nc)[:, None] * CHUNK - pad + kj[None, :]
    valid = (key_pos >= 0)[None, :, None, None, :]
    s = jnp.where(valid, s + bias[None, None], jnp.finfo(f32).min)
    p = jax.nn.softmax(s, axis=-1)
    o = jnp.einsum('bchqk,bckhd->bcqhd', p.astype(v.dtype), v_band)
    return o.reshape(B, T, GROUP_WIDTH)


def rglru_mixer(gate_in, x_in, conv_w, conv_b, w_a, b_a, w_x, b_x, lam):
    B, T, W = x_in.shape
    f32 = jnp.float32
    xc = lax.conv_general_dilated(x_in, conv_w[:, None, :], window_strides=(1,),
                                  padding=[(CONV_WIDTH - 1, 0)],
                                  dimension_numbers=('NWC', 'WIO', 'NWC'),
                                  feature_group_count=W) + conv_b
    xb = xc.reshape(B, T, LRU_BLOCKS, LRU_BLOCK_DIM)
    r = jax.nn.sigmoid((jnp.einsum('btnd,nde->btne', xb, w_a).reshape(B, T, W) + b_a).astype(f32))
    i = jax.nn.sigmoid((jnp.einsum('btnd,nde->btne', xb, w_x).reshape(B, T, W) + b_x).astype(f32))
    log_a = LRU_C * r * jax.nn.log_sigmoid(lam.astype(f32))
    a = jnp.exp(log_a)
    b = jnp.sqrt(-jnp.expm1(2.0 * log_a)) * (i * xc.astype(f32))

    def combine(left, right):
        a_l, b_l = left
        a_r, b_r = right
        return a_l * a_r, a_r * b_l + b_r

    _, h = lax.associative_scan(combine, (a, b), axis=1)
    return (h * jax.nn.gelu(gate_in.astype(f32))).astype(x_in.dtype)


def sq_relu_mlp(h, w_up, w_down):
    return jnp.square(jax.nn.relu(h @ w_up)) @ w_down


def setup_inputs(seed: int = 0) -> dict:
    key = jax.random.key(seed)
    ks = jax.random.split(key, 24)
    nrm = jax.random.normal
    f32 = jnp.float32
    x = nrm(ks[0], (BATCH, SEQ, D_MODEL), f32)
    norm_w = 1.0 + 0.05 * nrm(ks[1], (DEPTH, 4, D_MODEL), f32)
    w_in_even = nrm(ks[2], (N_EVEN, D_MODEL, EVEN_IN), f32) * D_MODEL ** -0.5
    gla_w_a_up = nrm(ks[3], (N_EVEN, GLA_RANK, GLA_KW), f32) * GLA_RANK ** -0.5
    gla_b_a = jax.random.uniform(ks[4], (N_EVEN, GLA_KW), f32, 0.0, 2.0)
    gla_norm_w = 1.0 + 0.05 * nrm(ks[5], (N_EVEN, GROUP_WIDTH), f32)
    fox_b_f = jax.random.uniform(ks[6], (N_EVEN, FOX_HEADS), f32, 1.0, 4.0)
    w_out_even = nrm(ks[7], (N_EVEN, 2 * GROUP_WIDTH, D_MODEL), f32) * (2 * GROUP_WIDTH) ** -0.5
    w_in_odd = nrm(ks[8], (N_ODD, D_MODEL, ODD_IN), f32) * D_MODEL ** -0.5
    rel_bias = 0.5 * nrm(ks[9], (N_ODD, CA_HEADS, 2 * REL_CLIP + 1), f32)
    conv_w = nrm(ks[10], (N_ODD, CONV_WIDTH, LRU_WIDTH), f32) * CONV_WIDTH ** -0.5
    conv_b = 0.01 * nrm(ks[11], (N_ODD, LRU_WIDTH), f32)
    lru_w_a = nrm(ks[12], (N_ODD, LRU_BLOCKS, LRU_BLOCK_DIM, LRU_BLOCK_DIM), f32) * LRU_BLOCK_DIM ** -0.5
    lru_b_a = 0.1 * nrm(ks[13], (N_ODD, LRU_WIDTH), f32)
    lru_w_x = nrm(ks[14], (N_ODD, LRU_BLOCKS, LRU_BLOCK_DIM, LRU_BLOCK_DIM), f32) * LRU_BLOCK_DIM ** -0.5
    lru_b_x = 0.1 * nrm(ks[15], (N_ODD, LRU_WIDTH), f32)
    a_c = jax.random.uniform(ks[16], (N_ODD, LRU_WIDTH), f32, 0.9, 0.999)
    a_base = a_c ** (1.0 / LRU_C)
    lru_lambda = jnp.log(a_base) - jnp.log1p(-a_base)
    w_out_odd = nrm(ks[17], (N_ODD, 2 * GROUP_WIDTH, D_MODEL), f32) * (2 * GROUP_WIDTH) ** -0.5
    w_mlp_up = nrm(ks[18], (DEPTH, D_MODEL, D_FF), f32) * D_MODEL ** -0.5
    w_mlp_down = nrm(ks[19], (DEPTH, D_FF, D_MODEL), f32) * D_FF ** -0.5
    return {'x': x, 'norm_w': norm_w, 'w_in_even': w_in_even, 'gla_w_a_up': gla_w_a_up,
            'gla_b_a': gla_b_a, 'gla_norm_w': gla_norm_w, 'fox_b_f': fox_b_f,
            'w_out_even': w_out_even, 'w_in_odd': w_in_odd, 'rel_bias': rel_bias,
            'conv_w': conv_w, 'conv_b': conv_b, 'lru_w_a': lru_w_a, 'lru_b_a': lru_b_a,
            'lru_w_x': lru_w_x, 'lru_b_x': lru_b_x, 'lru_lambda': lru_lambda,
            'w_out_odd': w_out_odd, 'w_mlp_up': w_mlp_up, 'w_mlp_down': w_mlp_down}


def reference(x, norm_w, w_in_even, gla_w_a_up, gla_b_a, gla_norm_w, fox_b_f, w_out_even,
              w_in_odd, rel_bias, conv_w, conv_b, lru_w_a, lru_b_a, lru_w_x, lru_b_x,
              lru_lambda, w_out_odd, w_mlp_up, w_mlp_down):
    for layer in range(DEPTH):
        j = layer // 2
        h = rmsnorm(x, norm_w[layer, 0])
        if layer % 2 == 0:
            proj = h @ w_in_even[j]
            g_q, g_k, g_v, g_r, g_a, f_q, f_k, f_v, f_f = _split(proj, EVEN_SIZES)
            out_a = gla_mixer(g_q, g_k, g_v, g_r, g_a, gla_w_a_up[j], gla_b_a[j], gla_norm_w[j])
            out_b = fox_mixer(f_q, f_k, f_v, f_f + fox_b_f[j])
            mix = jnp.concatenate([out_a, out_b], axis=-1) @ w_out_even[j]
        else:
            proj = h @ w_in_odd[j]
            c_q, c_k, c_v, d_gate, d_in = _split(proj, ODD_SIZES)
            out_c = chunk_rel_attention(c_q, c_k, c_v, rel_bias[j])
            out_d = rglru_mixer(d_gate, d_in, conv_w[j], conv_b[j], lru_w_a[j], lru_b_a[j],
                                lru_w_x[j], lru_b_x[j], lru_lambda[j])
            mix = jnp.concatenate([out_c, out_d], axis=-1) @ w_out_odd[j]
        x = x + rmsnorm(mix, norm_w[layer, 1])
        h = rmsnorm(x, norm_w[layer, 2])
        x = x + rmsnorm(sq_relu_mlp(h, w_mlp_up[layer], w_mlp_down[layer]), norm_w[layer, 3])
    return x
```

```python
import functools

import jax
import jax.numpy as jnp
from jax import lax
from jax.experimental import pallas as pl
from jax.experimental.pallas import tpu as pltpu

F32 = jnp.float32
BF16 = jnp.bfloat16

NORM_EPS = 1e-6
CHUNK = 64
GLA_HEADS = 4
GLA_DK = 64
GLA_DV = 128
GLA_RANK = 16
GLA_GATE_TAU = 16.0
FOX_HEADS = 8
HEAD_DIM = 64
CA_LEFT_CHUNKS = 8
REL_CLIP = 128
CONV_WIDTH = 4
LRU_BLOCKS = 8
LRU_C = 8.0

LANES = 128
MXU_DIM = 256
MASK_VALUE = -1e30
VMEM_LIMIT_BYTES = 56 * 1024 * 1024

ROW_TILE = 512
FOX_BLOCK = 512
CA_QBLOCK = 4 * CHUNK
CA_WINDOW = CA_QBLOCK + CA_LEFT_CHUNKS * CHUNK
CA_PAD = CA_LEFT_CHUNKS * CHUNK
LRU_ROWS = 256
SCAN_ROWS = 16


def _params(n_axes):
    return pltpu.CompilerParams(
        dimension_semantics=("arbitrary",) * n_axes,
        vmem_limit_bytes=VMEM_LIMIT_BYTES,
    )


def _const_spec(shape):
    nd = len(shape)
    return pl.BlockSpec(shape, lambda *_: (0,) * nd, pipeline_mode=pl.Buffered(1))


def _rmsnorm(x, w):
    y = x * lax.rsqrt(jnp.mean(x * x, axis=-1, keepdims=True) + NORM_EPS)
    return y * w


def _log_sigmoid(z):
    return jnp.minimum(z, 0.0) - jnp.log1p(jnp.exp(-jnp.abs(z)))


def _sigmoid(z):
    return 1.0 / (1.0 + jnp.exp(-z))


def _dot(a, b):
    return jnp.dot(a, b, preferred_element_type=F32)


def _dot_nt(a, b):
    return lax.dot_general(a, b, (((1,), (1,)), ((), ())), preferred_element_type=F32)


def _dot_tn(a, b):
    return lax.dot_general(a, b, (((0,), (0,)), ((), ())), preferred_element_type=F32)


def _half_mask(lane, j):
    return lane < HEAD_DIM if j == 0 else lane >= HEAD_DIM


def _split3(x):
    h1 = x.astype(BF16)
    r1 = x - h1.astype(F32)
    h2 = r1.astype(BF16)
    h3 = (r1 - h2.astype(F32)).astype(BF16)
    return h1, h2, h3


def _norm_proj_kernel(x_ref, nw_ref, w_ref, *rest, n_chunk, with_small):
    if with_small:
        ws_ref, o_ref, os_ref = rest
    else:
        (o_ref,) = rest
    h = _rmsnorm(x_ref[...], nw_ref[...]).astype(BF16)
    n_total = o_ref.shape[1]
    for n0 in range(0, n_total, n_chunk):
        o_ref[:, n0:n0 + n_chunk] = _dot(h, w_ref[:, n0:n0 + n_chunk]).astype(BF16)
    if with_small:
        os_ref[...] = _dot(h, ws_ref[...])


def _norm_proj(x2d, nw, w_main, w_small=None):
    m, d = x2d.shape
    n = w_main.shape[1]
    with_small = w_small is not None
    in_specs = [
        pl.BlockSpec((ROW_TILE, d), lambda i: (i, 0)),
        _const_spec((1, d)),
        _const_spec((d, n)),
    ]
    out_shape = [jax.ShapeDtypeStruct((m, n), BF16)]
    out_specs = [pl.BlockSpec((ROW_TILE, n), lambda i: (i, 0))]
    args = [x2d, nw.reshape(1, d), w_main]
    if with_small:
        in_specs.append(_const_spec((d, LANES)))
        out_shape.append(jax.ShapeDtypeStruct((m, LANES), F32))
        out_specs.append(pl.BlockSpec((ROW_TILE, LANES), lambda i: (i, 0)))
        args.append(w_small)
    return pl.pallas_call(
        functools.partial(_norm_proj_kernel, n_chunk=2 * MXU_DIM, with_small=with_small),
        grid=(m // ROW_TILE,),
        in_specs=in_specs,
        out_specs=out_specs,
        out_shape=out_shape,
        compiler_params=_params(1),
        name="norm_proj",
    )(*args)


def _out_proj_kernel(a_ref, b_ref, w_ref, nw_ref, x_ref, o_ref):
    g = a_ref.shape[1]
    y = _dot(a_ref[...], w_ref[:g, :]) + _dot(b_ref[...], w_ref[g:, :])
    o_ref[...] = x_ref[...] + _rmsnorm(y, nw_ref[...])


def _out_proj(mix_a, mix_b, w_out, nw, x2d):
    m, d = x2d.shape
    g = mix_a.shape[1]
    return pl.pallas_call(
        _out_proj_kernel,
        grid=(m // ROW_TILE,),
        in_specs=[
            pl.BlockSpec((ROW_TILE, g), lambda i: (i, 0)),
            pl.BlockSpec((ROW_TILE, g), lambda i: (i, 0)),
            _const_spec((2 * g, d)),
            _const_spec((1, d)),
            pl.BlockSpec((ROW_TILE, d), lambda i: (i, 0)),
        ],
        out_specs=pl.BlockSpec((ROW_TILE, d), lambda i: (i, 0)),
        out_shape=jax.ShapeDtypeStruct((m, d), F32),
        compiler_params=_params(1),
        name="out_proj",
    )(mix_a, mix_b, w_out, nw.reshape(1, d), x2d)


def _mlp_kernel(x_ref, nw_pre_ref, wu_ref, wd_ref, nw_post_ref, o_ref, u_ref, y_ref, *, chunk):
    x = x_ref[...]
    h = _rmsnorm(x, nw_pre_ref[...]).astype(BF16)
    d_ff = wu_ref.shape[1]
    d = wd_ref.shape[1]
    for f0 in range(0, d_ff, chunk):
        u = jnp.maximum(_dot(h, wu_ref[:, f0:f0 + chunk]), 0.0)
        u_ref[:, f0:f0 + chunk] = (u * u).astype(BF16)
    for n0 in range(0, d, chunk):
        y_ref[:, n0:n0 + chunk] = _dot(u_ref[...], wd_ref[:, n0:n0 + chunk])
    o_ref[...] = x + _rmsnorm(y_ref[...], nw_post_ref[...])


def _mlp(x2d, nw_pre, w_up, w_down, nw_post):
    m, d = x2d.shape
    d_ff = w_up.shape[1]
    return pl.pallas_call(
        functools.partial(_mlp_kernel, chunk=2 * MXU_DIM),
        grid=(m // ROW_TILE,),
        in_specs=[
            pl.BlockSpec((ROW_TILE, d), lambda i: (i, 0)),
            _const_spec((1, d)),
            _const_spec((d, d_ff)),
            _const_spec((d_ff, d)),
            _const_spec((1, d)),
        ],
        out_specs=pl.BlockSpec((ROW_TILE, d), lambda i: (i, 0)),
        out_shape=jax.ShapeDtypeStruct((m, d), F32),
        scratch_shapes=[pltpu.VMEM((ROW_TILE, d_ff), BF16), pltpu.VMEM((ROW_TILE, d), F32)],
        compiler_params=_params(1),
        name="mlp",
    )(x2d, nw_pre.reshape(1, d), w_up, w_down, nw_post.reshape(1, d))


def _gla_kernel(q_ref, k_ref, v_ref, r_ref, sm_ref, wa_ref, ba_ref, nw_ref, o_ref,
                la_ref, st_ref, dec_ref):
    t = q_ref.shape[0]
    nc = t // CHUNK
    n_pairs = GLA_HEADS // 2

    z = _dot(sm_ref[...].astype(BF16), wa_ref[...]) + ba_ref[...]
    la_ref[...] = _log_sigmoid(z) * (1.0 / GLA_GATE_TAU)

    row = lax.broadcasted_iota(jnp.int32, (CHUNK, CHUNK), 0)
    col = lax.broadcasted_iota(jnp.int32, (CHUNK, CHUNK), 1)
    tri = jnp.where(row >= col, 1.0, 0.0).astype(BF16)
    ones = jnp.ones((CHUNK, LANES), BF16)

    def increments(c, carry):
        r0 = pl.multiple_of(c * CHUNK, CHUNK)
        la = la_ref[pl.ds(r0, CHUNK), :]
        hi = la.astype(BF16)
        lo = (la - hi.astype(F32)).astype(BF16)
        cum = _dot(tri, hi) + _dot(tri, lo)
        total = cum[CHUNK - 1:CHUNK, :]
        k_dec = (k_ref[pl.ds(r0, CHUNK), :].astype(F32) * jnp.exp(total - cum)).astype(BF16)
        dec_ref[c] = jnp.exp(_dot_tn(hi, ones) + _dot_tn(lo, ones))
        v_c = v_ref[pl.ds(r0, CHUNK), :]
        for h in range(GLA_HEADS):
            p, j = divmod(h, 2)
            full = _dot_tn(k_dec[:, p * LANES:(p + 1) * LANES], v_c[:, h * GLA_DV:(h + 1) * GLA_DV])
            st_ref[c, p, j * GLA_DK:(j + 1) * GLA_DK, :] = full[j * GLA_DK:(j + 1) * GLA_DK, :]
        return carry

    lax.fori_loop(0, nc, increments, 0)

    def scan(c, carry):
        for p in range(n_pairs):
            st_ref[c, p] = st_ref[c, p] + dec_ref[c, p * LANES:(p + 1) * LANES, :] * st_ref[c - 1, p]
        return carry

    lax.fori_loop(1, nc, scan, 0)

    lane = lax.broadcasted_iota(jnp.int32, (CHUNK, LANES), 1)
    scale = GLA_DK ** -0.5

    def outputs(c, carry):
        r0 = pl.multiple_of(c * CHUNK, CHUNK)
        q_c = q_ref[pl.ds(r0, CHUNK), :] * jnp.asarray(scale, BF16)
        r_c = r_ref[pl.ds(r0, CHUNK), :].astype(F32)
        for h in range(GLA_HEADS):
            p, j = divmod(h, 2)
            q_pair = q_c[:, p * LANES:(p + 1) * LANES]
            q_h = jnp.where(_half_mask(lane, j), q_pair, jnp.zeros_like(q_pair))
            o = _dot(q_h, st_ref[c, p].astype(BF16))
            o = o * lax.rsqrt(jnp.mean(o * o, axis=-1, keepdims=True) + NORM_EPS)
            o = o * nw_ref[:, h * GLA_DV:(h + 1) * GLA_DV]
            r_h = r_c[:, h * GLA_DV:(h + 1) * GLA_DV]
            o_ref[pl.ds(r0, CHUNK), h * GLA_DV:(h + 1) * GLA_DV] = (o * (r_h * _sigmoid(r_h))).astype(BF16)
        return carry

    lax.fori_loop(0, nc, outputs, 0)


def _gla(proj, small, wa_pad, b_a, norm_w, batch, t):
    kw = GLA_HEADS * GLA_DK
    gw = GLA_HEADS * GLA_DV
    nc = t // CHUNK
    return pl.pallas_call(
        _gla_kernel,
        grid=(batch,),
        in_specs=[
            pl.BlockSpec((t, kw), lambda b: (b, 0)),
            pl.BlockSpec((t, kw), lambda b: (b, 1)),
            pl.BlockSpec((t, gw), lambda b: (b, 1)),
            pl.BlockSpec((t, gw), lambda b: (b, 2)),
            pl.BlockSpec((t, LANES), lambda b: (b, 0)),
            _const_spec((LANES, kw)),
            _const_spec((1, kw)),
            _const_spec((1, gw)),
        ],
        out_specs=pl.BlockSpec((t, gw), lambda b: (b, 0)),
        out_shape=jax.ShapeDtypeStruct((batch * t, gw), BF16),
        scratch_shapes=[
            pltpu.VMEM((t, kw), F32),
            pltpu.VMEM((nc, GLA_HEADS // 2, LANES, GLA_DV), F32),
            pltpu.VMEM((nc, kw, LANES), F32),
        ],
        compiler_params=_params(1),
        name="gla",
    )(proj, proj, proj, proj, small, wa_pad, b_a.reshape(1, kw), norm_w.reshape(1, gw))


def _fox_cum_kernel(sm_ref, bias_ref, o_ref):
    t = sm_ref.shape[0]
    row = lax.broadcasted_iota(jnp.int32, (LANES, LANES), 0)
    col = lax.broadcasted_iota(jnp.int32, (LANES, LANES), 1)
    tri = jnp.where(row >= col, 1.0, 0.0).astype(BF16)
    carry = jnp.zeros((1, LANES), F32)
    for blk in range(t // LANES):
        rows = slice(blk * LANES, (blk + 1) * LANES)
        ls = _log_sigmoid(sm_ref[rows, :] + bias_ref[...])
        h1, h2, h3 = _split3(ls)
        cb = _dot(tri, h1) + _dot(tri, h2) + _dot(tri, h3) + carry
        o_ref[rows, :] = cb
        carry = cb[LANES - 1:LANES, :]


def _fox_cum(small, bias_row, batch, t):
    return pl.pallas_call(
        _fox_cum_kernel,
        grid=(batch,),
        in_specs=[pl.BlockSpec((t, LANES), lambda b: (b, 0)), _const_spec((1, LANES))],
        out_specs=pl.BlockSpec((t, LANES), lambda b: (b, 0)),
        out_shape=jax.ShapeDtypeStruct((batch * t, LANES), F32),
        compiler_params=_params(1),
        name="fox_cum",
    )(small, bias_row)


def _fox_kernel(q_ref, k_ref, v_ref, c_ref, o_ref, vt_ref, cb_ref):
    blk = q_ref.shape[0]
    t = k_ref.shape[0]
    nkv = t // blk
    pair = pl.program_id(1)
    i = pl.program_id(2)

    @pl.when(i == 0)
    def _():
        sel_r = lax.broadcasted_iota(jnp.int32, (LANES, LANES), 0)
        for kb in range(nkv):
            rows = slice(kb * blk, (kb + 1) * blk)
            vt_ref[kb] = v_ref[rows, :].astype(F32).T.astype(BF16)
            c1, c2, c3 = _split3(c_ref[rows, :])
            for j in range(2):
                sel = jnp.where(sel_r == 2 * pair + j, 1.0, 0.0).astype(BF16)
                cb_ref[j, rows, :] = _dot(c1, sel) + _dot(c2, sel) + _dot(c3, sel)

    lane = lax.broadcasted_iota(jnp.int32, (blk, LANES), 1)
    q = q_ref[...] * jnp.asarray(HEAD_DIM ** -0.5, BF16)
    krow = lax.broadcasted_iota(jnp.int32, (blk, blk), 0)
    qcol = lax.broadcasted_iota(jnp.int32, (blk, blk), 1)
    causal = krow <= qcol

    outs = []
    for j in range(2):
        q_h = jnp.where(_half_mask(lane, j), q, jnp.zeros_like(q))

        def step(kb, carry, masked):
            m, l, acc = carry
            r0 = pl.multiple_of(kb * blk, blk)
            s = _dot_nt(k_ref[pl.ds(r0, blk), :], q_h)
            s = s - pltpu.repeat(cb_ref[j, pl.ds(r0, blk), :], blk // LANES, axis=1)
            if masked:
                s = jnp.where(causal, s, MASK_VALUE)
            m_new = jnp.maximum(m, jnp.max(s, axis=0, keepdims=True))
            alpha = jnp.exp(m - m_new)
            p = jnp.exp(s - m_new)
            l = l * alpha + jnp.sum(p, axis=0, keepdims=True)
            pv = _dot(vt_ref[kb, j * HEAD_DIM:(j + 1) * HEAD_DIM, :], p.astype(BF16))
            return m_new, l, acc * alpha + pv

        init = (jnp.full((1, blk), MASK_VALUE, F32), jnp.zeros((1, blk), F32),
                jnp.zeros((HEAD_DIM, blk), F32))
        carry = lax.fori_loop(0, i, functools.partial(step, masked=False), init)
        m, l, acc = step(i, carry, masked=True)
        outs.append(acc * (1.0 / l))
    o_ref[...] = jnp.concatenate(outs, axis=0).T.astype(BF16)


def _fox(proj, cum, batch, t, col0):
    nq = t // FOX_BLOCK
    n_pairs = FOX_HEADS // 2
    qc, kc, vc = col0 // LANES, col0 // LANES + n_pairs, col0 // LANES + 2 * n_pairs
    return pl.pallas_call(
        _fox_kernel,
        grid=(batch, n_pairs, nq),
        in_specs=[
            pl.BlockSpec((FOX_BLOCK, LANES), lambda b, p, i: (b * nq + i, qc + p)),
            pl.BlockSpec((t, LANES), lambda b, p, i: (b, kc + p)),
            pl.BlockSpec((t, LANES), lambda b, p, i: (b, vc + p)),
            pl.BlockSpec((t, LANES), lambda b, p, i: (b, 0)),
        ],
        out_specs=pl.BlockSpec((FOX_BLOCK, LANES), lambda b, p, i: (b * nq + i, p)),
        out_shape=jax.ShapeDtypeStruct((batch * t, FOX_HEADS * HEAD_DIM), BF16),
        scratch_shapes=[
            pltpu.VMEM((nq, LANES, FOX_BLOCK), BF16),
            pltpu.VMEM((2, t, LANES), F32),
        ],
        compiler_params=_params(3),
        name="fox",
    )(proj, proj, proj, cum)


def _chunk_attn_kernel(q_ref, k_ref, v_ref, tab_ref, o_ref, kp_ref, vt_ref):
    qb = q_ref.shape[0]
    t = k_ref.shape[0]
    n_pad = CA_PAD // qb
    n_win = CA_WINDOW // qb
    i = pl.program_id(2)

    @pl.when(i == 0)
    def _():
        for kb in range(n_pad):
            kp_ref[kb] = jnp.zeros((qb, LANES), BF16)
            vt_ref[kb] = jnp.zeros((LANES, qb), BF16)
        for kb in range(t // qb):
            rows = slice(kb * qb, (kb + 1) * qb)
            kp_ref[n_pad + kb] = k_ref[rows, :]
            vt_ref[n_pad + kb] = v_ref[rows, :].astype(F32).T.astype(BF16)

    lane = lax.broadcasted_iota(jnp.int32, (qb, LANES), 1)
    q = q_ref[...] * jnp.asarray(HEAD_DIM ** -0.5, BF16)
    outs = []
    for j in range(2):
        q_h = jnp.where(_half_mask(lane, j), q, jnp.zeros_like(q))
        s_blocks = []
        for w in range(n_win):
            s = _dot_nt(kp_ref[i + w], q_h) + tab_ref[j, w * qb:(w + 1) * qb, :]
            s_blocks.append(s + jnp.where(i + w >= n_pad, 0.0, MASK_VALUE))
        m = s_blocks[0].max(axis=0, keepdims=True)
        for s in s_blocks[1:]:
            m = jnp.maximum(m, s.max(axis=0, keepdims=True))
        l = jnp.zeros((1, qb), F32)
        acc = jnp.zeros((HEAD_DIM, qb), F32)
        for w, s in enumerate(s_blocks):
            p = jnp.exp(s - m)
            l = l + jnp.sum(p, axis=0, keepdims=True)
            acc = acc + _dot(vt_ref[i + w, j * HEAD_DIM:(j + 1) * HEAD_DIM, :], p.astype(BF16))
        outs.append(acc * (1.0 / l))
    o_ref[...] = jnp.concatenate(outs, axis=0).T.astype(BF16)


def _chunk_attn(proj, table, batch, t):
    nq = t // CA_QBLOCK
    n_pairs = table.shape[0]
    n_blocks = (CA_PAD + t) // CA_QBLOCK
    return pl.pallas_call(
        _chunk_attn_kernel,
        grid=(batch, n_pairs, nq),
        in_specs=[
            pl.BlockSpec((CA_QBLOCK, LANES), lambda b, p, i: (b * nq + i, p)),
            pl.BlockSpec((t, LANES), lambda b, p, i: (b, n_pairs + p)),
            pl.BlockSpec((t, LANES), lambda b, p, i: (b, 2 * n_pairs + p)),
            pl.BlockSpec((None, 2, CA_WINDOW, CA_QBLOCK), lambda b, p, i: (p, 0, 0, 0)),
        ],
        out_specs=pl.BlockSpec((CA_QBLOCK, LANES), lambda b, p, i: (b * nq + i, p)),
        out_shape=jax.ShapeDtypeStruct((batch * t, 2 * n_pairs * HEAD_DIM), BF16),
        scratch_shapes=[
            pltpu.VMEM((n_blocks, CA_QBLOCK, LANES), BF16),
            pltpu.VMEM((n_blocks, LANES, CA_QBLOCK), BF16),
        ],
        compiler_params=_params(3),
        name="chunk_attn",
    )(proj, proj, proj, table)


def _chunk_attn_table(rel_bias):
    krow = jnp.arange(CA_WINDOW)[:, None]
    qcol = jnp.arange(CA_QBLOCK)[None, :]
    rel = jnp.clip(CA_PAD + qcol - krow, -REL_CLIP, REL_CLIP) + REL_CLIP
    kc = krow // CHUNK
    qc = qcol // CHUNK
    allowed = (kc >= qc) & (kc <= qc + CA_LEFT_CHUNKS)
    tab = jnp.where(allowed[None], rel_bias.astype(F32)[:, rel], MASK_VALUE)
    return tab.reshape(rel_bias.shape[0] // 2, 2, CA_WINDOW, CA_QBLOCK)


def _gelu(x):
    return 0.5 * x * (1.0 + jnp.tanh(0.7978845608028654 * (x + 0.044715 * (x * x * x))))


def _lru_kernel(g_ref, x_ref, cw_ref, cb_ref, wa_ref, ba_ref, wx_ref, bx_ref, lam_ref, o_ref,
                xp_ref, a_ref, b_ref):
    t, w = x_ref.shape
    halo = 8
    xp_ref[0:halo, :] = jnp.zeros((halo, w), F32)
    xp_ref[halo:halo + t, :] = x_ref[...].astype(F32)
    log_base = LRU_C * _log_sigmoid(lam_ref[...])
    half = w // 2

    for blk in range(t // LRU_ROWS):
        r0 = blk * LRU_ROWS
        xc = cb_ref[...] + cw_ref[CONV_WIDTH - 1:CONV_WIDTH, :] * xp_ref[halo + r0:halo + r0 + LRU_ROWS, :]
        for tap in range(1, CONV_WIDTH):
            lo = halo + r0 - tap
            xc = xc + cw_ref[CONV_WIDTH - 1 - tap:CONV_WIDTH - tap, :] * xp_ref[lo:lo + LRU_ROWS, :]
        xcb = xc.astype(BF16)
        gr, gi = [], []
        for hb in range(2):
            cols = slice(hb * half, (hb + 1) * half)
            gr.append(_dot(xcb[:, cols], wa_ref[hb]))
            gi.append(_dot(xcb[:, cols], wx_ref[hb]))
        r = _sigmoid(jnp.concatenate(gr, axis=1) + ba_ref[...])
        gate_i = _sigmoid(jnp.concatenate(gi, axis=1) + bx_ref[...])
        a = jnp.exp(r * log_base)
        a_ref[r0:r0 + LRU_ROWS, :] = a
        b_ref[r0:r0 + LRU_ROWS, :] = jnp.sqrt(1.0 - a * a) * (gate_i * xc)

    row = lax.broadcasted_iota(jnp.int32, (SCAN_ROWS, w), 0)

    def scan(g, h_prev):
        r0 = pl.multiple_of(g * SCAN_ROWS, SCAN_ROWS)
        a = a_ref[pl.ds(r0, SCAN_ROWS), :]
        b = b_ref[pl.ds(r0, SCAN_ROWS), :]
        shift = 1
        while shift < SCAN_ROWS:
            keep = row >= shift
            a_sh = pltpu.roll(a, shift, axis=0)
            b_sh = pltpu.roll(b, shift, axis=0)
            b = jnp.where(keep, b + a * b_sh, b)
            a = jnp.where(keep, a * a_sh, a)
            shift *= 2
        h = b + a * h_prev
        b_ref[pl.ds(r0, SCAN_ROWS), :] = h
        return h[SCAN_ROWS - 1:SCAN_ROWS, :]

    lax.fori_loop(0, t // SCAN_ROWS, scan, jnp.zeros((1, w), F32))

    for blk in range(t // LRU_ROWS):
        rows = slice(blk * LRU_ROWS, (blk + 1) * LRU_ROWS)
        o_ref[rows, :] = (b_ref[rows, :] * _gelu(g_ref[rows, :].astype(F32))).astype(BF16)


def _lru(proj, conv_w, conv_b, wa_bd, b_a, wx_bd, b_x, lam, batch, t, col0):
    w = conv_w.shape[1]
    gc = col0 // w
    return pl.pallas_call(
        _lru_kernel,
        grid=(batch,),
        in_specs=[
            pl.BlockSpec((t, w), lambda b: (b, gc)),
            pl.BlockSpec((t, w), lambda b: (b, gc + 1)),
            _const_spec((CONV_WIDTH, w)),
            _const_spec((1, w)),
            _const_spec(wa_bd.shape),
            _const_spec((1, w)),
            _const_spec(wx_bd.shape),
            _const_spec((1, w)),
            _const_spec((1, w)),
        ],
        out_specs=pl.BlockSpec((t, w), lambda b: (b, 0)),
        out_shape=jax.ShapeDtypeStruct((batch * t, w), BF16),
        scratch_shapes=[
            pltpu.VMEM((t + 8, w), F32),
            pltpu.VMEM((t, w), F32),
            pltpu.VMEM((t, w), F32),
        ],
        compiler_params=_params(1),
        name="rglru",
    )(proj, proj, conv_w, conv_b.reshape(1, w), wa_bd, b_a.reshape(1, w), wx_bd,
      b_x.reshape(1, w), lam.reshape(1, w))


def _block_diag_halves(wblk):
    nb, d, _ = wblk.shape
    per = nb // 2
    eye = jnp.eye(per, dtype=wblk.dtype)
    halves = [jnp.einsum("nde,nm->ndme", wblk[h * per:(h + 1) * per], eye).reshape(per * d, per * d)
              for h in range(2)]
    return jnp.stack(halves).astype(BF16)


def kernel(x, norm_w, w_in_even, gla_w_a_up, gla_b_a, gla_norm_w, fox_b_f, w_out_even,
           w_in_odd, rel_bias, conv_w, conv_b, lru_w_a, lru_b_a, lru_w_x, lru_b_x,
           lru_lambda, w_out_odd, w_mlp_up, w_mlp_down):
    batch, t, d = x.shape
    x2d = x.reshape(batch * t, d)
    group = d // 2
    kw = GLA_HEADS * GLA_DK

    w_in = w_in_even[0]
    o_ga = 2 * kw + 2 * group
    o_fq = o_ga + GLA_RANK
    o_ff = o_fq + 3 * group
    w_main = jnp.concatenate([w_in[:, :o_ga], w_in[:, o_fq:o_ff]], axis=1).astype(BF16)
    n_small = FOX_HEADS + GLA_RANK
    w_small = jnp.concatenate(
        [w_in[:, o_ff:], w_in[:, o_ga:o_fq], jnp.zeros((d, LANES - n_small), F32)], axis=1).astype(BF16)
    wa_pad = jnp.zeros((LANES, kw), F32).at[FOX_HEADS:n_small].set(gla_w_a_up[0]).astype(BF16)
    fox_bias = jnp.zeros((1, LANES), F32).at[0, :FOX_HEADS].set(fox_b_f[0])

    proj, small = _norm_proj(x2d, norm_w[0, 0], w_main, w_small)
    out_a = _gla(proj, small, wa_pad, gla_b_a[0], gla_norm_w[0], batch, t)
    cum = _fox_cum(small, fox_bias, batch, t)
    out_b = _fox(proj, cum, batch, t, o_ga)
    x2d = _out_proj(out_a, out_b, w_out_even[0].astype(BF16), norm_w[0, 1], x2d)
    x2d = _mlp(x2d, norm_w[0, 2], w_mlp_up[0].astype(BF16), w_mlp_down[0].astype(BF16), norm_w[0, 3])

    (proj,) = _norm_proj(x2d, norm_w[1, 0], w_in_odd[0].astype(BF16))
    out_c = _chunk_attn(proj, _chunk_attn_table(rel_bias[0]), batch, t)
    out_d = _lru(proj, conv_w[0], conv_b[0], _block_diag_halves(lru_w_a[0]), lru_b_a[0],
                 _block_diag_halves(lru_w_x[0]), lru_b_x[0], lru_lambda[0], batch, t, 3 * group)
    x2d = _out_proj(out_c, out_d, w_out_odd[0].astype(BF16), norm_w[1, 1], x2d)
    x2d = _mlp(x2d, norm_w[1, 2], w_mlp_up[1].astype(BF16), w_mlp_down[1].astype(BF16), norm_w[1, 3])
    return x2d.reshape(batch, t, d)
```

```python
import functools

import jax
import jax.numpy as jnp
from jax import lax
from jax.experimental import pallas as pl
from jax.experimental.pallas import tpu as pltpu

F32 = jnp.float32
BF16 = jnp.bfloat16

NORM_EPS = 1e-6
CHUNK = 64
GLA_HEADS = 4
GLA_DK = 64
GLA_DV = 128
GLA_RANK = 16
GLA_GATE_TAU = 16.0
FOX_HEADS = 8
HEAD_DIM = 64
CA_LEFT_CHUNKS = 8
REL_CLIP = 128
CONV_WIDTH = 4
LRU_BLOCKS = 8
LRU_C = 8.0

LANES = 128
MXU_DIM = 256
MASK_VALUE = -1e30
VMEM_LIMIT_BYTES = 56 * 1024 * 1024

ROW_TILE = 512
FOX_BLOCK = 512
CA_QBLOCK = 4 * CHUNK
CA_WINDOW = CA_QBLOCK + CA_LEFT_CHUNKS * CHUNK
CA_PAD = CA_LEFT_CHUNKS * CHUNK
CA_LINE = 1024
LOG2E = 1.4426950408889634
CUM_TERMS = 3
LRU_ROWS = 256
SCAN_ROWS = 16


def _params(n_axes):
    return pltpu.CompilerParams(
        dimension_semantics=("arbitrary",) * n_axes,
        vmem_limit_bytes=VMEM_LIMIT_BYTES,
    )


def _const_spec(shape):
    nd = len(shape)
    return pl.BlockSpec(shape, lambda *_: (0,) * nd, pipeline_mode=pl.Buffered(1))


def _rmsnorm(x, w):
    y = x * lax.rsqrt(jnp.mean(x * x, axis=-1, keepdims=True) + NORM_EPS)
    return y * w


def _log_sigmoid(z):
    return jnp.minimum(z, 0.0) - jnp.log1p(jnp.exp(-jnp.abs(z)))


def _sigmoid(z):
    return 1.0 / (1.0 + jnp.exp(-z))


def _dot(a, b):
    return jnp.dot(a, b, preferred_element_type=F32)


def _dot_nt(a, b):
    return lax.dot_general(a, b, (((1,), (1,)), ((), ())), preferred_element_type=F32)


def _dot_tn(a, b):
    return lax.dot_general(a, b, (((0,), (0,)), ((), ())), preferred_element_type=F32)


def _half_mask(lane, j):
    return lane < HEAD_DIM if j == 0 else lane >= HEAD_DIM


def _split3(x):
    h1 = x.astype(BF16)
    r1 = x - h1.astype(F32)
    h2 = r1.astype(BF16)
    h3 = (r1 - h2.astype(F32)).astype(BF16)
    return h1, h2, h3


def _norm_proj_kernel(x_ref, nw_ref, w_ref, *rest, n_chunk, with_small):
    if with_small:
        ws_ref, o_ref, os_ref = rest
    else:
        (o_ref,) = rest
    h = _rmsnorm(x_ref[...], nw_ref[...]).astype(BF16)
    n_total = o_ref.shape[1]
    for n0 in range(0, n_total, n_chunk):
        o_ref[:, n0:n0 + n_chunk] = _dot(h, w_ref[:, n0:n0 + n_chunk]).astype(BF16)
    if with_small:
        os_ref[...] = _dot(h, ws_ref[...])


def _norm_proj(x2d, nw, w_main, w_small=None):
    m, d = x2d.shape
    n = w_main.shape[1]
    with_small = w_small is not None
    in_specs = [
        pl.BlockSpec((ROW_TILE, d), lambda i: (i, 0)),
        _const_spec((1, d)),
        _const_spec((d, n)),
    ]
    out_shape = [jax.ShapeDtypeStruct((m, n), BF16)]
    out_specs = [pl.BlockSpec((ROW_TILE, n), lambda i: (i, 0))]
    args = [x2d, nw.reshape(1, d), w_main]
    if with_small:
        in_specs.append(_const_spec((d, LANES)))
        out_shape.append(jax.ShapeDtypeStruct((m, LANES), F32))
        out_specs.append(pl.BlockSpec((ROW_TILE, LANES), lambda i: (i, 0)))
        args.append(w_small)
    return pl.pallas_call(
        functools.partial(_norm_proj_kernel, n_chunk=2 * MXU_DIM, with_small=with_small),
        grid=(m // ROW_TILE,),
        in_specs=in_specs,
        out_specs=out_specs,
        out_shape=out_shape,
        compiler_params=_params(1),
        name="norm_proj",
    )(*args)


def _out_proj_kernel(a_ref, b_ref, w_ref, nw_ref, x_ref, o_ref):
    g = a_ref.shape[1]
    y = _dot(a_ref[...], w_ref[:g, :]) + _dot(b_ref[...], w_ref[g:, :])
    o_ref[...] = x_ref[...] + _rmsnorm(y, nw_ref[...])


def _out_proj(mix_a, mix_b, w_out, nw, x2d):
    m, d = x2d.shape
    g = mix_a.shape[1]
    return pl.pallas_call(
        _out_proj_kernel,
        grid=(m // ROW_TILE,),
        in_specs=[
            pl.BlockSpec((ROW_TILE, g), lambda i: (i, 0)),
            pl.BlockSpec((ROW_TILE, g), lambda i: (i, 0)),
            _const_spec((2 * g, d)),
            _const_spec((1, d)),
            pl.BlockSpec((ROW_TILE, d), lambda i: (i, 0)),
        ],
        out_specs=pl.BlockSpec((ROW_TILE, d), lambda i: (i, 0)),
        out_shape=jax.ShapeDtypeStruct((m, d), F32),
        compiler_params=_params(1),
        name="out_proj",
    )(mix_a, mix_b, w_out, nw.reshape(1, d), x2d)


def _mlp_kernel(x_ref, nw_pre_ref, wu_ref, wd_ref, nw_post_ref, o_ref, u_ref, y_ref, *, chunk):
    x = x_ref[...]
    h = _rmsnorm(x, nw_pre_ref[...]).astype(BF16)
    d_ff = wu_ref.shape[1]
    d = wd_ref.shape[1]
    for f0 in range(0, d_ff, chunk):
        u = jnp.maximum(_dot(h, wu_ref[:, f0:f0 + chunk]), 0.0)
        u_ref[:, f0:f0 + chunk] = (u * u).astype(BF16)
    for n0 in range(0, d, chunk):
        y_ref[:, n0:n0 + chunk] = _dot(u_ref[...], wd_ref[:, n0:n0 + chunk])
    o_ref[...] = x + _rmsnorm(y_ref[...], nw_post_ref[...])


def _mlp(x2d, nw_pre, w_up, w_down, nw_post):
    m, d = x2d.shape
    d_ff = w_up.shape[1]
    return pl.pallas_call(
        functools.partial(_mlp_kernel, chunk=2 * MXU_DIM),
        grid=(m // ROW_TILE,),
        in_specs=[
            pl.BlockSpec((ROW_TILE, d), lambda i: (i, 0)),
            _const_spec((1, d)),
            _const_spec((d, d_ff)),
            _const_spec((d_ff, d)),
            _const_spec((1, d)),
        ],
        out_specs=pl.BlockSpec((ROW_TILE, d), lambda i: (i, 0)),
        out_shape=jax.ShapeDtypeStruct((m, d), F32),
        scratch_shapes=[pltpu.VMEM((ROW_TILE, d_ff), BF16), pltpu.VMEM((ROW_TILE, d), F32)],
        compiler_params=_params(1),
        name="mlp",
    )(x2d, nw_pre.reshape(1, d), w_up, w_down, nw_post.reshape(1, d))


def _gla_kernel(q_ref, k_ref, v_ref, r_ref, sm_ref, wa_ref, ba_ref, nw_ref, o_ref,
                la_ref, st_ref, dec_ref):
    t = q_ref.shape[0]
    nc = t // CHUNK
    n_pairs = GLA_HEADS // 2

    z = _dot(sm_ref[...].astype(BF16), wa_ref[...]) + ba_ref[...]
    la_ref[...] = _log_sigmoid(z) * (1.0 / GLA_GATE_TAU)

    row = lax.broadcasted_iota(jnp.int32, (CHUNK, CHUNK), 0)
    col = lax.broadcasted_iota(jnp.int32, (CHUNK, CHUNK), 1)
    tri = jnp.where(row >= col, 1.0, 0.0).astype(BF16)
    ones = jnp.ones((CHUNK, LANES), BF16)

    def increments(c, carry):
        r0 = pl.multiple_of(c * CHUNK, CHUNK)
        la = la_ref[pl.ds(r0, CHUNK), :]
        hi = la.astype(BF16)
        lo = (la - hi.astype(F32)).astype(BF16)
        cum = _dot(tri, hi) + _dot(tri, lo)
        total = cum[CHUNK - 1:CHUNK, :]
        k_dec = (k_ref[pl.ds(r0, CHUNK), :].astype(F32) * jnp.exp(total - cum)).astype(BF16)
        dec_ref[c] = jnp.exp(_dot_tn(hi, ones) + _dot_tn(lo, ones))
        v_c = v_ref[pl.ds(r0, CHUNK), :]
        for h in range(GLA_HEADS):
            p, j = divmod(h, 2)
            full = _dot_tn(k_dec[:, p * LANES:(p + 1) * LANES], v_c[:, h * GLA_DV:(h + 1) * GLA_DV])
            st_ref[c, p, j * GLA_DK:(j + 1) * GLA_DK, :] = full[j * GLA_DK:(j + 1) * GLA_DK, :]
        return carry

    lax.fori_loop(0, nc, increments, 0)

    def scan(c, carry):
        for p in range(n_pairs):
            st_ref[c, p] = st_ref[c, p] + dec_ref[c, p * LANES:(p + 1) * LANES, :] * st_ref[c - 1, p]
        return carry

    lax.fori_loop(1, nc, scan, 0)

    lane = lax.broadcasted_iota(jnp.int32, (CHUNK, LANES), 1)
    scale = GLA_DK ** -0.5

    def outputs(c, carry):
        r0 = pl.multiple_of(c * CHUNK, CHUNK)
        q_c = q_ref[pl.ds(r0, CHUNK), :] * jnp.asarray(scale, BF16)
        r_c = r_ref[pl.ds(r0, CHUNK), :].astype(F32)
        for h in range(GLA_HEADS):
            p, j = divmod(h, 2)
            q_pair = q_c[:, p * LANES:(p + 1) * LANES]
            q_h = jnp.where(_half_mask(lane, j), q_pair, jnp.zeros_like(q_pair))
            o = _dot(q_h, st_ref[c, p].astype(BF16))
            o = o * lax.rsqrt(jnp.mean(o * o, axis=-1, keepdims=True) + NORM_EPS)
            o = o * nw_ref[:, h * GLA_DV:(h + 1) * GLA_DV]
            r_h = r_c[:, h * GLA_DV:(h + 1) * GLA_DV]
            o_ref[pl.ds(r0, CHUNK), h * GLA_DV:(h + 1) * GLA_DV] = (o * (r_h * _sigmoid(r_h))).astype(BF16)
        return carry

    lax.fori_loop(0, nc, outputs, 0)


def _gla(proj, small, wa_pad, b_a, norm_w, batch, t):
    kw = GLA_HEADS * GLA_DK
    gw = GLA_HEADS * GLA_DV
    nc = t // CHUNK
    return pl.pallas_call(
        _gla_kernel,
        grid=(batch,),
        in_specs=[
            pl.BlockSpec((t, kw), lambda b: (b, 0)),
            pl.BlockSpec((t, kw), lambda b: (b, 1)),
            pl.BlockSpec((t, gw), lambda b: (b, 1)),
            pl.BlockSpec((t, gw), lambda b: (b, 2)),
            pl.BlockSpec((t, LANES), lambda b: (b, 0)),
            _const_spec((LANES, kw)),
            _const_spec((1, kw)),
            _const_spec((1, gw)),
        ],
        out_specs=pl.BlockSpec((t, gw), lambda b: (b, 0)),
        out_shape=jax.ShapeDtypeStruct((batch * t, gw), BF16),
        scratch_shapes=[
            pltpu.VMEM((t, kw), F32),
            pltpu.VMEM((nc, GLA_HEADS // 2, LANES, GLA_DV), F32),
            pltpu.VMEM((nc, kw, LANES), F32),
        ],
        compiler_params=_params(1),
        name="gla",
    )(proj, proj, proj, proj, small, wa_pad, b_a.reshape(1, kw), norm_w.reshape(1, gw))


def _fox_cum_kernel(sm_ref, bias_ref, o_ref):
    t = sm_ref.shape[0]
    row = lax.broadcasted_iota(jnp.int32, (LANES, LANES), 0)
    col = lax.broadcasted_iota(jnp.int32, (LANES, LANES), 1)
    tri = jnp.where(row >= col, 1.0, 0.0).astype(BF16)
    carry = jnp.zeros((1, LANES), F32)
    for blk in range(t // LANES):
        rows = slice(blk * LANES, (blk + 1) * LANES)
        ls = _log_sigmoid(sm_ref[rows, :] + bias_ref[...])
        h1, h2, h3 = _split3(ls)
        cb = _dot(tri, h1) + _dot(tri, h2) + _dot(tri, h3) + carry
        o_ref[rows, :] = cb
        carry = cb[LANES - 1:LANES, :]


def _fox_cum(small, bias_row, batch, t):
    return pl.pallas_call(
        _fox_cum_kernel,
        grid=(batch,),
        in_specs=[pl.BlockSpec((t, LANES), lambda b: (b, 0)), _const_spec((1, LANES))],
        out_specs=pl.BlockSpec((t, LANES), lambda b: (b, 0)),
        out_shape=jax.ShapeDtypeStruct((batch * t, LANES), F32),
        compiler_params=_params(1),
        name="fox_cum",
    )(small, bias_row)


def _fox_kernel(q_ref, k_ref, v_ref, c_ref, o_ref, ka_ref, vt_ref, s_ref, mb_ref):
    blk = q_ref.shape[0]
    t = k_ref.shape[0]
    nkv = t // blk
    pair = pl.program_id(1)
    i = pl.program_id(2)

    @pl.when(i == 0)
    def _():
        sel_r = lax.broadcasted_iota(jnp.int32, (LANES, LANES), 0)
        sel_c = lax.broadcasted_iota(jnp.int32, (LANES, LANES), 1)
        sels = []
        for term in range(CUM_TERMS):
            hit = (((sel_r == 2 * pair) & (sel_c == term))
                   | ((sel_r == 2 * pair + 1) & (sel_c == CUM_TERMS + term)))
            sels.append(jnp.where(hit, 1.0, 0.0).astype(BF16))
        for kb in range(nkv):
            rows = slice(kb * blk, (kb + 1) * blk)
            vt_ref[kb] = v_ref[rows, :].astype(F32).T.astype(BF16)
            parts = _split3(c_ref[rows, :] * (-LOG2E))
            extra = _dot(parts[0], sels[0]) + _dot(parts[1], sels[1]) + _dot(parts[2], sels[2])
            ka_ref[kb, :, 0:LANES] = k_ref[rows, :]
            ka_ref[kb, :, LANES:2 * LANES] = extra.astype(BF16)

    lane = lax.broadcasted_iota(jnp.int32, (blk, LANES), 1)
    q = q_ref[...]
    q_aug = []
    for j in range(2):
        q_h = jnp.where(_half_mask(lane, j), q, jnp.zeros_like(q))
        ones = jnp.where((lane >= CUM_TERMS * j) & (lane < CUM_TERMS * (j + 1)), 1.0, 0.0).astype(BF16)
        q_aug.append(jnp.concatenate([q_h, ones], axis=1))
    krow = lax.broadcasted_iota(jnp.int32, (blk, blk), 0)
    qcol = lax.broadcasted_iota(jnp.int32, (blk, blk), 1)
    causal = krow <= qcol

    def scores(kb, slot, masked):
        k_blk = ka_ref[kb]
        for j in range(2):
            s = _dot_nt(k_blk, q_aug[j])
            if masked:
                s = jnp.where(causal, s, MASK_VALUE)
            s_ref[slot, j] = s
            mb_ref[slot, j] = jnp.max(s, axis=0, keepdims=True)

    def accumulate(kb, slot, state):
        new = []
        for j in range(2):
            m, l, acc = state[j]
            m_new = jnp.maximum(m, mb_ref[slot, j])
            alpha = jnp.exp2(m - m_new)
            p = jnp.exp2(s_ref[slot, j] - m_new)
            l = l * alpha + jnp.sum(p, axis=0, keepdims=True)
            pv = _dot(vt_ref[kb, j * HEAD_DIM:(j + 1) * HEAD_DIM, :], p.astype(BF16))
            new.append((m_new, l, acc * alpha + pv))
        return new

    for n_prev in range(nkv):
        @pl.when(i == n_prev)
        def _(n_prev=n_prev):
            state = [(jnp.full((1, blk), MASK_VALUE, F32), jnp.zeros((1, blk), F32),
                      jnp.zeros((HEAD_DIM, blk), F32))] * 2
            scores(0, 0, masked=(n_prev == 0))
            for kb in range(n_prev):
                scores(kb + 1, (kb + 1) % 2, masked=(kb + 1 == n_prev))
                state = accumulate(kb, kb % 2, state)
            state = accumulate(n_prev, n_prev % 2, state)
            outs = [acc * (1.0 / l) for _, l, acc in state]
            o_ref[...] = jnp.concatenate(outs, axis=0).T.astype(BF16)


def _fox(proj, cum, batch, t, col0):
    nq = t // FOX_BLOCK
    n_pairs = FOX_HEADS // 2
    qc, kc, vc = col0 // LANES, col0 // LANES + n_pairs, col0 // LANES + 2 * n_pairs
    return pl.pallas_call(
        _fox_kernel,
        grid=(batch, n_pairs, nq),
        in_specs=[
            pl.BlockSpec((FOX_BLOCK, LANES), lambda b, p, i: (b * nq + i, qc + p)),
            pl.BlockSpec((t, LANES), lambda b, p, i: (b, kc + p)),
            pl.BlockSpec((t, LANES), lambda b, p, i: (b, vc + p)),
            pl.BlockSpec((t, LANES), lambda b, p, i: (b, 0)),
        ],
        out_specs=pl.BlockSpec((FOX_BLOCK, LANES), lambda b, p, i: (b * nq + i, p)),
        out_shape=jax.ShapeDtypeStruct((batch * t, FOX_HEADS * HEAD_DIM), BF16),
        scratch_shapes=[
            pltpu.VMEM((nq, FOX_BLOCK, 2 * LANES), BF16),
            pltpu.VMEM((nq, LANES, FOX_BLOCK), BF16),
            pltpu.VMEM((2, 2, FOX_BLOCK, FOX_BLOCK), F32),
            pltpu.VMEM((2, 2, 1, FOX_BLOCK), F32),
        ],
        compiler_params=_params(3),
        name="fox",
    )(proj, proj, proj, cum)


def _chunk_attn_kernel(q_ref, k_ref, v_ref, tab_ref, o_ref, kp_ref, vt_ref):
    qb, g = q_ref.shape
    t = k_ref.shape[0]
    n_pad = CA_PAD // qb
    n_win = CA_WINDOW // qb
    i = pl.program_id(1)

    @pl.when(i == 0)
    def _():
        for kb in range(n_pad):
            kp_ref[kb] = jnp.zeros((qb, g), BF16)
            vt_ref[kb] = jnp.zeros((g, qb), BF16)
        for kb in range(t // qb):
            rows = slice(kb * qb, (kb + 1) * qb)
            kp_ref[n_pad + kb] = k_ref[rows, :]
            vt_ref[n_pad + kb] = v_ref[rows, :].astype(F32).T.astype(BF16)

    lane = lax.broadcasted_iota(jnp.int32, (qb, LANES), 1)
    n_heads = g // HEAD_DIM

    def scores(h):
        pair, j = divmod(h, 2)
        cols = slice(pair * LANES, (pair + 1) * LANES)
        q_pair = q_ref[:, cols]
        q_h = jnp.where(_half_mask(lane, j), q_pair, jnp.zeros_like(q_pair))
        s_blocks = []
        for w in range(n_win):
            tab_blk = jnp.where(i + w >= n_pad, w, n_win)
            s_blocks.append(_dot_nt(kp_ref[i + w, :, cols], q_h) + tab_ref[h, tab_blk])
        m = s_blocks[0].max(axis=0, keepdims=True)
        for s in s_blocks[1:]:
            m = jnp.maximum(m, s.max(axis=0, keepdims=True))
        return s_blocks, m

    def weighted_values(h, s_blocks, m):
        l = jnp.zeros((1, qb), F32)
        acc = jnp.zeros((HEAD_DIM, qb), F32)
        for w, s in enumerate(s_blocks):
            p = jnp.exp2(s - m)
            l = l + jnp.sum(p, axis=0, keepdims=True)
            acc = acc + _dot(vt_ref[i + w, h * HEAD_DIM:(h + 1) * HEAD_DIM, :], p.astype(BF16))
        return acc * (1.0 / l)

    outs = []
    pending = scores(0)
    for h in range(n_heads):
        upcoming = scores(h + 1) if h + 1 < n_heads else None
        outs.append(weighted_values(h, *pending))
        pending = upcoming
    o_ref[...] = jnp.concatenate(outs, axis=0).T.astype(BF16)


def _chunk_attn(proj, table, batch, t):
    nq = t // CA_QBLOCK
    g = table.shape[0] * HEAD_DIM
    n_blocks = (CA_PAD + t) // CA_QBLOCK
    return pl.pallas_call(
        _chunk_attn_kernel,
        grid=(batch, nq),
        in_specs=[
            pl.BlockSpec((CA_QBLOCK, g), lambda b, i: (b * nq + i, 0)),
            pl.BlockSpec((t, g), lambda b, i: (b, 1)),
            pl.BlockSpec((t, g), lambda b, i: (b, 2)),
            _const_spec(table.shape),
        ],
        out_specs=pl.BlockSpec((CA_QBLOCK, g), lambda b, i: (b * nq + i, 0)),
        out_shape=jax.ShapeDtypeStruct((batch * t, g), BF16),
        scratch_shapes=[
            pltpu.VMEM((n_blocks, CA_QBLOCK, g), BF16),
            pltpu.VMEM((n_blocks, g, CA_QBLOCK), BF16),
        ],
        compiler_params=_params(2),
        name="chunk_attn",
    )(proj, proj, proj, table)


def _ca_table_kernel(line_ref, o_ref):
    n_win = CA_WINDOW // CA_QBLOCK
    rows = 8
    x = jnp.broadcast_to(line_ref[0] * LOG2E, (rows, CA_LINE))
    qc = lax.broadcasted_iota(jnp.int32, (rows, CA_QBLOCK), 1) // CHUNK
    for grp in range(CA_WINDOW // rows):
        r0 = grp * rows
        y = pltpu.roll(x, (CA_LINE - CA_WINDOW + 1 + r0) % CA_LINE, axis=1, stride=1, stride_axis=0)
        kc = r0 // CHUNK
        allowed = (qc <= kc) & (qc >= kc - CA_LEFT_CHUNKS)
        w, r = divmod(r0, CA_QBLOCK)
        o_ref[0, w, r:r + rows, :] = jnp.where(allowed, y[:, 0:CA_QBLOCK], MASK_VALUE)
    o_ref[0, n_win] = jnp.full((CA_QBLOCK, CA_QBLOCK), MASK_VALUE, F32)


def _chunk_attn_table(rel_bias):
    heads = rel_bias.shape[0]
    n_win = CA_WINDOW // CA_QBLOCK
    left = CA_WINDOW - 1 - CA_PAD - REL_CLIP
    line = jnp.pad(rel_bias.astype(F32), ((0, 0), (left, CA_LINE - left - rel_bias.shape[1])), mode="edge")
    return pl.pallas_call(
        _ca_table_kernel,
        grid=(heads,),
        in_specs=[pl.BlockSpec((1, 1, CA_LINE), lambda h: (h, 0, 0))],
        out_specs=pl.BlockSpec((1, n_win + 1, CA_QBLOCK, CA_QBLOCK), lambda h: (h, 0, 0, 0)),
        out_shape=jax.ShapeDtypeStruct((heads, n_win + 1, CA_QBLOCK, CA_QBLOCK), F32),
        compiler_params=_params(1),
        name="ca_table",
    )(line.reshape(heads, 1, CA_LINE))


def _gelu(x):
    return 0.5 * x * (1.0 + jnp.tanh(0.7978845608028654 * (x + 0.044715 * (x * x * x))))


def _lru_kernel(g_ref, x_ref, cw_ref, cb_ref, wa_ref, ba_ref, wx_ref, bx_ref, lam_ref, o_ref,
                xp_ref, a_ref, b_ref):
    t, w = x_ref.shape
    halo = 8
    xp_ref[0:halo, :] = jnp.zeros((halo, w), F32)
    xp_ref[halo:halo + t, :] = x_ref[...].astype(F32)
    log_base = LRU_C * _log_sigmoid(lam_ref[...])
    half = w // 2

    for blk in range(t // LRU_ROWS):
        r0 = blk * LRU_ROWS
        xc = cb_ref[...] + cw_ref[CONV_WIDTH - 1:CONV_WIDTH, :] * xp_ref[halo + r0:halo + r0 + LRU_ROWS, :]
        for tap in range(1, CONV_WIDTH):
            lo = halo + r0 - tap
            xc = xc + cw_ref[CONV_WIDTH - 1 - tap:CONV_WIDTH - tap, :] * xp_ref[lo:lo + LRU_ROWS, :]
        xcb = xc.astype(BF16)
        gr, gi = [], []
        for hb in range(2):
            cols = slice(hb * half, (hb + 1) * half)
            gr.append(_dot(xcb[:, cols], wa_ref[hb]))
            gi.append(_dot(xcb[:, cols], wx_ref[hb]))
        r = _sigmoid(jnp.concatenate(gr, axis=1) + ba_ref[...])
        gate_i = _sigmoid(jnp.concatenate(gi, axis=1) + bx_ref[...])
        a = jnp.exp(r * log_base)
        a_ref[r0:r0 + LRU_ROWS, :] = a
        b_ref[r0:r0 + LRU_ROWS, :] = jnp.sqrt(1.0 - a * a) * (gate_i * xc)

    row = lax.broadcasted_iota(jnp.int32, (SCAN_ROWS, w), 0)

    def scan(g, h_prev):
        r0 = pl.multiple_of(g * SCAN_ROWS, SCAN_ROWS)
        a = a_ref[pl.ds(r0, SCAN_ROWS), :]
        b = b_ref[pl.ds(r0, SCAN_ROWS), :]
        shift = 1
        while shift < SCAN_ROWS:
            keep = row >= shift
            a_sh = pltpu.roll(a, shift, axis=0)
            b_sh = pltpu.roll(b, shift, axis=0)
            b = jnp.where(keep, b + a * b_sh, b)
            a = jnp.where(keep, a * a_sh, a)
            shift *= 2
        h = b + a * h_prev
        b_ref[pl.ds(r0, SCAN_ROWS), :] = h
        return h[SCAN_ROWS - 1:SCAN_ROWS, :]

    lax.fori_loop(0, t // SCAN_ROWS, scan, jnp.zeros((1, w), F32))

    for blk in range(t // LRU_ROWS):
        rows = slice(blk * LRU_ROWS, (blk + 1) * LRU_ROWS)
        o_ref[rows, :] = (b_ref[rows, :] * _gelu(g_ref[rows, :].astype(F32))).astype(BF16)


def _lru(proj, conv_w, conv_b, wa_bd, b_a, wx_bd, b_x, lam, batch, t, col0):
    w = conv_w.shape[1]
    gc = col0 // w
    return pl.pallas_call(
        _lru_kernel,
        grid=(batch,),
        in_specs=[
            pl.BlockSpec((t, w), lambda b: (b, gc)),
            pl.BlockSpec((t, w), lambda b: (b, gc + 1)),
            _const_spec((CONV_WIDTH, w)),
            _const_spec((1, w)),
            _const_spec(wa_bd.shape),
            _const_spec((1, w)),
            _const_spec(wx_bd.shape),
            _const_spec((1, w)),
            _const_spec((1, w)),
        ],
        out_specs=pl.BlockSpec((t, w), lambda b: (b, 0)),
        out_shape=jax.ShapeDtypeStruct((batch * t, w), BF16),
        scratch_shapes=[
            pltpu.VMEM((t + 8, w), F32),
            pltpu.VMEM((t, w), F32),
            pltpu.VMEM((t, w), F32),
        ],
        compiler_params=_params(1),
        name="rglru",
    )(proj, proj, conv_w, conv_b.reshape(1, w), wa_bd, b_a.reshape(1, w), wx_bd,
      b_x.reshape(1, w), lam.reshape(1, w))


def _block_diag_halves(wblk):
    nb, d, _ = wblk.shape
    per = nb // 2
    eye = jnp.eye(per, dtype=wblk.dtype)
    halves = [jnp.einsum("nde,nm->ndme", wblk[h * per:(h + 1) * per], eye).reshape(per * d, per * d)
              for h in range(2)]
    return jnp.stack(halves).astype(BF16)


def kernel(x, norm_w, w_in_even, gla_w_a_up, gla_b_a, gla_norm_w, fox_b_f, w_out_even,
           w_in_odd, rel_bias, conv_w, conv_b, lru_w_a, lru_b_a, lru_w_x, lru_b_x,
           lru_lambda, w_out_odd, w_mlp_up, w_mlp_down):
    batch, t, d = x.shape
    x2d = x.reshape(batch * t, d)
    group = d // 2
    kw = GLA_HEADS * GLA_DK

    w_in = w_in_even[0]
    o_ga = 2 * kw + 2 * group
    o_fq = o_ga + GLA_RANK
    o_ff = o_fq + 3 * group
    q_scale = LOG2E * HEAD_DIM ** -0.5
    w_main = jnp.concatenate(
        [w_in[:, :o_ga], w_in[:, o_fq:o_fq + group] * q_scale, w_in[:, o_fq + group:o_ff]],
        axis=1).astype(BF16)
    n_small = FOX_HEADS + GLA_RANK
    w_small = jnp.concatenate(
        [w_in[:, o_ff:], w_in[:, o_ga:o_fq], jnp.zeros((d, LANES - n_small), F32)], axis=1).astype(BF16)
    wa_pad = jnp.zeros((LANES, kw), F32).at[FOX_HEADS:n_small].set(gla_w_a_up[0]).astype(BF16)
    fox_bias = jnp.zeros((1, LANES), F32).at[0, :FOX_HEADS].set(fox_b_f[0])

    proj, small = _norm_proj(x2d, norm_w[0, 0], w_main, w_small)
    out_a = _gla(proj, small, wa_pad, gla_b_a[0], gla_norm_w[0], batch, t)
    cum = _fox_cum(small, fox_bias, batch, t)
    out_b = _fox(proj, cum, batch, t, o_ga)
    x2d = _out_proj(out_a, out_b, w_out_even[0].astype(BF16), norm_w[0, 1], x2d)
    x2d = _mlp(x2d, norm_w[0, 2], w_mlp_up[0].astype(BF16), w_mlp_down[0].astype(BF16), norm_w[0, 3])

    w_in = w_in_odd[0]
    w_main = jnp.concatenate([w_in[:, :group] * q_scale, w_in[:, group:]], axis=1).astype(BF16)
    (proj,) = _norm_proj(x2d, norm_w[1, 0], w_main)
    out_c = _chunk_attn(proj, _chunk_attn_table(rel_bias[0]), batch, t)
    out_d = _lru(proj, conv_w[0], conv_b[0], _block_diag_halves(lru_w_a[0]), lru_b_a[0],
                 _block_diag_halves(lru_w_x[0]), lru_b_x[0], lru_lambda[0], batch, t, 3 * group)
    x2d = _out_proj(out_c, out_d, w_out_odd[0].astype(BF16), norm_w[1, 1], x2d)
    x2d = _mlp(x2d, norm_w[1, 2], w_mlp_up[1].astype(BF16), w_mlp_down[1].astype(BF16), norm_w[1, 3])
    return x2d.reshape(batch, t, d)
```

```python
import functools

import jax
import jax.numpy as jnp
from jax import lax
from jax.experimental import pallas as pl
from jax.experimental.pallas import tpu as pltpu

F32 = jnp.float32
BF16 = jnp.bfloat16

NORM_EPS = 1e-6
CHUNK = 64
GLA_HEADS = 4
GLA_DK = 64
GLA_DV = 128
GLA_RANK = 16
GLA_GATE_TAU = 16.0
FOX_HEADS = 8
HEAD_DIM = 64
CA_LEFT_CHUNKS = 8
REL_CLIP = 128
CONV_WIDTH = 4
LRU_BLOCKS = 8
LRU_C = 8.0

LANES = 128
MXU_DIM = 256
MASK_VALUE = -1e30
VMEM_LIMIT_BYTES = 56 * 1024 * 1024

ROW_TILE = 512
FOX_BLOCK = 512
CA_QBLOCK = 4 * CHUNK
CA_WINDOW = CA_QBLOCK + CA_LEFT_CHUNKS * CHUNK
CA_PAD = CA_LEFT_CHUNKS * CHUNK
CA_LINE = 1024
LOG2E = 1.4426950408889634
CUM_TERMS = 3
GLA_GROUP = 4
LRU_ROWS = 256
LRU_FRAMES = 256
CONV_HIST = 8
SCAN_UNROLL = 8


def _params(n_axes):
    return pltpu.CompilerParams(
        dimension_semantics=("arbitrary",) * n_axes,
        vmem_limit_bytes=VMEM_LIMIT_BYTES,
    )


def _const_spec(shape):
    nd = len(shape)
    return pl.BlockSpec(shape, lambda *_: (0,) * nd, pipeline_mode=pl.Buffered(1))


def _rmsnorm(x, w):
    y = x * lax.rsqrt(jnp.mean(x * x, axis=-1, keepdims=True) + NORM_EPS)
    return y * w


def _log_sigmoid(z):
    return jnp.minimum(z, 0.0) - jnp.log1p(jnp.exp(-jnp.abs(z)))


def _sigmoid(z):
    return 0.5 * jnp.tanh(0.5 * z) + 0.5


def _dot(a, b):
    return jnp.dot(a, b, preferred_element_type=F32)


def _dot_nt(a, b):
    return lax.dot_general(a, b, (((1,), (1,)), ((), ())), preferred_element_type=F32)


def _dot_tn(a, b):
    return lax.dot_general(a, b, (((0,), (0,)), ((), ())), preferred_element_type=F32)


def _half_mask(lane, j):
    return lane < HEAD_DIM if j == 0 else lane >= HEAD_DIM


def _split3(x):
    h1 = x.astype(BF16)
    r1 = x - h1.astype(F32)
    h2 = r1.astype(BF16)
    h3 = (r1 - h2.astype(F32)).astype(BF16)
    return h1, h2, h3


def _row_major(i):
    return (i, 0)


def _time_major(tiles_per_row):
    def index_map(i):
        return (i % tiles_per_row, i // tiles_per_row)
    return index_map


def _norm_proj_kernel(x_ref, nw_ref, *refs, n_chunk):
    n_out = len(refs) // 2
    h = _rmsnorm(x_ref[...], nw_ref[...]).astype(BF16)
    for w_ref, o_ref in zip(refs[:n_out], refs[n_out:]):
        n_total = o_ref.shape[1]
        for n0 in range(0, n_total, n_chunk):
            n1 = min(n0 + n_chunk, n_total)
            o_ref[:, n0:n1] = _dot(h, w_ref[:, n0:n1]).astype(o_ref.dtype)


def _norm_proj(x2d, nw, outputs):
    m, d = x2d.shape
    in_specs = [pl.BlockSpec((ROW_TILE, d), _row_major), _const_spec((1, d))]
    in_specs += [_const_spec(w.shape) for w, _, _, _ in outputs]
    return pl.pallas_call(
        functools.partial(_norm_proj_kernel, n_chunk=2 * MXU_DIM),
        grid=(m // ROW_TILE,),
        in_specs=in_specs,
        out_specs=[pl.BlockSpec((ROW_TILE, w.shape[1]), imap) for w, _, _, imap in outputs],
        out_shape=[jax.ShapeDtypeStruct(shape, dt) for _, dt, shape, _ in outputs],
        compiler_params=_params(1),
        name="norm_proj",
    )(x2d, nw.reshape(1, d), *[w for w, _, _, _ in outputs])


def _mix_mlp_kernel(a_ref, b_ref, x_ref, wo_ref, nw_ref, wu_ref, wd_ref, o_ref, u_ref, y_ref, *, chunk):
    g = a_ref.shape[1]
    d_ff = wu_ref.shape[1]
    d = wd_ref.shape[1]
    for n0 in range(0, d, chunk):
        cols = slice(n0, n0 + chunk)
        y_ref[:, cols] = _dot(a_ref[...], wo_ref[:g, cols]) + _dot(b_ref[...], wo_ref[g:, cols])
    x1 = x_ref[...] + _rmsnorm(y_ref[...], nw_ref[0:1, :])
    h = _rmsnorm(x1, nw_ref[1:2, :]).astype(BF16)
    for f0 in range(0, d_ff, chunk):
        u = jnp.maximum(_dot(h, wu_ref[:, f0:f0 + chunk]), 0.0)
        u_ref[:, f0:f0 + chunk] = (u * u).astype(BF16)
    for n0 in range(0, d, chunk):
        y_ref[:, n0:n0 + chunk] = _dot(u_ref[...], wd_ref[:, n0:n0 + chunk])
    o_ref[...] = x1 + _rmsnorm(y_ref[...], nw_ref[2:3, :])


def _mix_mlp(mix_a, mix_b, b_index_map, x2d, w_out, nw3, w_up, w_down):
    m, d = x2d.shape
    g = mix_a.shape[1]
    d_ff = w_up.shape[1]
    return pl.pallas_call(
        functools.partial(_mix_mlp_kernel, chunk=2 * MXU_DIM),
        grid=(m // ROW_TILE,),
        in_specs=[
            pl.BlockSpec((ROW_TILE, g), _row_major),
            pl.BlockSpec((ROW_TILE, g), b_index_map),
            pl.BlockSpec((ROW_TILE, d), _row_major),
            _const_spec((2 * g, d)),
            _const_spec((3, d)),
            _const_spec((d, d_ff)),
            _const_spec((d_ff, d)),
        ],
        out_specs=pl.BlockSpec((ROW_TILE, d), _row_major),
        out_shape=jax.ShapeDtypeStruct((m, d), F32),
        scratch_shapes=[pltpu.VMEM((ROW_TILE, d_ff), BF16), pltpu.VMEM((ROW_TILE, d), F32)],
        compiler_params=_params(1),
        name="mix_mlp",
    )(mix_a, mix_b, x2d, w_out, nw3, w_up, w_down)


def _gla_kernel(q_ref, k_ref, v_ref, r_ref, sm_ref, wa_ref, ba_ref, nw_ref, o_ref,
                la_ref, st_ref, dec_ref):
    t = q_ref.shape[0]
    nc = t // CHUNK
    n_pairs = GLA_HEADS // 2

    z = _dot(sm_ref[...].astype(BF16), wa_ref[...]) + ba_ref[...]
    la_ref[...] = _log_sigmoid(z) * (1.0 / GLA_GATE_TAU)

    grp = GLA_GROUP * CHUNK
    row = lax.broadcasted_iota(jnp.int32, (grp, grp), 0)
    col = lax.broadcasted_iota(jnp.int32, (grp, grp), 1)
    tri = jnp.where((row >= col) & (row // CHUNK == col // CHUNK), 1.0, 0.0).astype(BF16)
    erow = lax.broadcasted_iota(jnp.int32, (grp, GLA_GROUP * LANES), 0)
    ecol = lax.broadcasted_iota(jnp.int32, (grp, GLA_GROUP * LANES), 1)
    chunk_ones = jnp.where(erow // CHUNK == ecol // LANES, 1.0, 0.0).astype(BF16)

    def increments(g, carry):
        r0 = pl.multiple_of(g * grp, grp)
        la = la_ref[pl.ds(r0, grp), :]
        hi = la.astype(BF16)
        lo = (la - hi.astype(F32)).astype(BF16)
        cum = _dot(tri, hi) + _dot(tri, lo)
        dec = jnp.exp(_dot_tn(hi, chunk_ones) + _dot_tn(lo, chunk_ones))
        k_g = k_ref[pl.ds(r0, grp), :].astype(F32)
        v_g = v_ref[pl.ds(r0, grp), :]
        k_dec = []
        for c in range(GLA_GROUP):
            rows = slice(c * CHUNK, (c + 1) * CHUNK)
            total = cum[(c + 1) * CHUNK - 1:(c + 1) * CHUNK, :]
            k_dec.append((k_g[rows] * jnp.exp(total - cum[rows])).astype(BF16))
        for c in range(GLA_GROUP):
            rows = slice(c * CHUNK, (c + 1) * CHUNK)
            dec_ref[g * GLA_GROUP + c] = dec[:, c * LANES:(c + 1) * LANES]
            for h in range(GLA_HEADS):
                p, j = divmod(h, 2)
                full = _dot_tn(k_dec[c][:, p * LANES:(p + 1) * LANES], v_g[rows, h * GLA_DV:(h + 1) * GLA_DV])
                st_ref[g * GLA_GROUP + c, p, j * GLA_DK:(j + 1) * GLA_DK, :] = full[j * GLA_DK:(j + 1) * GLA_DK, :]
        return carry

    lax.fori_loop(0, nc // GLA_GROUP, increments, 0)

    def scan(c, carry):
        for p in range(n_pairs):
            st_ref[c, p] = st_ref[c, p] + dec_ref[c, p * LANES:(p + 1) * LANES, :] * st_ref[c - 1, p]
        return carry

    lax.fori_loop(1, nc, scan, 0)

    lane = lax.broadcasted_iota(jnp.int32, (CHUNK, LANES), 1)
    scale = GLA_DK ** -0.5

    def outputs(g, carry):
        r0 = pl.multiple_of(g * grp, grp)
        q_g = q_ref[pl.ds(r0, grp), :] * jnp.asarray(scale, BF16)
        raw = []
        for c in range(GLA_GROUP):
            rows = slice(c * CHUNK, (c + 1) * CHUNK)
            for h in range(GLA_HEADS):
                p, j = divmod(h, 2)
                q_pair = q_g[rows, p * LANES:(p + 1) * LANES]
                q_h = jnp.where(_half_mask(lane, j), q_pair, jnp.zeros_like(q_pair))
                raw.append(_dot(q_h, st_ref[g * GLA_GROUP + c, p].astype(BF16)))
        for c in range(GLA_GROUP):
            for h in range(GLA_HEADS):
                o = raw[c * GLA_HEADS + h]
                o = o * lax.rsqrt(jnp.mean(o * o, axis=-1, keepdims=True) + NORM_EPS)
                o = o * nw_ref[:, h * GLA_DV:(h + 1) * GLA_DV]
                rows = pl.ds(r0 + c * CHUNK, CHUNK)
                r_h = r_ref[rows, h * GLA_DV:(h + 1) * GLA_DV].astype(F32)
                o_ref[rows, h * GLA_DV:(h + 1) * GLA_DV] = (o * (r_h * _sigmoid(r_h))).astype(BF16)
        return carry

    lax.fori_loop(0, nc // GLA_GROUP, outputs, 0)


def _gla(proj, small, wa_pad, b_a, norm_w, batch, t):
    kw = GLA_HEADS * GLA_DK
    gw = GLA_HEADS * GLA_DV
    nc = t // CHUNK
    return pl.pallas_call(
        _gla_kernel,
        grid=(batch,),
        in_specs=[
            pl.BlockSpec((t, kw), lambda b: (b, 0)),
            pl.BlockSpec((t, kw), lambda b: (b, 1)),
            pl.BlockSpec((t, gw), lambda b: (b, 1)),
            pl.BlockSpec((t, gw), lambda b: (b, 2)),
            pl.BlockSpec((t, LANES), lambda b: (b, 0)),
            _const_spec((LANES, kw)),
            _const_spec((1, kw)),
            _const_spec((1, gw)),
        ],
        out_specs=pl.BlockSpec((t, gw), lambda b: (b, 0)),
        out_shape=jax.ShapeDtypeStruct((batch * t, gw), BF16),
        scratch_shapes=[
            pltpu.VMEM((t, kw), F32),
            pltpu.VMEM((nc, GLA_HEADS // 2, LANES, GLA_DV), F32),
            pltpu.VMEM((nc, kw, LANES), F32),
        ],
        compiler_params=_params(1),
        name="gla",
    )(proj, proj, proj, proj, small, wa_pad, b_a.reshape(1, kw), norm_w.reshape(1, gw))


def _fox_cum_kernel(sm_ref, bias_ref, o_ref):
    t = sm_ref.shape[0]
    row = lax.broadcasted_iota(jnp.int32, (LANES, LANES), 0)
    col = lax.broadcasted_iota(jnp.int32, (LANES, LANES), 1)
    tri = jnp.where(row >= col, 1.0, 0.0).astype(BF16)
    carry = jnp.zeros((1, LANES), F32)
    for blk in range(t // LANES):
        rows = slice(blk * LANES, (blk + 1) * LANES)
        ls = _log_sigmoid(sm_ref[rows, :] + bias_ref[...])
        h1, h2, h3 = _split3(ls)
        cb = _dot(tri, h1) + _dot(tri, h2) + _dot(tri, h3) + carry
        o_ref[rows, :] = cb
        carry = cb[LANES - 1:LANES, :]


def _fox_cum(small, bias_row, batch, t):
    return pl.pallas_call(
        _fox_cum_kernel,
        grid=(batch,),
        in_specs=[pl.BlockSpec((t, LANES), lambda b: (b, 0)), _const_spec((1, LANES))],
        out_specs=pl.BlockSpec((t, LANES), lambda b: (b, 0)),
        out_shape=jax.ShapeDtypeStruct((batch * t, LANES), F32),
        compiler_params=_params(1),
        name="fox_cum",
    )(small, bias_row)


def _fox_kernel(q_ref, k_ref, v_ref, c_ref, o_ref, ka_ref, vt_ref, s_ref, mb_ref):
    blk = q_ref.shape[0]
    t = k_ref.shape[0]
    nkv = t // blk
    pair = pl.program_id(1)
    i = pl.program_id(2)

    @pl.when(i == 0)
    def _():
        sel_r = lax.broadcasted_iota(jnp.int32, (LANES, LANES), 0)
        sel_c = lax.broadcasted_iota(jnp.int32, (LANES, LANES), 1)
        sels = []
        for term in range(CUM_TERMS):
            hit = (((sel_r == 2 * pair) & (sel_c == term))
                   | ((sel_r == 2 * pair + 1) & (sel_c == CUM_TERMS + term)))
            sels.append(jnp.where(hit, 1.0, 0.0).astype(BF16))
        for kb in range(nkv):
            rows = slice(kb * blk, (kb + 1) * blk)
            vt_ref[kb] = v_ref[rows, :].astype(F32).T.astype(BF16)
            parts = _split3(c_ref[rows, :] * (-LOG2E))
            extra = _dot(parts[0], sels[0]) + _dot(parts[1], sels[1]) + _dot(parts[2], sels[2])
            ka_ref[kb, :, 0:LANES] = k_ref[rows, :]
            ka_ref[kb, :, LANES:2 * LANES] = extra.astype(BF16)

    lane = lax.broadcasted_iota(jnp.int32, (blk, LANES), 1)
    q = q_ref[...]
    q_aug = []
    for j in range(2):
        q_h = jnp.where(_half_mask(lane, j), q, jnp.zeros_like(q))
        ones = jnp.where((lane >= CUM_TERMS * j) & (lane < CUM_TERMS * (j + 1)), 1.0, 0.0).astype(BF16)
        q_aug.append(jnp.concatenate([q_h, ones], axis=1))
    krow = lax.broadcasted_iota(jnp.int32, (blk, blk), 0)
    qcol = lax.broadcasted_iota(jnp.int32, (blk, blk), 1)
    causal = krow <= qcol

    def scores(kb, slot, masked):
        k_blk = ka_ref[kb]
        for j in range(2):
            s = _dot_nt(k_blk, q_aug[j])
            if masked:
                s = jnp.where(causal, s, MASK_VALUE)
            s_ref[slot, j] = s
            mb_ref[slot, j] = jnp.max(s, axis=0, keepdims=True)

    def accumulate(kb, slot, state):
        new = []
        for j in range(2):
            m, l, acc = state[j]
            m_new = jnp.maximum(m, mb_ref[slot, j])
            alpha = jnp.exp2(m - m_new)
            p = jnp.exp2(s_ref[slot, j] - m_new)
            l = l * alpha + jnp.sum(p, axis=0, keepdims=True)
            pv = _dot(vt_ref[kb, j * HEAD_DIM:(j + 1) * HEAD_DIM, :], p.astype(BF16))
            new.append((m_new, l, acc * alpha + pv))
        return new

    for n_prev in range(nkv):
        @pl.when(i == n_prev)
        def _(n_prev=n_prev):
            state = [(jnp.full((1, blk), MASK_VALUE, F32), jnp.zeros((1, blk), F32),
                      jnp.zeros((HEAD_DIM, blk), F32))] * 2
            scores(0, 0, masked=(n_prev == 0))
            for kb in range(n_prev):
                scores(kb + 1, (kb + 1) % 2, masked=(kb + 1 == n_prev))
                state = accumulate(kb, kb % 2, state)
            state = accumulate(n_prev, n_prev % 2, state)
            outs = [acc * (1.0 / l) for _, l, acc in state]
            o_ref[...] = jnp.concatenate(outs, axis=0).T.astype(BF16)


def _fox(proj, cum, batch, t, col0):
    nq = t // FOX_BLOCK
    n_pairs = FOX_HEADS // 2
    qc, kc, vc = col0 // LANES, col0 // LANES + n_pairs, col0 // LANES + 2 * n_pairs
    return pl.pallas_call(
        _fox_kernel,
        grid=(batch, n_pairs, nq),
        in_specs=[
            pl.BlockSpec((FOX_BLOCK, LANES), lambda b, p, i: (b * nq + i, qc + p)),
            pl.BlockSpec((t, LANES), lambda b, p, i: (b, kc + p)),
            pl.BlockSpec((t, LANES), lambda b, p, i: (b, vc + p)),
            pl.BlockSpec((t, LANES), lambda b, p, i: (b, 0)),
        ],
        out_specs=pl.BlockSpec((FOX_BLOCK, LANES), lambda b, p, i: (b * nq + i, p)),
        out_shape=jax.ShapeDtypeStruct((batch * t, FOX_HEADS * HEAD_DIM), BF16),
        scratch_shapes=[
            pltpu.VMEM((nq, FOX_BLOCK, 2 * LANES), BF16),
            pltpu.VMEM((nq, LANES, FOX_BLOCK), BF16),
            pltpu.VMEM((2, 2, FOX_BLOCK, FOX_BLOCK), F32),
            pltpu.VMEM((2, 2, 1, FOX_BLOCK), F32),
        ],
        compiler_params=_params(3),
        name="fox",
    )(proj, proj, proj, cum)


def _chunk_attn_kernel(q_ref, k_ref, v_ref, tab_ref, o_ref, kp_ref, vt_ref):
    qb, g = q_ref.shape
    t = k_ref.shape[0]
    n_pad = CA_PAD // qb
    n_win = CA_WINDOW // qb
    i = pl.program_id(1)

    @pl.when(i == 0)
    def _():
        for kb in range(n_pad):
            kp_ref[kb] = jnp.zeros((qb, g), BF16)
            vt_ref[kb] = jnp.zeros((g, qb), BF16)
        for kb in range(t // qb):
            rows = slice(kb * qb, (kb + 1) * qb)
            kp_ref[n_pad + kb] = k_ref[rows, :]
            vt_ref[n_pad + kb] = v_ref[rows, :].astype(F32).T.astype(BF16)

    lane = lax.broadcasted_iota(jnp.int32, (qb, LANES), 1)
    n_heads = g // HEAD_DIM

    def scores(h):
        pair, j = divmod(h, 2)
        cols = slice(pair * LANES, (pair + 1) * LANES)
        q_pair = q_ref[:, cols]
        q_h = jnp.where(_half_mask(lane, j), q_pair, jnp.zeros_like(q_pair))
        s_blocks = []
        for w in range(n_win):
            tab_blk = jnp.where(i + w >= n_pad, w, n_win)
            s_blocks.append(_dot_nt(kp_ref[i + w, :, cols], q_h) + tab_ref[h, tab_blk])
        m = s_blocks[0].max(axis=0, keepdims=True)
        for s in s_blocks[1:]:
            m = jnp.maximum(m, s.max(axis=0, keepdims=True))
        return s_blocks, m

    def weighted_values(h, s_blocks, m):
        l = jnp.zeros((1, qb), F32)
        acc = jnp.zeros((HEAD_DIM, qb), F32)
        for w, s in enumerate(s_blocks):
            p = jnp.exp2(s - m)
            l = l + jnp.sum(p, axis=0, keepdims=True)
            acc = acc + _dot(vt_ref[i + w, h * HEAD_DIM:(h + 1) * HEAD_DIM, :], p.astype(BF16))
        return acc * (1.0 / l)

    outs = []
    pending = scores(0)
    for h in range(n_heads):
        upcoming = scores(h + 1) if h + 1 < n_heads else None
        outs.append(weighted_values(h, *pending))
        pending = upcoming
    o_ref[...] = jnp.concatenate(outs, axis=0).T.astype(BF16)


def _chunk_attn(proj, table, batch, t):
    nq = t // CA_QBLOCK
    g = table.shape[0] * HEAD_DIM
    n_blocks = (CA_PAD + t) // CA_QBLOCK
    return pl.pallas_call(
        _chunk_attn_kernel,
        grid=(batch, nq),
        in_specs=[
            pl.BlockSpec((CA_QBLOCK, g), lambda b, i: (b * nq + i, 0)),
            pl.BlockSpec((t, g), lambda b, i: (b, 1)),
            pl.BlockSpec((t, g), lambda b, i: (b, 2)),
            _const_spec(table.shape),
        ],
        out_specs=pl.BlockSpec((CA_QBLOCK, g), lambda b, i: (b * nq + i, 0)),
        out_shape=jax.ShapeDtypeStruct((batch * t, g), BF16),
        scratch_shapes=[
            pltpu.VMEM((n_blocks, CA_QBLOCK, g), BF16),
            pltpu.VMEM((n_blocks, g, CA_QBLOCK), BF16),
        ],
        compiler_params=_params(2),
        name="chunk_attn",
    )(proj, proj, proj, table)


def _ca_table_kernel(line_ref, o_ref):
    n_win = CA_WINDOW // CA_QBLOCK
    rows = 8
    x = jnp.broadcast_to(line_ref[0] * LOG2E, (rows, CA_LINE))
    qc = lax.broadcasted_iota(jnp.int32, (rows, CA_QBLOCK), 1) // CHUNK
    for grp in range(CA_WINDOW // rows):
        r0 = grp * rows
        y = pltpu.roll(x, (CA_LINE - CA_WINDOW + 1 + r0) % CA_LINE, axis=1, stride=1, stride_axis=0)
        kc = r0 // CHUNK
        allowed = (qc <= kc) & (qc >= kc - CA_LEFT_CHUNKS)
        w, r = divmod(r0, CA_QBLOCK)
        o_ref[0, w, r:r + rows, :] = jnp.where(allowed, y[:, 0:CA_QBLOCK], MASK_VALUE)
    o_ref[0, n_win] = jnp.full((CA_QBLOCK, CA_QBLOCK), MASK_VALUE, F32)


def _chunk_attn_table(rel_bias):
    heads = rel_bias.shape[0]
    n_win = CA_WINDOW // CA_QBLOCK
    left = CA_WINDOW - 1 - CA_PAD - REL_CLIP
    line = jnp.pad(rel_bias.astype(F32), ((0, 0), (left, CA_LINE - left - rel_bias.shape[1])), mode="edge")
    return pl.pallas_call(
        _ca_table_kernel,
        grid=(heads,),
        in_specs=[pl.BlockSpec((1, 1, CA_LINE), lambda h: (h, 0, 0))],
        out_specs=pl.BlockSpec((1, n_win + 1, CA_QBLOCK, CA_QBLOCK), lambda h: (h, 0, 0, 0)),
        out_shape=jax.ShapeDtypeStruct((heads, n_win + 1, CA_QBLOCK, CA_QBLOCK), F32),
        compiler_params=_params(1),
        name="ca_table",
    )(line.reshape(heads, 1, CA_LINE))


GELU_C0 = 0.7978845608028654
GELU_C1 = GELU_C0 * 0.044715


def _gelu(x):
    inner = x * (GELU_C0 + GELU_C1 * (x * x))
    return (0.5 * x) * (1.0 + jnp.tanh(inner))


def _lru_kernel(g_ref, x_ref, cw_ref, cb_ref, wa_ref, ba_ref, wx_ref, bx_ref, lam_ref, o_ref,
                xf_ref, a_ref, b_ref, h_ref):
    slabs, rows2, w = x_ref.shape
    nb = rows2 // 2
    frames = 2 * slabs
    sub = LRU_ROWS // nb
    i = pl.program_id(0)

    @pl.when(i == 0)
    def _():
        xf_ref[0:CONV_HIST] = jnp.zeros((CONV_HIST, nb, w), F32)
        h_ref[...] = jnp.zeros((nb, w), F32)

    for blk in range(frames // sub):
        t0 = blk * sub
        xs = x_ref[t0 // 2:(t0 + sub) // 2].astype(F32)
        xf_ref[CONV_HIST + t0:CONV_HIST + t0 + sub] = xs.reshape(sub, nb, w)

    log2_base = (LRU_C * LOG2E) * _log_sigmoid(lam_ref[...])
    half = w // 2
    for blk in range(frames // sub):
        t0 = blk * sub
        xc = cb_ref[...]
        for tap in range(CONV_WIDTH):
            lo = CONV_HIST + t0 - tap
            xc = xc + (cw_ref[CONV_WIDTH - 1 - tap:CONV_WIDTH - tap, :]
                       * xf_ref[lo:lo + sub].reshape(LRU_ROWS, w))
        xcb = xc.astype(BF16)
        gr, gi = [], []
        for hb in range(2):
            cols = slice(hb * half, (hb + 1) * half)
            gr.append(_dot(xcb[:, cols], wa_ref[hb]))
            gi.append(_dot(xcb[:, cols], wx_ref[hb]))
        r = _sigmoid(jnp.concatenate(gr, axis=1) + ba_ref[...])
        gate_i = _sigmoid(jnp.concatenate(gi, axis=1) + bx_ref[...])
        a = jnp.exp2(r * log2_base)
        a_ref[t0:t0 + sub] = a.reshape(sub, nb, w)
        y = 1.0 - a * a
        root = jnp.where(y > 0.0, y * lax.rsqrt(y), 0.0)
        b_ref[t0:t0 + sub] = (root * (gate_i * xc)).reshape(sub, nb, w)

    xf_ref[0:CONV_HIST] = xf_ref[frames:frames + CONV_HIST]

    def scan(t, h):
        h = a_ref[t] * h + b_ref[t]
        b_ref[t] = h
        return h

    h_ref[...] = lax.fori_loop(0, frames, scan, h_ref[...], unroll=SCAN_UNROLL)

    for blk in range(frames // sub):
        t0 = blk * sub
        hv = b_ref[t0:t0 + sub].reshape(LRU_ROWS, w)
        gate = g_ref[t0 // 2:(t0 + sub) // 2].astype(F32).reshape(LRU_ROWS, w)
        o_ref[t0 // 2:(t0 + sub) // 2] = (hv * _gelu(gate)).astype(BF16).reshape(sub // 2, rows2, w)


def _lru(proj_tm, conv_w, conv_b, wa_bd, b_a, wx_bd, b_x, lam):
    slabs_total, rows2, _ = proj_tm.shape
    w = conv_w.shape[1]
    slabs = LRU_FRAMES // 2
    return pl.pallas_call(
        _lru_kernel,
        grid=(slabs_total // slabs,),
        in_specs=[
            pl.BlockSpec((slabs, rows2, w), lambda i: (i, 0, 0)),
            pl.BlockSpec((slabs, rows2, w), lambda i: (i, 0, 1)),
            _const_spec((CONV_WIDTH, w)),
            _const_spec((1, w)),
            _const_spec(wa_bd.shape),
            _const_spec((1, w)),
            _const_spec(wx_bd.shape),
            _const_spec((1, w)),
            _const_spec((1, w)),
        ],
        out_specs=pl.BlockSpec((slabs, rows2, w), lambda i: (i, 0, 0)),
        out_shape=jax.ShapeDtypeStruct((slabs_total, rows2, w), BF16),
        scratch_shapes=[
            pltpu.VMEM((CONV_HIST + LRU_FRAMES, rows2 // 2, w), F32),
            pltpu.VMEM((LRU_FRAMES, rows2 // 2, w), F32),
            pltpu.VMEM((LRU_FRAMES, rows2 // 2, w), F32),
            pltpu.VMEM((rows2 // 2, w), F32),
        ],
        compiler_params=_params(1),
        name="rglru",
    )(proj_tm, proj_tm, conv_w, conv_b.reshape(1, w), wa_bd, b_a.reshape(1, w), wx_bd,
      b_x.reshape(1, w), lam.reshape(1, w))


def _block_diag_halves(wblk):
    nb, d, _ = wblk.shape
    per = nb // 2
    eye = jnp.eye(per, dtype=wblk.dtype)
    halves = [jnp.einsum("nde,nm->ndme", wblk[h * per:(h + 1) * per], eye).reshape(per * d, per * d)
              for h in range(2)]
    return jnp.stack(halves).astype(BF16)


def kernel(x, norm_w, w_in_even, gla_w_a_up, gla_b_a, gla_norm_w, fox_b_f, w_out_even,
           w_in_odd, rel_bias, conv_w, conv_b, lru_w_a, lru_b_a, lru_w_x, lru_b_x,
           lru_lambda, w_out_odd, w_mlp_up, w_mlp_down):
    batch, t, d = x.shape
    x2d = x.reshape(batch * t, d)
    group = d // 2
    kw = GLA_HEADS * GLA_DK

    w_in = w_in_even[0]
    o_ga = 2 * kw + 2 * group
    o_fq = o_ga + GLA_RANK
    o_ff = o_fq + 3 * group
    q_scale = LOG2E * HEAD_DIM ** -0.5
    w_main = jnp.concatenate(
        [w_in[:, :o_ga], w_in[:, o_fq:o_fq + group] * q_scale, w_in[:, o_fq + group:o_ff]],
        axis=1).astype(BF16)
    n_small = FOX_HEADS + GLA_RANK
    w_small = jnp.concatenate(
        [w_in[:, o_ff:], w_in[:, o_ga:o_fq], jnp.zeros((d, LANES - n_small), F32)], axis=1).astype(BF16)
    wa_pad = jnp.zeros((LANES, kw), F32).at[FOX_HEADS:n_small].set(gla_w_a_up[0]).astype(BF16)
    fox_bias = jnp.zeros((1, LANES), F32).at[0, :FOX_HEADS].set(fox_b_f[0])

    m = batch * t
    proj, small = _norm_proj(x2d, norm_w[0, 0], [
        (w_main, BF16, (m, w_main.shape[1]), _row_major),
        (w_small, F32, (m, LANES), _row_major)])
    out_a = _gla(proj, small, wa_pad, gla_b_a[0], gla_norm_w[0], batch, t)
    cum = _fox_cum(small, fox_bias, batch, t)
    out_b = _fox(proj, cum, batch, t, o_ga)
    x2d = _mix_mlp(out_a, out_b, _row_major, x2d, w_out_even[0].astype(BF16), norm_w[0, 1:4],
                   w_mlp_up[0].astype(BF16), w_mlp_down[0].astype(BF16))

    w_in = w_in_odd[0]
    w_attn = jnp.concatenate([w_in[:, :group] * q_scale, w_in[:, group:3 * group]], axis=1).astype(BF16)
    w_lru = w_in[:, 3 * group:].astype(BF16)
    time_major = _time_major(t // ROW_TILE)
    proj, proj_tm = _norm_proj(x2d, norm_w[1, 0], [
        (w_attn, BF16, (m, 3 * group), _row_major),
        (w_lru, BF16, (t, batch * 2 * group), time_major)])
    out_c = _chunk_attn(proj, _chunk_attn_table(rel_bias[0]), batch, t)
    out_d = _lru(proj_tm.reshape(t // 2, 2 * batch, 2 * group), conv_w[0], conv_b[0],
                 _block_diag_halves(lru_w_a[0]), lru_b_a[0],
                 _block_diag_halves(lru_w_x[0]), lru_b_x[0], lru_lambda[0])
    x2d = _mix_mlp(out_c, out_d.reshape(t, batch * group), time_major, x2d,
                   w_out_odd[0].astype(BF16), norm_w[1, 1:4],
                   w_mlp_up[1].astype(BF16), w_mlp_down[1].astype(BF16))
    return x2d.reshape(batch, t, d)
```

```python
import functools

import jax
import jax.numpy as jnp
from jax import lax
from jax.experimental import pallas as pl
from jax.experimental.pallas import tpu as pltpu

F32 = jnp.float32
BF16 = jnp.bfloat16

NORM_EPS = 1e-6
CHUNK = 64
GLA_HEADS = 4
GLA_DK = 64
GLA_DV = 128
GLA_RANK = 16
GLA_GATE_TAU = 16.0
FOX_HEADS = 8
HEAD_DIM = 64
CA_LEFT_CHUNKS = 8
REL_CLIP = 128
CONV_WIDTH = 4
LRU_BLOCKS = 8
LRU_C = 8.0

LANES = 128
MXU_DIM = 256
MASK_VALUE = -1e30
VMEM_LIMIT_BYTES = 56 * 1024 * 1024

ROW_TILE = 512
PROJ_ROW_TILE = 1024
FOX_BLOCK = 512
CA_QBLOCK = 4 * CHUNK
CA_WINDOW = CA_QBLOCK + CA_LEFT_CHUNKS * CHUNK
CA_PAD = CA_LEFT_CHUNKS * CHUNK
CA_LINE = 1024
LOG2E = 1.4426950408889634
CUM_TERMS = 3
GLA_GROUP = 4
LRU_ROWS = 256
LRU_FRAMES = 256
LRU_PITCH = LRU_FRAMES + 8
CONV_HIST = 8
SCAN_UNROLL = 8


def _params(n_axes):
    return pltpu.CompilerParams(
        dimension_semantics=("arbitrary",) * n_axes,
        vmem_limit_bytes=VMEM_LIMIT_BYTES,
    )


def _const_spec(shape):
    nd = len(shape)
    return pl.BlockSpec(shape, lambda *_: (0,) * nd, pipeline_mode=pl.Buffered(1))


def _rmsnorm(x, w):
    y = x * lax.rsqrt(jnp.mean(x * x, axis=-1, keepdims=True) + NORM_EPS)
    return y * w


def _log_sigmoid(z):
    return jnp.minimum(z, 0.0) - jnp.log1p(jnp.exp(-jnp.abs(z)))


def _sigmoid(z):
    return 0.5 * jnp.tanh(0.5 * z) + 0.5


def _dot(a, b):
    return jnp.dot(a, b, preferred_element_type=F32)


def _dot_nt(a, b):
    return lax.dot_general(a, b, (((1,), (1,)), ((), ())), preferred_element_type=F32)


def _dot_tn(a, b):
    return lax.dot_general(a, b, (((0,), (0,)), ((), ())), preferred_element_type=F32)


def _half_mask(lane, j):
    return lane < HEAD_DIM if j == 0 else lane >= HEAD_DIM


def _split3(x):
    h1 = x.astype(BF16)
    r1 = x - h1.astype(F32)
    h2 = r1.astype(BF16)
    h3 = (r1 - h2.astype(F32)).astype(BF16)
    return h1, h2, h3


def _row_major(i):
    return (i, 0)


def _norm_proj_kernel(x_ref, nw_ref, *refs, n_chunk):
    n_out = len(refs) // 2
    h = _rmsnorm(x_ref[...], nw_ref[...]).astype(BF16)
    for w_ref, o_ref in zip(refs[:n_out], refs[n_out:]):
        n_total = o_ref.shape[1]
        for n0 in range(0, n_total, n_chunk):
            n1 = min(n0 + n_chunk, n_total)
            o_ref[:, n0:n1] = _dot(h, w_ref[:, n0:n1]).astype(o_ref.dtype)


def _norm_proj(x2d, nw, outputs):
    m, d = x2d.shape
    in_specs = [pl.BlockSpec((PROJ_ROW_TILE, d), _row_major), _const_spec((1, d))]
    in_specs += [_const_spec(w.shape) for w, _ in outputs]
    return pl.pallas_call(
        functools.partial(_norm_proj_kernel, n_chunk=2 * MXU_DIM),
        grid=(m // PROJ_ROW_TILE,),
        in_specs=in_specs,
        out_specs=[pl.BlockSpec((PROJ_ROW_TILE, w.shape[1]), _row_major) for w, _ in outputs],
        out_shape=[jax.ShapeDtypeStruct((m, w.shape[1]), dt) for w, dt in outputs],
        compiler_params=_params(1),
        name="norm_proj",
    )(x2d, nw.reshape(1, d), *[w for w, _ in outputs])


def _mix_mlp_kernel(a_ref, b_ref, x_ref, wo_ref, nw_ref, wu_ref, wd_ref, o_ref, u_ref, y_ref, *, chunk):
    d_ff = wu_ref.shape[1]
    d = wd_ref.shape[1]
    mix = jnp.concatenate([a_ref[...], b_ref[...]], axis=1)
    for n0 in range(0, d, chunk):
        cols = slice(n0, n0 + chunk)
        y_ref[:, cols] = _dot(mix, wo_ref[:, cols])
    x1 = x_ref[...] + _rmsnorm(y_ref[...], nw_ref[0:1, :])
    h = _rmsnorm(x1, nw_ref[1:2, :]).astype(BF16)
    for f0 in range(0, d_ff, chunk):
        u = jnp.maximum(_dot(h, wu_ref[:, f0:f0 + chunk]), 0.0)
        u_ref[:, f0:f0 + chunk] = (u * u).astype(BF16)
    for n0 in range(0, d, chunk):
        y_ref[:, n0:n0 + chunk] = _dot(u_ref[...], wd_ref[:, n0:n0 + chunk])
    o_ref[...] = x1 + _rmsnorm(y_ref[...], nw_ref[2:3, :])


def _mix_mlp(mix_a, mix_b, x2d, w_out, nw3, w_up, w_down):
    m, d = x2d.shape
    g = mix_a.shape[1]
    d_ff = w_up.shape[1]
    return pl.pallas_call(
        functools.partial(_mix_mlp_kernel, chunk=2 * MXU_DIM),
        grid=(m // ROW_TILE,),
        in_specs=[
            pl.BlockSpec((ROW_TILE, g), _row_major),
            pl.BlockSpec((ROW_TILE, g), _row_major),
            pl.BlockSpec((ROW_TILE, d), _row_major),
            _const_spec((2 * g, d)),
            _const_spec((3, d)),
            _const_spec((d, d_ff)),
            _const_spec((d_ff, d)),
        ],
        out_specs=pl.BlockSpec((ROW_TILE, d), _row_major),
        out_shape=jax.ShapeDtypeStruct((m, d), F32),
        scratch_shapes=[pltpu.VMEM((ROW_TILE, d_ff), BF16), pltpu.VMEM((ROW_TILE, d), F32)],
        compiler_params=_params(1),
        name="mix_mlp",
    )(mix_a, mix_b, x2d, w_out, nw3, w_up, w_down)


def _gla_kernel(q_ref, k_ref, v_ref, r_ref, sm_ref, wa_ref, ba_ref, nw_ref, o_ref,
                la_ref, st_ref, dec_ref):
    t = q_ref.shape[0]
    nc = t // CHUNK
    n_pairs = GLA_HEADS // 2

    z = _dot(sm_ref[...].astype(BF16), wa_ref[...]) + ba_ref[...]
    la_ref[...] = _log_sigmoid(z) * (1.0 / GLA_GATE_TAU)

    grp = GLA_GROUP * CHUNK
    row = lax.broadcasted_iota(jnp.int32, (grp, grp), 0)
    col = lax.broadcasted_iota(jnp.int32, (grp, grp), 1)
    tri = jnp.where((row >= col) & (row // CHUNK == col // CHUNK), 1.0, 0.0).astype(BF16)
    erow = lax.broadcasted_iota(jnp.int32, (grp, GLA_GROUP * LANES), 0)
    ecol = lax.broadcasted_iota(jnp.int32, (grp, GLA_GROUP * LANES), 1)
    chunk_ones = jnp.where(erow // CHUNK == ecol // LANES, 1.0, 0.0).astype(BF16)

    def increments(g, carry):
        r0 = pl.multiple_of(g * grp, grp)
        la = la_ref[pl.ds(r0, grp), :]
        hi = la.astype(BF16)
        lo = (la - hi.astype(F32)).astype(BF16)
        cum = _dot(tri, hi) + _dot(tri, lo)
        dec = jnp.exp(_dot_tn(hi, chunk_ones) + _dot_tn(lo, chunk_ones))
        k_g = k_ref[pl.ds(r0, grp), :].astype(F32)
        v_g = v_ref[pl.ds(r0, grp), :]
        k_dec = []
        for c in range(GLA_GROUP):
            rows = slice(c * CHUNK, (c + 1) * CHUNK)
            total = cum[(c + 1) * CHUNK - 1:(c + 1) * CHUNK, :]
            k_dec.append((k_g[rows] * jnp.exp(total - cum[rows])).astype(BF16))
        for c in range(GLA_GROUP):
            rows = slice(c * CHUNK, (c + 1) * CHUNK)
            dec_ref[g * GLA_GROUP + c] = dec[:, c * LANES:(c + 1) * LANES]
            for h in range(GLA_HEADS):
                p, j = divmod(h, 2)
                full = _dot_tn(k_dec[c][:, p * LANES:(p + 1) * LANES], v_g[rows, h * GLA_DV:(h + 1) * GLA_DV])
                st_ref[g * GLA_GROUP + c, p, j * GLA_DK:(j + 1) * GLA_DK, :] = full[j * GLA_DK:(j + 1) * GLA_DK, :]
        return carry

    lax.fori_loop(0, nc // GLA_GROUP, increments, 0)

    def scan(c, carry):
        for p in range(n_pairs):
            st_ref[c, p] = st_ref[c, p] + dec_ref[c, p * LANES:(p + 1) * LANES, :] * st_ref[c - 1, p]
        return carry

    lax.fori_loop(1, nc, scan, 0)

    lane = lax.broadcasted_iota(jnp.int32, (CHUNK, LANES), 1)
    scale = GLA_DK ** -0.5

    def outputs(g, carry):
        r0 = pl.multiple_of(g * grp, grp)
        q_g = q_ref[pl.ds(r0, grp), :] * jnp.asarray(scale, BF16)
        raw = []
        for c in range(GLA_GROUP):
            rows = slice(c * CHUNK, (c + 1) * CHUNK)
            for h in range(GLA_HEADS):
                p, j = divmod(h, 2)
                q_pair = q_g[rows, p * LANES:(p + 1) * LANES]
                q_h = jnp.where(_half_mask(lane, j), q_pair, jnp.zeros_like(q_pair))
                raw.append(_dot(q_h, st_ref[g * GLA_GROUP + c, p].astype(BF16)))
        for c in range(GLA_GROUP):
            for h in range(GLA_HEADS):
                o = raw[c * GLA_HEADS + h]
                o = o * lax.rsqrt(jnp.mean(o * o, axis=-1, keepdims=True) + NORM_EPS)
                o = o * nw_ref[:, h * GLA_DV:(h + 1) * GLA_DV]
                rows = pl.ds(r0 + c * CHUNK, CHUNK)
                r_h = r_ref[rows, h * GLA_DV:(h + 1) * GLA_DV].astype(F32)
                o_ref[rows, h * GLA_DV:(h + 1) * GLA_DV] = (o * (r_h * _sigmoid(r_h))).astype(BF16)
        return carry

    lax.fori_loop(0, nc // GLA_GROUP, outputs, 0)


def _gla(proj, small, wa_pad, b_a, norm_w, batch, t):
    kw = GLA_HEADS * GLA_DK
    gw = GLA_HEADS * GLA_DV
    nc = t // CHUNK
    return pl.pallas_call(
        _gla_kernel,
        grid=(batch,),
        in_specs=[
            pl.BlockSpec((t, kw), lambda b: (b, 0)),
            pl.BlockSpec((t, kw), lambda b: (b, 1)),
            pl.BlockSpec((t, gw), lambda b: (b, 1)),
            pl.BlockSpec((t, gw), lambda b: (b, 2)),
            pl.BlockSpec((t, LANES), lambda b: (b, 0)),
            _const_spec((LANES, kw)),
            _const_spec((1, kw)),
            _const_spec((1, gw)),
        ],
        out_specs=pl.BlockSpec((t, gw), lambda b: (b, 0)),
        out_shape=jax.ShapeDtypeStruct((batch * t, gw), BF16),
        scratch_shapes=[
            pltpu.VMEM((t, kw), F32),
            pltpu.VMEM((nc, GLA_HEADS // 2, LANES, GLA_DV), F32),
            pltpu.VMEM((nc, kw, LANES), F32),
        ],
        compiler_params=_params(1),
        name="gla",
    )(proj, proj, proj, proj, small, wa_pad, b_a.reshape(1, kw), norm_w.reshape(1, gw))


def _fox_cum_kernel(sm_ref, bias_ref, o_ref):
    t = sm_ref.shape[0]
    row = lax.broadcasted_iota(jnp.int32, (LANES, LANES), 0)
    col = lax.broadcasted_iota(jnp.int32, (LANES, LANES), 1)
    tri = jnp.where(row >= col, 1.0, 0.0).astype(BF16)
    carry = jnp.zeros((1, LANES), F32)
    for blk in range(t // LANES):
        rows = slice(blk * LANES, (blk + 1) * LANES)
        ls = _log_sigmoid(sm_ref[rows, :] + bias_ref[...])
        h1, h2, h3 = _split3(ls)
        cb = _dot(tri, h1) + _dot(tri, h2) + _dot(tri, h3) + carry
        o_ref[rows, :] = cb
        carry = cb[LANES - 1:LANES, :]


def _fox_cum(small, bias_row, batch, t):
    return pl.pallas_call(
        _fox_cum_kernel,
        grid=(batch,),
        in_specs=[pl.BlockSpec((t, LANES), lambda b: (b, 0)), _const_spec((1, LANES))],
        out_specs=pl.BlockSpec((t, LANES), lambda b: (b, 0)),
        out_shape=jax.ShapeDtypeStruct((batch * t, LANES), F32),
        compiler_params=_params(1),
        name="fox_cum",
    )(small, bias_row)


def _fox_kernel(q_ref, k_ref, v_ref, c_ref, o_ref, ka_ref, vt_ref, s_ref, mb_ref):
    t = k_ref.shape[0]
    blk = FOX_BLOCK
    nkv = t // blk
    pair = pl.program_id(1)

    sel_r = lax.broadcasted_iota(jnp.int32, (LANES, LANES), 0)
    sel_c = lax.broadcasted_iota(jnp.int32, (LANES, LANES), 1)
    sels = []
    for term in range(CUM_TERMS):
        hit = (((sel_r == 2 * pair) & (sel_c == term))
               | ((sel_r == 2 * pair + 1) & (sel_c == CUM_TERMS + term)))
        sels.append(jnp.where(hit, 1.0, 0.0).astype(BF16))
    for kb in range(nkv):
        rows = slice(kb * blk, (kb + 1) * blk)
        vt_ref[kb] = v_ref[rows, :].astype(F32).T.astype(BF16)
        parts = _split3(c_ref[rows, :] * (-LOG2E))
        extra = _dot(parts[0], sels[0]) + _dot(parts[1], sels[1]) + _dot(parts[2], sels[2])
        ka_ref[kb, :, 0:LANES] = k_ref[rows, :]
        ka_ref[kb, :, LANES:2 * LANES] = extra.astype(BF16)

    lane = lax.broadcasted_iota(jnp.int32, (blk, LANES), 1)
    krow = lax.broadcasted_iota(jnp.int32, (blk, blk), 0)
    qcol = lax.broadcasted_iota(jnp.int32, (blk, blk), 1)
    causal = krow <= qcol

    def augmented_queries(qi):
        q = q_ref[qi * blk:(qi + 1) * blk, :]
        q_aug = []
        for j in range(2):
            q_h = jnp.where(_half_mask(lane, j), q, jnp.zeros_like(q))
            ones = jnp.where((lane >= CUM_TERMS * j) & (lane < CUM_TERMS * (j + 1)), 1.0, 0.0).astype(BF16)
            q_aug.append(jnp.concatenate([q_h, ones], axis=1).astype(F32).T.astype(BF16))
        return q_aug

    def scores(q_aug, kb, slot, masked):
        k_blk = ka_ref[kb]
        for j in range(2):
            s = _dot(k_blk, q_aug[j])
            if masked:
                s = jnp.where(causal, s, MASK_VALUE)
            s_ref[slot, j] = s
            mb_ref[slot, j] = jnp.max(s, axis=0, keepdims=True)

    def accumulate(kb, slot, state):
        new = []
        for j in range(2):
            m, l, acc = state[j]
            m_new = jnp.maximum(m, mb_ref[slot, j])
            alpha = jnp.exp2(m - m_new)
            p = jnp.exp2(s_ref[slot, j] - m_new)
            l = l * alpha + jnp.sum(p, axis=0, keepdims=True)
            pv = _dot(vt_ref[kb, j * HEAD_DIM:(j + 1) * HEAD_DIM, :], p.astype(BF16))
            new.append((m_new, l, acc * alpha + pv))
        return new

    tasks = [(qi, kb) for qi in range(nkv) for kb in range(qi + 1)]
    fresh = [(jnp.full((1, blk), MASK_VALUE, F32), jnp.zeros((1, blk), F32),
              jnp.zeros((HEAD_DIM, blk), F32))] * 2
    q_aug = augmented_queries(0)
    scores(q_aug, 0, 0, masked=True)
    state = fresh
    for n, (qi, kb) in enumerate(tasks):
        if n + 1 < len(tasks):
            qi_next, kb_next = tasks[n + 1]
            if qi_next != qi:
                q_aug = augmented_queries(qi_next)
            scores(q_aug, kb_next, (n + 1) % 2, masked=(kb_next == qi_next))
        state = accumulate(kb, n % 2, state)
        if kb == qi:
            outs = [acc * (1.0 / l) for _, l, acc in state]
            o_ref[qi * blk:(qi + 1) * blk, :] = jnp.concatenate(outs, axis=0).T.astype(BF16)
            state = fresh


def _fox(proj, cum, batch, t, col0):
    nq = t // FOX_BLOCK
    n_pairs = FOX_HEADS // 2
    qc, kc, vc = col0 // LANES, col0 // LANES + n_pairs, col0 // LANES + 2 * n_pairs
    return pl.pallas_call(
        _fox_kernel,
        grid=(batch, n_pairs),
        in_specs=[
            pl.BlockSpec((t, LANES), lambda b, p: (b, qc + p)),
            pl.BlockSpec((t, LANES), lambda b, p: (b, kc + p)),
            pl.BlockSpec((t, LANES), lambda b, p: (b, vc + p)),
            pl.BlockSpec((t, LANES), lambda b, p: (b, 0)),
        ],
        out_specs=pl.BlockSpec((t, LANES), lambda b, p: (b, p)),
        out_shape=jax.ShapeDtypeStruct((batch * t, FOX_HEADS * HEAD_DIM), BF16),
        scratch_shapes=[
            pltpu.VMEM((nq, FOX_BLOCK, 2 * LANES), BF16),
            pltpu.VMEM((nq, LANES, FOX_BLOCK), BF16),
            pltpu.VMEM((2, 2, FOX_BLOCK, FOX_BLOCK), F32),
            pltpu.VMEM((2, 2, 1, FOX_BLOCK), F32),
        ],
        compiler_params=_params(2),
        name="fox",
    )(proj, proj, proj, cum)


def _chunk_attn_kernel(q_ref, k_ref, v_ref, tab_ref, o_ref, kp_ref, vt_ref, s_ref):
    t, g = q_ref.shape
    qb = CA_QBLOCK
    n_pad = CA_PAD // qb
    n_win = CA_WINDOW // qb

    for kb in range(n_pad):
        kp_ref[kb] = jnp.zeros((qb, g), BF16)
        vt_ref[kb] = jnp.zeros((g, qb), BF16)
    for kb in range(t // qb):
        rows = slice(kb * qb, (kb + 1) * qb)
        kp_ref[n_pad + kb] = k_ref[rows, :]
        vt_ref[n_pad + kb] = v_ref[rows, :].astype(F32).T.astype(BF16)

    lane = lax.broadcasted_iota(jnp.int32, (qb, LANES), 1)
    n_heads = g // HEAD_DIM
    lax.fori_loop(0, t // qb, functools.partial(
        _chunk_attn_block, q_ref=q_ref, tab_ref=tab_ref, o_ref=o_ref, kp_ref=kp_ref, vt_ref=vt_ref,
        s_ref=s_ref, lane=lane, n_heads=n_heads, n_pad=n_pad, n_win=n_win), 0)


def _chunk_attn_block(i, carry, *, q_ref, tab_ref, o_ref, kp_ref, vt_ref, s_ref, lane, n_heads,
                      n_pad, n_win):
    qb = CA_QBLOCK
    q_rows = pl.ds(pl.multiple_of(i * qb, qb), qb)

    def scores(h):
        pair, j = divmod(h, 2)
        cols = slice(pair * LANES, (pair + 1) * LANES)
        q_pair = q_ref[q_rows, cols]
        q_h = jnp.where(_half_mask(lane, j), q_pair, jnp.zeros_like(q_pair))
        q_ht = q_h.astype(F32).T.astype(BF16)
        m = None
        for w in range(n_win):
            tab_blk = jnp.where(i + w >= n_pad, w, n_win)
            s = _dot(kp_ref[i + w, :, cols], q_ht) + tab_ref[h, tab_blk]
            s_ref[h % 2, w] = s
            m_w = s.max(axis=0, keepdims=True)
            m = m_w if m is None else jnp.maximum(m, m_w)
        return m

    def weighted_values(h, m):
        l = jnp.zeros((1, qb), F32)
        acc = jnp.zeros((HEAD_DIM, qb), F32)
        for w in range(n_win):
            p = jnp.exp2(s_ref[h % 2, w] - m)
            l = l + jnp.sum(p, axis=0, keepdims=True)
            acc = acc + _dot(vt_ref[i + w, h * HEAD_DIM:(h + 1) * HEAD_DIM, :], p.astype(BF16))
        return acc * (1.0 / l)

    outs = []
    pending = scores(0)
    for h in range(n_heads):
        upcoming = scores(h + 1) if h + 1 < n_heads else None
        outs.append(weighted_values(h, pending))
        pending = upcoming
    o_ref[q_rows, :] = jnp.concatenate(outs, axis=0).T.astype(BF16)
    return carry


def _chunk_attn(proj, table, batch, t):
    g = table.shape[0] * HEAD_DIM
    n_blocks = (CA_PAD + t) // CA_QBLOCK
    return pl.pallas_call(
        _chunk_attn_kernel,
        grid=(batch,),
        in_specs=[
            pl.BlockSpec((t, g), lambda b: (b, 0)),
            pl.BlockSpec((t, g), lambda b: (b, 1)),
            pl.BlockSpec((t, g), lambda b: (b, 2)),
            _const_spec(table.shape),
        ],
        out_specs=pl.BlockSpec((t, g), lambda b: (b, 0)),
        out_shape=jax.ShapeDtypeStruct((batch * t, g), BF16),
        scratch_shapes=[
            pltpu.VMEM((n_blocks, CA_QBLOCK, g), BF16),
            pltpu.VMEM((n_blocks, g, CA_QBLOCK), BF16),
            pltpu.VMEM((2, CA_WINDOW // CA_QBLOCK, CA_QBLOCK, CA_QBLOCK), F32),
        ],
        compiler_params=_params(1),
        name="chunk_attn",
    )(proj, proj, proj, table)


def _ca_table_kernel(line_ref, o_ref):
    n_win = CA_WINDOW // CA_QBLOCK
    rows = 8
    x = jnp.broadcast_to(line_ref[0] * LOG2E, (rows, CA_LINE))
    qc = lax.broadcasted_iota(jnp.int32, (rows, CA_QBLOCK), 1) // CHUNK
    for grp in range(CA_WINDOW // rows):
        r0 = grp * rows
        y = pltpu.roll(x, (CA_LINE - CA_WINDOW + 1 + r0) % CA_LINE, axis=1, stride=1, stride_axis=0)
        kc = r0 // CHUNK
        allowed = (qc <= kc) & (qc >= kc - CA_LEFT_CHUNKS)
        w, r = divmod(r0, CA_QBLOCK)
        o_ref[0, w, r:r + rows, :] = jnp.where(allowed, y[:, 0:CA_QBLOCK], MASK_VALUE)
    o_ref[0, n_win] = jnp.full((CA_QBLOCK, CA_QBLOCK), MASK_VALUE, F32)


def _chunk_attn_table(rel_bias):
    heads = rel_bias.shape[0]
    n_win = CA_WINDOW // CA_QBLOCK
    left = CA_WINDOW - 1 - CA_PAD - REL_CLIP
    line = jnp.pad(rel_bias.astype(F32), ((0, 0), (left, CA_LINE - left - rel_bias.shape[1])), mode="edge")
    return pl.pallas_call(
        _ca_table_kernel,
        grid=(heads,),
        in_specs=[pl.BlockSpec((1, 1, CA_LINE), lambda h: (h, 0, 0))],
        out_specs=pl.BlockSpec((1, n_win + 1, CA_QBLOCK, CA_QBLOCK), lambda h: (h, 0, 0, 0)),
        out_shape=jax.ShapeDtypeStruct((heads, n_win + 1, CA_QBLOCK, CA_QBLOCK), F32),
        compiler_params=_params(1),
        name="ca_table",
    )(line.reshape(heads, 1, CA_LINE))


GELU_C0 = 0.7978845608028654
GELU_C1 = GELU_C0 * 0.044715


def _gelu(x):
    inner = x * (GELU_C0 + GELU_C1 * (x * x))
    return (0.5 * x) * (1.0 + jnp.tanh(inner))


def _lru_kernel(g_ref, x_ref, cw_ref, cb_ref, wa_ref, ba_ref, wx_ref, bx_ref, lam_ref, o_ref,
                xin_ref, xf_ref, a_ref, b_ref, hout_ref, h_ref):
    nb, frames, w = x_ref.shape
    n_slab = w // LANES
    sub = LRU_ROWS // nb
    i = pl.program_id(0)

    @pl.when(i == 0)
    def _():
        xf_ref[0:CONV_HIST] = jnp.zeros((CONV_HIST, nb, w), F32)
        h_ref[...] = jnp.zeros((nb, w), F32)

    for b in range(nb):
        xb = x_ref[b].astype(F32)
        for s in range(n_slab):
            xin_ref[s, b * LRU_PITCH:b * LRU_PITCH + frames, :] = xb[:, s * LANES:(s + 1) * LANES]

    def gather(t, carry):
        for s in range(n_slab):
            xf_ref[CONV_HIST + t, :, s * LANES:(s + 1) * LANES] = (
                xin_ref[s, pl.ds(t, nb, stride=LRU_PITCH), :])
        return carry

    lax.fori_loop(0, frames, gather, 0, unroll=SCAN_UNROLL)

    log2_base = (LRU_C * LOG2E) * _log_sigmoid(lam_ref[...])
    half = w // 2
    for blk in range(frames // sub):
        t0 = blk * sub
        xc = cb_ref[...]
        for tap in range(CONV_WIDTH):
            lo = CONV_HIST + t0 - tap
            xc = xc + (cw_ref[CONV_WIDTH - 1 - tap:CONV_WIDTH - tap, :]
                       * xf_ref[lo:lo + sub].reshape(LRU_ROWS, w))
        xcb = xc.astype(BF16)
        gr, gi = [], []
        for hb in range(2):
            cols = slice(hb * half, (hb + 1) * half)
            gr.append(_dot(xcb[:, cols], wa_ref[hb]))
            gi.append(_dot(xcb[:, cols], wx_ref[hb]))
        r = _sigmoid(jnp.concatenate(gr, axis=1) + ba_ref[...])
        gate_i = _sigmoid(jnp.concatenate(gi, axis=1) + bx_ref[...])
        a = jnp.exp2(r * log2_base)
        a_ref[t0:t0 + sub] = a.reshape(sub, nb, w)
        y = 1.0 - a * a
        root = jnp.where(y > 0.0, y * lax.rsqrt(y), 0.0)
        b_ref[t0:t0 + sub] = (root * (gate_i * xc)).reshape(sub, nb, w)

    xf_ref[0:CONV_HIST] = xf_ref[frames:frames + CONV_HIST]

    def scan(t, h):
        h = a_ref[t] * h + b_ref[t]
        for s in range(n_slab):
            hout_ref[s, pl.ds(t, nb, stride=LRU_PITCH), :] = h[:, s * LANES:(s + 1) * LANES]
        return h

    h_ref[...] = lax.fori_loop(0, frames, scan, h_ref[...], unroll=SCAN_UNROLL)

    for b in range(nb):
        rows = slice(b * LRU_PITCH, b * LRU_PITCH + frames)
        hv = jnp.concatenate([hout_ref[s, rows, :] for s in range(n_slab)], axis=1)
        o_ref[b] = (hv * _gelu(g_ref[b].astype(F32))).astype(BF16)


def _lru(proj3, conv_w, conv_b, wa_bd, b_a, wx_bd, b_x, lam, col0):
    nb, t, _ = proj3.shape
    w = conv_w.shape[1]
    gc = col0 // w
    n_slab = w // LANES
    return pl.pallas_call(
        _lru_kernel,
        grid=(t // LRU_FRAMES,),
        in_specs=[
            pl.BlockSpec((nb, LRU_FRAMES, w), lambda i: (0, i, gc)),
            pl.BlockSpec((nb, LRU_FRAMES, w), lambda i: (0, i, gc + 1)),
            _const_spec((CONV_WIDTH, w)),
            _const_spec((1, w)),
            _const_spec(wa_bd.shape),
            _const_spec((1, w)),
            _const_spec(wx_bd.shape),
            _const_spec((1, w)),
            _const_spec((1, w)),
        ],
        out_specs=pl.BlockSpec((nb, LRU_FRAMES, w), lambda i: (0, i, 0)),
        out_shape=jax.ShapeDtypeStruct((nb, t, w), BF16),
        scratch_shapes=[
            pltpu.VMEM((n_slab, nb * LRU_PITCH, LANES), F32),
            pltpu.VMEM((CONV_HIST + LRU_FRAMES, nb, w), F32),
            pltpu.VMEM((LRU_FRAMES, nb, w), F32),
            pltpu.VMEM((LRU_FRAMES, nb, w), F32),
            pltpu.VMEM((n_slab, nb * LRU_PITCH, LANES), F32),
            pltpu.VMEM((nb, w), F32),
        ],
        compiler_params=_params(1),
        name="rglru",
    )(proj3, proj3, conv_w, conv_b.reshape(1, w), wa_bd, b_a.reshape(1, w), wx_bd,
      b_x.reshape(1, w), lam.reshape(1, w))


def _block_diag_halves(wblk):
    nb, d, _ = wblk.shape
    per = nb // 2
    eye = jnp.eye(per, dtype=wblk.dtype)
    halves = [jnp.einsum("nde,nm->ndme", wblk[h * per:(h + 1) * per], eye).reshape(per * d, per * d)
              for h in range(2)]
    return jnp.stack(halves).astype(BF16)


def kernel(x, norm_w, w_in_even, gla_w_a_up, gla_b_a, gla_norm_w, fox_b_f, w_out_even,
           w_in_odd, rel_bias, conv_w, conv_b, lru_w_a, lru_b_a, lru_w_x, lru_b_x,
           lru_lambda, w_out_odd, w_mlp_up, w_mlp_down):
    batch, t, d = x.shape
    x2d = x.reshape(batch * t, d)
    group = d // 2
    kw = GLA_HEADS * GLA_DK

    w_in = w_in_even[0]
    o_ga = 2 * kw + 2 * group
    o_fq = o_ga + GLA_RANK
    o_ff = o_fq + 3 * group
    q_scale = LOG2E * HEAD_DIM ** -0.5
    w_main = jnp.concatenate(
        [w_in[:, :o_ga], w_in[:, o_fq:o_fq + group] * q_scale, w_in[:, o_fq + group:o_ff]],
        axis=1).astype(BF16)
    n_small = FOX_HEADS + GLA_RANK
    w_small = jnp.concatenate(
        [w_in[:, o_ff:], w_in[:, o_ga:o_fq], jnp.zeros((d, LANES - n_small), F32)], axis=1).astype(BF16)
    wa_pad = jnp.zeros((LANES, kw), F32).at[FOX_HEADS:n_small].set(gla_w_a_up[0]).astype(BF16)
    fox_bias = jnp.zeros((1, LANES), F32).at[0, :FOX_HEADS].set(fox_b_f[0])

    m = batch * t
    proj, small = _norm_proj(x2d, norm_w[0, 0], [(w_main, BF16), (w_small, F32)])
    out_a = _gla(proj, small, wa_pad, gla_b_a[0], gla_norm_w[0], batch, t)
    cum = _fox_cum(small, fox_bias, batch, t)
    out_b = _fox(proj, cum, batch, t, o_ga)
    x2d = _mix_mlp(out_a, out_b, x2d, w_out_even[0].astype(BF16), norm_w[0, 1:4],
                   w_mlp_up[0].astype(BF16), w_mlp_down[0].astype(BF16))

    w_in = w_in_odd[0]
    w_main = jnp.concatenate([w_in[:, :group] * q_scale, w_in[:, group:]], axis=1).astype(BF16)
    (proj,) = _norm_proj(x2d, norm_w[1, 0], [(w_main, BF16)])
    out_c = _chunk_attn(proj, _chunk_attn_table(rel_bias[0]), batch, t)
    out_d = _lru(proj.reshape(batch, t, w_main.shape[1]), conv_w[0], conv_b[0],
                 _block_diag_halves(lru_w_a[0]), lru_b_a[0],
                 _block_diag_halves(lru_w_x[0]), lru_b_x[0], lru_lambda[0], 3 * group)
    x2d = _mix_mlp(out_c, out_d.reshape(m, group), x2d,
                   w_out_odd[0].astype(BF16), norm_w[1, 1:4],
                   w_mlp_up[1].astype(BF16), w_mlp_down[1].astype(BF16))
    return x2d.reshape(batch, t, d)
```

```python
import functools

import jax
import jax.numpy as jnp
from jax import lax
from jax.experimental import pallas as pl
from jax.experimental.pallas import tpu as pltpu

F32 = jnp.float32
BF16 = jnp.bfloat16

NORM_EPS = 1e-6
CHUNK = 64
GLA_HEADS = 4
GLA_DK = 64
GLA_DV = 128
GLA_RANK = 16
GLA_GATE_TAU = 16.0
FOX_HEADS = 8
HEAD_DIM = 64
CA_LEFT_CHUNKS = 8
REL_CLIP = 128
CONV_WIDTH = 4
LRU_BLOCKS = 8
LRU_C = 8.0

LANES = 128
MXU_DIM = 256
MASK_VALUE = -1e30
VMEM_LIMIT_BYTES = 56 * 1024 * 1024

ROW_TILE = 512
PROJ_ROW_TILE = 1024
STAGE_ROWS = 512
STAGE_COLS = 1024
FOX_BLOCK = 512
CA_QBLOCK = 4 * CHUNK
CA_WINDOW = CA_QBLOCK + CA_LEFT_CHUNKS * CHUNK
CA_PAD = CA_LEFT_CHUNKS * CHUNK
CA_LINE = 1024
LOG2E = 1.4426950408889634
CUM_TERMS = 3
GLA_GROUP = 4
LRU_ROWS = 256
LRU_FRAMES = 256
LRU_PITCH = LRU_FRAMES + 8
CONV_HIST = 8
SCAN_UNROLL = 8


def _params(n_axes):
    return pltpu.CompilerParams(
        dimension_semantics=("arbitrary",) * n_axes,
        vmem_limit_bytes=VMEM_LIMIT_BYTES,
    )


def _const_spec(shape):
    nd = len(shape)
    return pl.BlockSpec(shape, lambda *_: (0,) * nd, pipeline_mode=pl.Buffered(1))


def _rmsnorm(x, w):
    y = x * lax.rsqrt(jnp.mean(x * x, axis=-1, keepdims=True) + NORM_EPS)
    return y * w


def _log_sigmoid(z):
    return jnp.minimum(z, 0.0) - jnp.log1p(jnp.exp(-jnp.abs(z)))


def _sigmoid(z):
    return 0.5 * jnp.tanh(0.5 * z) + 0.5


def _dot(a, b):
    return jnp.dot(a, b, preferred_element_type=F32)


def _dot_nt(a, b):
    return lax.dot_general(a, b, (((1,), (1,)), ((), ())), preferred_element_type=F32)


def _dot_tn(a, b):
    return lax.dot_general(a, b, (((0,), (0,)), ((), ())), preferred_element_type=F32)


def _half_mask(lane, j):
    return lane < HEAD_DIM if j == 0 else lane >= HEAD_DIM


def _split3(x):
    h1 = x.astype(BF16)
    r1 = x - h1.astype(F32)
    h2 = r1.astype(BF16)
    h3 = (r1 - h2.astype(F32)).astype(BF16)
    return h1, h2, h3


def _row_major(i):
    return (i, 0)


def _norm_proj_kernel(x_ref, nw_ref, *refs, n_chunk):
    n_out = len(refs) // 2
    h = _rmsnorm(x_ref[...], nw_ref[...]).astype(BF16)
    for w_ref, o_ref in zip(refs[:n_out], refs[n_out:]):
        n_total = o_ref.shape[1]
        for n0 in range(0, n_total, n_chunk):
            n1 = min(n0 + n_chunk, n_total)
            o_ref[:, n0:n1] = _dot(h, w_ref[:, n0:n1]).astype(o_ref.dtype)


def _norm_proj(x2d, nw, outputs):
    m, d = x2d.shape
    in_specs = [pl.BlockSpec((PROJ_ROW_TILE, d), _row_major), _const_spec((1, d))]
    in_specs += [_const_spec(w.shape) for w, _ in outputs]
    return pl.pallas_call(
        functools.partial(_norm_proj_kernel, n_chunk=2 * MXU_DIM),
        grid=(m // PROJ_ROW_TILE,),
        in_specs=in_specs,
        out_specs=[pl.BlockSpec((PROJ_ROW_TILE, w.shape[1]), _row_major) for w, _ in outputs],
        out_shape=[jax.ShapeDtypeStruct((m, w.shape[1]), dt) for w, dt in outputs],
        compiler_params=_params(1),
        name="norm_proj",
    )(x2d, nw.reshape(1, d), *[w for w, _ in outputs])


def _stage_bf16_weights(pairs, stage_ref, sem_ref):
    pieces = []
    for src, dst in pairs:
        rows, cols = dst.shape
        for r0 in range(0, rows, STAGE_ROWS):
            for c0 in range(0, cols, STAGE_COLS):
                pieces.append((src, dst, r0, c0))

    def copy(k):
        src, _, r0, c0 = pieces[k]
        return pltpu.make_async_copy(
            src.at[pl.ds(r0, STAGE_ROWS), pl.ds(c0, STAGE_COLS)], stage_ref.at[k % 2], sem_ref.at[k % 2])

    copy(0).start()
    for k, (_, dst, r0, c0) in enumerate(pieces):
        if k + 1 < len(pieces):
            copy(k + 1).start()
        copy(k).wait()
        dst[r0:r0 + STAGE_ROWS, c0:c0 + STAGE_COLS] = stage_ref[k % 2].astype(BF16)


def _mix_mlp_kernel(a_ref, b_ref, x_ref, nw_ref, wo_hbm, wu_hbm, wd_hbm, o_ref,
                    wo_ref, wu_ref, wd_ref, u_ref, y_ref, stage_ref, sem_ref, *, chunk):
    @pl.when(pl.program_id(0) == 0)
    def _():
        _stage_bf16_weights([(wo_hbm, wo_ref), (wu_hbm, wu_ref), (wd_hbm, wd_ref)], stage_ref, sem_ref)

    d_ff = wu_ref.shape[1]
    d = wd_ref.shape[1]
    mix = jnp.concatenate([a_ref[...], b_ref[...]], axis=1)
    for n0 in range(0, d, chunk):
        cols = slice(n0, n0 + chunk)
        y_ref[:, cols] = _dot(mix, wo_ref[:, cols])
    x1 = x_ref[...] + _rmsnorm(y_ref[...], nw_ref[0:1, :])
    h = _rmsnorm(x1, nw_ref[1:2, :]).astype(BF16)
    for f0 in range(0, d_ff, chunk):
        u = jnp.maximum(_dot(h, wu_ref[:, f0:f0 + chunk]), 0.0)
        u_ref[:, f0:f0 + chunk] = (u * u).astype(BF16)
    for n0 in range(0, d, chunk):
        y_ref[:, n0:n0 + chunk] = _dot(u_ref[...], wd_ref[:, n0:n0 + chunk])
    o_ref[...] = x1 + _rmsnorm(y_ref[...], nw_ref[2:3, :])


def _mix_mlp(mix_a, mix_b, x2d, w_out, nw3, w_up, w_down):
    m, d = x2d.shape
    g = mix_a.shape[1]
    d_ff = w_up.shape[1]
    return pl.pallas_call(
        functools.partial(_mix_mlp_kernel, chunk=2 * MXU_DIM),
        grid=(m // ROW_TILE,),
        in_specs=[
            pl.BlockSpec((ROW_TILE, g), _row_major),
            pl.BlockSpec((ROW_TILE, g), _row_major),
            pl.BlockSpec((ROW_TILE, d), _row_major),
            _const_spec((3, d)),
            pl.BlockSpec(memory_space=pl.ANY),
            pl.BlockSpec(memory_space=pl.ANY),
            pl.BlockSpec(memory_space=pl.ANY),
        ],
        out_specs=pl.BlockSpec((ROW_TILE, d), _row_major),
        out_shape=jax.ShapeDtypeStruct((m, d), F32),
        scratch_shapes=[
            pltpu.VMEM((2 * g, d), BF16),
            pltpu.VMEM((d, d_ff), BF16),
            pltpu.VMEM((d_ff, d), BF16),
            pltpu.VMEM((ROW_TILE, d_ff), BF16),
            pltpu.VMEM((ROW_TILE, d), F32),
            pltpu.VMEM((2, STAGE_ROWS, STAGE_COLS), F32),
            pltpu.SemaphoreType.DMA((2,)),
        ],
        compiler_params=_params(1),
        name="mix_mlp",
    )(mix_a, mix_b, x2d, nw3, w_out, w_up, w_down)


def _gla_kernel(q_ref, k_ref, v_ref, r_ref, sm_ref, wa_ref, ba_ref, nw_ref, o_ref,
                la_ref, st_ref, dec_ref):
    t = q_ref.shape[0]
    nc = t // CHUNK
    n_pairs = GLA_HEADS // 2

    z = _dot(sm_ref[...].astype(BF16), wa_ref[...]) + ba_ref[...]
    la_ref[...] = _log_sigmoid(z) * (1.0 / GLA_GATE_TAU)

    grp = GLA_GROUP * CHUNK
    row = lax.broadcasted_iota(jnp.int32, (grp, grp), 0)
    col = lax.broadcasted_iota(jnp.int32, (grp, grp), 1)
    tri = jnp.where((row >= col) & (row // CHUNK == col // CHUNK), 1.0, 0.0).astype(BF16)
    erow = lax.broadcasted_iota(jnp.int32, (grp, GLA_GROUP * LANES), 0)
    ecol = lax.broadcasted_iota(jnp.int32, (grp, GLA_GROUP * LANES), 1)
    chunk_ones = jnp.where(erow // CHUNK == ecol // LANES, 1.0, 0.0).astype(BF16)

    def increments(g, carry):
        r0 = pl.multiple_of(g * grp, grp)
        la = la_ref[pl.ds(r0, grp), :]
        hi = la.astype(BF16)
        lo = (la - hi.astype(F32)).astype(BF16)
        cum = _dot(tri, hi) + _dot(tri, lo)
        dec = jnp.exp(_dot_tn(hi, chunk_ones) + _dot_tn(lo, chunk_ones))
        k_g = k_ref[pl.ds(r0, grp), :].astype(F32)
        v_g = v_ref[pl.ds(r0, grp), :]
        k_dec = []
        for c in range(GLA_GROUP):
            rows = slice(c * CHUNK, (c + 1) * CHUNK)
            total = cum[(c + 1) * CHUNK - 1:(c + 1) * CHUNK, :]
            k_dec.append((k_g[rows] * jnp.exp(total - cum[rows])).astype(BF16))
        for c in range(GLA_GROUP):
            rows = slice(c * CHUNK, (c + 1) * CHUNK)
            dec_ref[g * GLA_GROUP + c] = dec[:, c * LANES:(c + 1) * LANES]
            for h in range(GLA_HEADS):
                p, j = divmod(h, 2)
                full = _dot_tn(k_dec[c][:, p * LANES:(p + 1) * LANES], v_g[rows, h * GLA_DV:(h + 1) * GLA_DV])
                st_ref[g * GLA_GROUP + c, p, j * GLA_DK:(j + 1) * GLA_DK, :] = full[j * GLA_DK:(j + 1) * GLA_DK, :]
        return carry

    lax.fori_loop(0, nc // GLA_GROUP, increments, 0)

    def scan(c, carry):
        for p in range(n_pairs):
            st_ref[c, p] = st_ref[c, p] + dec_ref[c, p * LANES:(p + 1) * LANES, :] * st_ref[c - 1, p]
        return carry

    lax.fori_loop(1, nc, scan, 0)

    lane = lax.broadcasted_iota(jnp.int32, (CHUNK, LANES), 1)
    scale = GLA_DK ** -0.5

    def outputs(g, carry):
        r0 = pl.multiple_of(g * grp, grp)
        q_g = q_ref[pl.ds(r0, grp), :] * jnp.asarray(scale, BF16)
        raw = []
        for c in range(GLA_GROUP):
            rows = slice(c * CHUNK, (c + 1) * CHUNK)
            for h in range(GLA_HEADS):
                p, j = divmod(h, 2)
                q_pair = q_g[rows, p * LANES:(p + 1) * LANES]
                q_h = jnp.where(_half_mask(lane, j), q_pair, jnp.zeros_like(q_pair))
                raw.append(_dot(q_h, st_ref[g * GLA_GROUP + c, p].astype(BF16)))
        for c in range(GLA_GROUP):
            for h in range(GLA_HEADS):
                o = raw[c * GLA_HEADS + h]
                o = o * lax.rsqrt(jnp.mean(o * o, axis=-1, keepdims=True) + NORM_EPS)
                o = o * nw_ref[:, h * GLA_DV:(h + 1) * GLA_DV]
                rows = pl.ds(r0 + c * CHUNK, CHUNK)
                r_h = r_ref[rows, h * GLA_DV:(h + 1) * GLA_DV].astype(F32)
                o_ref[rows, h * GLA_DV:(h + 1) * GLA_DV] = (o * (r_h * _sigmoid(r_h))).astype(BF16)
        return carry

    lax.fori_loop(0, nc // GLA_GROUP, outputs, 0)


def _gla(proj, small, wa_pad, b_a, norm_w, batch, t):
    kw = GLA_HEADS * GLA_DK
    gw = GLA_HEADS * GLA_DV
    nc = t // CHUNK
    return pl.pallas_call(
        _gla_kernel,
        grid=(batch,),
        in_specs=[
            pl.BlockSpec((t, kw), lambda b: (b, 0)),
            pl.BlockSpec((t, kw), lambda b: (b, 1)),
            pl.BlockSpec((t, gw), lambda b: (b, 1)),
            pl.BlockSpec((t, gw), lambda b: (b, 2)),
            pl.BlockSpec((t, LANES), lambda b: (b, 0)),
            _const_spec((LANES, kw)),
            _const_spec((1, kw)),
            _const_spec((1, gw)),
        ],
        out_specs=pl.BlockSpec((t, gw), lambda b: (b, 0)),
        out_shape=jax.ShapeDtypeStruct((batch * t, gw), BF16),
        scratch_shapes=[
            pltpu.VMEM((t, kw), F32),
            pltpu.VMEM((nc, GLA_HEADS // 2, LANES, GLA_DV), F32),
            pltpu.VMEM((nc, kw, LANES), F32),
        ],
        compiler_params=_params(1),
        name="gla",
    )(proj, proj, proj, proj, small, wa_pad, b_a.reshape(1, kw), norm_w.reshape(1, gw))


def _fox_cum_kernel(sm_ref, bias_ref, o_ref):
    t = sm_ref.shape[0]
    row = lax.broadcasted_iota(jnp.int32, (LANES, LANES), 0)
    col = lax.broadcasted_iota(jnp.int32, (LANES, LANES), 1)
    tri = jnp.where(row >= col, 1.0, 0.0).astype(BF16)
    carry = jnp.zeros((1, LANES), F32)
    for blk in range(t // LANES):
        rows = slice(blk * LANES, (blk + 1) * LANES)
        ls = _log_sigmoid(sm_ref[rows, :] + bias_ref[...])
        h1, h2, h3 = _split3(ls)
        cb = _dot(tri, h1) + _dot(tri, h2) + _dot(tri, h3) + carry
        o_ref[rows, :] = cb
        carry = cb[LANES - 1:LANES, :]


def _fox_cum(small, bias_row, batch, t):
    return pl.pallas_call(
        _fox_cum_kernel,
        grid=(batch,),
        in_specs=[pl.BlockSpec((t, LANES), lambda b: (b, 0)), _const_spec((1, LANES))],
        out_specs=pl.BlockSpec((t, LANES), lambda b: (b, 0)),
        out_shape=jax.ShapeDtypeStruct((batch * t, LANES), F32),
        compiler_params=_params(1),
        name="fox_cum",
    )(small, bias_row)


def _fox_kernel(q_ref, k_ref, v_ref, c_ref, o_ref, ka_ref, vt_ref, s_ref, mb_ref):
    t = k_ref.shape[0]
    blk = FOX_BLOCK
    nkv = t // blk
    pair = pl.program_id(1)

    sel_r = lax.broadcasted_iota(jnp.int32, (LANES, LANES), 0)
    sel_c = lax.broadcasted_iota(jnp.int32, (LANES, LANES), 1)
    sels = []
    for term in range(CUM_TERMS):
        hit = (((sel_r == 2 * pair) & (sel_c == term))
               | ((sel_r == 2 * pair + 1) & (sel_c == CUM_TERMS + term)))
        sels.append(jnp.where(hit, 1.0, 0.0).astype(BF16))
    for kb in range(nkv):
        rows = slice(kb * blk, (kb + 1) * blk)
        vt_ref[kb] = v_ref[rows, :].astype(F32).T.astype(BF16)
        parts = _split3(c_ref[rows, :] * (-LOG2E))
        extra = _dot(parts[0], sels[0]) + _dot(parts[1], sels[1]) + _dot(parts[2], sels[2])
        ka_ref[kb, :, 0:LANES] = k_ref[rows, :]
        ka_ref[kb, :, LANES:2 * LANES] = extra.astype(BF16)

    lane = lax.broadcasted_iota(jnp.int32, (blk, LANES), 1)
    krow = lax.broadcasted_iota(jnp.int32, (blk, blk), 0)
    qcol = lax.broadcasted_iota(jnp.int32, (blk, blk), 1)
    causal = krow <= qcol

    def augmented_queries(qi):
        q = q_ref[qi * blk:(qi + 1) * blk, :]
        q_aug = []
        for j in range(2):
            q_h = jnp.where(_half_mask(lane, j), q, jnp.zeros_like(q))
            ones = jnp.where((lane >= CUM_TERMS * j) & (lane < CUM_TERMS * (j + 1)), 1.0, 0.0).astype(BF16)
            q_aug.append(jnp.concatenate([q_h, ones], axis=1).astype(F32).T.astype(BF16))
        return q_aug

    def scores(q_aug, kb, slot, masked):
        k_blk = ka_ref[kb]
        for j in range(2):
            s = _dot(k_blk, q_aug[j])
            if masked:
                s = jnp.where(causal, s, MASK_VALUE)
            s_ref[slot, j] = s
            mb_ref[slot, j] = jnp.max(s, axis=0, keepdims=True)

    def accumulate(kb, slot, state):
        new = []
        for j in range(2):
            m, l, acc = state[j]
            m_new = jnp.maximum(m, mb_ref[slot, j])
            alpha = jnp.exp2(m - m_new)
            p = jnp.exp2(s_ref[slot, j] - m_new)
            l = l * alpha + jnp.sum(p, axis=0, keepdims=True)
            pv = _dot(vt_ref[kb, j * HEAD_DIM:(j + 1) * HEAD_DIM, :], p.astype(BF16))
            new.append((m_new, l, acc * alpha + pv))
        return new

    tasks = [(qi, kb) for qi in range(nkv) for kb in range(qi + 1)]
    fresh = [(jnp.full((1, blk), MASK_VALUE, F32), jnp.zeros((1, blk), F32),
              jnp.zeros((HEAD_DIM, blk), F32))] * 2
    q_aug = augmented_queries(0)
    scores(q_aug, 0, 0, masked=True)
    state = fresh
    for n, (qi, kb) in enumerate(tasks):
        if n + 1 < len(tasks):
            qi_next, kb_next = tasks[n + 1]
            if qi_next != qi:
                q_aug = augmented_queries(qi_next)
            scores(q_aug, kb_next, (n + 1) % 2, masked=(kb_next == qi_next))
        state = accumulate(kb, n % 2, state)
        if kb == qi:
            outs = [acc * (1.0 / l) for _, l, acc in state]
            o_ref[qi * blk:(qi + 1) * blk, :] = jnp.concatenate(outs, axis=0).T.astype(BF16)
            state = fresh


def _fox(proj, cum, batch, t, col0):
    nq = t // FOX_BLOCK
    n_pairs = FOX_HEADS // 2
    qc, kc, vc = col0 // LANES, col0 // LANES + n_pairs, col0 // LANES + 2 * n_pairs
    return pl.pallas_call(
        _fox_kernel,
        grid=(batch, n_pairs),
        in_specs=[
            pl.BlockSpec((t, LANES), lambda b, p: (b, qc + p)),
            pl.BlockSpec((t, LANES), lambda b, p: (b, kc + p)),
            pl.BlockSpec((t, LANES), lambda b, p: (b, vc + p)),
            pl.BlockSpec((t, LANES), lambda b, p: (b, 0)),
        ],
        out_specs=pl.BlockSpec((t, LANES), lambda b, p: (b, p)),
        out_shape=jax.ShapeDtypeStruct((batch * t, FOX_HEADS * HEAD_DIM), BF16),
        scratch_shapes=[
            pltpu.VMEM((nq, FOX_BLOCK, 2 * LANES), BF16),
            pltpu.VMEM((nq, LANES, FOX_BLOCK), BF16),
            pltpu.VMEM((2, 2, FOX_BLOCK, FOX_BLOCK), F32),
            pltpu.VMEM((2, 2, 1, FOX_BLOCK), F32),
        ],
        compiler_params=_params(2),
        name="fox",
    )(proj, proj, proj, cum)


def _chunk_attn_kernel(q_ref, k_ref, v_ref, tab_ref, o_ref, kp_ref, vt_ref, s_ref):
    t, g = q_ref.shape
    qb = CA_QBLOCK
    n_pad = CA_PAD // qb
    n_win = CA_WINDOW // qb

    for kb in range(n_pad):
        kp_ref[kb] = jnp.zeros((qb, g), BF16)
        vt_ref[kb] = jnp.zeros((g, qb), BF16)
    for kb in range(t // qb):
        rows = slice(kb * qb, (kb + 1) * qb)
        kp_ref[n_pad + kb] = k_ref[rows, :]
        vt_ref[n_pad + kb] = v_ref[rows, :].astype(F32).T.astype(BF16)

    lane = lax.broadcasted_iota(jnp.int32, (qb, LANES), 1)
    n_heads = g // HEAD_DIM
    lax.fori_loop(0, t // qb, functools.partial(
        _chunk_attn_block, q_ref=q_ref, tab_ref=tab_ref, o_ref=o_ref, kp_ref=kp_ref, vt_ref=vt_ref,
        s_ref=s_ref, lane=lane, n_heads=n_heads, n_pad=n_pad, n_win=n_win), 0)


def _chunk_attn_block(i, carry, *, q_ref, tab_ref, o_ref, kp_ref, vt_ref, s_ref, lane, n_heads,
                      n_pad, n_win):
    qb = CA_QBLOCK
    q_rows = pl.ds(pl.multiple_of(i * qb, qb), qb)

    def scores(h):
        pair, j = divmod(h, 2)
        cols = slice(pair * LANES, (pair + 1) * LANES)
        q_pair = q_ref[q_rows, cols]
        q_h = jnp.where(_half_mask(lane, j), q_pair, jnp.zeros_like(q_pair))
        q_ht = q_h.astype(F32).T.astype(BF16)
        m = None
        for w in range(n_win):
            tab_blk = jnp.where(i + w >= n_pad, w, n_win)
            s = _dot(kp_ref[i + w, :, cols], q_ht) + tab_ref[h, tab_blk]
            s_ref[h % 2, w] = s
            m_w = s.max(axis=0, keepdims=True)
            m = m_w if m is None else jnp.maximum(m, m_w)
        return m

    def weighted_values(h, m):
        l = jnp.zeros((1, qb), F32)
        acc = jnp.zeros((HEAD_DIM, qb), F32)
        for w in range(n_win):
            p = jnp.exp2(s_ref[h % 2, w] - m)
            l = l + jnp.sum(p, axis=0, keepdims=True)
            acc = acc + _dot(vt_ref[i + w, h * HEAD_DIM:(h + 1) * HEAD_DIM, :], p.astype(BF16))
        return acc * (1.0 / l)

    outs = []
    pending = scores(0)
    for h in range(n_heads):
        upcoming = scores(h + 1) if h + 1 < n_heads else None
        outs.append(weighted_values(h, pending))
        pending = upcoming
    o_ref[q_rows, :] = jnp.concatenate(outs, axis=0).T.astype(BF16)
    return carry


def _chunk_attn(proj, table, batch, t):
    g = table.shape[0] * HEAD_DIM
    n_blocks = (CA_PAD + t) // CA_QBLOCK
    return pl.pallas_call(
        _chunk_attn_kernel,
        grid=(batch,),
        in_specs=[
            pl.BlockSpec((t, g), lambda b: (b, 0)),
            pl.BlockSpec((t, g), lambda b: (b, 1)),
            pl.BlockSpec((t, g), lambda b: (b, 2)),
            _const_spec(table.shape),
        ],
        out_specs=pl.BlockSpec((t, g), lambda b: (b, 0)),
        out_shape=jax.ShapeDtypeStruct((batch * t, g), BF16),
        scratch_shapes=[
            pltpu.VMEM((n_blocks, CA_QBLOCK, g), BF16),
            pltpu.VMEM((n_blocks, g, CA_QBLOCK), BF16),
            pltpu.VMEM((2, CA_WINDOW // CA_QBLOCK, CA_QBLOCK, CA_QBLOCK), F32),
        ],
        compiler_params=_params(1),
        name="chunk_attn",
    )(proj, proj, proj, table)


def _ca_table_kernel(line_ref, o_ref):
    n_win = CA_WINDOW // CA_QBLOCK
    rows = 8
    x = jnp.broadcast_to(line_ref[0] * LOG2E, (rows, CA_LINE))
    qc = lax.broadcasted_iota(jnp.int32, (rows, CA_QBLOCK), 1) // CHUNK
    for grp in range(CA_WINDOW // rows):
        r0 = grp * rows
        y = pltpu.roll(x, (CA_LINE - CA_WINDOW + 1 + r0) % CA_LINE, axis=1, stride=1, stride_axis=0)
        kc = r0 // CHUNK
        allowed = (qc <= kc) & (qc >= kc - CA_LEFT_CHUNKS)
        w, r = divmod(r0, CA_QBLOCK)
        o_ref[0, w, r:r + rows, :] = jnp.where(allowed, y[:, 0:CA_QBLOCK], MASK_VALUE)
    o_ref[0, n_win] = jnp.full((CA_QBLOCK, CA_QBLOCK), MASK_VALUE, F32)


def _chunk_attn_table(rel_bias):
    heads = rel_bias.shape[0]
    n_win = CA_WINDOW // CA_QBLOCK
    left = CA_WINDOW - 1 - CA_PAD - REL_CLIP
    line = jnp.pad(rel_bias.astype(F32), ((0, 0), (left, CA_LINE - left - rel_bias.shape[1])), mode="edge")
    return pl.pallas_call(
        _ca_table_kernel,
        grid=(heads,),
        in_specs=[pl.BlockSpec((1, 1, CA_LINE), lambda h: (h, 0, 0))],
        out_specs=pl.BlockSpec((1, n_win + 1, CA_QBLOCK, CA_QBLOCK), lambda h: (h, 0, 0, 0)),
        out_shape=jax.ShapeDtypeStruct((heads, n_win + 1, CA_QBLOCK, CA_QBLOCK), F32),
        compiler_params=_params(1),
        name="ca_table",
    )(line.reshape(heads, 1, CA_LINE))


GELU_C0 = 0.7978845608028654
GELU_C1 = GELU_C0 * 0.044715


def _gelu(x):
    inner = x * (GELU_C0 + GELU_C1 * (x * x))
    return (0.5 * x) * (1.0 + jnp.tanh(inner))


def _lru_kernel(g_ref, x_ref, cw_ref, cb_ref, wa_ref, ba_ref, wx_ref, bx_ref, lam_ref, o_ref,
                xin_ref, xf_ref, a_ref, b_ref, hout_ref, h_ref):
    nb, frames, w = x_ref.shape
    n_slab = w // LANES
    sub = LRU_ROWS // nb
    i = pl.program_id(0)

    @pl.when(i == 0)
    def _():
        xf_ref[0:CONV_HIST] = jnp.zeros((CONV_HIST, nb, w), F32)
        h_ref[...] = jnp.zeros((nb, w), F32)

    for b in range(nb):
        xb = x_ref[b].astype(F32)
        for s in range(n_slab):
            xin_ref[s, b * LRU_PITCH:b * LRU_PITCH + frames, :] = xb[:, s * LANES:(s + 1) * LANES]

    def gather(t, carry):
        for s in range(n_slab):
            xf_ref[CONV_HIST + t, :, s * LANES:(s + 1) * LANES] = (
                xin_ref[s, pl.ds(t, nb, stride=LRU_PITCH), :])
        return carry

    lax.fori_loop(0, frames, gather, 0, unroll=SCAN_UNROLL)

    log2_base = (LRU_C * LOG2E) * _log_sigmoid(lam_ref[...])
    half = w // 2
    for blk in range(frames // sub):
        t0 = blk * sub
        xc = cb_ref[...]
        for tap in range(CONV_WIDTH):
            lo = CONV_HIST + t0 - tap
            xc = xc + (cw_ref[CONV_WIDTH - 1 - tap:CONV_WIDTH - tap, :]
                       * xf_ref[lo:lo + sub].reshape(LRU_ROWS, w))
        xcb = xc.astype(BF16)
        gr, gi = [], []
        for hb in range(2):
            cols = slice(hb * half, (hb + 1) * half)
            gr.append(_dot(xcb[:, cols], wa_ref[hb]))
            gi.append(_dot(xcb[:, cols], wx_ref[hb]))
        r = _sigmoid(jnp.concatenate(gr, axis=1) + ba_ref[...])
        gate_i = _sigmoid(jnp.concatenate(gi, axis=1) + bx_ref[...])
        a = jnp.exp2(r * log2_base)
        a_ref[t0:t0 + sub] = a.reshape(sub, nb, w)
        y = 1.0 - a * a
        root = jnp.where(y > 0.0, y * lax.rsqrt(y), 0.0)
        b_ref[t0:t0 + sub] = (root * (gate_i * xc)).reshape(sub, nb, w)

    xf_ref[0:CONV_HIST] = xf_ref[frames:frames + CONV_HIST]

    def scan(t, h):
        h = a_ref[t] * h + b_ref[t]
        for s in range(n_slab):
            hout_ref[s, pl.ds(t, nb, stride=LRU_PITCH), :] = h[:, s * LANES:(s + 1) * LANES]
        return h

    h_ref[...] = lax.fori_loop(0, frames, scan, h_ref[...], unroll=SCAN_UNROLL)

    for b in range(nb):
        rows = slice(b * LRU_PITCH, b * LRU_PITCH + frames)
        hv = jnp.concatenate([hout_ref[s, rows, :] for s in range(n_slab)], axis=1)
        o_ref[b] = (hv * _gelu(g_ref[b].astype(F32))).astype(BF16)


def _lru(proj3, conv_w, conv_b, wa_bd, b_a, wx_bd, b_x, lam, col0):
    nb, t, _ = proj3.shape
    w = conv_w.shape[1]
    gc = col0 // w
    n_slab = w // LANES
    return pl.pallas_call(
        _lru_kernel,
        grid=(t // LRU_FRAMES,),
        in_specs=[
            pl.BlockSpec((nb, LRU_FRAMES, w), lambda i: (0, i, gc)),
            pl.BlockSpec((nb, LRU_FRAMES, w), lambda i: (0, i, gc + 1)),
            _const_spec((CONV_WIDTH, w)),
            _const_spec((1, w)),
            _const_spec(wa_bd.shape),
            _const_spec((1, w)),
            _const_spec(wx_bd.shape),
            _const_spec((1, w)),
            _const_spec((1, w)),
        ],
        out_specs=pl.BlockSpec((nb, LRU_FRAMES, w), lambda i: (0, i, 0)),
        out_shape=jax.ShapeDtypeStruct((nb, t, w), BF16),
        scratch_shapes=[
            pltpu.VMEM((n_slab, nb * LRU_PITCH, LANES), F32),
            pltpu.VMEM((CONV_HIST + LRU_FRAMES, nb, w), F32),
            pltpu.VMEM((LRU_FRAMES, nb, w), F32),
            pltpu.VMEM((LRU_FRAMES, nb, w), F32),
            pltpu.VMEM((n_slab, nb * LRU_PITCH, LANES), F32),
            pltpu.VMEM((nb, w), F32),
        ],
        compiler_params=_params(1),
        name="rglru",
    )(proj3, proj3, conv_w, conv_b.reshape(1, w), wa_bd, b_a.reshape(1, w), wx_bd,
      b_x.reshape(1, w), lam.reshape(1, w))


def _block_diag_halves(wblk):
    nb, d, _ = wblk.shape
    per = nb // 2
    eye = jnp.eye(per, dtype=wblk.dtype)
    halves = [jnp.einsum("nde,nm->ndme", wblk[h * per:(h + 1) * per], eye).reshape(per * d, per * d)
              for h in range(2)]
    return jnp.stack(halves).astype(BF16)


def kernel(x, norm_w, w_in_even, gla_w_a_up, gla_b_a, gla_norm_w, fox_b_f, w_out_even,
           w_in_odd, rel_bias, conv_w, conv_b, lru_w_a, lru_b_a, lru_w_x, lru_b_x,
           lru_lambda, w_out_odd, w_mlp_up, w_mlp_down):
    batch, t, d = x.shape
    x2d = x.reshape(batch * t, d)
    group = d // 2
    kw = GLA_HEADS * GLA_DK

    w_in = w_in_even[0]
    o_ga = 2 * kw + 2 * group
    o_fq = o_ga + GLA_RANK
    o_ff = o_fq + 3 * group
    q_scale = LOG2E * HEAD_DIM ** -0.5
    w_main = jnp.concatenate(
        [w_in[:, :o_ga], w_in[:, o_fq:o_fq + group] * q_scale, w_in[:, o_fq + group:o_ff]],
        axis=1).astype(BF16)
    n_small = FOX_HEADS + GLA_RANK
    w_small = jnp.concatenate(
        [w_in[:, o_ff:], w_in[:, o_ga:o_fq], jnp.zeros((d, LANES - n_small), F32)], axis=1).astype(BF16)
    wa_pad = jnp.zeros((LANES, kw), F32).at[FOX_HEADS:n_small].set(gla_w_a_up[0]).astype(BF16)
    fox_bias = jnp.zeros((1, LANES), F32).at[0, :FOX_HEADS].set(fox_b_f[0])

    m = batch * t
    proj, small = _norm_proj(x2d, norm_w[0, 0], [(w_main, BF16), (w_small, F32)])
    out_a = _gla(proj, small, wa_pad, gla_b_a[0], gla_norm_w[0], batch, t)
    cum = _fox_cum(small, fox_bias, batch, t)
    out_b = _fox(proj, cum, batch, t, o_ga)
    x2d = _mix_mlp(out_a, out_b, x2d, w_out_even[0], norm_w[0, 1:4], w_mlp_up[0], w_mlp_down[0])

    w_in = w_in_odd[0]
    w_main = jnp.concatenate([w_in[:, :group] * q_scale, w_in[:, group:]], axis=1).astype(BF16)
    (proj,) = _norm_proj(x2d, norm_w[1, 0], [(w_main, BF16)])
    out_c = _chunk_attn(proj, _chunk_attn_table(rel_bias[0]), batch, t)
    out_d = _lru(proj.reshape(batch, t, w_main.shape[1]), conv_w[0], conv_b[0],
                 _block_diag_halves(lru_w_a[0]), lru_b_a[0],
                 _block_diag_halves(lru_w_x[0]), lru_b_x[0], lru_lambda[0], 3 * group)
    x2d = _mix_mlp(out_c, out_d.reshape(m, group), x2d, w_out_odd[0], norm_w[1, 1:4],
                   w_mlp_up[1], w_mlp_down[1])
    return x2d.reshape(batch, t, d)
```

```python
import functools

import jax
import jax.numpy as jnp
from jax import lax
from jax.experimental import pallas as pl
from jax.experimental.pallas import tpu as pltpu

F32 = jnp.float32
BF16 = jnp.bfloat16

NORM_EPS = 1e-6
CHUNK = 64
GLA_HEADS = 4
GLA_DK = 64
GLA_DV = 128
GLA_RANK = 16
GLA_GATE_TAU = 16.0
FOX_HEADS = 8
HEAD_DIM = 64
CA_LEFT_CHUNKS = 8
REL_CLIP = 128
CONV_WIDTH = 4
LRU_BLOCKS = 8
LRU_C = 8.0

LANES = 128
MXU_DIM = 256
MASK_VALUE = -1e30
VMEM_LIMIT_BYTES = 56 * 1024 * 1024

ROW_TILE = 512
PROJ_ROW_TILE = 1024
STAGE_ROWS = 512
STAGE_COLS = 1024
FOX_BLOCK = 512
CA_QBLOCK = 4 * CHUNK
CA_WINDOW = CA_QBLOCK + CA_LEFT_CHUNKS * CHUNK
CA_PAD = CA_LEFT_CHUNKS * CHUNK
CA_LINE = 1024
LOG2E = 1.4426950408889634
CUM_TERMS = 3
GLA_GROUP = 4
LRU_ROWS = 256
LRU_FRAMES = 256
LRU_PITCH = LRU_FRAMES + 8
CONV_HIST = 8
SCAN_UNROLL = 8


def _params(n_axes):
    return pltpu.CompilerParams(
        dimension_semantics=("arbitrary",) * n_axes,
        vmem_limit_bytes=VMEM_LIMIT_BYTES,
    )


def _const_spec(shape):
    nd = len(shape)
    return pl.BlockSpec(shape, lambda *_: (0,) * nd, pipeline_mode=pl.Buffered(1))


def _rmsnorm(x, w):
    y = x * lax.rsqrt(jnp.mean(x * x, axis=-1, keepdims=True) + NORM_EPS)
    return y * w


def _log_sigmoid(z):
    return jnp.minimum(z, 0.0) - jnp.log1p(jnp.exp(-jnp.abs(z)))


def _sigmoid(z):
    return 0.5 * jnp.tanh(0.5 * z) + 0.5


def _dot(a, b):
    return jnp.dot(a, b, preferred_element_type=F32)


def _dot_nt(a, b):
    return lax.dot_general(a, b, (((1,), (1,)), ((), ())), preferred_element_type=F32)


def _dot_tn(a, b):
    return lax.dot_general(a, b, (((0,), (0,)), ((), ())), preferred_element_type=F32)


def _half_mask(lane, j):
    return lane < HEAD_DIM if j == 0 else lane >= HEAD_DIM


def _split3(x):
    h1 = x.astype(BF16)
    r1 = x - h1.astype(F32)
    h2 = r1.astype(BF16)
    h3 = (r1 - h2.astype(F32)).astype(BF16)
    return h1, h2, h3


def _row_major(i):
    return (i, 0)


def _norm_proj_kernel(x_ref, nw_ref, *refs, n_chunk):
    n_out = len(refs) // 2
    h = _rmsnorm(x_ref[...], nw_ref[...]).astype(BF16)
    for w_ref, o_ref in zip(refs[:n_out], refs[n_out:]):
        n_total = o_ref.shape[1]
        for n0 in range(0, n_total, n_chunk):
            n1 = min(n0 + n_chunk, n_total)
            o_ref[:, n0:n1] = _dot(h, w_ref[:, n0:n1]).astype(o_ref.dtype)


def _norm_proj(x2d, nw, outputs):
    m, d = x2d.shape
    in_specs = [pl.BlockSpec((PROJ_ROW_TILE, d), _row_major), _const_spec((1, d))]
    in_specs += [_const_spec(w.shape) for w, _ in outputs]
    return pl.pallas_call(
        functools.partial(_norm_proj_kernel, n_chunk=2 * MXU_DIM),
        grid=(m // PROJ_ROW_TILE,),
        in_specs=in_specs,
        out_specs=[pl.BlockSpec((PROJ_ROW_TILE, w.shape[1]), _row_major) for w, _ in outputs],
        out_shape=[jax.ShapeDtypeStruct((m, w.shape[1]), dt) for w, dt in outputs],
        compiler_params=_params(1),
        name="norm_proj",
    )(x2d, nw.reshape(1, d), *[w for w, _ in outputs])


def _stage_bf16_weights(pairs, stage_ref, sem_ref):
    pieces = []
    for src, dst in pairs:
        rows, cols = dst.shape
        for r0 in range(0, rows, STAGE_ROWS):
            for c0 in range(0, cols, STAGE_COLS):
                pieces.append((src, dst, r0, c0))

    def copy(k):
        src, _, r0, c0 = pieces[k]
        return pltpu.make_async_copy(
            src.at[pl.ds(r0, STAGE_ROWS), pl.ds(c0, STAGE_COLS)], stage_ref.at[k % 2], sem_ref.at[k % 2])

    copy(0).start()
    for k, (_, dst, r0, c0) in enumerate(pieces):
        if k + 1 < len(pieces):
            copy(k + 1).start()
        copy(k).wait()
        dst[r0:r0 + STAGE_ROWS, c0:c0 + STAGE_COLS] = stage_ref[k % 2].astype(BF16)


def _mix_mlp_kernel(a_ref, b_ref, x_ref, nw_ref, wo_hbm, wu_hbm, wd_hbm, o_ref,
                    wo_ref, wu_ref, wd_ref, u_ref, y_ref, stage_ref, sem_ref, *, chunk, layer):
    @pl.when(pl.program_id(0) == 0)
    def _():
        _stage_bf16_weights([(wo_hbm, wo_ref), (wu_hbm.at[layer], wu_ref), (wd_hbm.at[layer], wd_ref)],
                            stage_ref, sem_ref)

    d_ff = wu_ref.shape[1]
    d = wd_ref.shape[1]
    mix = jnp.concatenate([a_ref[...], b_ref[...]], axis=1)
    for n0 in range(0, d, chunk):
        cols = slice(n0, n0 + chunk)
        y_ref[:, cols] = _dot(mix, wo_ref[:, cols])
    x1 = x_ref[...] + _rmsnorm(y_ref[...], nw_ref[0:1, :])
    h = _rmsnorm(x1, nw_ref[1:2, :]).astype(BF16)
    for f0 in range(0, d_ff, chunk):
        u = jnp.maximum(_dot(h, wu_ref[:, f0:f0 + chunk]), 0.0)
        u_ref[:, f0:f0 + chunk] = (u * u).astype(BF16)
    for n0 in range(0, d, chunk):
        y_ref[:, n0:n0 + chunk] = _dot(u_ref[...], wd_ref[:, n0:n0 + chunk])
    o_ref[...] = x1 + _rmsnorm(y_ref[...], nw_ref[2:3, :])


def _mix_mlp(mix_a, mix_b, x2d, w_out, nw3, w_up_all, w_down_all, layer):
    m, d = x2d.shape
    g = mix_a.shape[1]
    d_ff = w_up_all.shape[2]
    return pl.pallas_call(
        functools.partial(_mix_mlp_kernel, chunk=2 * MXU_DIM, layer=layer),
        grid=(m // ROW_TILE,),
        in_specs=[
            pl.BlockSpec((ROW_TILE, g), _row_major),
            pl.BlockSpec((ROW_TILE, g), _row_major),
            pl.BlockSpec((ROW_TILE, d), _row_major),
            _const_spec((3, d)),
            pl.BlockSpec(memory_space=pl.ANY),
            pl.BlockSpec(memory_space=pl.ANY),
            pl.BlockSpec(memory_space=pl.ANY),
        ],
        out_specs=pl.BlockSpec((ROW_TILE, d), _row_major),
        out_shape=jax.ShapeDtypeStruct((m, d), F32),
        scratch_shapes=[
            pltpu.VMEM((2 * g, d), BF16),
            pltpu.VMEM((d, d_ff), BF16),
            pltpu.VMEM((d_ff, d), BF16),
            pltpu.VMEM((ROW_TILE, d_ff), BF16),
            pltpu.VMEM((ROW_TILE, d), F32),
            pltpu.VMEM((2, STAGE_ROWS, STAGE_COLS), F32),
            pltpu.SemaphoreType.DMA((2,)),
        ],
        compiler_params=_params(1),
        name="mix_mlp",
    )(mix_a, mix_b, x2d, nw3, w_out, w_up_all, w_down_all)


def _gla_kernel(q_ref, k_ref, v_ref, r_ref, sm_ref, wa_ref, ba_ref, nw_ref, o_ref,
                la_ref, st_ref, dec_ref):
    t = q_ref.shape[0]
    nc = t // CHUNK
    n_pairs = GLA_HEADS // 2

    z = _dot(sm_ref[...].astype(BF16), wa_ref[...]) + ba_ref[...]
    la_ref[...] = _log_sigmoid(z) * (1.0 / GLA_GATE_TAU)

    grp = GLA_GROUP * CHUNK
    row = lax.broadcasted_iota(jnp.int32, (grp, grp), 0)
    col = lax.broadcasted_iota(jnp.int32, (grp, grp), 1)
    tri = jnp.where((row >= col) & (row // CHUNK == col // CHUNK), 1.0, 0.0).astype(BF16)
    erow = lax.broadcasted_iota(jnp.int32, (grp, GLA_GROUP * LANES), 0)
    ecol = lax.broadcasted_iota(jnp.int32, (grp, GLA_GROUP * LANES), 1)
    chunk_ones = jnp.where(erow // CHUNK == ecol // LANES, 1.0, 0.0).astype(BF16)

    def increments(g, carry):
        r0 = pl.multiple_of(g * grp, grp)
        la = la_ref[pl.ds(r0, grp), :]
        hi = la.astype(BF16)
        lo = (la - hi.astype(F32)).astype(BF16)
        cum = _dot(tri, hi) + _dot(tri, lo)
        dec = jnp.exp(_dot_tn(hi, chunk_ones) + _dot_tn(lo, chunk_ones))
        k_g = k_ref[pl.ds(r0, grp), :].astype(F32)
        v_g = v_ref[pl.ds(r0, grp), :]
        k_dec = []
        for c in range(GLA_GROUP):
            rows = slice(c * CHUNK, (c + 1) * CHUNK)
            total = cum[(c + 1) * CHUNK - 1:(c + 1) * CHUNK, :]
            k_dec.append((k_g[rows] * jnp.exp(total - cum[rows])).astype(BF16))
        for c in range(GLA_GROUP):
            rows = slice(c * CHUNK, (c + 1) * CHUNK)
            dec_ref[g * GLA_GROUP + c] = dec[:, c * LANES:(c + 1) * LANES]
            for h in range(GLA_HEADS):
                p, j = divmod(h, 2)
                full = _dot_tn(k_dec[c][:, p * LANES:(p + 1) * LANES], v_g[rows, h * GLA_DV:(h + 1) * GLA_DV])
                st_ref[g * GLA_GROUP + c, p, j * GLA_DK:(j + 1) * GLA_DK, :] = full[j * GLA_DK:(j + 1) * GLA_DK, :]
        return carry

    lax.fori_loop(0, nc // GLA_GROUP, increments, 0)

    def scan(c, carry):
        for p in range(n_pairs):
            st_ref[c, p] = st_ref[c, p] + dec_ref[c, p * LANES:(p + 1) * LANES, :] * st_ref[c - 1, p]
        return carry

    lax.fori_loop(1, nc, scan, 0)

    lane = lax.broadcasted_iota(jnp.int32, (CHUNK, LANES), 1)
    scale = GLA_DK ** -0.5

    def outputs(g, carry):
        r0 = pl.multiple_of(g * grp, grp)
        q_g = q_ref[pl.ds(r0, grp), :] * jnp.asarray(scale, BF16)
        raw = []
        for c in range(GLA_GROUP):
            rows = slice(c * CHUNK, (c + 1) * CHUNK)
            for h in range(GLA_HEADS):
                p, j = divmod(h, 2)
                q_pair = q_g[rows, p * LANES:(p + 1) * LANES]
                q_h = jnp.where(_half_mask(lane, j), q_pair, jnp.zeros_like(q_pair))
                raw.append(_dot(q_h, st_ref[g * GLA_GROUP + c, p].astype(BF16)))
        for c in range(GLA_GROUP):
            for h in range(GLA_HEADS):
                o = raw[c * GLA_HEADS + h]
                o = o * lax.rsqrt(jnp.mean(o * o, axis=-1, keepdims=True) + NORM_EPS)
                o = o * nw_ref[:, h * GLA_DV:(h + 1) * GLA_DV]
                rows = pl.ds(r0 + c * CHUNK, CHUNK)
                r_h = r_ref[rows, h * GLA_DV:(h + 1) * GLA_DV].astype(F32)
                o_ref[rows, h * GLA_DV:(h + 1) * GLA_DV] = (o * (r_h * _sigmoid(r_h))).astype(BF16)
        return carry

    lax.fori_loop(0, nc // GLA_GROUP, outputs, 0)


def _gla(proj, small, wa_pad, b_a, norm_w, batch, t):
    kw = GLA_HEADS * GLA_DK
    gw = GLA_HEADS * GLA_DV
    nc = t // CHUNK
    return pl.pallas_call(
        _gla_kernel,
        grid=(batch,),
        in_specs=[
            pl.BlockSpec((t, kw), lambda b: (b, 0)),
            pl.BlockSpec((t, kw), lambda b: (b, 1)),
            pl.BlockSpec((t, gw), lambda b: (b, 1)),
            pl.BlockSpec((t, gw), lambda b: (b, 2)),
            pl.BlockSpec((t, LANES), lambda b: (b, 0)),
            _const_spec((LANES, kw)),
            _const_spec((1, kw)),
            _const_spec((1, gw)),
        ],
        out_specs=pl.BlockSpec((t, gw), lambda b: (b, 0)),
        out_shape=jax.ShapeDtypeStruct((batch * t, gw), BF16),
        scratch_shapes=[
            pltpu.VMEM((t, kw), F32),
            pltpu.VMEM((nc, GLA_HEADS // 2, LANES, GLA_DV), F32),
            pltpu.VMEM((nc, kw, LANES), F32),
        ],
        compiler_params=_params(1),
        name="gla",
    )(proj, proj, proj, proj, small, wa_pad, b_a.reshape(1, kw), norm_w.reshape(1, gw))


def _fox_cum_kernel(sm_ref, bias_ref, o_ref):
    t = sm_ref.shape[0]
    row = lax.broadcasted_iota(jnp.int32, (LANES, LANES), 0)
    col = lax.broadcasted_iota(jnp.int32, (LANES, LANES), 1)
    tri = jnp.where(row >= col, 1.0, 0.0).astype(BF16)
    carry = jnp.zeros((1, LANES), F32)
    for blk in range(t // LANES):
        rows = slice(blk * LANES, (blk + 1) * LANES)
        ls = _log_sigmoid(sm_ref[rows, :] + bias_ref[...])
        h1, h2, h3 = _split3(ls)
        cb = _dot(tri, h1) + _dot(tri, h2) + _dot(tri, h3) + carry
        o_ref[rows, :] = cb
        carry = cb[LANES - 1:LANES, :]


def _fox_cum(small, bias_row, batch, t):
    return pl.pallas_call(
        _fox_cum_kernel,
        grid=(batch,),
        in_specs=[pl.BlockSpec((t, LANES), lambda b: (b, 0)), _const_spec((1, LANES))],
        out_specs=pl.BlockSpec((t, LANES), lambda b: (b, 0)),
        out_shape=jax.ShapeDtypeStruct((batch * t, LANES), F32),
        compiler_params=_params(1),
        name="fox_cum",
    )(small, bias_row)


def _fox_kernel(q_ref, k_ref, v_ref, c_ref, o_ref, ka_ref, vt_ref, s_ref, mb_ref):
    t = k_ref.shape[0]
    blk = FOX_BLOCK
    nkv = t // blk
    pair = pl.program_id(1)

    sel_r = lax.broadcasted_iota(jnp.int32, (LANES, LANES), 0)
    sel_c = lax.broadcasted_iota(jnp.int32, (LANES, LANES), 1)
    sels = []
    for term in range(CUM_TERMS):
        hit = (((sel_r == 2 * pair) & (sel_c == term))
               | ((sel_r == 2 * pair + 1) & (sel_c == CUM_TERMS + term)))
        sels.append(jnp.where(hit, 1.0, 0.0).astype(BF16))
    for kb in range(nkv):
        rows = slice(kb * blk, (kb + 1) * blk)
        vt_ref[kb] = v_ref[rows, :].astype(F32).T.astype(BF16)
        parts = _split3(c_ref[rows, :] * (-LOG2E))
        extra = _dot(parts[0], sels[0]) + _dot(parts[1], sels[1]) + _dot(parts[2], sels[2])
        ka_ref[kb, :, 0:LANES] = k_ref[rows, :]
        ka_ref[kb, :, LANES:2 * LANES] = extra.astype(BF16)

    lane = lax.broadcasted_iota(jnp.int32, (blk, LANES), 1)
    krow = lax.broadcasted_iota(jnp.int32, (blk, blk), 0)
    qcol = lax.broadcasted_iota(jnp.int32, (blk, blk), 1)
    causal = krow <= qcol

    def augmented_queries(qi):
        q = q_ref[qi * blk:(qi + 1) * blk, :]
        q_aug = []
        for j in range(2):
            q_h = jnp.where(_half_mask(lane, j), q, jnp.zeros_like(q))
            ones = jnp.where((lane >= CUM_TERMS * j) & (lane < CUM_TERMS * (j + 1)), 1.0, 0.0).astype(BF16)
            q_aug.append(jnp.concatenate([q_h, ones], axis=1).astype(F32).T.astype(BF16))
        return q_aug

    def scores(q_aug, kb, slot, masked):
        k_blk = ka_ref[kb]
        for j in range(2):
            s = _dot(k_blk, q_aug[j])
            if masked:
                s = jnp.where(causal, s, MASK_VALUE)
            s_ref[slot, j] = s
            mb_ref[slot, j] = jnp.max(s, axis=0, keepdims=True)

    def accumulate(kb, slot, state):
        new = []
        for j in range(2):
            m, l, acc = state[j]
            m_new = jnp.maximum(m, mb_ref[slot, j])
            alpha = jnp.exp2(m - m_new)
            p = jnp.exp2(s_ref[slot, j] - m_new)
            l = l * alpha + jnp.sum(p, axis=0, keepdims=True)
            pv = _dot(vt_ref[kb, j * HEAD_DIM:(j + 1) * HEAD_DIM, :], p.astype(BF16))
            new.append((m_new, l, acc * alpha + pv))
        return new

    tasks = [(qi, kb) for qi in range(nkv) for kb in range(qi + 1)]
    fresh = [(jnp.full((1, blk), MASK_VALUE, F32), jnp.zeros((1, blk), F32),
              jnp.zeros((HEAD_DIM, blk), F32))] * 2
    q_aug = augmented_queries(0)
    scores(q_aug, 0, 0, masked=True)
    state = fresh
    for n, (qi, kb) in enumerate(tasks):
        if n + 1 < len(tasks):
            qi_next, kb_next = tasks[n + 1]
            if qi_next != qi:
                q_aug = augmented_queries(qi_next)
            scores(q_aug, kb_next, (n + 1) % 2, masked=(kb_next == qi_next))
        state = accumulate(kb, n % 2, state)
        if kb == qi:
            outs = [acc * (1.0 / l) for _, l, acc in state]
            o_ref[qi * blk:(qi + 1) * blk, :] = jnp.concatenate(outs, axis=0).T.astype(BF16)
            state = fresh


def _fox(proj, cum, batch, t, col0):
    nq = t // FOX_BLOCK
    n_pairs = FOX_HEADS // 2
    qc, kc, vc = col0 // LANES, col0 // LANES + n_pairs, col0 // LANES + 2 * n_pairs
    return pl.pallas_call(
        _fox_kernel,
        grid=(batch, n_pairs),
        in_specs=[
            pl.BlockSpec((t, LANES), lambda b, p: (b, qc + p)),
            pl.BlockSpec((t, LANES), lambda b, p: (b, kc + p)),
            pl.BlockSpec((t, LANES), lambda b, p: (b, vc + p)),
            pl.BlockSpec((t, LANES), lambda b, p: (b, 0)),
        ],
        out_specs=pl.BlockSpec((t, LANES), lambda b, p: (b, p)),
        out_shape=jax.ShapeDtypeStruct((batch * t, FOX_HEADS * HEAD_DIM), BF16),
        scratch_shapes=[
            pltpu.VMEM((nq, FOX_BLOCK, 2 * LANES), BF16),
            pltpu.VMEM((nq, LANES, FOX_BLOCK), BF16),
            pltpu.VMEM((2, 2, FOX_BLOCK, FOX_BLOCK), F32),
            pltpu.VMEM((2, 2, 1, FOX_BLOCK), F32),
        ],
        compiler_params=_params(2),
        name="fox",
    )(proj, proj, proj, cum)


def _chunk_attn_kernel(q_ref, k_ref, v_ref, tab_ref, o_ref, kp_ref, vt_ref, s_ref):
    t, g = q_ref.shape
    qb = CA_QBLOCK
    n_pad = CA_PAD // qb
    n_win = CA_WINDOW // qb

    for kb in range(n_pad):
        kp_ref[kb] = jnp.zeros((qb, g), BF16)
        vt_ref[kb] = jnp.zeros((g, qb), BF16)
    for kb in range(t // qb):
        rows = slice(kb * qb, (kb + 1) * qb)
        kp_ref[n_pad + kb] = k_ref[rows, :]
        vt_ref[n_pad + kb] = v_ref[rows, :].astype(F32).T.astype(BF16)

    lane = lax.broadcasted_iota(jnp.int32, (qb, LANES), 1)
    n_heads = g // HEAD_DIM
    lax.fori_loop(0, t // qb, functools.partial(
        _chunk_attn_block, q_ref=q_ref, tab_ref=tab_ref, o_ref=o_ref, kp_ref=kp_ref, vt_ref=vt_ref,
        s_ref=s_ref, lane=lane, n_heads=n_heads, n_pad=n_pad, n_win=n_win), 0)


def _chunk_attn_block(i, carry, *, q_ref, tab_ref, o_ref, kp_ref, vt_ref, s_ref, lane, n_heads,
                      n_pad, n_win):
    qb = CA_QBLOCK
    q_rows = pl.ds(pl.multiple_of(i * qb, qb), qb)

    def scores(h):
        pair, j = divmod(h, 2)
        cols = slice(pair * LANES, (pair + 1) * LANES)
        q_pair = q_ref[q_rows, cols]
        q_h = jnp.where(_half_mask(lane, j), q_pair, jnp.zeros_like(q_pair))
        q_ht = q_h.astype(F32).T.astype(BF16)
        m = None
        for w in range(n_win):
            tab_blk = jnp.where(i + w >= n_pad, w, n_win)
            s = _dot(kp_ref[i + w, :, cols], q_ht) + tab_ref[h, tab_blk]
            s_ref[h % 2, w] = s
            m_w = s.max(axis=0, keepdims=True)
            m = m_w if m is None else jnp.maximum(m, m_w)
        return m

    def weighted_values(h, m):
        l = jnp.zeros((1, qb), F32)
        acc = jnp.zeros((HEAD_DIM, qb), F32)
        for w in range(n_win):
            p = jnp.exp2(s_ref[h % 2, w] - m)
            l = l + jnp.sum(p, axis=0, keepdims=True)
            acc = acc + _dot(vt_ref[i + w, h * HEAD_DIM:(h + 1) * HEAD_DIM, :], p.astype(BF16))
        return acc * (1.0 / l)

    outs = []
    pending = scores(0)
    for h in range(n_heads):
        upcoming = scores(h + 1) if h + 1 < n_heads else None
        outs.append(weighted_values(h, pending))
        pending = upcoming
    o_ref[q_rows, :] = jnp.concatenate(outs, axis=0).T.astype(BF16)
    return carry


def _chunk_attn(proj, table, batch, t):
    g = table.shape[0] * HEAD_DIM
    n_blocks = (CA_PAD + t) // CA_QBLOCK
    return pl.pallas_call(
        _chunk_attn_kernel,
        grid=(batch,),
        in_specs=[
            pl.BlockSpec((t, g), lambda b: (b, 0)),
            pl.BlockSpec((t, g), lambda b: (b, 1)),
            pl.BlockSpec((t, g), lambda b: (b, 2)),
            _const_spec(table.shape),
        ],
        out_specs=pl.BlockSpec((t, g), lambda b: (b, 0)),
        out_shape=jax.ShapeDtypeStruct((batch * t, g), BF16),
        scratch_shapes=[
            pltpu.VMEM((n_blocks, CA_QBLOCK, g), BF16),
            pltpu.VMEM((n_blocks, g, CA_QBLOCK), BF16),
            pltpu.VMEM((2, CA_WINDOW // CA_QBLOCK, CA_QBLOCK, CA_QBLOCK), F32),
        ],
        compiler_params=_params(1),
        name="chunk_attn",
    )(proj, proj, proj, table)


def _ca_table_kernel(line_ref, o_ref):
    n_win = CA_WINDOW // CA_QBLOCK
    rows = 8
    x = jnp.broadcast_to(line_ref[0] * LOG2E, (rows, CA_LINE))
    qc = lax.broadcasted_iota(jnp.int32, (rows, CA_QBLOCK), 1) // CHUNK
    for grp in range(CA_WINDOW // rows):
        r0 = grp * rows
        y = pltpu.roll(x, (CA_LINE - CA_WINDOW + 1 + r0) % CA_LINE, axis=1, stride=1, stride_axis=0)
        kc = r0 // CHUNK
        allowed = (qc <= kc) & (qc >= kc - CA_LEFT_CHUNKS)
        w, r = divmod(r0, CA_QBLOCK)
        o_ref[0, w, r:r + rows, :] = jnp.where(allowed, y[:, 0:CA_QBLOCK], MASK_VALUE)
    o_ref[0, n_win] = jnp.full((CA_QBLOCK, CA_QBLOCK), MASK_VALUE, F32)


def _chunk_attn_table(rel_bias):
    heads = rel_bias.shape[0]
    n_win = CA_WINDOW // CA_QBLOCK
    left = CA_WINDOW - 1 - CA_PAD - REL_CLIP
    line = jnp.pad(rel_bias.astype(F32), ((0, 0), (left, CA_LINE - left - rel_bias.shape[1])), mode="edge")
    return pl.pallas_call(
        _ca_table_kernel,
        grid=(heads,),
        in_specs=[pl.BlockSpec((1, 1, CA_LINE), lambda h: (h, 0, 0))],
        out_specs=pl.BlockSpec((1, n_win + 1, CA_QBLOCK, CA_QBLOCK), lambda h: (h, 0, 0, 0)),
        out_shape=jax.ShapeDtypeStruct((heads, n_win + 1, CA_QBLOCK, CA_QBLOCK), F32),
        compiler_params=_params(1),
        name="ca_table",
    )(line.reshape(heads, 1, CA_LINE))


GELU_C0 = 0.7978845608028654
GELU_C1 = GELU_C0 * 0.044715


def _gelu(x):
    inner = x * (GELU_C0 + GELU_C1 * (x * x))
    return (0.5 * x) * (1.0 + jnp.tanh(inner))


def _lru_kernel(g_ref, x_ref, cw_ref, cb_ref, wa_ref, ba_ref, wx_ref, bx_ref, lam_ref, o_ref,
                xin_ref, xf_ref, a_ref, b_ref, hout_ref, h_ref):
    nb, frames, w = x_ref.shape
    n_slab = w // LANES
    sub = LRU_ROWS // nb
    i = pl.program_id(0)

    @pl.when(i == 0)
    def _():
        xf_ref[0:CONV_HIST] = jnp.zeros((CONV_HIST, nb, w), F32)
        h_ref[...] = jnp.zeros((nb, w), F32)

    for b in range(nb):
        xb = x_ref[b].astype(F32)
        for s in range(n_slab):
            xin_ref[s, b * LRU_PITCH:b * LRU_PITCH + frames, :] = xb[:, s * LANES:(s + 1) * LANES]

    def gather(t, carry):
        for s in range(n_slab):
            xf_ref[CONV_HIST + t, :, s * LANES:(s + 1) * LANES] = (
                xin_ref[s, pl.ds(t, nb, stride=LRU_PITCH), :])
        return carry

    lax.fori_loop(0, frames, gather, 0, unroll=SCAN_UNROLL)

    log2_base = (LRU_C * LOG2E) * _log_sigmoid(lam_ref[...])
    half = w // 2
    for blk in range(frames // sub):
        t0 = blk * sub
        xc = cb_ref[...]
        for tap in range(CONV_WIDTH):
            lo = CONV_HIST + t0 - tap
            xc = xc + (cw_ref[CONV_WIDTH - 1 - tap:CONV_WIDTH - tap, :]
                       * xf_ref[lo:lo + sub].reshape(LRU_ROWS, w))
        xcb = xc.astype(BF16)
        gr, gi = [], []
        for hb in range(2):
            cols = slice(hb * half, (hb + 1) * half)
            gr.append(_dot(xcb[:, cols], wa_ref[hb]))
            gi.append(_dot(xcb[:, cols], wx_ref[hb]))
        r = _sigmoid(jnp.concatenate(gr, axis=1) + ba_ref[...])
        gate_i = _sigmoid(jnp.concatenate(gi, axis=1) + bx_ref[...])
        a = jnp.exp2(r * log2_base)
        a_ref[t0:t0 + sub] = a.reshape(sub, nb, w)
        y = 1.0 - a * a
        root = jnp.where(y > 0.0, y * lax.rsqrt(y), 0.0)
        b_ref[t0:t0 + sub] = (root * (gate_i * xc)).reshape(sub, nb, w)

    xf_ref[0:CONV_HIST] = xf_ref[frames:frames + CONV_HIST]

    def scan(t, h):
        h = a_ref[t] * h + b_ref[t]
        for s in range(n_slab):
            hout_ref[s, pl.ds(t, nb, stride=LRU_PITCH), :] = h[:, s * LANES:(s + 1) * LANES]
        return h

    h_ref[...] = lax.fori_loop(0, frames, scan, h_ref[...], unroll=SCAN_UNROLL)

    for b in range(nb):
        rows = slice(b * LRU_PITCH, b * LRU_PITCH + frames)
        hv = jnp.concatenate([hout_ref[s, rows, :] for s in range(n_slab)], axis=1)
        o_ref[b] = (hv * _gelu(g_ref[b].astype(F32))).astype(BF16)


def _lru(proj3, conv_w, conv_b, wa_bd, b_a, wx_bd, b_x, lam, col0):
    nb, t, _ = proj3.shape
    w = conv_w.shape[1]
    gc = col0 // w
    n_slab = w // LANES
    return pl.pallas_call(
        _lru_kernel,
        grid=(t // LRU_FRAMES,),
        in_specs=[
            pl.BlockSpec((nb, LRU_FRAMES, w), lambda i: (0, i, gc)),
            pl.BlockSpec((nb, LRU_FRAMES, w), lambda i: (0, i, gc + 1)),
            _const_spec((CONV_WIDTH, w)),
            _const_spec((1, w)),
            _const_spec(wa_bd.shape),
            _const_spec((1, w)),
            _const_spec(wx_bd.shape),
            _const_spec((1, w)),
            _const_spec((1, w)),
        ],
        out_specs=pl.BlockSpec((nb, LRU_FRAMES, w), lambda i: (0, i, 0)),
        out_shape=jax.ShapeDtypeStruct((nb, t, w), BF16),
        scratch_shapes=[
            pltpu.VMEM((n_slab, nb * LRU_PITCH, LANES), F32),
            pltpu.VMEM((CONV_HIST + LRU_FRAMES, nb, w), F32),
            pltpu.VMEM((LRU_FRAMES, nb, w), F32),
            pltpu.VMEM((LRU_FRAMES, nb, w), F32),
            pltpu.VMEM((n_slab, nb * LRU_PITCH, LANES), F32),
            pltpu.VMEM((nb, w), F32),
        ],
        compiler_params=_params(1),
        name="rglru",
    )(proj3, proj3, conv_w, conv_b.reshape(1, w), wa_bd, b_a.reshape(1, w), wx_bd,
      b_x.reshape(1, w), lam.reshape(1, w))


def _block_diag_halves(wblk):
    nb, d, _ = wblk.shape
    per = nb // 2
    eye = jnp.eye(per, dtype=wblk.dtype)
    halves = [jnp.einsum("nde,nm->ndme", wblk[h * per:(h + 1) * per], eye).reshape(per * d, per * d)
              for h in range(2)]
    return jnp.stack(halves).astype(BF16)


def kernel(x, norm_w, w_in_even, gla_w_a_up, gla_b_a, gla_norm_w, fox_b_f, w_out_even,
           w_in_odd, rel_bias, conv_w, conv_b, lru_w_a, lru_b_a, lru_w_x, lru_b_x,
           lru_lambda, w_out_odd, w_mlp_up, w_mlp_down):
    batch, t, d = x.shape
    x2d = x.reshape(batch * t, d)
    group = d // 2
    kw = GLA_HEADS * GLA_DK

    w_in = w_in_even[0]
    o_ga = 2 * kw + 2 * group
    o_fq = o_ga + GLA_RANK
    o_ff = o_fq + 3 * group
    q_scale = LOG2E * HEAD_DIM ** -0.5
    w_main = jnp.concatenate(
        [w_in[:, :o_ga], w_in[:, o_fq:o_fq + group] * q_scale, w_in[:, o_fq + group:o_ff]],
        axis=1).astype(BF16)
    n_small = FOX_HEADS + GLA_RANK
    w_small = jnp.concatenate(
        [w_in[:, o_ff:], w_in[:, o_ga:o_fq], jnp.zeros((d, LANES - n_small), F32)], axis=1).astype(BF16)
    wa_pad = jnp.zeros((LANES, kw), F32).at[FOX_HEADS:n_small].set(gla_w_a_up[0]).astype(BF16)
    fox_bias = jnp.zeros((1, LANES), F32).at[0, :FOX_HEADS].set(fox_b_f[0])

    m = batch * t
    proj, small = _norm_proj(x2d, norm_w[0, 0], [(w_main, BF16), (w_small, F32)])
    out_a = _gla(proj, small, wa_pad, gla_b_a[0], gla_norm_w[0], batch, t)
    cum = _fox_cum(small, fox_bias, batch, t)
    out_b = _fox(proj, cum, batch, t, o_ga)
    x2d = _mix_mlp(out_a, out_b, x2d, w_out_even[0], norm_w[0, 1:4], w_mlp_up, w_mlp_down, 0)

    w_in = w_in_odd[0]
    w_main = jnp.concatenate([w_in[:, :group] * q_scale, w_in[:, group:]], axis=1).astype(BF16)
    (proj,) = _norm_proj(x2d, norm_w[1, 0], [(w_main, BF16)])
    out_c = _chunk_attn(proj, _chunk_attn_table(rel_bias[0]), batch, t)
    out_d = _lru(proj.reshape(batch, t, w_main.shape[1]), conv_w[0], conv_b[0],
                 _block_diag_halves(lru_w_a[0]), lru_b_a[0],
                 _block_diag_halves(lru_w_x[0]), lru_b_x[0], lru_lambda[0], 3 * group)
    x2d = _mix_mlp(out_c, out_d.reshape(m, group), x2d, w_out_odd[0], norm_w[1, 1:4],
                   w_mlp_up, w_mlp_down, 1)
    return x2d.reshape(batch, t, d)
```

```python
import functools

import jax
import jax.numpy as jnp
from jax import lax
from jax.experimental import pallas as pl
from jax.experimental.pallas import tpu as pltpu

F32 = jnp.float32
BF16 = jnp.bfloat16

NORM_EPS = 1e-6
CHUNK = 64
GLA_HEADS = 4
GLA_DK = 64
GLA_DV = 128
GLA_RANK = 16
GLA_GATE_TAU = 16.0
FOX_HEADS = 8
HEAD_DIM = 64
CA_LEFT_CHUNKS = 8
REL_CLIP = 128
CONV_WIDTH = 4
LRU_BLOCKS = 8
LRU_C = 8.0

LANES = 128
MXU_DIM = 256
MASK_VALUE = -1e30
VMEM_LIMIT_BYTES = 56 * 1024 * 1024

ROW_TILE = 512
PROJ_ROW_TILE = 1024
STAGE_ROWS = 512
STAGE_COLS = 1024
FOX_BLOCK = 512
ONES_ROWS = 16
CA_QBLOCK = 4 * CHUNK
CA_WINDOW = CA_QBLOCK + CA_LEFT_CHUNKS * CHUNK
CA_PAD = CA_LEFT_CHUNKS * CHUNK
CA_LINE = 1024
LOG2E = 1.4426950408889634
CUM_TERMS = 3
GLA_GROUP = 4
LRU_ROWS = 256
LRU_FRAMES = 256
LRU_PITCH = LRU_FRAMES + 8
CONV_HIST = 8
SCAN_UNROLL = 8


def _params(n_axes):
    return pltpu.CompilerParams(
        dimension_semantics=("arbitrary",) * n_axes,
        vmem_limit_bytes=VMEM_LIMIT_BYTES,
    )


def _const_spec(shape):
    nd = len(shape)
    return pl.BlockSpec(shape, lambda *_: (0,) * nd, pipeline_mode=pl.Buffered(1))


def _rmsnorm(x, w):
    y = x * lax.rsqrt(jnp.mean(x * x, axis=-1, keepdims=True) + NORM_EPS)
    return y * w


def _log_sigmoid(z):
    return jnp.minimum(z, 0.0) - jnp.log1p(jnp.exp(-jnp.abs(z)))


def _sigmoid(z):
    return 0.5 * jnp.tanh(0.5 * z) + 0.5


def _dot(a, b):
    return jnp.dot(a, b, preferred_element_type=F32)


def _dot_nt(a, b):
    return lax.dot_general(a, b, (((1,), (1,)), ((), ())), preferred_element_type=F32)


def _dot_tn(a, b):
    return lax.dot_general(a, b, (((0,), (0,)), ((), ())), preferred_element_type=F32)


def _half_mask(lane, j):
    return lane < HEAD_DIM if j == 0 else lane >= HEAD_DIM


def _split3(x):
    h1 = x.astype(BF16)
    r1 = x - h1.astype(F32)
    h2 = r1.astype(BF16)
    h3 = (r1 - h2.astype(F32)).astype(BF16)
    return h1, h2, h3


def _row_major(i):
    return (i, 0)


def _norm_proj_kernel(x_ref, nw_ref, *refs, n_chunk):
    n_out = len(refs) // 2
    h = _rmsnorm(x_ref[...], nw_ref[...]).astype(BF16)
    for w_ref, o_ref in zip(refs[:n_out], refs[n_out:]):
        n_total = o_ref.shape[1]
        for n0 in range(0, n_total, n_chunk):
            n1 = min(n0 + n_chunk, n_total)
            o_ref[:, n0:n1] = _dot(h, w_ref[:, n0:n1]).astype(o_ref.dtype)


def _norm_proj(x2d, nw, outputs):
    m, d = x2d.shape
    in_specs = [pl.BlockSpec((PROJ_ROW_TILE, d), _row_major), _const_spec((1, d))]
    in_specs += [_const_spec(w.shape) for w, _ in outputs]
    return pl.pallas_call(
        functools.partial(_norm_proj_kernel, n_chunk=2 * MXU_DIM),
        grid=(m // PROJ_ROW_TILE,),
        in_specs=in_specs,
        out_specs=[pl.BlockSpec((PROJ_ROW_TILE, w.shape[1]), _row_major) for w, _ in outputs],
        out_shape=[jax.ShapeDtypeStruct((m, w.shape[1]), dt) for w, dt in outputs],
        compiler_params=_params(1),
        name="norm_proj",
    )(x2d, nw.reshape(1, d), *[w for w, _ in outputs])


def _stage_bf16_weights(pairs, stage_ref, sem_ref):
    pieces = []
    for src, dst in pairs:
        rows, cols = dst.shape
        for r0 in range(0, rows, STAGE_ROWS):
            for c0 in range(0, cols, STAGE_COLS):
                pieces.append((src, dst, r0, c0))

    def copy(k):
        src, _, r0, c0 = pieces[k]
        return pltpu.make_async_copy(
            src.at[pl.ds(r0, STAGE_ROWS), pl.ds(c0, STAGE_COLS)], stage_ref.at[k % 2], sem_ref.at[k % 2])

    copy(0).start()
    for k, (_, dst, r0, c0) in enumerate(pieces):
        if k + 1 < len(pieces):
            copy(k + 1).start()
        copy(k).wait()
        dst[r0:r0 + STAGE_ROWS, c0:c0 + STAGE_COLS] = stage_ref[k % 2].astype(BF16)


def _mix_mlp_kernel(a_ref, b_ref, x_ref, nw_ref, wo_hbm, wu_hbm, wd_hbm, o_ref,
                    wo_ref, wu_ref, wd_ref, u_ref, y_ref, stage_ref, sem_ref, *, chunk, layer):
    @pl.when(pl.program_id(0) == 0)
    def _():
        _stage_bf16_weights([(wo_hbm, wo_ref), (wu_hbm.at[layer], wu_ref), (wd_hbm.at[layer], wd_ref)],
                            stage_ref, sem_ref)

    d_ff = wu_ref.shape[1]
    d = wd_ref.shape[1]
    mix = jnp.concatenate([a_ref[...], b_ref[...]], axis=1)
    for n0 in range(0, d, chunk):
        cols = slice(n0, n0 + chunk)
        y_ref[:, cols] = _dot(mix, wo_ref[:, cols])
    x1 = x_ref[...] + _rmsnorm(y_ref[...], nw_ref[0:1, :])
    h = _rmsnorm(x1, nw_ref[1:2, :]).astype(BF16)
    for f0 in range(0, d_ff, chunk):
        u = jnp.maximum(_dot(h, wu_ref[:, f0:f0 + chunk]), 0.0)
        u_ref[:, f0:f0 + chunk] = (u * u).astype(BF16)
    for n0 in range(0, d, chunk):
        y_ref[:, n0:n0 + chunk] = _dot(u_ref[...], wd_ref[:, n0:n0 + chunk])
    o_ref[...] = x1 + _rmsnorm(y_ref[...], nw_ref[2:3, :])


def _mix_mlp(mix_a, mix_b, x2d, w_out, nw3, w_up_all, w_down_all, layer):
    m, d = x2d.shape
    g = mix_a.shape[1]
    d_ff = w_up_all.shape[2]
    return pl.pallas_call(
        functools.partial(_mix_mlp_kernel, chunk=2 * MXU_DIM, layer=layer),
        grid=(m // ROW_TILE,),
        in_specs=[
            pl.BlockSpec((ROW_TILE, g), _row_major),
            pl.BlockSpec((ROW_TILE, g), _row_major),
            pl.BlockSpec((ROW_TILE, d), _row_major),
            _const_spec((3, d)),
            pl.BlockSpec(memory_space=pl.ANY),
            pl.BlockSpec(memory_space=pl.ANY),
            pl.BlockSpec(memory_space=pl.ANY),
        ],
        out_specs=pl.BlockSpec((ROW_TILE, d), _row_major),
        out_shape=jax.ShapeDtypeStruct((m, d), F32),
        scratch_shapes=[
            pltpu.VMEM((2 * g, d), BF16),
            pltpu.VMEM((d, d_ff), BF16),
            pltpu.VMEM((d_ff, d), BF16),
            pltpu.VMEM((ROW_TILE, d_ff), BF16),
            pltpu.VMEM((ROW_TILE, d), F32),
            pltpu.VMEM((2, STAGE_ROWS, STAGE_COLS), F32),
            pltpu.SemaphoreType.DMA((2,)),
        ],
        compiler_params=_params(1),
        name="mix_mlp",
    )(mix_a, mix_b, x2d, nw3, w_out, w_up_all, w_down_all)


def _gla_kernel(q_ref, k_ref, v_ref, r_ref, sm_ref, wa_ref, ba_ref, nw_ref, o_ref,
                la_ref, st_ref, dec_ref):
    t = q_ref.shape[0]
    nc = t // CHUNK
    n_pairs = GLA_HEADS // 2

    z = _dot(sm_ref[...].astype(BF16), wa_ref[...]) + ba_ref[...]
    la_ref[...] = _log_sigmoid(z) * (1.0 / GLA_GATE_TAU)

    grp = GLA_GROUP * CHUNK
    row = lax.broadcasted_iota(jnp.int32, (grp, grp), 0)
    col = lax.broadcasted_iota(jnp.int32, (grp, grp), 1)
    tri = jnp.where((row >= col) & (row // CHUNK == col // CHUNK), 1.0, 0.0).astype(BF16)
    erow = lax.broadcasted_iota(jnp.int32, (grp, GLA_GROUP * LANES), 0)
    ecol = lax.broadcasted_iota(jnp.int32, (grp, GLA_GROUP * LANES), 1)
    chunk_ones = jnp.where(erow // CHUNK == ecol // LANES, 1.0, 0.0).astype(BF16)

    def increments(g, carry):
        r0 = pl.multiple_of(g * grp, grp)
        la = la_ref[pl.ds(r0, grp), :]
        hi = la.astype(BF16)
        lo = (la - hi.astype(F32)).astype(BF16)
        cum = _dot(tri, hi) + _dot(tri, lo)
        dec = jnp.exp(_dot_tn(hi, chunk_ones) + _dot_tn(lo, chunk_ones))
        k_g = k_ref[pl.ds(r0, grp), :].astype(F32)
        v_g = v_ref[pl.ds(r0, grp), :]
        k_dec = []
        for c in range(GLA_GROUP):
            rows = slice(c * CHUNK, (c + 1) * CHUNK)
            total = cum[(c + 1) * CHUNK - 1:(c + 1) * CHUNK, :]
            k_dec.append((k_g[rows] * jnp.exp(total - cum[rows])).astype(BF16))
        for c in range(GLA_GROUP):
            rows = slice(c * CHUNK, (c + 1) * CHUNK)
            dec_ref[g * GLA_GROUP + c] = dec[:, c * LANES:(c + 1) * LANES]
            for h in range(GLA_HEADS):
                p, j = divmod(h, 2)
                full = _dot_tn(k_dec[c][:, p * LANES:(p + 1) * LANES], v_g[rows, h * GLA_DV:(h + 1) * GLA_DV])
                st_ref[g * GLA_GROUP + c, p, j * GLA_DK:(j + 1) * GLA_DK, :] = full[j * GLA_DK:(j + 1) * GLA_DK, :]
        return carry

    lax.fori_loop(0, nc // GLA_GROUP, increments, 0)

    def scan(c, carry):
        for p in range(n_pairs):
            st_ref[c, p] = st_ref[c, p] + dec_ref[c, p * LANES:(p + 1) * LANES, :] * st_ref[c - 1, p]
        return carry

    lax.fori_loop(1, nc, scan, 0)

    lane = lax.broadcasted_iota(jnp.int32, (CHUNK, LANES), 1)
    scale = GLA_DK ** -0.5

    def outputs(g, carry):
        r0 = pl.multiple_of(g * grp, grp)
        q_g = q_ref[pl.ds(r0, grp), :] * jnp.asarray(scale, BF16)
        raw = []
        for c in range(GLA_GROUP):
            rows = slice(c * CHUNK, (c + 1) * CHUNK)
            for h in range(GLA_HEADS):
                p, j = divmod(h, 2)
                q_pair = q_g[rows, p * LANES:(p + 1) * LANES]
                q_h = jnp.where(_half_mask(lane, j), q_pair, jnp.zeros_like(q_pair))
                raw.append(_dot(q_h, st_ref[g * GLA_GROUP + c, p].astype(BF16)))
        for c in range(GLA_GROUP):
            for h in range(GLA_HEADS):
                o = raw[c * GLA_HEADS + h]
                o = o * lax.rsqrt(jnp.mean(o * o, axis=-1, keepdims=True) + NORM_EPS)
                o = o * nw_ref[:, h * GLA_DV:(h + 1) * GLA_DV]
                rows = pl.ds(r0 + c * CHUNK, CHUNK)
                r_h = r_ref[rows, h * GLA_DV:(h + 1) * GLA_DV].astype(F32)
                o_ref[rows, h * GLA_DV:(h + 1) * GLA_DV] = (o * (r_h * _sigmoid(r_h))).astype(BF16)
        return carry

    lax.fori_loop(0, nc // GLA_GROUP, outputs, 0)


def _gla(proj, small, wa_pad, b_a, norm_w, batch, t):
    kw = GLA_HEADS * GLA_DK
    gw = GLA_HEADS * GLA_DV
    nc = t // CHUNK
    return pl.pallas_call(
        _gla_kernel,
        grid=(batch,),
        in_specs=[
            pl.BlockSpec((t, kw), lambda b: (b, 0)),
            pl.BlockSpec((t, kw), lambda b: (b, 1)),
            pl.BlockSpec((t, gw), lambda b: (b, 1)),
            pl.BlockSpec((t, gw), lambda b: (b, 2)),
            pl.BlockSpec((t, LANES), lambda b: (b, 0)),
            _const_spec((LANES, kw)),
            _const_spec((1, kw)),
            _const_spec((1, gw)),
        ],
        out_specs=pl.BlockSpec((t, gw), lambda b: (b, 0)),
        out_shape=jax.ShapeDtypeStruct((batch * t, gw), BF16),
        scratch_shapes=[
            pltpu.VMEM((t, kw), F32),
            pltpu.VMEM((nc, GLA_HEADS // 2, LANES, GLA_DV), F32),
            pltpu.VMEM((nc, kw, LANES), F32),
        ],
        compiler_params=_params(1),
        name="gla",
    )(proj, proj, proj, proj, small, wa_pad, b_a.reshape(1, kw), norm_w.reshape(1, gw))


def _fox_cum_kernel(sm_ref, bias_ref, o_ref):
    t = sm_ref.shape[0]
    row = lax.broadcasted_iota(jnp.int32, (LANES, LANES), 0)
    col = lax.broadcasted_iota(jnp.int32, (LANES, LANES), 1)
    tri = jnp.where(row >= col, 1.0, 0.0).astype(BF16)
    carry = jnp.zeros((1, LANES), F32)
    for blk in range(t // LANES):
        rows = slice(blk * LANES, (blk + 1) * LANES)
        ls = _log_sigmoid(sm_ref[rows, :] + bias_ref[...])
        h1, h2, h3 = _split3(ls)
        cb = _dot(tri, h1) + _dot(tri, h2) + _dot(tri, h3) + carry
        o_ref[rows, :] = cb
        carry = cb[LANES - 1:LANES, :]


def _fox_cum(small, bias_row, batch, t):
    return pl.pallas_call(
        _fox_cum_kernel,
        grid=(batch,),
        in_specs=[pl.BlockSpec((t, LANES), lambda b: (b, 0)), _const_spec((1, LANES))],
        out_specs=pl.BlockSpec((t, LANES), lambda b: (b, 0)),
        out_shape=jax.ShapeDtypeStruct((batch * t, LANES), F32),
        compiler_params=_params(1),
        name="fox_cum",
    )(small, bias_row)


def _fox_kernel(q_ref, k_ref, v_ref, c_ref, o_ref, ka_ref, vt_ref, s_ref, mb_ref):
    t = k_ref.shape[0]
    blk = FOX_BLOCK
    nkv = t // blk
    pair = pl.program_id(1)

    sel_r = lax.broadcasted_iota(jnp.int32, (LANES, LANES), 0)
    sel_c = lax.broadcasted_iota(jnp.int32, (LANES, LANES), 1)
    sels = []
    for term in range(CUM_TERMS):
        hit = (((sel_r == 2 * pair) & (sel_c == term))
               | ((sel_r == 2 * pair + 1) & (sel_c == CUM_TERMS + term)))
        sels.append(jnp.where(hit, 1.0, 0.0).astype(BF16))
    for kb in range(nkv):
        rows = slice(kb * blk, (kb + 1) * blk)
        v_t = v_ref[rows, :].astype(F32).T.astype(BF16)
        for j in range(2):
            vt_ref[kb, j, 0:HEAD_DIM, :] = v_t[j * HEAD_DIM:(j + 1) * HEAD_DIM, :]
            vt_ref[kb, j, HEAD_DIM:HEAD_DIM + ONES_ROWS, :] = jnp.ones((ONES_ROWS, blk), BF16)
        parts = _split3(c_ref[rows, :] * (-LOG2E))
        extra = _dot(parts[0], sels[0]) + _dot(parts[1], sels[1]) + _dot(parts[2], sels[2])
        ka_ref[kb, :, 0:LANES] = k_ref[rows, :]
        ka_ref[kb, :, LANES:2 * LANES] = extra.astype(BF16)

    lane = lax.broadcasted_iota(jnp.int32, (blk, LANES), 1)
    krow = lax.broadcasted_iota(jnp.int32, (blk, blk), 0)
    qcol = lax.broadcasted_iota(jnp.int32, (blk, blk), 1)
    causal = krow <= qcol

    def augmented_queries(qi):
        q = q_ref[qi * blk:(qi + 1) * blk, :]
        q_aug = []
        for j in range(2):
            q_h = jnp.where(_half_mask(lane, j), q, jnp.zeros_like(q))
            ones = jnp.where((lane >= CUM_TERMS * j) & (lane < CUM_TERMS * (j + 1)), 1.0, 0.0).astype(BF16)
            q_aug.append(jnp.concatenate([q_h, ones], axis=1).astype(F32).T.astype(BF16))
        return q_aug

    def scores(q_aug, kb, slot, masked):
        k_blk = ka_ref[kb]
        for j in range(2):
            s = _dot(k_blk, q_aug[j])
            if masked:
                s = jnp.where(causal, s, MASK_VALUE)
            s_ref[slot, j] = s
            mb_ref[slot, j] = jnp.max(s, axis=0, keepdims=True)

    def accumulate(kb, slot, state):
        new = []
        for j in range(2):
            m, acc = state[j]
            m_new = jnp.maximum(m, mb_ref[slot, j])
            alpha = jnp.exp2(m - m_new)
            p = jnp.exp2((s_ref[slot, j] - m_new).astype(BF16))
            new.append((m_new, acc * alpha + _dot(vt_ref[kb, j], p)))
        return new

    tasks = [(qi, kb) for qi in range(nkv) for kb in range(qi + 1)]
    fresh = [(jnp.full((1, blk), MASK_VALUE, F32), jnp.zeros((HEAD_DIM + ONES_ROWS, blk), F32))] * 2
    q_aug = augmented_queries(0)
    scores(q_aug, 0, 0, masked=True)
    state = fresh
    for n, (qi, kb) in enumerate(tasks):
        if n + 1 < len(tasks):
            qi_next, kb_next = tasks[n + 1]
            if qi_next != qi:
                q_aug = augmented_queries(qi_next)
            scores(q_aug, kb_next, (n + 1) % 2, masked=(kb_next == qi_next))
        state = accumulate(kb, n % 2, state)
        if kb == qi:
            outs = [acc[0:HEAD_DIM] * (1.0 / acc[HEAD_DIM:HEAD_DIM + 1]) for _, acc in state]
            o_ref[qi * blk:(qi + 1) * blk, :] = jnp.concatenate(outs, axis=0).T.astype(BF16)
            state = fresh


def _fox(proj, cum, batch, t, col0):
    nq = t // FOX_BLOCK
    n_pairs = FOX_HEADS // 2
    qc, kc, vc = col0 // LANES, col0 // LANES + n_pairs, col0 // LANES + 2 * n_pairs
    return pl.pallas_call(
        _fox_kernel,
        grid=(batch, n_pairs),
        in_specs=[
            pl.BlockSpec((t, LANES), lambda b, p: (b, qc + p)),
            pl.BlockSpec((t, LANES), lambda b, p: (b, kc + p)),
            pl.BlockSpec((t, LANES), lambda b, p: (b, vc + p)),
            pl.BlockSpec((t, LANES), lambda b, p: (b, 0)),
        ],
        out_specs=pl.BlockSpec((t, LANES), lambda b, p: (b, p)),
        out_shape=jax.ShapeDtypeStruct((batch * t, FOX_HEADS * HEAD_DIM), BF16),
        scratch_shapes=[
            pltpu.VMEM((nq, FOX_BLOCK, 2 * LANES), BF16),
            pltpu.VMEM((nq, 2, HEAD_DIM + ONES_ROWS, FOX_BLOCK), BF16),
            pltpu.VMEM((2, 2, FOX_BLOCK, FOX_BLOCK), F32),
            pltpu.VMEM((2, 2, 1, FOX_BLOCK), F32),
        ],
        compiler_params=_params(2),
        name="fox",
    )(proj, proj, proj, cum)


def _chunk_attn_kernel(q_ref, k_ref, v_ref, tab_ref, o_ref, kp_ref, vt_ref, s_ref):
    t, g = q_ref.shape
    qb = CA_QBLOCK
    n_pad = CA_PAD // qb
    n_win = CA_WINDOW // qb

    for kb in range(n_pad):
        kp_ref[kb] = jnp.zeros((qb, g), BF16)
        vt_ref[kb] = jnp.zeros(vt_ref.shape[1:], BF16)
    for kb in range(t // qb):
        rows = slice(kb * qb, (kb + 1) * qb)
        kp_ref[n_pad + kb] = k_ref[rows, :]
        v_t = v_ref[rows, :].astype(F32).T.astype(BF16)
        for h in range(g // HEAD_DIM):
            vt_ref[n_pad + kb, h, 0:HEAD_DIM, :] = v_t[h * HEAD_DIM:(h + 1) * HEAD_DIM, :]
            vt_ref[n_pad + kb, h, HEAD_DIM:HEAD_DIM + ONES_ROWS, :] = jnp.ones((ONES_ROWS, qb), BF16)

    lane = lax.broadcasted_iota(jnp.int32, (qb, LANES), 1)
    n_heads = g // HEAD_DIM
    lax.fori_loop(0, t // qb, functools.partial(
        _chunk_attn_block, q_ref=q_ref, tab_ref=tab_ref, o_ref=o_ref, kp_ref=kp_ref, vt_ref=vt_ref,
        s_ref=s_ref, lane=lane, n_heads=n_heads, n_pad=n_pad, n_win=n_win), 0)


def _chunk_attn_block(i, carry, *, q_ref, tab_ref, o_ref, kp_ref, vt_ref, s_ref, lane, n_heads,
                      n_pad, n_win):
    qb = CA_QBLOCK
    q_rows = pl.ds(pl.multiple_of(i * qb, qb), qb)

    def scores(h):
        pair, j = divmod(h, 2)
        cols = slice(pair * LANES, (pair + 1) * LANES)
        q_pair = q_ref[q_rows, cols]
        q_h = jnp.where(_half_mask(lane, j), q_pair, jnp.zeros_like(q_pair))
        q_ht = q_h.astype(F32).T.astype(BF16)
        m = None
        for w in range(n_win):
            tab_blk = jnp.where(i + w >= n_pad, w, n_win)
            s = _dot(kp_ref[i + w, :, cols], q_ht) + tab_ref[h, tab_blk]
            s_ref[h % 2, w] = s
            m_w = s.max(axis=0, keepdims=True)
            m = m_w if m is None else jnp.maximum(m, m_w)
        return m

    def weighted_values(h, m):
        acc = jnp.zeros((HEAD_DIM + ONES_ROWS, qb), F32)
        for w in range(n_win):
            p = jnp.exp2((s_ref[h % 2, w] - m).astype(BF16))
            acc = acc + _dot(vt_ref[i + w, h], p)
        return acc[0:HEAD_DIM] * (1.0 / acc[HEAD_DIM:HEAD_DIM + 1])

    outs = []
    pending = scores(0)
    for h in range(n_heads):
        upcoming = scores(h + 1) if h + 1 < n_heads else None
        outs.append(weighted_values(h, pending))
        pending = upcoming
    o_ref[q_rows, :] = jnp.concatenate(outs, axis=0).T.astype(BF16)
    return carry


def _chunk_attn(proj, table, batch, t):
    g = table.shape[0] * HEAD_DIM
    n_blocks = (CA_PAD + t) // CA_QBLOCK
    return pl.pallas_call(
        _chunk_attn_kernel,
        grid=(batch,),
        in_specs=[
            pl.BlockSpec((t, g), lambda b: (b, 0)),
            pl.BlockSpec((t, g), lambda b: (b, 1)),
            pl.BlockSpec((t, g), lambda b: (b, 2)),
            _const_spec(table.shape),
        ],
        out_specs=pl.BlockSpec((t, g), lambda b: (b, 0)),
        out_shape=jax.ShapeDtypeStruct((batch * t, g), BF16),
        scratch_shapes=[
            pltpu.VMEM((n_blocks, CA_QBLOCK, g), BF16),
            pltpu.VMEM((n_blocks, g // HEAD_DIM, HEAD_DIM + ONES_ROWS, CA_QBLOCK), BF16),
            pltpu.VMEM((2, CA_WINDOW // CA_QBLOCK, CA_QBLOCK, CA_QBLOCK), F32),
        ],
        compiler_params=_params(1),
        name="chunk_attn",
    )(proj, proj, proj, table)


def _ca_table_kernel(line_ref, o_ref):
    n_win = CA_WINDOW // CA_QBLOCK
    rows = 8
    x = jnp.broadcast_to(line_ref[0] * LOG2E, (rows, CA_LINE))
    qc = lax.broadcasted_iota(jnp.int32, (rows, CA_QBLOCK), 1) // CHUNK
    for grp in range(CA_WINDOW // rows):
        r0 = grp * rows
        y = pltpu.roll(x, (CA_LINE - CA_WINDOW + 1 + r0) % CA_LINE, axis=1, stride=1, stride_axis=0)
        kc = r0 // CHUNK
        allowed = (qc <= kc) & (qc >= kc - CA_LEFT_CHUNKS)
        w, r = divmod(r0, CA_QBLOCK)
        o_ref[0, w, r:r + rows, :] = jnp.where(allowed, y[:, 0:CA_QBLOCK], MASK_VALUE)
    o_ref[0, n_win] = jnp.full((CA_QBLOCK, CA_QBLOCK), MASK_VALUE, F32)


def _chunk_attn_table(rel_bias):
    heads = rel_bias.shape[0]
    n_win = CA_WINDOW // CA_QBLOCK
    left = CA_WINDOW - 1 - CA_PAD - REL_CLIP
    line = jnp.pad(rel_bias.astype(F32), ((0, 0), (left, CA_LINE - left - rel_bias.shape[1])), mode="edge")
    return pl.pallas_call(
        _ca_table_kernel,
        grid=(heads,),
        in_specs=[pl.BlockSpec((1, 1, CA_LINE), lambda h: (h, 0, 0))],
        out_specs=pl.BlockSpec((1, n_win + 1, CA_QBLOCK, CA_QBLOCK), lambda h: (h, 0, 0, 0)),
        out_shape=jax.ShapeDtypeStruct((heads, n_win + 1, CA_QBLOCK, CA_QBLOCK), F32),
        compiler_params=_params(1),
        name="ca_table",
    )(line.reshape(heads, 1, CA_LINE))


GELU_C0 = 0.7978845608028654
GELU_C1 = GELU_C0 * 0.044715


def _gelu(x):
    inner = x * (GELU_C0 + GELU_C1 * (x * x))
    return (0.5 * x) * (1.0 + jnp.tanh(inner))


def _lru_kernel(g_ref, x_ref, cw_ref, cb_ref, wa_ref, ba_ref, wx_ref, bx_ref, lam_ref, o_ref,
                xin_ref, xf_ref, a_ref, b_ref, hout_ref, h_ref):
    nb, frames, w = x_ref.shape
    n_slab = w // LANES
    sub = LRU_ROWS // nb
    i = pl.program_id(0)

    @pl.when(i == 0)
    def _():
        xf_ref[0:CONV_HIST] = jnp.zeros((CONV_HIST, nb, w), F32)
        h_ref[...] = jnp.zeros((nb, w), F32)

    for b in range(nb):
        xb = x_ref[b].astype(F32)
        for s in range(n_slab):
            xin_ref[s, b * LRU_PITCH:b * LRU_PITCH + frames, :] = xb[:, s * LANES:(s + 1) * LANES]

    def gather(t, carry):
        for s in range(n_slab):
            xf_ref[CONV_HIST + t, :, s * LANES:(s + 1) * LANES] = (
                xin_ref[s, pl.ds(t, nb, stride=LRU_PITCH), :])
        return carry

    lax.fori_loop(0, frames, gather, 0, unroll=SCAN_UNROLL)

    log2_base = (LRU_C * LOG2E) * _log_sigmoid(lam_ref[...])
    half = w // 2
    for blk in range(frames // sub):
        t0 = blk * sub
        xc = cb_ref[...]
        for tap in range(CONV_WIDTH):
            lo = CONV_HIST + t0 - tap
            xc = xc + (cw_ref[CONV_WIDTH - 1 - tap:CONV_WIDTH - tap, :]
                       * xf_ref[lo:lo + sub].reshape(LRU_ROWS, w))
        xcb = xc.astype(BF16)
        gr, gi = [], []
        for hb in range(2):
            cols = slice(hb * half, (hb + 1) * half)
            gr.append(_dot(xcb[:, cols], wa_ref[hb]))
            gi.append(_dot(xcb[:, cols], wx_ref[hb]))
        r = _sigmoid(jnp.concatenate(gr, axis=1) + ba_ref[...])
        gate_i = _sigmoid(jnp.concatenate(gi, axis=1) + bx_ref[...])
        a = jnp.exp2(r * log2_base)
        a_ref[t0:t0 + sub] = a.reshape(sub, nb, w)
        y = 1.0 - a * a
        root = jnp.where(y > 0.0, y * lax.rsqrt(y), 0.0)
        b_ref[t0:t0 + sub] = (root * (gate_i * xc)).reshape(sub, nb, w)

    xf_ref[0:CONV_HIST] = xf_ref[frames:frames + CONV_HIST]

    def scan(t, h):
        h = a_ref[t] * h + b_ref[t]
        for s in range(n_slab):
            hout_ref[s, pl.ds(t, nb, stride=LRU_PITCH), :] = h[:, s * LANES:(s + 1) * LANES]
        return h

    h_ref[...] = lax.fori_loop(0, frames, scan, h_ref[...], unroll=SCAN_UNROLL)

    for b in range(nb):
        rows = slice(b * LRU_PITCH, b * LRU_PITCH + frames)
        hv = jnp.concatenate([hout_ref[s, rows, :] for s in range(n_slab)], axis=1)
        o_ref[b] = (hv * _gelu(g_ref[b].astype(F32))).astype(BF16)


def _lru(proj3, conv_w, conv_b, wa_bd, b_a, wx_bd, b_x, lam, col0):
    nb, t, _ = proj3.shape
    w = conv_w.shape[1]
    gc = col0 // w
    n_slab = w // LANES
    return pl.pallas_call(
        _lru_kernel,
        grid=(t // LRU_FRAMES,),
        in_specs=[
            pl.BlockSpec((nb, LRU_FRAMES, w), lambda i: (0, i, gc)),
            pl.BlockSpec((nb, LRU_FRAMES, w), lambda i: (0, i, gc + 1)),
            _const_spec((CONV_WIDTH, w)),
            _const_spec((1, w)),
            _const_spec(wa_bd.shape),
            _const_spec((1, w)),
            _const_spec(wx_bd.shape),
            _const_spec((1, w)),
            _const_spec((1, w)),
        ],
        out_specs=pl.BlockSpec((nb, LRU_FRAMES, w), lambda i: (0, i, 0)),
        out_shape=jax.ShapeDtypeStruct((nb, t, w), BF16),
        scratch_shapes=[
            pltpu.VMEM((n_slab, nb * LRU_PITCH, LANES), F32),
            pltpu.VMEM((CONV_HIST + LRU_FRAMES, nb, w), F32),
            pltpu.VMEM((LRU_FRAMES, nb, w), F32),
            pltpu.VMEM((LRU_FRAMES, nb, w), F32),
            pltpu.VMEM((n_slab, nb * LRU_PITCH, LANES), F32),
            pltpu.VMEM((nb, w), F32),
        ],
        compiler_params=_params(1),
        name="rglru",
    )(proj3, proj3, conv_w, conv_b.reshape(1, w), wa_bd, b_a.reshape(1, w), wx_bd,
      b_x.reshape(1, w), lam.reshape(1, w))


def _block_diag_halves(wblk):
    nb, d, _ = wblk.shape
    per = nb // 2
    eye = jnp.eye(per, dtype=wblk.dtype)
    halves = [jnp.einsum("nde,nm->ndme", wblk[h * per:(h + 1) * per], eye).reshape(per * d, per * d)
              for h in range(2)]
    return jnp.stack(halves).astype(BF16)


def kernel(x, norm_w, w_in_even, gla_w_a_up, gla_b_a, gla_norm_w, fox_b_f, w_out_even,
           w_in_odd, rel_bias, conv_w, conv_b, lru_w_a, lru_b_a, lru_w_x, lru_b_x,
           lru_lambda, w_out_odd, w_mlp_up, w_mlp_down):
    batch, t, d = x.shape
    x2d = x.reshape(batch * t, d)
    group = d // 2
    kw = GLA_HEADS * GLA_DK

    w_in = w_in_even[0]
    o_ga = 2 * kw + 2 * group
    o_fq = o_ga + GLA_RANK
    o_ff = o_fq + 3 * group
    q_scale = LOG2E * HEAD_DIM ** -0.5
    w_main = jnp.concatenate(
        [w_in[:, :o_ga], w_in[:, o_fq:o_fq + group] * q_scale, w_in[:, o_fq + group:o_ff]],
        axis=1).astype(BF16)
    n_small = FOX_HEADS + GLA_RANK
    w_small = jnp.concatenate(
        [w_in[:, o_ff:], w_in[:, o_ga:o_fq], jnp.zeros((d, LANES - n_small), F32)], axis=1).astype(BF16)
    wa_pad = jnp.zeros((LANES, kw), F32).at[FOX_HEADS:n_small].set(gla_w_a_up[0]).astype(BF16)
    fox_bias = jnp.zeros((1, LANES), F32).at[0, :FOX_HEADS].set(fox_b_f[0])

    m = batch * t
    proj, small = _norm_proj(x2d, norm_w[0, 0], [(w_main, BF16), (w_small, F32)])
    out_a = _gla(proj, small, wa_pad, gla_b_a[0], gla_norm_w[0], batch, t)
    cum = _fox_cum(small, fox_bias, batch, t)
    out_b = _fox(proj, cum, batch, t, o_ga)
    x2d = _mix_mlp(out_a, out_b, x2d, w_out_even[0], norm_w[0, 1:4], w_mlp_up, w_mlp_down, 0)

    w_in = w_in_odd[0]
    w_main = jnp.concatenate([w_in[:, :group] * q_scale, w_in[:, group:]], axis=1).astype(BF16)
    (proj,) = _norm_proj(x2d, norm_w[1, 0], [(w_main, BF16)])
    out_c = _chunk_attn(proj, _chunk_attn_table(rel_bias[0]), batch, t)
    out_d = _lru(proj.reshape(batch, t, w_main.shape[1]), conv_w[0], conv_b[0],
                 _block_diag_halves(lru_w_a[0]), lru_b_a[0],
                 _block_diag_halves(lru_w_x[0]), lru_b_x[0], lru_lambda[0], 3 * group)
    x2d = _mix_mlp(out_c, out_d.reshape(m, group), x2d, w_out_odd[0], norm_w[1, 1:4],
                   w_mlp_up, w_mlp_down, 1)
    return x2d.reshape(batch, t, d)
```

```python
import functools

import jax
import jax.numpy as jnp
from jax import lax
from jax.experimental import pallas as pl
from jax.experimental.pallas import tpu as pltpu

F32 = jnp.float32
BF16 = jnp.bfloat16

NORM_EPS = 1e-6
CHUNK = 64
GLA_HEADS = 4
GLA_DK = 64
GLA_DV = 128
GLA_RANK = 16
GLA_GATE_TAU = 16.0
FOX_HEADS = 8
HEAD_DIM = 64
CA_LEFT_CHUNKS = 8
REL_CLIP = 128
CONV_WIDTH = 4
LRU_BLOCKS = 8
LRU_C = 8.0

LANES = 128
MXU_DIM = 256
MASK_VALUE = -1e30
VMEM_LIMIT_BYTES = 56 * 1024 * 1024

ROW_TILE = 512
PROJ_ROW_TILE = 1024
STAGE_ROWS = 512
STAGE_COLS = 1024
FOX_BLOCK = 512
ONES_ROWS = 16
CA_QBLOCK = 4 * CHUNK
CA_WINDOW = CA_QBLOCK + CA_LEFT_CHUNKS * CHUNK
CA_PAD = CA_LEFT_CHUNKS * CHUNK
CA_LINE = 1024
LOG2E = 1.4426950408889634
CUM_TERMS = 3
GLA_GROUP = 4
LRU_ROWS = 256
LRU_FRAMES = 256
LRU_PITCH = LRU_FRAMES + 8
CONV_HIST = 8
SCAN_UNROLL = 8


def _params(n_axes):
    return pltpu.CompilerParams(
        dimension_semantics=("arbitrary",) * n_axes,
        vmem_limit_bytes=VMEM_LIMIT_BYTES,
    )


def _const_spec(shape):
    nd = len(shape)
    return pl.BlockSpec(shape, lambda *_: (0,) * nd, pipeline_mode=pl.Buffered(1))


def _rmsnorm(x, w):
    y = x * lax.rsqrt(jnp.mean(x * x, axis=-1, keepdims=True) + NORM_EPS)
    return y * w


def _log_sigmoid(z):
    return jnp.minimum(z, 0.0) - jnp.log1p(jnp.exp(-jnp.abs(z)))


def _sigmoid(z):
    return 0.5 * jnp.tanh(0.5 * z) + 0.5


def _dot(a, b):
    return jnp.dot(a, b, preferred_element_type=F32)


def _dot_nt(a, b):
    return lax.dot_general(a, b, (((1,), (1,)), ((), ())), preferred_element_type=F32)


def _dot_tn(a, b):
    return lax.dot_general(a, b, (((0,), (0,)), ((), ())), preferred_element_type=F32)


def _half_mask(lane, j):
    return lane < HEAD_DIM if j == 0 else lane >= HEAD_DIM


def _split3(x):
    h1 = x.astype(BF16)
    r1 = x - h1.astype(F32)
    h2 = r1.astype(BF16)
    h3 = (r1 - h2.astype(F32)).astype(BF16)
    return h1, h2, h3


def _row_major(i):
    return (i, 0)


def _norm_proj_kernel(x_ref, nw_ref, *refs, n_chunk):
    n_out = len(refs) // 2
    h = _rmsnorm(x_ref[...], nw_ref[...]).astype(BF16)
    for w_ref, o_ref in zip(refs[:n_out], refs[n_out:]):
        n_total = o_ref.shape[1]
        for n0 in range(0, n_total, n_chunk):
            n1 = min(n0 + n_chunk, n_total)
            o_ref[:, n0:n1] = _dot(h, w_ref[:, n0:n1]).astype(o_ref.dtype)


def _norm_proj(x2d, nw, outputs):
    m, d = x2d.shape
    in_specs = [pl.BlockSpec((PROJ_ROW_TILE, d), _row_major), _const_spec((1, d))]
    in_specs += [_const_spec(w.shape) for w, _ in outputs]
    return pl.pallas_call(
        functools.partial(_norm_proj_kernel, n_chunk=2 * MXU_DIM),
        grid=(m // PROJ_ROW_TILE,),
        in_specs=in_specs,
        out_specs=[pl.BlockSpec((PROJ_ROW_TILE, w.shape[1]), _row_major) for w, _ in outputs],
        out_shape=[jax.ShapeDtypeStruct((m, w.shape[1]), dt) for w, dt in outputs],
        compiler_params=_params(1),
        name="norm_proj",
    )(x2d, nw.reshape(1, d), *[w for w, _ in outputs])


def _stage_bf16_weights(pairs, stage_ref, sem_ref):
    pieces = []
    for src, dst in pairs:
        rows, cols = dst.shape
        for r0 in range(0, rows, STAGE_ROWS):
            for c0 in range(0, cols, STAGE_COLS):
                pieces.append((src, dst, r0, c0))

    def copy(k):
        src, _, r0, c0 = pieces[k]
        return pltpu.make_async_copy(
            src.at[pl.ds(r0, STAGE_ROWS), pl.ds(c0, STAGE_COLS)], stage_ref.at[k % 2], sem_ref.at[k % 2])

    copy(0).start()
    for k, (_, dst, r0, c0) in enumerate(pieces):
        if k + 1 < len(pieces):
            copy(k + 1).start()
        copy(k).wait()
        dst[r0:r0 + STAGE_ROWS, c0:c0 + STAGE_COLS] = stage_ref[k % 2].astype(BF16)


def _mix_mlp_kernel(a_ref, b_ref, x_ref, nw_ref, wo_hbm, wu_hbm, wd_hbm, o_ref,
                    wo_ref, wu_ref, wd_ref, u_ref, y_ref, x1_ref, stage_ref, sem_ref, *, chunk, layer):
    @pl.when(pl.program_id(0) == 0)
    def _():
        _stage_bf16_weights([(wo_hbm, wo_ref), (wu_hbm.at[layer], wu_ref), (wd_hbm.at[layer], wd_ref)],
                            stage_ref, sem_ref)

    d_ff = wu_ref.shape[1]
    d = wd_ref.shape[1]
    tm = x_ref.shape[0]
    halves = [slice(0, tm // 2), slice(tm // 2, tm)]

    def out_proj(rows):
        mix = jnp.concatenate([a_ref[rows, :], b_ref[rows, :]], axis=1)
        for n0 in range(0, d, chunk):
            y_ref[rows, n0:n0 + chunk] = _dot(mix, wo_ref[:, n0:n0 + chunk])

    def residual_and_norm(rows):
        x1 = x_ref[rows, :] + _rmsnorm(y_ref[rows, :], nw_ref[0:1, :])
        x1_ref[rows, :] = x1
        return _rmsnorm(x1, nw_ref[1:2, :]).astype(BF16)

    def up_proj(rows, h):
        for f0 in range(0, d_ff, chunk):
            u = jnp.maximum(_dot(h, wu_ref[:, f0:f0 + chunk]), 0.0)
            u_ref[rows, f0:f0 + chunk] = (u * u).astype(BF16)

    def down_proj(rows):
        for n0 in range(0, d, chunk):
            y_ref[rows, n0:n0 + chunk] = _dot(u_ref[rows, :], wd_ref[:, n0:n0 + chunk])

    def finish(rows):
        o_ref[rows, :] = x1_ref[rows, :] + _rmsnorm(y_ref[rows, :], nw_ref[2:3, :])

    first, second = halves
    out_proj(first)
    out_proj(second)
    h_first = residual_and_norm(first)
    up_proj(first, h_first)
    h_second = residual_and_norm(second)
    up_proj(second, h_second)
    down_proj(first)
    down_proj(second)
    finish(first)
    finish(second)


def _mix_mlp(mix_a, mix_b, x2d, w_out, nw3, w_up_all, w_down_all, layer):
    m, d = x2d.shape
    g = mix_a.shape[1]
    d_ff = w_up_all.shape[2]
    return pl.pallas_call(
        functools.partial(_mix_mlp_kernel, chunk=2 * MXU_DIM, layer=layer),
        grid=(m // ROW_TILE,),
        in_specs=[
            pl.BlockSpec((ROW_TILE, g), _row_major),
            pl.BlockSpec((ROW_TILE, g), _row_major),
            pl.BlockSpec((ROW_TILE, d), _row_major),
            _const_spec((3, d)),
            pl.BlockSpec(memory_space=pl.ANY),
            pl.BlockSpec(memory_space=pl.ANY),
            pl.BlockSpec(memory_space=pl.ANY),
        ],
        out_specs=pl.BlockSpec((ROW_TILE, d), _row_major),
        out_shape=jax.ShapeDtypeStruct((m, d), F32),
        scratch_shapes=[
            pltpu.VMEM((2 * g, d), BF16),
            pltpu.VMEM((d, d_ff), BF16),
            pltpu.VMEM((d_ff, d), BF16),
            pltpu.VMEM((ROW_TILE, d_ff), BF16),
            pltpu.VMEM((ROW_TILE, d), F32),
            pltpu.VMEM((ROW_TILE, d), F32),
            pltpu.VMEM((2, STAGE_ROWS, STAGE_COLS), F32),
            pltpu.SemaphoreType.DMA((2,)),
        ],
        compiler_params=_params(1),
        name="mix_mlp",
    )(mix_a, mix_b, x2d, nw3, w_out, w_up_all, w_down_all)


def _gla_kernel(q_ref, k_ref, v_ref, r_ref, sm_ref, wa_ref, ba_ref, nw_ref, o_ref,
                la_ref, st_ref, dec_ref):
    t = q_ref.shape[0]
    nc = t // CHUNK
    n_pairs = GLA_HEADS // 2

    z = _dot(sm_ref[...].astype(BF16), wa_ref[...]) + ba_ref[...]
    la_ref[...] = _log_sigmoid(z) * (1.0 / GLA_GATE_TAU)

    grp = GLA_GROUP * CHUNK
    row = lax.broadcasted_iota(jnp.int32, (grp, grp), 0)
    col = lax.broadcasted_iota(jnp.int32, (grp, grp), 1)
    tri = jnp.where((row >= col) & (row // CHUNK == col // CHUNK), 1.0, 0.0).astype(BF16)
    erow = lax.broadcasted_iota(jnp.int32, (grp, GLA_GROUP * LANES), 0)
    ecol = lax.broadcasted_iota(jnp.int32, (grp, GLA_GROUP * LANES), 1)
    chunk_ones = jnp.where(erow // CHUNK == ecol // LANES, 1.0, 0.0).astype(BF16)

    def increments(g, carry):
        r0 = pl.multiple_of(g * grp, grp)
        la = la_ref[pl.ds(r0, grp), :]
        hi = la.astype(BF16)
        lo = (la - hi.astype(F32)).astype(BF16)
        cum = _dot(tri, hi) + _dot(tri, lo)
        dec = jnp.exp(_dot_tn(hi, chunk_ones) + _dot_tn(lo, chunk_ones))
        k_g = k_ref[pl.ds(r0, grp), :].astype(F32)
        v_g = v_ref[pl.ds(r0, grp), :]
        k_dec = []
        for c in range(GLA_GROUP):
            rows = slice(c * CHUNK, (c + 1) * CHUNK)
            total = cum[(c + 1) * CHUNK - 1:(c + 1) * CHUNK, :]
            k_dec.append((k_g[rows] * jnp.exp(total - cum[rows])).astype(BF16))
        for c in range(GLA_GROUP):
            rows = slice(c * CHUNK, (c + 1) * CHUNK)
            dec_ref[g * GLA_GROUP + c] = dec[:, c * LANES:(c + 1) * LANES]
            for h in range(GLA_HEADS):
                p, j = divmod(h, 2)
                full = _dot_tn(k_dec[c][:, p * LANES:(p + 1) * LANES], v_g[rows, h * GLA_DV:(h + 1) * GLA_DV])
                st_ref[g * GLA_GROUP + c, p, j * GLA_DK:(j + 1) * GLA_DK, :] = full[j * GLA_DK:(j + 1) * GLA_DK, :]
        return carry

    lax.fori_loop(0, nc // GLA_GROUP, increments, 0)

    def scan(c, carry):
        for p in range(n_pairs):
            st_ref[c, p] = st_ref[c, p] + dec_ref[c, p * LANES:(p + 1) * LANES, :] * st_ref[c - 1, p]
        return carry

    lax.fori_loop(1, nc, scan, 0)

    lane = lax.broadcasted_iota(jnp.int32, (CHUNK, LANES), 1)
    scale = GLA_DK ** -0.5

    def outputs(g, carry):
        r0 = pl.multiple_of(g * grp, grp)
        q_g = q_ref[pl.ds(r0, grp), :] * jnp.asarray(scale, BF16)
        raw = []
        for c in range(GLA_GROUP):
            rows = slice(c * CHUNK, (c + 1) * CHUNK)
            for h in range(GLA_HEADS):
                p, j = divmod(h, 2)
                q_pair = q_g[rows, p * LANES:(p + 1) * LANES]
                q_h = jnp.where(_half_mask(lane, j), q_pair, jnp.zeros_like(q_pair))
                raw.append(_dot(q_h, st_ref[g * GLA_GROUP + c, p].astype(BF16)))
        for c in range(GLA_GROUP):
            for h in range(GLA_HEADS):
                o = raw[c * GLA_HEADS + h]
                o = o * lax.rsqrt(jnp.mean(o * o, axis=-1, keepdims=True) + NORM_EPS)
                o = o * nw_ref[:, h * GLA_DV:(h + 1) * GLA_DV]
                rows = pl.ds(r0 + c * CHUNK, CHUNK)
                r_h = r_ref[rows, h * GLA_DV:(h + 1) * GLA_DV].astype(F32)
                o_ref[rows, h * GLA_DV:(h + 1) * GLA_DV] = (o * (r_h * _sigmoid(r_h))).astype(BF16)
        return carry

    lax.fori_loop(0, nc // GLA_GROUP, outputs, 0)


def _gla(proj, small, wa_pad, b_a, norm_w, batch, t):
    kw = GLA_HEADS * GLA_DK
    gw = GLA_HEADS * GLA_DV
    nc = t // CHUNK
    return pl.pallas_call(
        _gla_kernel,
        grid=(batch,),
        in_specs=[
            pl.BlockSpec((t, kw), lambda b: (b, 0)),
            pl.BlockSpec((t, kw), lambda b: (b, 1)),
            pl.BlockSpec((t, gw), lambda b: (b, 1)),
            pl.BlockSpec((t, gw), lambda b: (b, 2)),
            pl.BlockSpec((t, LANES), lambda b: (b, 0)),
            _const_spec((LANES, kw)),
            _const_spec((1, kw)),
            _const_spec((1, gw)),
        ],
        out_specs=pl.BlockSpec((t, gw), lambda b: (b, 0)),
        out_shape=jax.ShapeDtypeStruct((batch * t, gw), BF16),
        scratch_shapes=[
            pltpu.VMEM((t, kw), F32),
            pltpu.VMEM((nc, GLA_HEADS // 2, LANES, GLA_DV), F32),
            pltpu.VMEM((nc, kw, LANES), F32),
        ],
        compiler_params=_params(1),
        name="gla",
    )(proj, proj, proj, proj, small, wa_pad, b_a.reshape(1, kw), norm_w.reshape(1, gw))


def _fox_cum_kernel(sm_ref, bias_ref, o_ref):
    t = sm_ref.shape[0]
    row = lax.broadcasted_iota(jnp.int32, (LANES, LANES), 0)
    col = lax.broadcasted_iota(jnp.int32, (LANES, LANES), 1)
    tri = jnp.where(row >= col, 1.0, 0.0).astype(BF16)
    carry = jnp.zeros((1, LANES), F32)
    for blk in range(t // LANES):
        rows = slice(blk * LANES, (blk + 1) * LANES)
        ls = _log_sigmoid(sm_ref[rows, :] + bias_ref[...])
        h1, h2, h3 = _split3(ls)
        cb = _dot(tri, h1) + _dot(tri, h2) + _dot(tri, h3) + carry
        o_ref[rows, :] = cb
        carry = cb[LANES - 1:LANES, :]


def _fox_cum(small, bias_row, batch, t):
    return pl.pallas_call(
        _fox_cum_kernel,
        grid=(batch,),
        in_specs=[pl.BlockSpec((t, LANES), lambda b: (b, 0)), _const_spec((1, LANES))],
        out_specs=pl.BlockSpec((t, LANES), lambda b: (b, 0)),
        out_shape=jax.ShapeDtypeStruct((batch * t, LANES), F32),
        compiler_params=_params(1),
        name="fox_cum",
    )(small, bias_row)


def _fox_kernel(q_ref, k_ref, v_ref, c_ref, o_ref, ka_ref, vt_ref, s_ref, mb_ref):
    t = k_ref.shape[0]
    blk = FOX_BLOCK
    nkv = t // blk
    pair = pl.program_id(1)

    sel_r = lax.broadcasted_iota(jnp.int32, (LANES, LANES), 0)
    sel_c = lax.broadcasted_iota(jnp.int32, (LANES, LANES), 1)
    sels = []
    for term in range(CUM_TERMS):
        hit = (((sel_r == 2 * pair) & (sel_c == term))
               | ((sel_r == 2 * pair + 1) & (sel_c == CUM_TERMS + term)))
        sels.append(jnp.where(hit, 1.0, 0.0).astype(BF16))
    for kb in range(nkv):
        rows = slice(kb * blk, (kb + 1) * blk)
        vt_ref[kb] = v_ref[rows, :].astype(F32).T.astype(BF16)
        parts = _split3(c_ref[rows, :] * (-LOG2E))
        extra = _dot(parts[0], sels[0]) + _dot(parts[1], sels[1]) + _dot(parts[2], sels[2])
        ka_ref[kb, :, 0:LANES] = k_ref[rows, :]
        ka_ref[kb, :, LANES:2 * LANES] = extra.astype(BF16)

    lane = lax.broadcasted_iota(jnp.int32, (blk, LANES), 1)
    krow = lax.broadcasted_iota(jnp.int32, (blk, blk), 0)
    qcol = lax.broadcasted_iota(jnp.int32, (blk, blk), 1)
    causal = krow <= qcol

    def augmented_queries(qi):
        q = q_ref[qi * blk:(qi + 1) * blk, :]
        q_aug = []
        for j in range(2):
            q_h = jnp.where(_half_mask(lane, j), q, jnp.zeros_like(q))
            ones = jnp.where((lane >= CUM_TERMS * j) & (lane < CUM_TERMS * (j + 1)), 1.0, 0.0).astype(BF16)
            q_aug.append(jnp.concatenate([q_h, ones], axis=1).astype(F32).T.astype(BF16))
        return q_aug

    def scores(q_aug, kb, slot, masked):
        k_blk = ka_ref[kb]
        for j in range(2):
            s = _dot(k_blk, q_aug[j])
            if masked:
                s = jnp.where(causal, s, MASK_VALUE)
            s_ref[slot, j] = s
            mb_ref[slot, j] = jnp.max(s, axis=0, keepdims=True)

    def accumulate(kb, slot, state):
        new = []
        for j in range(2):
            m, l, acc = state[j]
            m_new = jnp.maximum(m, mb_ref[slot, j])
            alpha = jnp.exp2(m - m_new)
            p = jnp.exp2(s_ref[slot, j] - m_new)
            l = l * alpha + jnp.sum(p, axis=0, keepdims=True)
            pv = _dot(vt_ref[kb, j * HEAD_DIM:(j + 1) * HEAD_DIM, :], p.astype(BF16))
            new.append((m_new, l, acc * alpha + pv))
        return new

    tasks = [(qi, kb) for qi in range(nkv) for kb in range(qi + 1)]
    fresh = [(jnp.full((1, blk), MASK_VALUE, F32), jnp.zeros((1, blk), F32),
              jnp.zeros((HEAD_DIM, blk), F32))] * 2
    q_aug = augmented_queries(0)
    scores(q_aug, 0, 0, masked=True)
    state = fresh
    for n, (qi, kb) in enumerate(tasks):
        if n + 1 < len(tasks):
            qi_next, kb_next = tasks[n + 1]
            if qi_next != qi:
                q_aug = augmented_queries(qi_next)
            scores(q_aug, kb_next, (n + 1) % 2, masked=(kb_next == qi_next))
        state = accumulate(kb, n % 2, state)
        if kb == qi:
            outs = [acc * (1.0 / l) for _, l, acc in state]
            o_ref[qi * blk:(qi + 1) * blk, :] = jnp.concatenate(outs, axis=0).T.astype(BF16)
            state = fresh


def _fox(proj, cum, batch, t, col0):
    nq = t // FOX_BLOCK
    n_pairs = FOX_HEADS // 2
    qc, kc, vc = col0 // LANES, col0 // LANES + n_pairs, col0 // LANES + 2 * n_pairs
    return pl.pallas_call(
        _fox_kernel,
        grid=(batch, n_pairs),
        in_specs=[
            pl.BlockSpec((t, LANES), lambda b, p: (b, qc + p)),
            pl.BlockSpec((t, LANES), lambda b, p: (b, kc + p)),
            pl.BlockSpec((t, LANES), lambda b, p: (b, vc + p)),
            pl.BlockSpec((t, LANES), lambda b, p: (b, 0)),
        ],
        out_specs=pl.BlockSpec((t, LANES), lambda b, p: (b, p)),
        out_shape=jax.ShapeDtypeStruct((batch * t, FOX_HEADS * HEAD_DIM), BF16),
        scratch_shapes=[
            pltpu.VMEM((nq, FOX_BLOCK, 2 * LANES), BF16),
            pltpu.VMEM((nq, LANES, FOX_BLOCK), BF16),
            pltpu.VMEM((2, 2, FOX_BLOCK, FOX_BLOCK), F32),
            pltpu.VMEM((2, 2, 1, FOX_BLOCK), F32),
        ],
        compiler_params=_params(2),
        name="fox",
    )(proj, proj, proj, cum)


def _chunk_attn_kernel(q_ref, k_ref, v_ref, tab_ref, o_ref, kp_ref, vt_ref, s_ref):
    t, g = q_ref.shape
    qb = CA_QBLOCK
    n_pad = CA_PAD // qb
    n_win = CA_WINDOW // qb

    for kb in range(n_pad):
        kp_ref[kb] = jnp.zeros((qb, g), BF16)
        vt_ref[kb] = jnp.zeros(vt_ref.shape[1:], BF16)
    for kb in range(t // qb):
        rows = slice(kb * qb, (kb + 1) * qb)
        kp_ref[n_pad + kb] = k_ref[rows, :]
        v_t = v_ref[rows, :].astype(F32).T.astype(BF16)
        for h in range(g // HEAD_DIM):
            vt_ref[n_pad + kb, h, 0:HEAD_DIM, :] = v_t[h * HEAD_DIM:(h + 1) * HEAD_DIM, :]
            vt_ref[n_pad + kb, h, HEAD_DIM:HEAD_DIM + ONES_ROWS, :] = jnp.ones((ONES_ROWS, qb), BF16)

    lane = lax.broadcasted_iota(jnp.int32, (qb, LANES), 1)
    n_heads = g // HEAD_DIM
    lax.fori_loop(0, t // qb, functools.partial(
        _chunk_attn_block, q_ref=q_ref, tab_ref=tab_ref, o_ref=o_ref, kp_ref=kp_ref, vt_ref=vt_ref,
        s_ref=s_ref, lane=lane, n_heads=n_heads, n_pad=n_pad, n_win=n_win), 0)


def _chunk_attn_block(i, carry, *, q_ref, tab_ref, o_ref, kp_ref, vt_ref, s_ref, lane, n_heads,
                      n_pad, n_win):
    qb = CA_QBLOCK
    q_rows = pl.ds(pl.multiple_of(i * qb, qb), qb)

    def scores(h):
        pair, j = divmod(h, 2)
        cols = slice(pair * LANES, (pair + 1) * LANES)
        q_pair = q_ref[q_rows, cols]
        q_h = jnp.where(_half_mask(lane, j), q_pair, jnp.zeros_like(q_pair))
        q_ht = q_h.astype(F32).T.astype(BF16)
        m = None
        for w in range(n_win):
            tab_blk = jnp.where(i + w >= n_pad, w, n_win)
            s = _dot(kp_ref[i + w, :, cols], q_ht) + tab_ref[h, tab_blk]
            s_ref[h % 2, w] = s
            m_w = s.max(axis=0, keepdims=True)
            m = m_w if m is None else jnp.maximum(m, m_w)
        return m

    def weighted_values(h, m):
        acc = jnp.zeros((HEAD_DIM + ONES_ROWS, qb), F32)
        for w in range(n_win):
            p = jnp.exp2((s_ref[h % 2, w] - m).astype(BF16))
            acc = acc + _dot(vt_ref[i + w, h], p)
        return acc[0:HEAD_DIM] * (1.0 / acc[HEAD_DIM:HEAD_DIM + 1])

    outs = []
    pending = scores(0)
    for h in range(n_heads):
        upcoming = scores(h + 1) if h + 1 < n_heads else None
        outs.append(weighted_values(h, pending))
        pending = upcoming
    o_ref[q_rows, :] = jnp.concatenate(outs, axis=0).T.astype(BF16)
    return carry


def _chunk_attn(proj, table, batch, t):
    g = table.shape[0] * HEAD_DIM
    n_blocks = (CA_PAD + t) // CA_QBLOCK
    return pl.pallas_call(
        _chunk_attn_kernel,
        grid=(batch,),
        in_specs=[
            pl.BlockSpec((t, g), lambda b: (b, 0)),
            pl.BlockSpec((t, g), lambda b: (b, 1)),
            pl.BlockSpec((t, g), lambda b: (b, 2)),
            _const_spec(table.shape),
        ],
        out_specs=pl.BlockSpec((t, g), lambda b: (b, 0)),
        out_shape=jax.ShapeDtypeStruct((batch * t, g), BF16),
        scratch_shapes=[
            pltpu.VMEM((n_blocks, CA_QBLOCK, g), BF16),
            pltpu.VMEM((n_blocks, g // HEAD_DIM, HEAD_DIM + ONES_ROWS, CA_QBLOCK), BF16),
            pltpu.VMEM((2, CA_WINDOW // CA_QBLOCK, CA_QBLOCK, CA_QBLOCK), F32),
        ],
        compiler_params=_params(1),
        name="chunk_attn",
    )(proj, proj, proj, table)


def _ca_table_kernel(line_ref, o_ref):
    n_win = CA_WINDOW // CA_QBLOCK
    rows = 8
    x = jnp.broadcast_to(line_ref[0] * LOG2E, (rows, CA_LINE))
    qc = lax.broadcasted_iota(jnp.int32, (rows, CA_QBLOCK), 1) // CHUNK
    for grp in range(CA_WINDOW // rows):
        r0 = grp * rows
        y = pltpu.roll(x, (CA_LINE - CA_WINDOW + 1 + r0) % CA_LINE, axis=1, stride=1, stride_axis=0)
        kc = r0 // CHUNK
        allowed = (qc <= kc) & (qc >= kc - CA_LEFT_CHUNKS)
        w, r = divmod(r0, CA_QBLOCK)
        o_ref[0, w, r:r + rows, :] = jnp.where(allowed, y[:, 0:CA_QBLOCK], MASK_VALUE)
    o_ref[0, n_win] = jnp.full((CA_QBLOCK, CA_QBLOCK), MASK_VALUE, F32)


def _chunk_attn_table(rel_bias):
    heads = rel_bias.shape[0]
    n_win = CA_WINDOW // CA_QBLOCK
    left = CA_WINDOW - 1 - CA_PAD - REL_CLIP
    line = jnp.pad(rel_bias.astype(F32), ((0, 0), (left, CA_LINE - left - rel_bias.shape[1])), mode="edge")
    return pl.pallas_call(
        _ca_table_kernel,
        grid=(heads,),
        in_specs=[pl.BlockSpec((1, 1, CA_LINE), lambda h: (h, 0, 0))],
        out_specs=pl.BlockSpec((1, n_win + 1, CA_QBLOCK, CA_QBLOCK), lambda h: (h, 0, 0, 0)),
        out_shape=jax.ShapeDtypeStruct((heads, n_win + 1, CA_QBLOCK, CA_QBLOCK), F32),
        compiler_params=_params(1),
        name="ca_table",
    )(line.reshape(heads, 1, CA_LINE))


GELU_C0 = 0.7978845608028654
GELU_C1 = GELU_C0 * 0.044715


def _gelu(x):
    inner = x * (GELU_C0 + GELU_C1 * (x * x))
    return (0.5 * x) * (1.0 + jnp.tanh(inner))


def _lru_kernel(g_ref, x_ref, cw_ref, cb_ref, wa_ref, ba_ref, wx_ref, bx_ref, lam_ref, o_ref,
                xin_ref, xf_ref, a_ref, b_ref, hout_ref, h_ref):
    nb, frames, w = x_ref.shape
    n_slab = w // LANES
    sub = LRU_ROWS // nb
    i = pl.program_id(0)

    @pl.when(i == 0)
    def _():
        xf_ref[0:CONV_HIST] = jnp.zeros((CONV_HIST, nb, w), F32)
        h_ref[...] = jnp.zeros((nb, w), F32)

    for b in range(nb):
        xb = x_ref[b].astype(F32)
        for s in range(n_slab):
            xin_ref[s, b * LRU_PITCH:b * LRU_PITCH + frames, :] = xb[:, s * LANES:(s + 1) * LANES]

    def gather(t, carry):
        for s in range(n_slab):
            xf_ref[CONV_HIST + t, :, s * LANES:(s + 1) * LANES] = (
                xin_ref[s, pl.ds(t, nb, stride=LRU_PITCH), :])
        return carry

    lax.fori_loop(0, frames, gather, 0, unroll=SCAN_UNROLL)

    log2_base = (LRU_C * LOG2E) * _log_sigmoid(lam_ref[...])
    half = w // 2
    for blk in range(frames // sub):
        t0 = blk * sub
        xc = cb_ref[...]
        for tap in range(CONV_WIDTH):
            lo = CONV_HIST + t0 - tap
            xc = xc + (cw_ref[CONV_WIDTH - 1 - tap:CONV_WIDTH - tap, :]
                       * xf_ref[lo:lo + sub].reshape(LRU_ROWS, w))
        xcb = xc.astype(BF16)
        gr, gi = [], []
        for hb in range(2):
            cols = slice(hb * half, (hb + 1) * half)
            gr.append(_dot(xcb[:, cols], wa_ref[hb]))
            gi.append(_dot(xcb[:, cols], wx_ref[hb]))
        r = _sigmoid(jnp.concatenate(gr, axis=1) + ba_ref[...])
        gate_i = _sigmoid(jnp.concatenate(gi, axis=1) + bx_ref[...])
        a = jnp.exp2(r * log2_base)
        a_ref[t0:t0 + sub] = a.reshape(sub, nb, w)
        y = 1.0 - a * a
        root = jnp.where(y > 0.0, y * lax.rsqrt(y), 0.0)
        b_ref[t0:t0 + sub] = (root * (gate_i * xc)).reshape(sub, nb, w)

    xf_ref[0:CONV_HIST] = xf_ref[frames:frames + CONV_HIST]

    def scan(t, h):
        h = a_ref[t] * h + b_ref[t]
        for s in range(n_slab):
            hout_ref[s, pl.ds(t, nb, stride=LRU_PITCH), :] = h[:, s * LANES:(s + 1) * LANES]
        return h

    h_ref[...] = lax.fori_loop(0, frames, scan, h_ref[...], unroll=SCAN_UNROLL)

    for b in range(nb):
        rows = slice(b * LRU_PITCH, b * LRU_PITCH + frames)
        hv = jnp.concatenate([hout_ref[s, rows, :] for s in range(n_slab)], axis=1)
        o_ref[b] = (hv * _gelu(g_ref[b].astype(F32))).astype(BF16)


def _lru(proj3, conv_w, conv_b, wa_bd, b_a, wx_bd, b_x, lam, col0):
    nb, t, _ = proj3.shape
    w = conv_w.shape[1]
    gc = col0 // w
    n_slab = w // LANES
    return pl.pallas_call(
        _lru_kernel,
        grid=(t // LRU_FRAMES,),
        in_specs=[
            pl.BlockSpec((nb, LRU_FRAMES, w), lambda i: (0, i, gc)),
            pl.BlockSpec((nb, LRU_FRAMES, w), lambda i: (0, i, gc + 1)),
            _const_spec((CONV_WIDTH, w)),
            _const_spec((1, w)),
            _const_spec(wa_bd.shape),
            _const_spec((1, w)),
            _const_spec(wx_bd.shape),
            _const_spec((1, w)),
            _const_spec((1, w)),
        ],
        out_specs=pl.BlockSpec((nb, LRU_FRAMES, w), lambda i: (0, i, 0)),
        out_shape=jax.ShapeDtypeStruct((nb, t, w), BF16),
        scratch_shapes=[
            pltpu.VMEM((n_slab, nb * LRU_PITCH, LANES), F32),
            pltpu.VMEM((CONV_HIST + LRU_FRAMES, nb, w), F32),
            pltpu.VMEM((LRU_FRAMES, nb, w), F32),
            pltpu.VMEM((LRU_FRAMES, nb, w), F32),
            pltpu.VMEM((n_slab, nb * LRU_PITCH, LANES), F32),
            pltpu.VMEM((nb, w), F32),
        ],
        compiler_params=_params(1),
        name="rglru",
    )(proj3, proj3, conv_w, conv_b.reshape(1, w), wa_bd, b_a.reshape(1, w), wx_bd,
      b_x.reshape(1, w), lam.reshape(1, w))


def _block_diag_halves(wblk):
    nb, d, _ = wblk.shape
    per = nb // 2
    eye = jnp.eye(per, dtype=wblk.dtype)
    halves = [jnp.einsum("nde,nm->ndme", wblk[h * per:(h + 1) * per], eye).reshape(per * d, per * d)
              for h in range(2)]
    return jnp.stack(halves).astype(BF16)


def kernel(x, norm_w, w_in_even, gla_w_a_up, gla_b_a, gla_norm_w, fox_b_f, w_out_even,
           w_in_odd, rel_bias, conv_w, conv_b, lru_w_a, lru_b_a, lru_w_x, lru_b_x,
           lru_lambda, w_out_odd, w_mlp_up, w_mlp_down):
    batch, t, d = x.shape
    x2d = x.reshape(batch * t, d)
    group = d // 2
    kw = GLA_HEADS * GLA_DK

    w_in = w_in_even[0]
    o_ga = 2 * kw + 2 * group
    o_fq = o_ga + GLA_RANK
    o_ff = o_fq + 3 * group
    q_scale = LOG2E * HEAD_DIM ** -0.5
    w_main = jnp.concatenate(
        [w_in[:, :o_ga], w_in[:, o_fq:o_fq + group] * q_scale, w_in[:, o_fq + group:o_ff]],
        axis=1).astype(BF16)
    n_small = FOX_HEADS + GLA_RANK
    w_small = jnp.concatenate(
        [w_in[:, o_ff:], w_in[:, o_ga:o_fq], jnp.zeros((d, LANES - n_small), F32)], axis=1).astype(BF16)
    wa_pad = jnp.zeros((LANES, kw), F32).at[FOX_HEADS:n_small].set(gla_w_a_up[0]).astype(BF16)
    fox_bias = jnp.zeros((1, LANES), F32).at[0, :FOX_HEADS].set(fox_b_f[0])

    m = batch * t
    proj, small = _norm_proj(x2d, norm_w[0, 0], [(w_main, BF16), (w_small, F32)])
    out_a = _gla(proj, small, wa_pad, gla_b_a[0], gla_norm_w[0], batch, t)
    cum = _fox_cum(small, fox_bias, batch, t)
    out_b = _fox(proj, cum, batch, t, o_ga)
    x2d = _mix_mlp(out_a, out_b, x2d, w_out_even[0], norm_w[0, 1:4], w_mlp_up, w_mlp_down, 0)

    w_in = w_in_odd[0]
    w_main = jnp.concatenate([w_in[:, :group] * q_scale, w_in[:, group:]], axis=1).astype(BF16)
    (proj,) = _norm_proj(x2d, norm_w[1, 0], [(w_main, BF16)])
    out_c = _chunk_attn(proj, _chunk_attn_table(rel_bias[0]), batch, t)
    out_d = _lru(proj.reshape(batch, t, w_main.shape[1]), conv_w[0], conv_b[0],
                 _block_diag_halves(lru_w_a[0]), lru_b_a[0],
                 _block_diag_halves(lru_w_x[0]), lru_b_x[0], lru_lambda[0], 3 * group)
    x2d = _mix_mlp(out_c, out_d.reshape(m, group), x2d, w_out_odd[0], norm_w[1, 1:4],
                   w_mlp_up, w_mlp_down, 1)
    return x2d.reshape(batch, t, d)
```

```python
import functools

import jax
import jax.numpy as jnp
from jax import lax
from jax.experimental import pallas as pl
from jax.experimental.pallas import tpu as pltpu

F32 = jnp.float32
BF16 = jnp.bfloat16

NORM_EPS = 1e-6
CHUNK = 64
GLA_HEADS = 4
GLA_DK = 64
GLA_DV = 128
GLA_RANK = 16
GLA_GATE_TAU = 16.0
FOX_HEADS = 8
HEAD_DIM = 64
CA_LEFT_CHUNKS = 8
REL_CLIP = 128
CONV_WIDTH = 4
LRU_BLOCKS = 8
LRU_C = 8.0

LANES = 128
MXU_DIM = 256
MASK_VALUE = -1e30
VMEM_LIMIT_BYTES = 56 * 1024 * 1024

ROW_TILE = 512
PROJ_ROW_TILE = 1024
STAGE_ROWS = 512
STAGE_COLS = 1024
FOX_BLOCK = 512
ONES_ROWS = 16
CA_QBLOCK = 4 * CHUNK
CA_WINDOW = CA_QBLOCK + CA_LEFT_CHUNKS * CHUNK
CA_PAD = CA_LEFT_CHUNKS * CHUNK
CA_LINE = 1024
LOG2E = 1.4426950408889634
LN2 = 0.6931471805599453
CUM_TERMS = 3
GLA_GROUP = 4
LRU_ROWS = 256
LRU_FRAMES = 256
LRU_PITCH = LRU_FRAMES + 8
CONV_HIST = 8
SCAN_UNROLL = 8


def _params(n_axes):
    return pltpu.CompilerParams(
        dimension_semantics=("arbitrary",) * n_axes,
        vmem_limit_bytes=VMEM_LIMIT_BYTES,
    )


def _const_spec(shape):
    nd = len(shape)
    return pl.BlockSpec(shape, lambda *_: (0,) * nd, pipeline_mode=pl.Buffered(1))


def _rmsnorm(x, w):
    y = x * lax.rsqrt(jnp.mean(x * x, axis=-1, keepdims=True) + NORM_EPS)
    return y * w


def _log_sigmoid(z):
    return jnp.minimum(z, 0.0) - LN2 * jnp.log2(1.0 + jnp.exp2(-LOG2E * jnp.abs(z)))


def _sigmoid(z):
    return 0.5 * jnp.tanh(0.5 * z) + 0.5


def _dot(a, b):
    return jnp.dot(a, b, preferred_element_type=F32)


def _dot_nt(a, b):
    return lax.dot_general(a, b, (((1,), (1,)), ((), ())), preferred_element_type=F32)


def _dot_tn(a, b):
    return lax.dot_general(a, b, (((0,), (0,)), ((), ())), preferred_element_type=F32)


def _half_mask(lane, j):
    return lane < HEAD_DIM if j == 0 else lane >= HEAD_DIM


def _split3(x):
    h1 = x.astype(BF16)
    r1 = x - h1.astype(F32)
    h2 = r1.astype(BF16)
    h3 = (r1 - h2.astype(F32)).astype(BF16)
    return h1, h2, h3


def _row_major(i):
    return (i, 0)


def _norm_proj_kernel(x_ref, nw_ref, *refs, n_chunk):
    n_out = len(refs) // 2
    h = _rmsnorm(x_ref[...], nw_ref[...]).astype(BF16)
    for w_ref, o_ref in zip(refs[:n_out], refs[n_out:]):
        n_total = o_ref.shape[1]
        for n0 in range(0, n_total, n_chunk):
            n1 = min(n0 + n_chunk, n_total)
            o_ref[:, n0:n1] = _dot(h, w_ref[:, n0:n1]).astype(o_ref.dtype)


def _norm_proj(x2d, nw, outputs):
    m, d = x2d.shape
    in_specs = [pl.BlockSpec((PROJ_ROW_TILE, d), _row_major), _const_spec((1, d))]
    in_specs += [_const_spec(w.shape) for w, _ in outputs]
    return pl.pallas_call(
        functools.partial(_norm_proj_kernel, n_chunk=2 * MXU_DIM),
        grid=(m // PROJ_ROW_TILE,),
        in_specs=in_specs,
        out_specs=[pl.BlockSpec((PROJ_ROW_TILE, w.shape[1]), _row_major) for w, _ in outputs],
        out_shape=[jax.ShapeDtypeStruct((m, w.shape[1]), dt) for w, dt in outputs],
        compiler_params=_params(1),
        name="norm_proj",
    )(x2d, nw.reshape(1, d), *[w for w, _ in outputs])


def _stage_bf16_weights(pairs, stage_ref, sem_ref):
    pieces = []
    for src, dst in pairs:
        rows, cols = dst.shape
        for r0 in range(0, rows, STAGE_ROWS):
            for c0 in range(0, cols, STAGE_COLS):
                pieces.append((src, dst, r0, c0))

    def copy(k):
        src, _, r0, c0 = pieces[k]
        return pltpu.make_async_copy(
            src.at[pl.ds(r0, STAGE_ROWS), pl.ds(c0, STAGE_COLS)], stage_ref.at[k % 2], sem_ref.at[k % 2])

    copy(0).start()
    for k, (_, dst, r0, c0) in enumerate(pieces):
        if k + 1 < len(pieces):
            copy(k + 1).start()
        copy(k).wait()
        dst[r0:r0 + STAGE_ROWS, c0:c0 + STAGE_COLS] = stage_ref[k % 2].astype(BF16)


def _mix_mlp_kernel(a_ref, b_ref, x_ref, nw_ref, wo_hbm, wu_hbm, wd_hbm, o_ref,
                    wo_ref, wu_ref, wd_ref, u_ref, y_ref, x1_ref, stage_ref, sem_ref, *, chunk, layer):
    @pl.when(pl.program_id(0) == 0)
    def _():
        _stage_bf16_weights([(wo_hbm, wo_ref), (wu_hbm.at[layer], wu_ref), (wd_hbm.at[layer], wd_ref)],
                            stage_ref, sem_ref)

    d_ff = wu_ref.shape[1]
    d = wd_ref.shape[1]
    tm = x_ref.shape[0]
    halves = [slice(0, tm // 2), slice(tm // 2, tm)]

    def out_proj(rows):
        mix = jnp.concatenate([a_ref[rows, :], b_ref[rows, :]], axis=1)
        for n0 in range(0, d, chunk):
            y_ref[rows, n0:n0 + chunk] = _dot(mix, wo_ref[:, n0:n0 + chunk])

    def residual_and_norm(rows):
        x1 = x_ref[rows, :] + _rmsnorm(y_ref[rows, :], nw_ref[0:1, :])
        x1_ref[rows, :] = x1
        return _rmsnorm(x1, nw_ref[1:2, :]).astype(BF16)

    def up_proj(rows, h):
        for f0 in range(0, d_ff, chunk):
            u = jnp.maximum(_dot(h, wu_ref[:, f0:f0 + chunk]), 0.0)
            u_ref[rows, f0:f0 + chunk] = (u * u).astype(BF16)

    def down_proj(rows):
        for n0 in range(0, d, chunk):
            y_ref[rows, n0:n0 + chunk] = _dot(u_ref[rows, :], wd_ref[:, n0:n0 + chunk])

    def finish(rows):
        o_ref[rows, :] = x1_ref[rows, :] + _rmsnorm(y_ref[rows, :], nw_ref[2:3, :])

    first, second = halves
    out_proj(first)
    out_proj(second)
    h_first = residual_and_norm(first)
    up_proj(first, h_first)
    h_second = residual_and_norm(second)
    up_proj(second, h_second)
    down_proj(first)
    down_proj(second)
    finish(first)
    finish(second)


def _mix_mlp(mix_a, mix_b, x2d, w_out, nw3, w_up_all, w_down_all, layer):
    m, d = x2d.shape
    g = mix_a.shape[1]
    d_ff = w_up_all.shape[2]
    return pl.pallas_call(
        functools.partial(_mix_mlp_kernel, chunk=2 * MXU_DIM, layer=layer),
        grid=(m // ROW_TILE,),
        in_specs=[
            pl.BlockSpec((ROW_TILE, g), _row_major),
            pl.BlockSpec((ROW_TILE, g), _row_major),
            pl.BlockSpec((ROW_TILE, d), _row_major),
            _const_spec((3, d)),
            pl.BlockSpec(memory_space=pl.ANY),
            pl.BlockSpec(memory_space=pl.ANY),
            pl.BlockSpec(memory_space=pl.ANY),
        ],
        out_specs=pl.BlockSpec((ROW_TILE, d), _row_major),
        out_shape=jax.ShapeDtypeStruct((m, d), F32),
        scratch_shapes=[
            pltpu.VMEM((2 * g, d), BF16),
            pltpu.VMEM((d, d_ff), BF16),
            pltpu.VMEM((d_ff, d), BF16),
            pltpu.VMEM((ROW_TILE, d_ff), BF16),
            pltpu.VMEM((ROW_TILE, d), F32),
            pltpu.VMEM((ROW_TILE, d), F32),
            pltpu.VMEM((2, STAGE_ROWS, STAGE_COLS), F32),
            pltpu.SemaphoreType.DMA((2,)),
        ],
        compiler_params=_params(1),
        name="mix_mlp",
    )(mix_a, mix_b, x2d, nw3, w_out, w_up_all, w_down_all)


def _gla_kernel(q_ref, k_ref, v_ref, r_ref, sm_ref, wa_ref, ba_ref, nw_ref, o_ref,
                la_ref, st_ref, dec_ref):
    t = q_ref.shape[0]
    nc = t // CHUNK
    n_pairs = GLA_HEADS // 2

    z = _dot(sm_ref[...].astype(BF16), wa_ref[...]) + ba_ref[...]
    la_ref[...] = _log_sigmoid(z) * (1.0 / GLA_GATE_TAU)

    grp = GLA_GROUP * CHUNK
    row = lax.broadcasted_iota(jnp.int32, (grp, grp), 0)
    col = lax.broadcasted_iota(jnp.int32, (grp, grp), 1)
    tri = jnp.where((row >= col) & (row // CHUNK == col // CHUNK), 1.0, 0.0).astype(BF16)
    erow = lax.broadcasted_iota(jnp.int32, (grp, GLA_GROUP * LANES), 0)
    ecol = lax.broadcasted_iota(jnp.int32, (grp, GLA_GROUP * LANES), 1)
    chunk_ones = jnp.where(erow // CHUNK == ecol // LANES, 1.0, 0.0).astype(BF16)

    def increments(g, carry):
        r0 = pl.multiple_of(g * grp, grp)
        la = la_ref[pl.ds(r0, grp), :]
        hi = la.astype(BF16)
        lo = (la - hi.astype(F32)).astype(BF16)
        cum = _dot(tri, hi) + _dot(tri, lo)
        dec = jnp.exp(_dot_tn(hi, chunk_ones) + _dot_tn(lo, chunk_ones))
        k_g = k_ref[pl.ds(r0, grp), :].astype(F32)
        v_g = v_ref[pl.ds(r0, grp), :]
        k_dec = []
        for c in range(GLA_GROUP):
            rows = slice(c * CHUNK, (c + 1) * CHUNK)
            total = cum[(c + 1) * CHUNK - 1:(c + 1) * CHUNK, :]
            k_dec.append((k_g[rows] * jnp.exp(total - cum[rows])).astype(BF16))
        for c in range(GLA_GROUP):
            rows = slice(c * CHUNK, (c + 1) * CHUNK)
            dec_ref[g * GLA_GROUP + c] = dec[:, c * LANES:(c + 1) * LANES]
            for h in range(GLA_HEADS):
                p, j = divmod(h, 2)
                full = _dot_tn(k_dec[c][:, p * LANES:(p + 1) * LANES], v_g[rows, h * GLA_DV:(h + 1) * GLA_DV])
                st_ref[g * GLA_GROUP + c, p, j * GLA_DK:(j + 1) * GLA_DK, :] = full[j * GLA_DK:(j + 1) * GLA_DK, :]
        return carry

    lax.fori_loop(0, nc // GLA_GROUP, increments, 0)

    def scan(c, carry):
        for p in range(n_pairs):
            st_ref[c, p] = st_ref[c, p] + dec_ref[c, p * LANES:(p + 1) * LANES, :] * st_ref[c - 1, p]
        return carry

    lax.fori_loop(1, nc, scan, 0)

    lane = lax.broadcasted_iota(jnp.int32, (CHUNK, LANES), 1)
    scale = GLA_DK ** -0.5

    def outputs(g, carry):
        r0 = pl.multiple_of(g * grp, grp)
        q_g = q_ref[pl.ds(r0, grp), :] * jnp.asarray(scale, BF16)
        raw = []
        for c in range(GLA_GROUP):
            rows = slice(c * CHUNK, (c + 1) * CHUNK)
            for h in range(GLA_HEADS):
                p, j = divmod(h, 2)
                q_pair = q_g[rows, p * LANES:(p + 1) * LANES]
                q_h = jnp.where(_half_mask(lane, j), q_pair, jnp.zeros_like(q_pair))
                raw.append(_dot(q_h, st_ref[g * GLA_GROUP + c, p].astype(BF16)))
        for c in range(GLA_GROUP):
            for h in range(GLA_HEADS):
                o = raw[c * GLA_HEADS + h]
                o = o * lax.rsqrt(jnp.mean(o * o, axis=-1, keepdims=True) + NORM_EPS)
                o = o * nw_ref[:, h * GLA_DV:(h + 1) * GLA_DV]
                rows = pl.ds(r0 + c * CHUNK, CHUNK)
                r_h = r_ref[rows, h * GLA_DV:(h + 1) * GLA_DV].astype(F32)
                o_ref[rows, h * GLA_DV:(h + 1) * GLA_DV] = (o * (r_h * _sigmoid(r_h))).astype(BF16)
        return carry

    lax.fori_loop(0, nc // GLA_GROUP, outputs, 0)


def _gla(proj, small, wa_pad, b_a, norm_w, batch, t):
    kw = GLA_HEADS * GLA_DK
    gw = GLA_HEADS * GLA_DV
    nc = t // CHUNK
    return pl.pallas_call(
        _gla_kernel,
        grid=(batch,),
        in_specs=[
            pl.BlockSpec((t, kw), lambda b: (b, 0)),
            pl.BlockSpec((t, kw), lambda b: (b, 1)),
            pl.BlockSpec((t, gw), lambda b: (b, 1)),
            pl.BlockSpec((t, gw), lambda b: (b, 2)),
            pl.BlockSpec((t, LANES), lambda b: (b, 0)),
            _const_spec((LANES, kw)),
            _const_spec((1, kw)),
            _const_spec((1, gw)),
        ],
        out_specs=pl.BlockSpec((t, gw), lambda b: (b, 0)),
        out_shape=jax.ShapeDtypeStruct((batch * t, gw), BF16),
        scratch_shapes=[
            pltpu.VMEM((t, kw), F32),
            pltpu.VMEM((nc, GLA_HEADS // 2, LANES, GLA_DV), F32),
            pltpu.VMEM((nc, kw, LANES), F32),
        ],
        compiler_params=_params(1),
        name="gla",
    )(proj, proj, proj, proj, small, wa_pad, b_a.reshape(1, kw), norm_w.reshape(1, gw))


def _fox_cum_kernel(sm_ref, bias_ref, o_ref):
    t = sm_ref.shape[0]
    row = lax.broadcasted_iota(jnp.int32, (LANES, LANES), 0)
    col = lax.broadcasted_iota(jnp.int32, (LANES, LANES), 1)
    tri = jnp.where(row >= col, 1.0, 0.0).astype(BF16)
    carry = jnp.zeros((1, LANES), F32)
    for blk in range(t // LANES):
        rows = slice(blk * LANES, (blk + 1) * LANES)
        ls = _log_sigmoid(sm_ref[rows, :] + bias_ref[...])
        h1, h2, h3 = _split3(ls)
        cb = _dot(tri, h1) + _dot(tri, h2) + _dot(tri, h3) + carry
        o_ref[rows, :] = cb
        carry = cb[LANES - 1:LANES, :]


def _fox_cum(small, bias_row, batch, t):
    return pl.pallas_call(
        _fox_cum_kernel,
        grid=(batch,),
        in_specs=[pl.BlockSpec((t, LANES), lambda b: (b, 0)), _const_spec((1, LANES))],
        out_specs=pl.BlockSpec((t, LANES), lambda b: (b, 0)),
        out_shape=jax.ShapeDtypeStruct((batch * t, LANES), F32),
        compiler_params=_params(1),
        name="fox_cum",
    )(small, bias_row)


def _fox_kernel(q_ref, k_ref, v_ref, c_ref, o_ref, ka_ref, vt_ref, s_ref):
    t = k_ref.shape[0]
    blk = FOX_BLOCK
    nkv = t // blk
    pair = pl.program_id(1)

    sel_r = lax.broadcasted_iota(jnp.int32, (LANES, LANES), 0)
    sel_c = lax.broadcasted_iota(jnp.int32, (LANES, LANES), 1)
    sels = []
    for term in range(CUM_TERMS):
        hit = (((sel_r == 2 * pair) & (sel_c == term))
               | ((sel_r == 2 * pair + 1) & (sel_c == CUM_TERMS + term)))
        sels.append(jnp.where(hit, 1.0, 0.0).astype(BF16))
    for kb in range(nkv):
        rows = slice(kb * blk, (kb + 1) * blk)
        vt_ref[kb] = v_ref[rows, :].astype(F32).T.astype(BF16)
        parts = _split3(c_ref[rows, :] * (-LOG2E))
        extra = _dot(parts[0], sels[0]) + _dot(parts[1], sels[1]) + _dot(parts[2], sels[2])
        ka_ref[kb, :, 0:LANES] = k_ref[rows, :]
        ka_ref[kb, :, LANES:2 * LANES] = extra.astype(BF16)

    lane = lax.broadcasted_iota(jnp.int32, (blk, LANES), 1)
    krow = lax.broadcasted_iota(jnp.int32, (blk, blk), 0)
    qcol = lax.broadcasted_iota(jnp.int32, (blk, blk), 1)
    causal = krow <= qcol

    def augmented_queries(qi):
        q = q_ref[qi * blk:(qi + 1) * blk, :]
        q_aug = []
        for j in range(2):
            q_h = jnp.where(_half_mask(lane, j), q, jnp.zeros_like(q))
            ones = jnp.where((lane >= CUM_TERMS * j) & (lane < CUM_TERMS * (j + 1)), 1.0, 0.0).astype(BF16)
            q_aug.append(jnp.concatenate([q_h, ones], axis=1).astype(F32).T.astype(BF16))
        return q_aug

    def scores(q_aug, kb, slot, masked):
        k_blk = ka_ref[kb]
        for j in range(2):
            s = _dot(k_blk, q_aug[j])
            if masked:
                s = jnp.where(causal, s, MASK_VALUE)
            s_ref[slot, j] = s

    def accumulate(kb, slot, state):
        new = []
        for j in range(2):
            m, l, acc = state[j]
            m_new = jnp.maximum(m, jnp.max(s_ref[slot, j], axis=0, keepdims=True))
            alpha = jnp.exp2(m - m_new)
            p = jnp.exp2(s_ref[slot, j] - m_new)
            l = l * alpha + jnp.sum(p, axis=0, keepdims=True)
            pv = _dot(vt_ref[kb, j * HEAD_DIM:(j + 1) * HEAD_DIM, :], p.astype(BF16))
            new.append((m_new, l, acc * alpha + pv))
        return new

    tasks = [(qi, kb) for qi in range(nkv) for kb in range(qi + 1)]
    fresh = [(jnp.full((1, blk), MASK_VALUE, F32), jnp.zeros((1, blk), F32),
              jnp.zeros((HEAD_DIM, blk), F32))] * 2
    q_aug = augmented_queries(0)
    scores(q_aug, 0, 0, masked=True)
    state = fresh
    for n, (qi, kb) in enumerate(tasks):
        if n + 1 < len(tasks):
            qi_next, kb_next = tasks[n + 1]
            if qi_next != qi:
                q_aug = augmented_queries(qi_next)
            scores(q_aug, kb_next, (n + 1) % 2, masked=(kb_next == qi_next))
        state = accumulate(kb, n % 2, state)
        if kb == qi:
            outs = [acc * (1.0 / l) for _, l, acc in state]
            o_ref[qi * blk:(qi + 1) * blk, :] = jnp.concatenate(outs, axis=0).T.astype(BF16)
            state = fresh


def _fox(proj, cum, batch, t, col0):
    nq = t // FOX_BLOCK
    n_pairs = FOX_HEADS // 2
    qc, kc, vc = col0 // LANES, col0 // LANES + n_pairs, col0 // LANES + 2 * n_pairs
    return pl.pallas_call(
        _fox_kernel,
        grid=(batch, n_pairs),
        in_specs=[
            pl.BlockSpec((t, LANES), lambda b, p: (b, qc + p)),
            pl.BlockSpec((t, LANES), lambda b, p: (b, kc + p)),
            pl.BlockSpec((t, LANES), lambda b, p: (b, vc + p)),
            pl.BlockSpec((t, LANES), lambda b, p: (b, 0)),
        ],
        out_specs=pl.BlockSpec((t, LANES), lambda b, p: (b, p)),
        out_shape=jax.ShapeDtypeStruct((batch * t, FOX_HEADS * HEAD_DIM), BF16),
        scratch_shapes=[
            pltpu.VMEM((nq, FOX_BLOCK, 2 * LANES), BF16),
            pltpu.VMEM((nq, LANES, FOX_BLOCK), BF16),
            pltpu.VMEM((2, 2, FOX_BLOCK, FOX_BLOCK), F32),
        ],
        compiler_params=_params(2),
        name="fox",
    )(proj, proj, proj, cum)


def _chunk_attn_kernel(q_ref, k_ref, v_ref, tab_ref, o_ref, kp_ref, vt_ref, s_ref):
    t, g = q_ref.shape
    qb = CA_QBLOCK
    n_pad = CA_PAD // qb
    n_win = CA_WINDOW // qb

    for kb in range(n_pad):
        kp_ref[kb] = jnp.zeros((qb, g), BF16)
        vt_ref[kb] = jnp.zeros(vt_ref.shape[1:], BF16)
    for kb in range(t // qb):
        rows = slice(kb * qb, (kb + 1) * qb)
        kp_ref[n_pad + kb] = k_ref[rows, :]
        v_t = v_ref[rows, :].astype(F32).T.astype(BF16)
        for h in range(g // HEAD_DIM):
            vt_ref[n_pad + kb, h, 0:HEAD_DIM, :] = v_t[h * HEAD_DIM:(h + 1) * HEAD_DIM, :]
            vt_ref[n_pad + kb, h, HEAD_DIM:HEAD_DIM + ONES_ROWS, :] = jnp.ones((ONES_ROWS, qb), BF16)

    lane = lax.broadcasted_iota(jnp.int32, (qb, LANES), 1)
    n_heads = g // HEAD_DIM
    lax.fori_loop(0, t // qb, functools.partial(
        _chunk_attn_block, q_ref=q_ref, tab_ref=tab_ref, o_ref=o_ref, kp_ref=kp_ref, vt_ref=vt_ref,
        s_ref=s_ref, lane=lane, n_heads=n_heads, n_pad=n_pad, n_win=n_win), 0)


def _chunk_attn_block(i, carry, *, q_ref, tab_ref, o_ref, kp_ref, vt_ref, s_ref, lane, n_heads,
                      n_pad, n_win):
    qb = CA_QBLOCK
    q_rows = pl.ds(pl.multiple_of(i * qb, qb), qb)

    def scores(h):
        pair, j = divmod(h, 2)
        cols = slice(pair * LANES, (pair + 1) * LANES)
        q_pair = q_ref[q_rows, cols]
        q_h = jnp.where(_half_mask(lane, j), q_pair, jnp.zeros_like(q_pair))
        q_ht = q_h.astype(F32).T.astype(BF16)
        for w in range(n_win):
            tab_blk = jnp.where(i + w >= n_pad, w, n_win)
            s_ref[h % 2, w] = _dot(kp_ref[i + w, :, cols], q_ht) + tab_ref[h, tab_blk]

    def weighted_values(h):
        m = s_ref[h % 2, 0].max(axis=0, keepdims=True)
        for w in range(1, n_win):
            m = jnp.maximum(m, s_ref[h % 2, w].max(axis=0, keepdims=True))
        acc = jnp.zeros((HEAD_DIM + ONES_ROWS, qb), F32)
        for w in range(n_win):
            p = jnp.exp2((s_ref[h % 2, w] - m).astype(BF16))
            acc = acc + _dot(vt_ref[i + w, h], p)
        return acc[0:HEAD_DIM] * (1.0 / acc[HEAD_DIM:HEAD_DIM + 1])

    outs = []
    scores(0)
    for h in range(n_heads):
        if h + 1 < n_heads:
            scores(h + 1)
        outs.append(weighted_values(h))
    o_ref[q_rows, :] = jnp.concatenate(outs, axis=0).T.astype(BF16)
    return carry


def _chunk_attn(proj, table, batch, t):
    g = table.shape[0] * HEAD_DIM
    n_blocks = (CA_PAD + t) // CA_QBLOCK
    return pl.pallas_call(
        _chunk_attn_kernel,
        grid=(batch,),
        in_specs=[
            pl.BlockSpec((t, g), lambda b: (b, 0)),
            pl.BlockSpec((t, g), lambda b: (b, 1)),
            pl.BlockSpec((t, g), lambda b: (b, 2)),
            _const_spec(table.shape),
        ],
        out_specs=pl.BlockSpec((t, g), lambda b: (b, 0)),
        out_shape=jax.ShapeDtypeStruct((batch * t, g), BF16),
        scratch_shapes=[
            pltpu.VMEM((n_blocks, CA_QBLOCK, g), BF16),
            pltpu.VMEM((n_blocks, g // HEAD_DIM, HEAD_DIM + ONES_ROWS, CA_QBLOCK), BF16),
            pltpu.VMEM((2, CA_WINDOW // CA_QBLOCK, CA_QBLOCK, CA_QBLOCK), F32),
        ],
        compiler_params=_params(1),
        name="chunk_attn",
    )(proj, proj, proj, table)


def _ca_table_kernel(line_ref, o_ref):
    n_win = CA_WINDOW // CA_QBLOCK
    rows = 8
    x = jnp.broadcast_to(line_ref[0] * LOG2E, (rows, CA_LINE))
    qc = lax.broadcasted_iota(jnp.int32, (rows, CA_QBLOCK), 1) // CHUNK
    for grp in range(CA_WINDOW // rows):
        r0 = grp * rows
        y = pltpu.roll(x, (CA_LINE - CA_WINDOW + 1 + r0) % CA_LINE, axis=1, stride=1, stride_axis=0)
        kc = r0 // CHUNK
        allowed = (qc <= kc) & (qc >= kc - CA_LEFT_CHUNKS)
        w, r = divmod(r0, CA_QBLOCK)
        o_ref[0, w, r:r + rows, :] = jnp.where(allowed, y[:, 0:CA_QBLOCK], MASK_VALUE)
    o_ref[0, n_win] = jnp.full((CA_QBLOCK, CA_QBLOCK), MASK_VALUE, F32)


def _chunk_attn_table(rel_bias):
    heads = rel_bias.shape[0]
    n_win = CA_WINDOW // CA_QBLOCK
    left = CA_WINDOW - 1 - CA_PAD - REL_CLIP
    line = jnp.pad(rel_bias.astype(F32), ((0, 0), (left, CA_LINE - left - rel_bias.shape[1])), mode="edge")
    return pl.pallas_call(
        _ca_table_kernel,
        grid=(heads,),
        in_specs=[pl.BlockSpec((1, 1, CA_LINE), lambda h: (h, 0, 0))],
        out_specs=pl.BlockSpec((1, n_win + 1, CA_QBLOCK, CA_QBLOCK), lambda h: (h, 0, 0, 0)),
        out_shape=jax.ShapeDtypeStruct((heads, n_win + 1, CA_QBLOCK, CA_QBLOCK), F32),
        compiler_params=_params(1),
        name="ca_table",
    )(line.reshape(heads, 1, CA_LINE))


GELU_C0 = 0.7978845608028654
GELU_C1 = GELU_C0 * 0.044715


def _gelu(x):
    inner = x * (GELU_C0 + GELU_C1 * (x * x))
    return (0.5 * x) * (1.0 + jnp.tanh(inner))


def _lru_kernel(g_ref, x_ref, cw_ref, cb_ref, wa_ref, ba_ref, wx_ref, bx_ref, lam_ref, o_ref,
                xin_ref, xf_ref, a_ref, b_ref, hout_ref, h_ref):
    nb, frames, w = x_ref.shape
    n_slab = w // LANES
    sub = LRU_ROWS // nb
    i = pl.program_id(0)

    @pl.when(i == 0)
    def _():
        xf_ref[0:CONV_HIST] = jnp.zeros((CONV_HIST, nb, w), F32)
        h_ref[...] = jnp.zeros((nb, w), F32)

    for b in range(nb):
        xb = x_ref[b].astype(F32)
        for s in range(n_slab):
            xin_ref[s, b * LRU_PITCH:b * LRU_PITCH + frames, :] = xb[:, s * LANES:(s + 1) * LANES]

    def gather(t, carry):
        for s in range(n_slab):
            xf_ref[CONV_HIST + t, :, s * LANES:(s + 1) * LANES] = (
                xin_ref[s, pl.ds(t, nb, stride=LRU_PITCH), :])
        return carry

    lax.fori_loop(0, frames, gather, 0, unroll=SCAN_UNROLL)

    log2_base = (LRU_C * LOG2E) * _log_sigmoid(lam_ref[...])
    half = w // 2
    for blk in range(frames // sub):
        t0 = blk * sub
        xc = cb_ref[...]
        for tap in range(CONV_WIDTH):
            lo = CONV_HIST + t0 - tap
            xc = xc + (cw_ref[CONV_WIDTH - 1 - tap:CONV_WIDTH - tap, :]
                       * xf_ref[lo:lo + sub].reshape(LRU_ROWS, w))
        xcb = xc.astype(BF16)
        gr, gi = [], []
        for hb in range(2):
            cols = slice(hb * half, (hb + 1) * half)
            gr.append(_dot(xcb[:, cols], wa_ref[hb]))
            gi.append(_dot(xcb[:, cols], wx_ref[hb]))
        r = _sigmoid(jnp.concatenate(gr, axis=1) + ba_ref[...])
        gate_i = _sigmoid(jnp.concatenate(gi, axis=1) + bx_ref[...])
        a = jnp.exp2(r * log2_base)
        a_ref[t0:t0 + sub] = a.reshape(sub, nb, w)
        y = 1.0 - a * a
        root = jnp.where(y > 0.0, y * lax.rsqrt(y), 0.0)
        b_ref[t0:t0 + sub] = (root * (gate_i * xc)).reshape(sub, nb, w)

    xf_ref[0:CONV_HIST] = xf_ref[frames:frames + CONV_HIST]

    def scan(t, h):
        h = a_ref[t] * h + b_ref[t]
        for s in range(n_slab):
            hout_ref[s, pl.ds(t, nb, stride=LRU_PITCH), :] = h[:, s * LANES:(s + 1) * LANES]
        return h

    h_ref[...] = lax.fori_loop(0, frames, scan, h_ref[...], unroll=SCAN_UNROLL)

    for b in range(nb):
        rows = slice(b * LRU_PITCH, b * LRU_PITCH + frames)
        hv = jnp.concatenate([hout_ref[s, rows, :] for s in range(n_slab)], axis=1)
        o_ref[b] = (hv * _gelu(g_ref[b].astype(F32))).astype(BF16)


def _lru(proj3, conv_w, conv_b, wa_bd, b_a, wx_bd, b_x, lam, col0):
    nb, t, _ = proj3.shape
    w = conv_w.shape[1]
    gc = col0 // w
    n_slab = w // LANES
    return pl.pallas_call(
        _lru_kernel,
        grid=(t // LRU_FRAMES,),
        in_specs=[
            pl.BlockSpec((nb, LRU_FRAMES, w), lambda i: (0, i, gc)),
            pl.BlockSpec((nb, LRU_FRAMES, w), lambda i: (0, i, gc + 1)),
            _const_spec((CONV_WIDTH, w)),
            _const_spec((1, w)),
            _const_spec(wa_bd.shape),
            _const_spec((1, w)),
            _const_spec(wx_bd.shape),
            _const_spec((1, w)),
            _const_spec((1, w)),
        ],
        out_specs=pl.BlockSpec((nb, LRU_FRAMES, w), lambda i: (0, i, 0)),
        out_shape=jax.ShapeDtypeStruct((nb, t, w), BF16),
        scratch_shapes=[
            pltpu.VMEM((n_slab, nb * LRU_PITCH, LANES), F32),
            pltpu.VMEM((CONV_HIST + LRU_FRAMES, nb, w), F32),
            pltpu.VMEM((LRU_FRAMES, nb, w), F32),
            pltpu.VMEM((LRU_FRAMES, nb, w), F32),
            pltpu.VMEM((n_slab, nb * LRU_PITCH, LANES), F32),
            pltpu.VMEM((nb, w), F32),
        ],
        compiler_params=_params(1),
        name="rglru",
    )(proj3, proj3, conv_w, conv_b.reshape(1, w), wa_bd, b_a.reshape(1, w), wx_bd,
      b_x.reshape(1, w), lam.reshape(1, w))


def _block_diag_halves(wblk):
    nb, d, _ = wblk.shape
    per = nb // 2
    eye = jnp.eye(per, dtype=wblk.dtype)
    halves = [jnp.einsum("nde,nm->ndme", wblk[h * per:(h + 1) * per], eye).reshape(per * d, per * d)
              for h in range(2)]
    return jnp.stack(halves).astype(BF16)


def kernel(x, norm_w, w_in_even, gla_w_a_up, gla_b_a, gla_norm_w, fox_b_f, w_out_even,
           w_in_odd, rel_bias, conv_w, conv_b, lru_w_a, lru_b_a, lru_w_x, lru_b_x,
           lru_lambda, w_out_odd, w_mlp_up, w_mlp_down):
    batch, t, d = x.shape
    x2d = x.reshape(batch * t, d)
    group = d // 2
    kw = GLA_HEADS * GLA_DK

    w_in = w_in_even[0]
    o_ga = 2 * kw + 2 * group
    o_fq = o_ga + GLA_RANK
    o_ff = o_fq + 3 * group
    q_scale = LOG2E * HEAD_DIM ** -0.5
    w_main = jnp.concatenate(
        [w_in[:, :o_ga], w_in[:, o_fq:o_fq + group] * q_scale, w_in[:, o_fq + group:o_ff]],
        axis=1).astype(BF16)
    n_small = FOX_HEADS + GLA_RANK
    w_small = jnp.concatenate(
        [w_in[:, o_ff:], w_in[:, o_ga:o_fq], jnp.zeros((d, LANES - n_small), F32)], axis=1).astype(BF16)
    wa_pad = jnp.zeros((LANES, kw), F32).at[FOX_HEADS:n_small].set(gla_w_a_up[0]).astype(BF16)
    fox_bias = jnp.zeros((1, LANES), F32).at[0, :FOX_HEADS].set(fox_b_f[0])

    m = batch * t
    proj, small = _norm_proj(x2d, norm_w[0, 0], [(w_main, BF16), (w_small, F32)])
    out_a = _gla(proj, small, wa_pad, gla_b_a[0], gla_norm_w[0], batch, t)
    cum = _fox_cum(small, fox_bias, batch, t)
    out_b = _fox(proj, cum, batch, t, o_ga)
    x2d = _mix_mlp(out_a, out_b, x2d, w_out_even[0], norm_w[0, 1:4], w_mlp_up, w_mlp_down, 0)

    w_in = w_in_odd[0]
    w_main = jnp.concatenate([w_in[:, :group] * q_scale, w_in[:, group:]], axis=1).astype(BF16)
    (proj,) = _norm_proj(x2d, norm_w[1, 0], [(w_main, BF16)])
    out_c = _chunk_attn(proj, _chunk_attn_table(rel_bias[0]), batch, t)
    out_d = _lru(proj.reshape(batch, t, w_main.shape[1]), conv_w[0], conv_b[0],
                 _block_diag_halves(lru_w_a[0]), lru_b_a[0],
                 _block_diag_halves(lru_w_x[0]), lru_b_x[0], lru_lambda[0], 3 * group)
    x2d = _mix_mlp(out_c, out_d.reshape(m, group), x2d, w_out_odd[0], norm_w[1, 1:4],
                   w_mlp_up, w_mlp_down, 1)
    return x2d.reshape(batch, t, d)
```

```python
import functools

import jax
import jax.numpy as jnp
from jax import lax
from jax.experimental import pallas as pl
from jax.experimental.pallas import tpu as pltpu

F32 = jnp.float32
BF16 = jnp.bfloat16

NORM_EPS = 1e-6
CHUNK = 64
GLA_HEADS = 4
GLA_DK = 64
GLA_DV = 128
GLA_RANK = 16
GLA_GATE_TAU = 16.0
FOX_HEADS = 8
HEAD_DIM = 64
CA_LEFT_CHUNKS = 8
REL_CLIP = 128
CONV_WIDTH = 4
LRU_BLOCKS = 8
LRU_C = 8.0

LANES = 128
MXU_DIM = 256
MASK_VALUE = -1e30
VMEM_LIMIT_BYTES = 56 * 1024 * 1024

ROW_TILE = 1024
PROJ_ROW_TILE = 1024
STAGE_ROWS = 512
STAGE_COLS = 1024
FOX_BLOCK = 512
FOX_KBLOCK = 512
ONES_ROWS = 16
CA_QBLOCK = 4 * CHUNK
CA_WINDOW = CA_QBLOCK + CA_LEFT_CHUNKS * CHUNK
CA_PAD = CA_LEFT_CHUNKS * CHUNK
CA_LINE = 1024
LOG2E = 1.4426950408889634
LN2 = 0.6931471805599453
CUM_TERMS = 3
GLA_GROUP = 4
LRU_ROWS = 256
LRU_FRAMES = 256
LRU_PITCH = LRU_FRAMES + 8
CONV_HIST = 8
SCAN_UNROLL = 8


def _params(n_axes):
    return pltpu.CompilerParams(
        dimension_semantics=("arbitrary",) * n_axes,
        vmem_limit_bytes=VMEM_LIMIT_BYTES,
    )


def _const_spec(shape):
    nd = len(shape)
    return pl.BlockSpec(shape, lambda *_: (0,) * nd, pipeline_mode=pl.Buffered(1))


def _rmsnorm(x, w):
    y = x * lax.rsqrt(jnp.mean(x * x, axis=-1, keepdims=True) + NORM_EPS)
    return y * w


def _log_sigmoid(z):
    return jnp.minimum(z, 0.0) - LN2 * jnp.log2(1.0 + jnp.exp2(-LOG2E * jnp.abs(z)))


def _sigmoid(z):
    return 0.5 * jnp.tanh(0.5 * z) + 0.5


def _dot(a, b):
    return jnp.dot(a, b, preferred_element_type=F32)


def _dot_nt(a, b):
    return lax.dot_general(a, b, (((1,), (1,)), ((), ())), preferred_element_type=F32)


def _dot_tn(a, b):
    return lax.dot_general(a, b, (((0,), (0,)), ((), ())), preferred_element_type=F32)


def _half_mask(lane, j):
    return lane < HEAD_DIM if j == 0 else lane >= HEAD_DIM


def _split3(x):
    h1 = x.astype(BF16)
    r1 = x - h1.astype(F32)
    h2 = r1.astype(BF16)
    h3 = (r1 - h2.astype(F32)).astype(BF16)
    return h1, h2, h3


def _row_major(i):
    return (i, 0)


def _norm_proj_kernel(x_ref, nw_ref, *refs, n_chunk):
    n_out = len(refs) // 2
    h = _rmsnorm(x_ref[...], nw_ref[...]).astype(BF16)
    for w_ref, o_ref in zip(refs[:n_out], refs[n_out:]):
        n_total = o_ref.shape[1]
        for n0 in range(0, n_total, n_chunk):
            n1 = min(n0 + n_chunk, n_total)
            o_ref[:, n0:n1] = _dot(h, w_ref[:, n0:n1]).astype(o_ref.dtype)


def _norm_proj(x2d, nw, outputs):
    m, d = x2d.shape
    in_specs = [pl.BlockSpec((PROJ_ROW_TILE, d), _row_major), _const_spec((1, d))]
    in_specs += [_const_spec(w.shape) for w, _ in outputs]
    return pl.pallas_call(
        functools.partial(_norm_proj_kernel, n_chunk=2 * MXU_DIM),
        grid=(m // PROJ_ROW_TILE,),
        in_specs=in_specs,
        out_specs=[pl.BlockSpec((PROJ_ROW_TILE, w.shape[1]), _row_major) for w, _ in outputs],
        out_shape=[jax.ShapeDtypeStruct((m, w.shape[1]), dt) for w, dt in outputs],
        compiler_params=_params(1),
        name="norm_proj",
    )(x2d, nw.reshape(1, d), *[w for w, _ in outputs])


def _stage_bf16_weights(pairs, stage_ref, sem_ref):
    pieces = []
    for src, dst in pairs:
        rows, cols = dst.shape
        for r0 in range(0, rows, STAGE_ROWS):
            for c0 in range(0, cols, STAGE_COLS):
                pieces.append((src, dst, r0, c0))

    def copy(k):
        src, _, r0, c0 = pieces[k]
        return pltpu.make_async_copy(
            src.at[pl.ds(r0, STAGE_ROWS), pl.ds(c0, STAGE_COLS)], stage_ref.at[k % 2], sem_ref.at[k % 2])

    copy(0).start()
    for k, (_, dst, r0, c0) in enumerate(pieces):
        if k + 1 < len(pieces):
            copy(k + 1).start()
        copy(k).wait()
        dst[r0:r0 + STAGE_ROWS, c0:c0 + STAGE_COLS] = stage_ref[k % 2].astype(BF16)


def _mix_mlp_kernel(a_ref, b_ref, x_ref, nw_ref, wo_hbm, wu_hbm, wd_hbm, o_ref,
                    wo_ref, wu_ref, wd_ref, u_ref, y_ref, stage_ref, sem_ref, *, chunk, layer):
    @pl.when(pl.program_id(0) == 0)
    def _():
        _stage_bf16_weights([(wo_hbm, wo_ref), (wu_hbm.at[layer], wu_ref), (wd_hbm.at[layer], wd_ref)],
                            stage_ref, sem_ref)

    d_ff = wu_ref.shape[1]
    d = wd_ref.shape[1]
    tm = x_ref.shape[0]
    halves = [slice(0, tm // 2), slice(tm // 2, tm)]

    def out_proj(rows):
        mix = jnp.concatenate([a_ref[rows, :], b_ref[rows, :]], axis=1)
        for n0 in range(0, d, chunk):
            y_ref[rows, n0:n0 + chunk] = _dot(mix, wo_ref[:, n0:n0 + chunk])

    def residual_and_norm(rows):
        x1 = x_ref[rows, :] + _rmsnorm(y_ref[rows, :], nw_ref[0:1, :])
        o_ref[rows, :] = x1
        return _rmsnorm(x1, nw_ref[1:2, :]).astype(BF16)

    def up_proj(rows, h):
        for f0 in range(0, d_ff, chunk):
            u = jnp.maximum(_dot(h, wu_ref[:, f0:f0 + chunk]), 0.0)
            u_ref[rows, f0:f0 + chunk] = (u * u).astype(BF16)

    def down_proj(rows):
        for n0 in range(0, d, chunk):
            y_ref[rows, n0:n0 + chunk] = _dot(u_ref[rows, :], wd_ref[:, n0:n0 + chunk])

    def finish(rows):
        o_ref[rows, :] = o_ref[rows, :] + _rmsnorm(y_ref[rows, :], nw_ref[2:3, :])

    first, second = halves
    out_proj(first)
    out_proj(second)
    h_first = residual_and_norm(first)
    up_proj(first, h_first)
    h_second = residual_and_norm(second)
    up_proj(second, h_second)
    down_proj(first)
    down_proj(second)
    finish(first)
    finish(second)


def _mix_mlp(mix_a, mix_b, x2d, w_out, nw3, w_up_all, w_down_all, layer):
    m, d = x2d.shape
    g = mix_a.shape[1]
    d_ff = w_up_all.shape[2]
    return pl.pallas_call(
        functools.partial(_mix_mlp_kernel, chunk=2 * MXU_DIM, layer=layer),
        grid=(m // ROW_TILE,),
        in_specs=[
            pl.BlockSpec((ROW_TILE, g), _row_major),
            pl.BlockSpec((ROW_TILE, g), _row_major),
            pl.BlockSpec((ROW_TILE, d), _row_major),
            _const_spec((3, d)),
            pl.BlockSpec(memory_space=pl.ANY),
            pl.BlockSpec(memory_space=pl.ANY),
            pl.BlockSpec(memory_space=pl.ANY),
        ],
        out_specs=pl.BlockSpec((ROW_TILE, d), _row_major),
        out_shape=jax.ShapeDtypeStruct((m, d), F32),
        scratch_shapes=[
            pltpu.VMEM((2 * g, d), BF16),
            pltpu.VMEM((d, d_ff), BF16),
            pltpu.VMEM((d_ff, d), BF16),
            pltpu.VMEM((ROW_TILE, d_ff), BF16),
            pltpu.VMEM((ROW_TILE, d), F32),
            pltpu.VMEM((2, STAGE_ROWS, STAGE_COLS), F32),
            pltpu.SemaphoreType.DMA((2,)),
        ],
        compiler_params=_params(1),
        name="mix_mlp",
    )(mix_a, mix_b, x2d, nw3, w_out, w_up_all, w_down_all)


def _gla_kernel(q_ref, k_ref, v_ref, r_ref, sm_ref, wa_ref, ba_ref, nw_ref, o_ref,
                la_ref, st_ref, dec_ref):
    t = q_ref.shape[0]
    nc = t // CHUNK
    n_pairs = GLA_HEADS // 2

    z = _dot(sm_ref[...].astype(BF16), wa_ref[...]) + ba_ref[...]
    la_ref[...] = _log_sigmoid(z) * (1.0 / GLA_GATE_TAU)

    grp = GLA_GROUP * CHUNK
    row = lax.broadcasted_iota(jnp.int32, (grp, grp), 0)
    col = lax.broadcasted_iota(jnp.int32, (grp, grp), 1)
    tri = jnp.where((row >= col) & (row // CHUNK == col // CHUNK), 1.0, 0.0).astype(BF16)
    erow = lax.broadcasted_iota(jnp.int32, (grp, GLA_GROUP * LANES), 0)
    ecol = lax.broadcasted_iota(jnp.int32, (grp, GLA_GROUP * LANES), 1)
    chunk_ones = jnp.where(erow // CHUNK == ecol // LANES, 1.0, 0.0).astype(BF16)

    def increments(g, carry):
        r0 = pl.multiple_of(g * grp, grp)
        la = la_ref[pl.ds(r0, grp), :]
        hi = la.astype(BF16)
        lo = (la - hi.astype(F32)).astype(BF16)
        cum = _dot(tri, hi) + _dot(tri, lo)
        dec = jnp.exp(_dot_tn(hi, chunk_ones) + _dot_tn(lo, chunk_ones))
        k_g = k_ref[pl.ds(r0, grp), :].astype(F32)
        v_g = v_ref[pl.ds(r0, grp), :]
        k_dec = []
        for c in range(GLA_GROUP):
            rows = slice(c * CHUNK, (c + 1) * CHUNK)
            total = cum[(c + 1) * CHUNK - 1:(c + 1) * CHUNK, :]
            k_dec.append((k_g[rows] * jnp.exp(total - cum[rows])).astype(BF16))
        for c in range(GLA_GROUP):
            rows = slice(c * CHUNK, (c + 1) * CHUNK)
            dec_ref[g * GLA_GROUP + c] = dec[:, c * LANES:(c + 1) * LANES]
            for h in range(GLA_HEADS):
                p, j = divmod(h, 2)
                full = _dot_tn(k_dec[c][:, p * LANES:(p + 1) * LANES], v_g[rows, h * GLA_DV:(h + 1) * GLA_DV])
                st_ref[g * GLA_GROUP + c, p, j * GLA_DK:(j + 1) * GLA_DK, :] = full[j * GLA_DK:(j + 1) * GLA_DK, :]
        return carry

    lax.fori_loop(0, nc // GLA_GROUP, increments, 0)

    def scan(c, carry):
        for p in range(n_pairs):
            st_ref[c, p] = st_ref[c, p] + dec_ref[c, p * LANES:(p + 1) * LANES, :] * st_ref[c - 1, p]
        return carry

    lax.fori_loop(1, nc, scan, 0)

    lane = lax.broadcasted_iota(jnp.int32, (CHUNK, LANES), 1)
    scale = GLA_DK ** -0.5

    def outputs(g, carry):
        r0 = pl.multiple_of(g * grp, grp)
        q_g = q_ref[pl.ds(r0, grp), :] * jnp.asarray(scale, BF16)
        raw = []
        for c in range(GLA_GROUP):
            rows = slice(c * CHUNK, (c + 1) * CHUNK)
            for h in range(GLA_HEADS):
                p, j = divmod(h, 2)
                q_pair = q_g[rows, p * LANES:(p + 1) * LANES]
                q_h = jnp.where(_half_mask(lane, j), q_pair, jnp.zeros_like(q_pair))
                raw.append(_dot(q_h, st_ref[g * GLA_GROUP + c, p].astype(BF16)))
        for c in range(GLA_GROUP):
            for h in range(GLA_HEADS):
                o = raw[c * GLA_HEADS + h]
                o = o * lax.rsqrt(jnp.mean(o * o, axis=-1, keepdims=True) + NORM_EPS)
                o = o * nw_ref[:, h * GLA_DV:(h + 1) * GLA_DV]
                rows = pl.ds(r0 + c * CHUNK, CHUNK)
                r_h = r_ref[rows, h * GLA_DV:(h + 1) * GLA_DV].astype(F32)
                o_ref[rows, h * GLA_DV:(h + 1) * GLA_DV] = (o * (r_h * _sigmoid(r_h))).astype(BF16)
        return carry

    lax.fori_loop(0, nc // GLA_GROUP, outputs, 0)


def _gla(proj, small, wa_pad, b_a, norm_w, batch, t):
    kw = GLA_HEADS * GLA_DK
    gw = GLA_HEADS * GLA_DV
    nc = t // CHUNK
    return pl.pallas_call(
        _gla_kernel,
        grid=(batch,),
        in_specs=[
            pl.BlockSpec((t, kw), lambda b: (b, 0)),
            pl.BlockSpec((t, kw), lambda b: (b, 1)),
            pl.BlockSpec((t, gw), lambda b: (b, 1)),
            pl.BlockSpec((t, gw), lambda b: (b, 2)),
            pl.BlockSpec((t, LANES), lambda b: (b, 0)),
            _const_spec((LANES, kw)),
            _const_spec((1, kw)),
            _const_spec((1, gw)),
        ],
        out_specs=pl.BlockSpec((t, gw), lambda b: (b, 0)),
        out_shape=jax.ShapeDtypeStruct((batch * t, gw), BF16),
        scratch_shapes=[
            pltpu.VMEM((t, kw), F32),
            pltpu.VMEM((nc, GLA_HEADS // 2, LANES, GLA_DV), F32),
            pltpu.VMEM((nc, kw, LANES), F32),
        ],
        compiler_params=_params(1),
        name="gla",
    )(proj, proj, proj, proj, small, wa_pad, b_a.reshape(1, kw), norm_w.reshape(1, gw))


def _fox_cum_kernel(sm_ref, bias_ref, o_ref):
    t = sm_ref.shape[0]
    row = lax.broadcasted_iota(jnp.int32, (LANES, LANES), 0)
    col = lax.broadcasted_iota(jnp.int32, (LANES, LANES), 1)
    tri = jnp.where(row >= col, 1.0, 0.0).astype(BF16)
    carry = jnp.zeros((1, LANES), F32)
    for blk in range(t // LANES):
        rows = slice(blk * LANES, (blk + 1) * LANES)
        ls = _log_sigmoid(sm_ref[rows, :] + bias_ref[...])
        h1, h2, h3 = _split3(ls)
        cb = _dot(tri, h1) + _dot(tri, h2) + _dot(tri, h3) + carry
        o_ref[rows, :] = cb
        carry = cb[LANES - 1:LANES, :]


def _fox_cum(small, bias_row, batch, t):
    return pl.pallas_call(
        _fox_cum_kernel,
        grid=(batch,),
        in_specs=[pl.BlockSpec((t, LANES), lambda b: (b, 0)), _const_spec((1, LANES))],
        out_specs=pl.BlockSpec((t, LANES), lambda b: (b, 0)),
        out_shape=jax.ShapeDtypeStruct((batch * t, LANES), F32),
        compiler_params=_params(1),
        name="fox_cum",
    )(small, bias_row)


def _fox_kernel(q_ref, k_ref, v_ref, c_ref, o_ref, ka_ref, vt_ref, s_ref):
    t = k_ref.shape[0]
    blk = FOX_BLOCK
    kbs = FOX_KBLOCK
    pair = pl.program_id(1)

    sel_r = lax.broadcasted_iota(jnp.int32, (LANES, LANES), 0)
    sel_c = lax.broadcasted_iota(jnp.int32, (LANES, LANES), 1)
    sels = []
    for term in range(CUM_TERMS):
        hit = (((sel_r == 2 * pair) & (sel_c == term))
               | ((sel_r == 2 * pair + 1) & (sel_c == CUM_TERMS + term)))
        sels.append(jnp.where(hit, 1.0, 0.0).astype(BF16))
    for kb in range(t // kbs):
        rows = slice(kb * kbs, (kb + 1) * kbs)
        vt_ref[kb] = v_ref[rows, :].astype(F32).T.astype(BF16)
        parts = _split3(c_ref[rows, :] * (-LOG2E))
        extra = _dot(parts[0], sels[0]) + _dot(parts[1], sels[1]) + _dot(parts[2], sels[2])
        ka_ref[kb, :, 0:LANES] = k_ref[rows, :]
        ka_ref[kb, :, LANES:2 * LANES] = extra.astype(BF16)

    lane = lax.broadcasted_iota(jnp.int32, (blk, LANES), 1)
    krow = lax.broadcasted_iota(jnp.int32, (kbs, blk), 0)
    qcol = lax.broadcasted_iota(jnp.int32, (kbs, blk), 1)
    ahead = qcol - krow

    def augmented_queries(qi):
        q = q_ref[qi * blk:(qi + 1) * blk, :]
        q_aug = []
        for j in range(2):
            q_h = jnp.where(_half_mask(lane, j), q, jnp.zeros_like(q))
            ones = jnp.where((lane >= CUM_TERMS * j) & (lane < CUM_TERMS * (j + 1)), 1.0, 0.0).astype(BF16)
            q_aug.append(jnp.concatenate([q_h, ones], axis=1).astype(F32).T.astype(BF16))
        return q_aug

    def scores(q_aug, qi, kb, slot):
        k_blk = ka_ref[kb]
        offset = kb * kbs - qi * blk
        for j in range(2):
            s = _dot(k_blk, q_aug[j])
            if offset + kbs - 1 > 0:
                s = jnp.where(ahead >= offset, s, MASK_VALUE)
            s_ref[slot, j] = s

    def accumulate(kb, slot, state):
        new = []
        for j in range(2):
            m, l, acc = state[j]
            m_new = jnp.maximum(m, jnp.max(s_ref[slot, j], axis=0, keepdims=True))
            alpha = jnp.exp2(m - m_new)
            p = jnp.exp2(s_ref[slot, j] - m_new)
            l = l * alpha + jnp.sum(p, axis=0, keepdims=True)
            pv = _dot(vt_ref[kb, j * HEAD_DIM:(j + 1) * HEAD_DIM, :], p.astype(BF16))
            new.append((m_new, l, acc * alpha + pv))
        return new

    tasks = [(qi, kb) for qi in range(t // blk) for kb in range((qi + 1) * blk // kbs)]
    fresh = [(jnp.full((1, blk), MASK_VALUE, F32), jnp.zeros((1, blk), F32),
              jnp.zeros((HEAD_DIM, blk), F32))] * 2
    q_aug = augmented_queries(0)
    scores(q_aug, 0, 0, 0)
    state = fresh
    for n, (qi, kb) in enumerate(tasks):
        if n + 1 < len(tasks):
            qi_next, kb_next = tasks[n + 1]
            if qi_next != qi:
                q_aug = augmented_queries(qi_next)
            scores(q_aug, qi_next, kb_next, (n + 1) % 2)
        state = accumulate(kb, n % 2, state)
        if n + 1 == len(tasks) or tasks[n + 1][0] != qi:
            outs = [acc * (1.0 / l) for _, l, acc in state]
            o_ref[qi * blk:(qi + 1) * blk, :] = jnp.concatenate(outs, axis=0).T.astype(BF16)
            state = fresh


def _fox(proj, cum, batch, t, col0):
    nq = t // FOX_BLOCK
    n_pairs = FOX_HEADS // 2
    qc, kc, vc = col0 // LANES, col0 // LANES + n_pairs, col0 // LANES + 2 * n_pairs
    return pl.pallas_call(
        _fox_kernel,
        grid=(batch, n_pairs),
        in_specs=[
            pl.BlockSpec((t, LANES), lambda b, p: (b, qc + p)),
            pl.BlockSpec((t, LANES), lambda b, p: (b, kc + p)),
            pl.BlockSpec((t, LANES), lambda b, p: (b, vc + p)),
            pl.BlockSpec((t, LANES), lambda b, p: (b, 0)),
        ],
        out_specs=pl.BlockSpec((t, LANES), lambda b, p: (b, p)),
        out_shape=jax.ShapeDtypeStruct((batch * t, FOX_HEADS * HEAD_DIM), BF16),
        scratch_shapes=[
            pltpu.VMEM((t // FOX_KBLOCK, FOX_KBLOCK, 2 * LANES), BF16),
            pltpu.VMEM((t // FOX_KBLOCK, LANES, FOX_KBLOCK), BF16),
            pltpu.VMEM((2, 2, FOX_KBLOCK, FOX_BLOCK), F32),
        ],
        compiler_params=_params(2),
        name="fox",
    )(proj, proj, proj, cum)


def _chunk_attn_kernel(q_ref, k_ref, v_ref, tab_ref, o_ref, kp_ref, vt_ref, s_ref):
    t, g = q_ref.shape
    qb = CA_QBLOCK
    n_pad = CA_PAD // qb
    n_win = CA_WINDOW // qb

    for kb in range(n_pad):
        kp_ref[kb] = jnp.zeros((qb, g), BF16)
        vt_ref[kb] = jnp.zeros(vt_ref.shape[1:], BF16)
    for kb in range(t // qb):
        rows = slice(kb * qb, (kb + 1) * qb)
        kp_ref[n_pad + kb] = k_ref[rows, :]
        v_t = v_ref[rows, :].astype(F32).T.astype(BF16)
        for h in range(g // HEAD_DIM):
            vt_ref[n_pad + kb, h, 0:HEAD_DIM, :] = v_t[h * HEAD_DIM:(h + 1) * HEAD_DIM, :]
            vt_ref[n_pad + kb, h, HEAD_DIM:HEAD_DIM + ONES_ROWS, :] = jnp.ones((ONES_ROWS, qb), BF16)

    lane = lax.broadcasted_iota(jnp.int32, (qb, LANES), 1)
    n_heads = g // HEAD_DIM
    lax.fori_loop(0, t // qb, functools.partial(
        _chunk_attn_block, q_ref=q_ref, tab_ref=tab_ref, o_ref=o_ref, kp_ref=kp_ref, vt_ref=vt_ref,
        s_ref=s_ref, lane=lane, n_heads=n_heads, n_pad=n_pad, n_win=n_win), 0)


def _chunk_attn_block(i, carry, *, q_ref, tab_ref, o_ref, kp_ref, vt_ref, s_ref, lane, n_heads,
                      n_pad, n_win):
    qb = CA_QBLOCK
    q_rows = pl.ds(pl.multiple_of(i * qb, qb), qb)

    def scores(h):
        pair, j = divmod(h, 2)
        cols = slice(pair * LANES, (pair + 1) * LANES)
        q_pair = q_ref[q_rows, cols]
        q_h = jnp.where(_half_mask(lane, j), q_pair, jnp.zeros_like(q_pair))
        q_ht = q_h.astype(F32).T.astype(BF16)
        for w in range(n_win):
            tab_blk = jnp.where(i + w >= n_pad, w, n_win)
            s_ref[h % 2, w] = _dot(kp_ref[i + w, :, cols], q_ht) + tab_ref[h, tab_blk]

    def weighted_values(h):
        m = s_ref[h % 2, 0].max(axis=0, keepdims=True)
        for w in range(1, n_win):
            m = jnp.maximum(m, s_ref[h % 2, w].max(axis=0, keepdims=True))
        acc = jnp.zeros((HEAD_DIM + ONES_ROWS, qb), F32)
        for w in range(n_win):
            p = jnp.exp2((s_ref[h % 2, w] - m).astype(BF16))
            acc = acc + _dot(vt_ref[i + w, h], p)
        return acc[0:HEAD_DIM] * (1.0 / acc[HEAD_DIM:HEAD_DIM + 1])

    outs = []
    scores(0)
    for h in range(n_heads):
        if h + 1 < n_heads:
            scores(h + 1)
        outs.append(weighted_values(h))
    o_ref[q_rows, :] = jnp.concatenate(outs, axis=0).T.astype(BF16)
    return carry


def _chunk_attn(proj, table, batch, t):
    g = table.shape[0] * HEAD_DIM
    n_blocks = (CA_PAD + t) // CA_QBLOCK
    return pl.pallas_call(
        _chunk_attn_kernel,
        grid=(batch,),
        in_specs=[
            pl.BlockSpec((t, g), lambda b: (b, 0)),
            pl.BlockSpec((t, g), lambda b: (b, 1)),
            pl.BlockSpec((t, g), lambda b: (b, 2)),
            _const_spec(table.shape),
        ],
        out_specs=pl.BlockSpec((t, g), lambda b: (b, 0)),
        out_shape=jax.ShapeDtypeStruct((batch * t, g), BF16),
        scratch_shapes=[
            pltpu.VMEM((n_blocks, CA_QBLOCK, g), BF16),
            pltpu.VMEM((n_blocks, g // HEAD_DIM, HEAD_DIM + ONES_ROWS, CA_QBLOCK), BF16),
            pltpu.VMEM((2, CA_WINDOW // CA_QBLOCK, CA_QBLOCK, CA_QBLOCK), F32),
        ],
        compiler_params=_params(1),
        name="chunk_attn",
    )(proj, proj, proj, table)


def _ca_table_kernel(line_ref, o_ref):
    n_win = CA_WINDOW // CA_QBLOCK
    rows = 8
    x = jnp.broadcast_to(line_ref[0] * LOG2E, (rows, CA_LINE))
    qc = lax.broadcasted_iota(jnp.int32, (rows, CA_QBLOCK), 1) // CHUNK
    for grp in range(CA_WINDOW // rows):
        r0 = grp * rows
        y = pltpu.roll(x, (CA_LINE - CA_WINDOW + 1 + r0) % CA_LINE, axis=1, stride=1, stride_axis=0)
        kc = r0 // CHUNK
        allowed = (qc <= kc) & (qc >= kc - CA_LEFT_CHUNKS)
        w, r = divmod(r0, CA_QBLOCK)
        o_ref[0, w, r:r + rows, :] = jnp.where(allowed, y[:, 0:CA_QBLOCK], MASK_VALUE)
    o_ref[0, n_win] = jnp.full((CA_QBLOCK, CA_QBLOCK), MASK_VALUE, F32)


def _chunk_attn_table(rel_bias):
    heads = rel_bias.shape[0]
    n_win = CA_WINDOW // CA_QBLOCK
    left = CA_WINDOW - 1 - CA_PAD - REL_CLIP
    line = jnp.pad(rel_bias.astype(F32), ((0, 0), (left, CA_LINE - left - rel_bias.shape[1])), mode="edge")
    return pl.pallas_call(
        _ca_table_kernel,
        grid=(heads,),
        in_specs=[pl.BlockSpec((1, 1, CA_LINE), lambda h: (h, 0, 0))],
        out_specs=pl.BlockSpec((1, n_win + 1, CA_QBLOCK, CA_QBLOCK), lambda h: (h, 0, 0, 0)),
        out_shape=jax.ShapeDtypeStruct((heads, n_win + 1, CA_QBLOCK, CA_QBLOCK), F32),
        compiler_params=_params(1),
        name="ca_table",
    )(line.reshape(heads, 1, CA_LINE))


GELU_C0 = 0.7978845608028654
GELU_C1 = GELU_C0 * 0.044715


def _gelu(x):
    inner = x * (GELU_C0 + GELU_C1 * (x * x))
    return (0.5 * x) * (1.0 + jnp.tanh(inner))


def _lru_kernel(g_ref, x_ref, cw_ref, cb_ref, wa_ref, ba_ref, wx_ref, bx_ref, lam_ref, o_ref,
                xin_ref, xf_ref, a_ref, b_ref, hout_ref, h_ref):
    nb, frames, w = x_ref.shape
    n_slab = w // LANES
    sub = LRU_ROWS // nb
    i = pl.program_id(0)

    @pl.when(i == 0)
    def _():
        xf_ref[0:CONV_HIST] = jnp.zeros((CONV_HIST, nb, w), F32)
        h_ref[...] = jnp.zeros((nb, w), F32)

    for b in range(nb):
        xb = x_ref[b].astype(F32)
        for s in range(n_slab):
            xin_ref[s, b * LRU_PITCH:b * LRU_PITCH + frames, :] = xb[:, s * LANES:(s + 1) * LANES]

    def gather(t, carry):
        for s in range(n_slab):
            xf_ref[CONV_HIST + t, :, s * LANES:(s + 1) * LANES] = (
                xin_ref[s, pl.ds(t, nb, stride=LRU_PITCH), :])
        return carry

    lax.fori_loop(0, frames, gather, 0, unroll=SCAN_UNROLL)

    log2_base = (LRU_C * LOG2E) * _log_sigmoid(lam_ref[...])
    half = w // 2
    for blk in range(frames // sub):
        t0 = blk * sub
        xc = cb_ref[...]
        for tap in range(CONV_WIDTH):
            lo = CONV_HIST + t0 - tap
            xc = xc + (cw_ref[CONV_WIDTH - 1 - tap:CONV_WIDTH - tap, :]
                       * xf_ref[lo:lo + sub].reshape(LRU_ROWS, w))
        xcb = xc.astype(BF16)
        gr, gi = [], []
        for hb in range(2):
            cols = slice(hb * half, (hb + 1) * half)
            gr.append(_dot(xcb[:, cols], wa_ref[hb]))
            gi.append(_dot(xcb[:, cols], wx_ref[hb]))
        r = _sigmoid(jnp.concatenate(gr, axis=1) + ba_ref[...])
        gate_i = _sigmoid(jnp.concatenate(gi, axis=1) + bx_ref[...])
        a = jnp.exp2(r * log2_base)
        a_ref[t0:t0 + sub] = a.reshape(sub, nb, w)
        y = 1.0 - a * a
        root = jnp.where(y > 0.0, y * lax.rsqrt(y), 0.0)
        b_ref[t0:t0 + sub] = (root * (gate_i * xc)).reshape(sub, nb, w)

    xf_ref[0:CONV_HIST] = xf_ref[frames:frames + CONV_HIST]

    def scan(t, h):
        h = a_ref[t] * h + b_ref[t]
        for s in range(n_slab):
            hout_ref[s, pl.ds(t, nb, stride=LRU_PITCH), :] = h[:, s * LANES:(s + 1) * LANES]
        return h

    h_ref[...] = lax.fori_loop(0, frames, scan, h_ref[...], unroll=SCAN_UNROLL)

    for b in range(nb):
        rows = slice(b * LRU_PITCH, b * LRU_PITCH + frames)
        hv = jnp.concatenate([hout_ref[s, rows, :] for s in range(n_slab)], axis=1)
        o_ref[b] = (hv * _gelu(g_ref[b].astype(F32))).astype(BF16)


def _lru(proj3, conv_w, conv_b, wa_bd, b_a, wx_bd, b_x, lam, col0):
    nb, t, _ = proj3.shape
    w = conv_w.shape[1]
    gc = col0 // w
    n_slab = w // LANES
    return pl.pallas_call(
        _lru_kernel,
        grid=(t // LRU_FRAMES,),
        in_specs=[
            pl.BlockSpec((nb, LRU_FRAMES, w), lambda i: (0, i, gc)),
            pl.BlockSpec((nb, LRU_FRAMES, w), lambda i: (0, i, gc + 1)),
            _const_spec((CONV_WIDTH, w)),
            _const_spec((1, w)),
            _const_spec(wa_bd.shape),
            _const_spec((1, w)),
            _const_spec(wx_bd.shape),
            _const_spec((1, w)),
            _const_spec((1, w)),
        ],
        out_specs=pl.BlockSpec((nb, LRU_FRAMES, w), lambda i: (0, i, 0)),
        out_shape=jax.ShapeDtypeStruct((nb, t, w), BF16),
        scratch_shapes=[
            pltpu.VMEM((n_slab, nb * LRU_PITCH, LANES), F32),
            pltpu.VMEM((CONV_HIST + LRU_FRAMES, nb, w), F32),
            pltpu.VMEM((LRU_FRAMES, nb, w), F32),
            pltpu.VMEM((LRU_FRAMES, nb, w), F32),
            pltpu.VMEM((n_slab, nb * LRU_PITCH, LANES), F32),
            pltpu.VMEM((nb, w), F32),
        ],
        compiler_params=_params(1),
        name="rglru",
    )(proj3, proj3, conv_w, conv_b.reshape(1, w), wa_bd, b_a.reshape(1, w), wx_bd,
      b_x.reshape(1, w), lam.reshape(1, w))


def _block_diag_halves(wblk):
    nb, d, _ = wblk.shape
    per = nb // 2
    eye = jnp.eye(per, dtype=wblk.dtype)
    halves = [jnp.einsum("nde,nm->ndme", wblk[h * per:(h + 1) * per], eye).reshape(per * d, per * d)
              for h in range(2)]
    return jnp.stack(halves).astype(BF16)


def kernel(x, norm_w, w_in_even, gla_w_a_up, gla_b_a, gla_norm_w, fox_b_f, w_out_even,
           w_in_odd, rel_bias, conv_w, conv_b, lru_w_a, lru_b_a, lru_w_x, lru_b_x,
           lru_lambda, w_out_odd, w_mlp_up, w_mlp_down):
    batch, t, d = x.shape
    x2d = x.reshape(batch * t, d)
    group = d // 2
    kw = GLA_HEADS * GLA_DK

    w_in = w_in_even[0]
    o_ga = 2 * kw + 2 * group
    o_fq = o_ga + GLA_RANK
    o_ff = o_fq + 3 * group
    q_scale = LOG2E * HEAD_DIM ** -0.5
    w_main = jnp.concatenate(
        [w_in[:, :o_ga], w_in[:, o_fq:o_fq + group] * q_scale, w_in[:, o_fq + group:o_ff]],
        axis=1).astype(BF16)
    n_small = FOX_HEADS + GLA_RANK
    w_small = jnp.concatenate(
        [w_in[:, o_ff:], w_in[:, o_ga:o_fq], jnp.zeros((d, LANES - n_small), F32)], axis=1).astype(BF16)
    wa_pad = jnp.zeros((LANES, kw), F32).at[FOX_HEADS:n_small].set(gla_w_a_up[0]).astype(BF16)
    fox_bias = jnp.zeros((1, LANES), F32).at[0, :FOX_HEADS].set(fox_b_f[0])

    m = batch * t
    proj, small = _norm_proj(x2d, norm_w[0, 0], [(w_main, BF16), (w_small, F32)])
    out_a = _gla(proj, small, wa_pad, gla_b_a[0], gla_norm_w[0], batch, t)
    cum = _fox_cum(small, fox_bias, batch, t)
    out_b = _fox(proj, cum, batch, t, o_ga)
    x2d = _mix_mlp(out_a, out_b, x2d, w_out_even[0], norm_w[0, 1:4], w_mlp_up, w_mlp_down, 0)

    w_in = w_in_odd[0]
    w_main = jnp.concatenate([w_in[:, :group] * q_scale, w_in[:, group:]], axis=1).astype(BF16)
    (proj,) = _norm_proj(x2d, norm_w[1, 0], [(w_main, BF16)])
    out_c = _chunk_attn(proj, _chunk_attn_table(rel_bias[0]), batch, t)
    out_d = _lru(proj.reshape(batch, t, w_main.shape[1]), conv_w[0], conv_b[0],
                 _block_diag_halves(lru_w_a[0]), lru_b_a[0],
                 _block_diag_halves(lru_w_x[0]), lru_b_x[0], lru_lambda[0], 3 * group)
    x2d = _mix_mlp(out_c, out_d.reshape(m, group), x2d, w_out_odd[0], norm_w[1, 1:4],
                   w_mlp_up, w_mlp_down, 1)
    return x2d.reshape(batch, t, d)
```

```python
import functools

import jax
import jax.numpy as jnp
from jax import lax
from jax.experimental import pallas as pl
from jax.experimental.pallas import tpu as pltpu

F32 = jnp.float32
BF16 = jnp.bfloat16

NORM_EPS = 1e-6
CHUNK = 64
GLA_HEADS = 4
GLA_DK = 64
GLA_DV = 128
GLA_RANK = 16
GLA_GATE_TAU = 16.0
FOX_HEADS = 8
HEAD_DIM = 64
CA_LEFT_CHUNKS = 8
REL_CLIP = 128
CONV_WIDTH = 4
LRU_BLOCKS = 8
LRU_C = 8.0

LANES = 128
MXU_DIM = 256
MASK_VALUE = -1e30
VMEM_LIMIT_BYTES = 56 * 1024 * 1024

ROW_TILE = 1024
PROJ_ROW_TILE = 1024
STAGE_ROWS = 512
STAGE_COLS = 1024
FOX_BLOCK = 512
FOX_KBLOCK = 512
ONES_ROWS = 16
CA_QBLOCK = 4 * CHUNK
CA_WINDOW = CA_QBLOCK + CA_LEFT_CHUNKS * CHUNK
CA_PAD = CA_LEFT_CHUNKS * CHUNK
CA_LINE = 1024
LOG2E = 1.4426950408889634
LN2 = 0.6931471805599453
CUM_TERMS = 3
GLA_GROUP = 4
LRU_ROWS = 256
LRU_FRAMES = 256
LRU_PITCH = LRU_FRAMES + 8
CONV_HIST = 8
SCAN_UNROLL = 8


def _params(n_axes):
    return pltpu.CompilerParams(
        dimension_semantics=("arbitrary",) * n_axes,
        vmem_limit_bytes=VMEM_LIMIT_BYTES,
    )


def _const_spec(shape):
    nd = len(shape)
    return pl.BlockSpec(shape, lambda *_: (0,) * nd, pipeline_mode=pl.Buffered(1))


def _rmsnorm(x, w):
    y = x * lax.rsqrt(jnp.mean(x * x, axis=-1, keepdims=True) + NORM_EPS)
    return y * w


def _log_sigmoid(z):
    return jnp.minimum(z, 0.0) - LN2 * jnp.log2(1.0 + jnp.exp2(-LOG2E * jnp.abs(z)))


def _sigmoid(z):
    return 0.5 * jnp.tanh(0.5 * z) + 0.5


def _dot(a, b):
    return jnp.dot(a, b, preferred_element_type=F32)


def _dot_nt(a, b):
    return lax.dot_general(a, b, (((1,), (1,)), ((), ())), preferred_element_type=F32)


def _dot_tn(a, b):
    return lax.dot_general(a, b, (((0,), (0,)), ((), ())), preferred_element_type=F32)


def _half_mask(lane, j):
    return lane < HEAD_DIM if j == 0 else lane >= HEAD_DIM


def _split3(x):
    h1 = x.astype(BF16)
    r1 = x - h1.astype(F32)
    h2 = r1.astype(BF16)
    h3 = (r1 - h2.astype(F32)).astype(BF16)
    return h1, h2, h3


def _row_major(i):
    return (i, 0)


def _norm_proj_kernel(x_ref, nw_ref, *refs, n_chunk):
    n_out = len(refs) // 2
    tm = x_ref.shape[0]
    for rows in (slice(0, tm // 2), slice(tm // 2, tm)):
        h = _rmsnorm(x_ref[rows, :], nw_ref[...]).astype(BF16)
        for w_ref, o_ref in zip(refs[:n_out], refs[n_out:]):
            n_total = o_ref.shape[1]
            for n0 in range(0, n_total, n_chunk):
                n1 = min(n0 + n_chunk, n_total)
                o_ref[rows, n0:n1] = _dot(h, w_ref[:, n0:n1]).astype(o_ref.dtype)


def _norm_proj(x2d, nw, outputs):
    m, d = x2d.shape
    in_specs = [pl.BlockSpec((PROJ_ROW_TILE, d), _row_major), _const_spec((1, d))]
    in_specs += [_const_spec(w.shape) for w, _ in outputs]
    return pl.pallas_call(
        functools.partial(_norm_proj_kernel, n_chunk=2 * MXU_DIM),
        grid=(m // PROJ_ROW_TILE,),
        in_specs=in_specs,
        out_specs=[pl.BlockSpec((PROJ_ROW_TILE, w.shape[1]), _row_major) for w, _ in outputs],
        out_shape=[jax.ShapeDtypeStruct((m, w.shape[1]), dt) for w, dt in outputs],
        compiler_params=_params(1),
        name="norm_proj",
    )(x2d, nw.reshape(1, d), *[w for w, _ in outputs])


def _stage_bf16_weights(pairs, stage_ref, sem_ref):
    pieces = []
    for src, dst in pairs:
        rows, cols = dst.shape
        for r0 in range(0, rows, STAGE_ROWS):
            for c0 in range(0, cols, STAGE_COLS):
                pieces.append((src, dst, r0, c0))

    def copy(k):
        src, _, r0, c0 = pieces[k]
        return pltpu.make_async_copy(
            src.at[pl.ds(r0, STAGE_ROWS), pl.ds(c0, STAGE_COLS)], stage_ref.at[k % 2], sem_ref.at[k % 2])

    copy(0).start()
    for k, (_, dst, r0, c0) in enumerate(pieces):
        if k + 1 < len(pieces):
            copy(k + 1).start()
        copy(k).wait()
        dst[r0:r0 + STAGE_ROWS, c0:c0 + STAGE_COLS] = stage_ref[k % 2].astype(BF16)


def _mix_mlp_kernel(a_ref, b_ref, x_ref, nw_ref, wo_hbm, wu_hbm, wd_hbm, o_ref,
                    wo_ref, wu_ref, wd_ref, u_ref, y_ref, stage_ref, sem_ref, *, chunk, layer):
    @pl.when(pl.program_id(0) == 0)
    def _():
        _stage_bf16_weights([(wo_hbm, wo_ref), (wu_hbm.at[layer], wu_ref), (wd_hbm.at[layer], wd_ref)],
                            stage_ref, sem_ref)

    d_ff = wu_ref.shape[1]
    d = wd_ref.shape[1]
    tm = x_ref.shape[0]
    halves = [slice(0, tm // 2), slice(tm // 2, tm)]

    def out_proj(rows):
        mix = jnp.concatenate([a_ref[rows, :], b_ref[rows, :]], axis=1)
        for n0 in range(0, d, chunk):
            y_ref[rows, n0:n0 + chunk] = _dot(mix, wo_ref[:, n0:n0 + chunk])

    def residual_and_norm(rows):
        x1 = x_ref[rows, :] + _rmsnorm(y_ref[rows, :], nw_ref[0:1, :])
        o_ref[rows, :] = x1
        return _rmsnorm(x1, nw_ref[1:2, :]).astype(BF16)

    def up_proj(rows, h):
        for f0 in range(0, d_ff, chunk):
            u = jnp.maximum(_dot(h, wu_ref[:, f0:f0 + chunk]), 0.0)
            u_ref[rows, f0:f0 + chunk] = (u * u).astype(BF16)

    def down_proj(rows):
        for n0 in range(0, d, chunk):
            y_ref[rows, n0:n0 + chunk] = _dot(u_ref[rows, :], wd_ref[:, n0:n0 + chunk])

    def finish(rows):
        o_ref[rows, :] = o_ref[rows, :] + _rmsnorm(y_ref[rows, :], nw_ref[2:3, :])

    first, second = halves
    out_proj(first)
    out_proj(second)
    h_first = residual_and_norm(first)
    up_proj(first, h_first)
    h_second = residual_and_norm(second)
    up_proj(second, h_second)
    down_proj(first)
    down_proj(second)
    finish(first)
    finish(second)


def _mix_mlp(mix_a, mix_b, x2d, w_out, nw3, w_up_all, w_down_all, layer):
    m, d = x2d.shape
    g = mix_a.shape[1]
    d_ff = w_up_all.shape[2]
    return pl.pallas_call(
        functools.partial(_mix_mlp_kernel, chunk=2 * MXU_DIM, layer=layer),
        grid=(m // ROW_TILE,),
        in_specs=[
            pl.BlockSpec((ROW_TILE, g), _row_major),
            pl.BlockSpec((ROW_TILE, g), _row_major),
            pl.BlockSpec((ROW_TILE, d), _row_major),
            _const_spec((3, d)),
            pl.BlockSpec(memory_space=pl.ANY),
            pl.BlockSpec(memory_space=pl.ANY),
            pl.BlockSpec(memory_space=pl.ANY),
        ],
        out_specs=pl.BlockSpec((ROW_TILE, d), _row_major),
        out_shape=jax.ShapeDtypeStruct((m, d), F32),
        scratch_shapes=[
            pltpu.VMEM((2 * g, d), BF16),
            pltpu.VMEM((d, d_ff), BF16),
            pltpu.VMEM((d_ff, d), BF16),
            pltpu.VMEM((ROW_TILE, d_ff), BF16),
            pltpu.VMEM((ROW_TILE, d), F32),
            pltpu.VMEM((2, STAGE_ROWS, STAGE_COLS), F32),
            pltpu.SemaphoreType.DMA((2,)),
        ],
        compiler_params=_params(1),
        name="mix_mlp",
    )(mix_a, mix_b, x2d, nw3, w_out, w_up_all, w_down_all)


def _gla_kernel(q_ref, k_ref, v_ref, r_ref, sm_ref, wa_ref, ba_ref, nw_ref, o_ref,
                la_ref, st_ref, dec_ref):
    t = q_ref.shape[0]
    nc = t // CHUNK
    n_pairs = GLA_HEADS // 2

    z = _dot(sm_ref[...].astype(BF16), wa_ref[...]) + ba_ref[...]
    la_ref[...] = _log_sigmoid(z) * (1.0 / GLA_GATE_TAU)

    grp = GLA_GROUP * CHUNK
    row = lax.broadcasted_iota(jnp.int32, (grp, grp), 0)
    col = lax.broadcasted_iota(jnp.int32, (grp, grp), 1)
    tri = jnp.where((row >= col) & (row // CHUNK == col // CHUNK), 1.0, 0.0).astype(BF16)
    erow = lax.broadcasted_iota(jnp.int32, (grp, GLA_GROUP * LANES), 0)
    ecol = lax.broadcasted_iota(jnp.int32, (grp, GLA_GROUP * LANES), 1)
    chunk_ones = jnp.where(erow // CHUNK == ecol // LANES, 1.0, 0.0).astype(BF16)

    def increments(g, carry):
        r0 = pl.multiple_of(g * grp, grp)
        la = la_ref[pl.ds(r0, grp), :]
        hi = la.astype(BF16)
        lo = (la - hi.astype(F32)).astype(BF16)
        cum = _dot(tri, hi) + _dot(tri, lo)
        dec = jnp.exp(_dot_tn(hi, chunk_ones) + _dot_tn(lo, chunk_ones))
        k_g = k_ref[pl.ds(r0, grp), :].astype(F32)
        v_g = v_ref[pl.ds(r0, grp), :]
        k_dec = []
        for c in range(GLA_GROUP):
            rows = slice(c * CHUNK, (c + 1) * CHUNK)
            total = cum[(c + 1) * CHUNK - 1:(c + 1) * CHUNK, :]
            k_dec.append((k_g[rows] * jnp.exp(total - cum[rows])).astype(BF16))
        for c in range(GLA_GROUP):
            rows = slice(c * CHUNK, (c + 1) * CHUNK)
            dec_ref[g * GLA_GROUP + c] = dec[:, c * LANES:(c + 1) * LANES]
            for h in range(GLA_HEADS):
                p, j = divmod(h, 2)
                full = _dot_tn(k_dec[c][:, p * LANES:(p + 1) * LANES], v_g[rows, h * GLA_DV:(h + 1) * GLA_DV])
                st_ref[g * GLA_GROUP + c, p, j * GLA_DK:(j + 1) * GLA_DK, :] = full[j * GLA_DK:(j + 1) * GLA_DK, :]
        return carry

    lax.fori_loop(0, nc // GLA_GROUP, increments, 0)

    def scan(c, carry):
        for p in range(n_pairs):
            st_ref[c, p] = st_ref[c, p] + dec_ref[c, p * LANES:(p + 1) * LANES, :] * st_ref[c - 1, p]
        return carry

    lax.fori_loop(1, nc, scan, 0)

    lane = lax.broadcasted_iota(jnp.int32, (CHUNK, LANES), 1)
    scale = GLA_DK ** -0.5

    def outputs(g, carry):
        r0 = pl.multiple_of(g * grp, grp)
        q_g = q_ref[pl.ds(r0, grp), :] * jnp.asarray(scale, BF16)
        raw = []
        for c in range(GLA_GROUP):
            rows = slice(c * CHUNK, (c + 1) * CHUNK)
            for h in range(GLA_HEADS):
                p, j = divmod(h, 2)
                q_pair = q_g[rows, p * LANES:(p + 1) * LANES]
                q_h = jnp.where(_half_mask(lane, j), q_pair, jnp.zeros_like(q_pair))
                raw.append(_dot(q_h, st_ref[g * GLA_GROUP + c, p].astype(BF16)))
        for c in range(GLA_GROUP):
            for h in range(GLA_HEADS):
                o = raw[c * GLA_HEADS + h]
                o = o * lax.rsqrt(jnp.mean(o * o, axis=-1, keepdims=True) + NORM_EPS)
                o = o * nw_ref[:, h * GLA_DV:(h + 1) * GLA_DV]
                rows = pl.ds(r0 + c * CHUNK, CHUNK)
                r_h = r_ref[rows, h * GLA_DV:(h + 1) * GLA_DV].astype(F32)
                o_ref[rows, h * GLA_DV:(h + 1) * GLA_DV] = (o * (r_h * _sigmoid(r_h))).astype(BF16)
        return carry

    lax.fori_loop(0, nc // GLA_GROUP, outputs, 0)


def _gla(proj, small, wa_pad, b_a, norm_w, batch, t):
    kw = GLA_HEADS * GLA_DK
    gw = GLA_HEADS * GLA_DV
    nc = t // CHUNK
    return pl.pallas_call(
        _gla_kernel,
        grid=(batch,),
        in_specs=[
            pl.BlockSpec((t, kw), lambda b: (b, 0)),
            pl.BlockSpec((t, kw), lambda b: (b, 1)),
            pl.BlockSpec((t, gw), lambda b: (b, 1)),
            pl.BlockSpec((t, gw), lambda b: (b, 2)),
            pl.BlockSpec((t, LANES), lambda b: (b, 0)),
            _const_spec((LANES, kw)),
            _const_spec((1, kw)),
            _const_spec((1, gw)),
        ],
        out_specs=pl.BlockSpec((t, gw), lambda b: (b, 0)),
        out_shape=jax.ShapeDtypeStruct((batch * t, gw), BF16),
        scratch_shapes=[
            pltpu.VMEM((t, kw), F32),
            pltpu.VMEM((nc, GLA_HEADS // 2, LANES, GLA_DV), F32),
            pltpu.VMEM((nc, kw, LANES), F32),
        ],
        compiler_params=_params(1),
        name="gla",
    )(proj, proj, proj, proj, small, wa_pad, b_a.reshape(1, kw), norm_w.reshape(1, gw))


def _fox_cum_kernel(sm_ref, bias_ref, o_ref):
    t = sm_ref.shape[0]
    row = lax.broadcasted_iota(jnp.int32, (LANES, LANES), 0)
    col = lax.broadcasted_iota(jnp.int32, (LANES, LANES), 1)
    tri = jnp.where(row >= col, 1.0, 0.0).astype(BF16)
    carry = jnp.zeros((1, LANES), F32)
    for blk in range(t // LANES):
        rows = slice(blk * LANES, (blk + 1) * LANES)
        ls = _log_sigmoid(sm_ref[rows, :] + bias_ref[...])
        h1, h2, h3 = _split3(ls)
        cb = _dot(tri, h1) + _dot(tri, h2) + _dot(tri, h3) + carry
        o_ref[rows, :] = cb
        carry = cb[LANES - 1:LANES, :]


def _fox_cum(small, bias_row, batch, t):
    return pl.pallas_call(
        _fox_cum_kernel,
        grid=(batch,),
        in_specs=[pl.BlockSpec((t, LANES), lambda b: (b, 0)), _const_spec((1, LANES))],
        out_specs=pl.BlockSpec((t, LANES), lambda b: (b, 0)),
        out_shape=jax.ShapeDtypeStruct((batch * t, LANES), F32),
        compiler_params=_params(1),
        name="fox_cum",
    )(small, bias_row)


def _fox_kernel(q_ref, k_ref, v_ref, c_ref, o_ref, ka_ref, vt_ref, s_ref):
    t = k_ref.shape[0]
    blk = FOX_BLOCK
    kbs = FOX_KBLOCK
    pair = pl.program_id(1)

    sel_r = lax.broadcasted_iota(jnp.int32, (LANES, LANES), 0)
    sel_c = lax.broadcasted_iota(jnp.int32, (LANES, LANES), 1)
    sels = []
    for term in range(CUM_TERMS):
        hit = (((sel_r == 2 * pair) & (sel_c == term))
               | ((sel_r == 2 * pair + 1) & (sel_c == CUM_TERMS + term)))
        sels.append(jnp.where(hit, 1.0, 0.0).astype(BF16))
    for kb in range(t // kbs):
        rows = slice(kb * kbs, (kb + 1) * kbs)
        vt_ref[kb] = v_ref[rows, :].T
        parts = _split3(c_ref[rows, :] * (-LOG2E))
        extra = _dot(parts[0], sels[0]) + _dot(parts[1], sels[1]) + _dot(parts[2], sels[2])
        ka_ref[kb, :, 0:LANES] = k_ref[rows, :]
        ka_ref[kb, :, LANES:2 * LANES] = extra.astype(BF16)

    lane = lax.broadcasted_iota(jnp.int32, (blk, LANES), 1)
    krow = lax.broadcasted_iota(jnp.int32, (kbs, blk), 0)
    qcol = lax.broadcasted_iota(jnp.int32, (kbs, blk), 1)
    ahead = qcol - krow

    def augmented_queries(qi):
        q = q_ref[qi * blk:(qi + 1) * blk, :]
        q_aug = []
        for j in range(2):
            q_h = jnp.where(_half_mask(lane, j), q, jnp.zeros_like(q))
            ones = jnp.where((lane >= CUM_TERMS * j) & (lane < CUM_TERMS * (j + 1)), 1.0, 0.0).astype(BF16)
            q_aug.append(jnp.concatenate([q_h, ones], axis=1).T)
        return q_aug

    def scores(q_aug, qi, kb, slot):
        k_blk = ka_ref[kb]
        offset = kb * kbs - qi * blk
        for j in range(2):
            s = _dot(k_blk, q_aug[j])
            if offset + kbs - 1 > 0:
                s = jnp.where(ahead >= offset, s, MASK_VALUE)
            s_ref[slot, j] = s

    def accumulate(kb, slot, state):
        new = []
        for j in range(2):
            m, l, acc = state[j]
            m_new = jnp.maximum(m, jnp.max(s_ref[slot, j], axis=0, keepdims=True))
            alpha = jnp.exp2(m - m_new)
            p = jnp.exp2(s_ref[slot, j] - m_new)
            l = l * alpha + jnp.sum(p, axis=0, keepdims=True)
            pv = _dot(vt_ref[kb, j * HEAD_DIM:(j + 1) * HEAD_DIM, :], p.astype(BF16))
            new.append((m_new, l, acc * alpha + pv))
        return new

    tasks = [(qi, kb) for qi in range(t // blk) for kb in range((qi + 1) * blk // kbs)]
    fresh = [(jnp.full((1, blk), MASK_VALUE, F32), jnp.zeros((1, blk), F32),
              jnp.zeros((HEAD_DIM, blk), F32))] * 2
    q_aug = augmented_queries(0)
    scores(q_aug, 0, 0, 0)
    state = fresh
    for n, (qi, kb) in enumerate(tasks):
        if n + 1 < len(tasks):
            qi_next, kb_next = tasks[n + 1]
            if qi_next != qi:
                q_aug = augmented_queries(qi_next)
            scores(q_aug, qi_next, kb_next, (n + 1) % 2)
        state = accumulate(kb, n % 2, state)
        if n + 1 == len(tasks) or tasks[n + 1][0] != qi:
            outs = [acc * (1.0 / l) for _, l, acc in state]
            o_ref[qi * blk:(qi + 1) * blk, :] = jnp.concatenate(outs, axis=0).T.astype(BF16)
            state = fresh


def _fox(proj, cum, batch, t, col0):
    nq = t // FOX_BLOCK
    n_pairs = FOX_HEADS // 2
    qc, kc, vc = col0 // LANES, col0 // LANES + n_pairs, col0 // LANES + 2 * n_pairs
    return pl.pallas_call(
        _fox_kernel,
        grid=(batch, n_pairs),
        in_specs=[
            pl.BlockSpec((t, LANES), lambda b, p: (b, qc + p)),
            pl.BlockSpec((t, LANES), lambda b, p: (b, kc + p)),
            pl.BlockSpec((t, LANES), lambda b, p: (b, vc + p)),
            pl.BlockSpec((t, LANES), lambda b, p: (b, 0)),
        ],
        out_specs=pl.BlockSpec((t, LANES), lambda b, p: (b, p)),
        out_shape=jax.ShapeDtypeStruct((batch * t, FOX_HEADS * HEAD_DIM), BF16),
        scratch_shapes=[
            pltpu.VMEM((t // FOX_KBLOCK, FOX_KBLOCK, 2 * LANES), BF16),
            pltpu.VMEM((t // FOX_KBLOCK, LANES, FOX_KBLOCK), BF16),
            pltpu.VMEM((2, 2, FOX_KBLOCK, FOX_BLOCK), F32),
        ],
        compiler_params=_params(2),
        name="fox",
    )(proj, proj, proj, cum)


def _chunk_attn_kernel(q_ref, k_ref, v_ref, tab_ref, o_ref, kp_ref, vt_ref, s_ref):
    t, g = q_ref.shape
    qb = CA_QBLOCK
    n_pad = CA_PAD // qb
    n_win = CA_WINDOW // qb

    for kb in range(n_pad):
        kp_ref[kb] = jnp.zeros((qb, g), BF16)
        vt_ref[kb] = jnp.zeros(vt_ref.shape[1:], BF16)
    for kb in range(t // qb):
        rows = slice(kb * qb, (kb + 1) * qb)
        kp_ref[n_pad + kb] = k_ref[rows, :]
        v_t = v_ref[rows, :].T
        for h in range(g // HEAD_DIM):
            vt_ref[n_pad + kb, h, 0:HEAD_DIM, :] = v_t[h * HEAD_DIM:(h + 1) * HEAD_DIM, :]
            vt_ref[n_pad + kb, h, HEAD_DIM:HEAD_DIM + ONES_ROWS, :] = jnp.ones((ONES_ROWS, qb), BF16)

    lane = lax.broadcasted_iota(jnp.int32, (qb, LANES), 1)
    n_heads = g // HEAD_DIM
    lax.fori_loop(0, t // qb, functools.partial(
        _chunk_attn_block, q_ref=q_ref, tab_ref=tab_ref, o_ref=o_ref, kp_ref=kp_ref, vt_ref=vt_ref,
        s_ref=s_ref, lane=lane, n_heads=n_heads, n_pad=n_pad, n_win=n_win), 0)


def _chunk_attn_block(i, carry, *, q_ref, tab_ref, o_ref, kp_ref, vt_ref, s_ref, lane, n_heads,
                      n_pad, n_win):
    qb = CA_QBLOCK
    q_rows = pl.ds(pl.multiple_of(i * qb, qb), qb)

    def scores(h):
        pair, j = divmod(h, 2)
        cols = slice(pair * LANES, (pair + 1) * LANES)
        q_pair = q_ref[q_rows, cols]
        q_h = jnp.where(_half_mask(lane, j), q_pair, jnp.zeros_like(q_pair))
        q_ht = q_h.T
        for w in range(n_win):
            tab_blk = jnp.where(i + w >= n_pad, w, n_win)
            s_ref[h % 2, w] = _dot(kp_ref[i + w, :, cols], q_ht) + tab_ref[h, tab_blk]

    def weighted_values(h):
        m = s_ref[h % 2, 0].max(axis=0, keepdims=True)
        for w in range(1, n_win):
            m = jnp.maximum(m, s_ref[h % 2, w].max(axis=0, keepdims=True))
        acc = jnp.zeros((HEAD_DIM + ONES_ROWS, qb), F32)
        for w in range(n_win):
            p = jnp.exp2((s_ref[h % 2, w] - m).astype(BF16))
            acc = acc + _dot(vt_ref[i + w, h], p)
        return acc[0:HEAD_DIM] * (1.0 / acc[HEAD_DIM:HEAD_DIM + 1])

    outs = []
    scores(0)
    for h in range(n_heads):
        if h + 1 < n_heads:
            scores(h + 1)
        outs.append(weighted_values(h))
    o_ref[q_rows, :] = jnp.concatenate(outs, axis=0).T.astype(BF16)
    return carry


def _chunk_attn(proj, table, batch, t):
    g = table.shape[0] * HEAD_DIM
    n_blocks = (CA_PAD + t) // CA_QBLOCK
    return pl.pallas_call(
        _chunk_attn_kernel,
        grid=(batch,),
        in_specs=[
            pl.BlockSpec((t, g), lambda b: (b, 0)),
            pl.BlockSpec((t, g), lambda b: (b, 1)),
            pl.BlockSpec((t, g), lambda b: (b, 2)),
            _const_spec(table.shape),
        ],
        out_specs=pl.BlockSpec((t, g), lambda b: (b, 0)),
        out_shape=jax.ShapeDtypeStruct((batch * t, g), BF16),
        scratch_shapes=[
            pltpu.VMEM((n_blocks, CA_QBLOCK, g), BF16),
            pltpu.VMEM((n_blocks, g // HEAD_DIM, HEAD_DIM + ONES_ROWS, CA_QBLOCK), BF16),
            pltpu.VMEM((2, CA_WINDOW // CA_QBLOCK, CA_QBLOCK, CA_QBLOCK), F32),
        ],
        compiler_params=_params(1),
        name="chunk_attn",
    )(proj, proj, proj, table)


def _ca_table_kernel(line_ref, o_ref):
    n_win = CA_WINDOW // CA_QBLOCK
    rows = 8
    x = jnp.broadcast_to(line_ref[0] * LOG2E, (rows, CA_LINE))
    qc = lax.broadcasted_iota(jnp.int32, (rows, CA_QBLOCK), 1) // CHUNK
    for grp in range(CA_WINDOW // rows):
        r0 = grp * rows
        y = pltpu.roll(x, (CA_LINE - CA_WINDOW + 1 + r0) % CA_LINE, axis=1, stride=1, stride_axis=0)
        kc = r0 // CHUNK
        allowed = (qc <= kc) & (qc >= kc - CA_LEFT_CHUNKS)
        w, r = divmod(r0, CA_QBLOCK)
        o_ref[0, w, r:r + rows, :] = jnp.where(allowed, y[:, 0:CA_QBLOCK], MASK_VALUE)
    o_ref[0, n_win] = jnp.full((CA_QBLOCK, CA_QBLOCK), MASK_VALUE, F32)


def _chunk_attn_table(rel_bias):
    heads = rel_bias.shape[0]
    n_win = CA_WINDOW // CA_QBLOCK
    left = CA_WINDOW - 1 - CA_PAD - REL_CLIP
    line = jnp.pad(rel_bias.astype(F32), ((0, 0), (left, CA_LINE - left - rel_bias.shape[1])), mode="edge")
    return pl.pallas_call(
        _ca_table_kernel,
        grid=(heads,),
        in_specs=[pl.BlockSpec((1, 1, CA_LINE), lambda h: (h, 0, 0))],
        out_specs=pl.BlockSpec((1, n_win + 1, CA_QBLOCK, CA_QBLOCK), lambda h: (h, 0, 0, 0)),
        out_shape=jax.ShapeDtypeStruct((heads, n_win + 1, CA_QBLOCK, CA_QBLOCK), F32),
        compiler_params=_params(1),
        name="ca_table",
    )(line.reshape(heads, 1, CA_LINE))


GELU_C0 = 0.7978845608028654
GELU_C1 = GELU_C0 * 0.044715


def _gelu(x):
    inner = x * (GELU_C0 + GELU_C1 * (x * x))
    return (0.5 * x) * (1.0 + jnp.tanh(inner))


def _lru_kernel(g_ref, x_ref, cw_ref, cb_ref, wa_ref, ba_ref, wx_ref, bx_ref, lam_ref, o_ref,
                xin_ref, xf_ref, a_ref, b_ref, hout_ref, h_ref):
    nb, frames, w = x_ref.shape
    n_slab = w // LANES
    sub = LRU_ROWS // nb
    i = pl.program_id(0)

    @pl.when(i == 0)
    def _():
        xf_ref[0:CONV_HIST] = jnp.zeros((CONV_HIST, nb, w), F32)
        h_ref[...] = jnp.zeros((nb, w), F32)

    for b in range(nb):
        xb = x_ref[b].astype(F32)
        for s in range(n_slab):
            xin_ref[s, b * LRU_PITCH:b * LRU_PITCH + frames, :] = xb[:, s * LANES:(s + 1) * LANES]

    def gather(t, carry):
        for s in range(n_slab):
            xf_ref[CONV_HIST + t, :, s * LANES:(s + 1) * LANES] = (
                xin_ref[s, pl.ds(t, nb, stride=LRU_PITCH), :])
        return carry

    lax.fori_loop(0, frames, gather, 0, unroll=SCAN_UNROLL)

    log2_base = (LRU_C * LOG2E) * _log_sigmoid(lam_ref[...])
    half = w // 2
    for blk in range(frames // sub):
        t0 = blk * sub
        xc = cb_ref[...]
        for tap in range(CONV_WIDTH):
            lo = CONV_HIST + t0 - tap
            xc = xc + (cw_ref[CONV_WIDTH - 1 - tap:CONV_WIDTH - tap, :]
                       * xf_ref[lo:lo + sub].reshape(LRU_ROWS, w))
        xcb = xc.astype(BF16)
        gr, gi = [], []
        for hb in range(2):
            cols = slice(hb * half, (hb + 1) * half)
            gr.append(_dot(xcb[:, cols], wa_ref[hb]))
            gi.append(_dot(xcb[:, cols], wx_ref[hb]))
        r = _sigmoid(jnp.concatenate(gr, axis=1) + ba_ref[...])
        gate_i = _sigmoid(jnp.concatenate(gi, axis=1) + bx_ref[...])
        a = jnp.exp2(r * log2_base)
        a_ref[t0:t0 + sub] = a.reshape(sub, nb, w)
        y = 1.0 - a * a
        root = jnp.where(y > 0.0, y * lax.rsqrt(y), 0.0)
        b_ref[t0:t0 + sub] = (root * (gate_i * xc)).reshape(sub, nb, w)

    xf_ref[0:CONV_HIST] = xf_ref[frames:frames + CONV_HIST]

    def scan(t, h):
        h = a_ref[t] * h + b_ref[t]
        for s in range(n_slab):
            hout_ref[s, pl.ds(t, nb, stride=LRU_PITCH), :] = h[:, s * LANES:(s + 1) * LANES]
        return h

    h_ref[...] = lax.fori_loop(0, frames, scan, h_ref[...], unroll=SCAN_UNROLL)

    for b in range(nb):
        rows = slice(b * LRU_PITCH, b * LRU_PITCH + frames)
        hv = jnp.concatenate([hout_ref[s, rows, :] for s in range(n_slab)], axis=1)
        o_ref[b] = (hv * _gelu(g_ref[b].astype(F32))).astype(BF16)


def _lru(proj3, conv_w, conv_b, wa_bd, b_a, wx_bd, b_x, lam, col0):
    nb, t, _ = proj3.shape
    w = conv_w.shape[1]
    gc = col0 // w
    n_slab = w // LANES
    return pl.pallas_call(
        _lru_kernel,
        grid=(t // LRU_FRAMES,),
        in_specs=[
            pl.BlockSpec((nb, LRU_FRAMES, w), lambda i: (0, i, gc)),
            pl.BlockSpec((nb, LRU_FRAMES, w), lambda i: (0, i, gc + 1)),
            _const_spec((CONV_WIDTH, w)),
            _const_spec((1, w)),
            _const_spec(wa_bd.shape),
            _const_spec((1, w)),
            _const_spec(wx_bd.shape),
            _const_spec((1, w)),
            _const_spec((1, w)),
        ],
        out_specs=pl.BlockSpec((nb, LRU_FRAMES, w), lambda i: (0, i, 0)),
        out_shape=jax.ShapeDtypeStruct((nb, t, w), BF16),
        scratch_shapes=[
            pltpu.VMEM((n_slab, nb * LRU_PITCH, LANES), F32),
            pltpu.VMEM((CONV_HIST + LRU_FRAMES, nb, w), F32),
            pltpu.VMEM((LRU_FRAMES, nb, w), F32),
            pltpu.VMEM((LRU_FRAMES, nb, w), F32),
            pltpu.VMEM((n_slab, nb * LRU_PITCH, LANES), F32),
            pltpu.VMEM((nb, w), F32),
        ],
        compiler_params=_params(1),
        name="rglru",
    )(proj3, proj3, conv_w, conv_b.reshape(1, w), wa_bd, b_a.reshape(1, w), wx_bd,
      b_x.reshape(1, w), lam.reshape(1, w))


def _block_diag_halves(wblk):
    nb, d, _ = wblk.shape
    per = nb // 2
    eye = jnp.eye(per, dtype=wblk.dtype)
    halves = [jnp.einsum("nde,nm->ndme", wblk[h * per:(h + 1) * per], eye).reshape(per * d, per * d)
              for h in range(2)]
    return jnp.stack(halves).astype(BF16)


def kernel(x, norm_w, w_in_even, gla_w_a_up, gla_b_a, gla_norm_w, fox_b_f, w_out_even,
           w_in_odd, rel_bias, conv_w, conv_b, lru_w_a, lru_b_a, lru_w_x, lru_b_x,
           lru_lambda, w_out_odd, w_mlp_up, w_mlp_down):
    batch, t, d = x.shape
    x2d = x.reshape(batch * t, d)
    group = d // 2
    kw = GLA_HEADS * GLA_DK

    w_in = w_in_even[0]
    o_ga = 2 * kw + 2 * group
    o_fq = o_ga + GLA_RANK
    o_ff = o_fq + 3 * group
    q_scale = LOG2E * HEAD_DIM ** -0.5
    w_main = jnp.concatenate(
        [w_in[:, :o_ga], w_in[:, o_fq:o_fq + group] * q_scale, w_in[:, o_fq + group:o_ff]],
        axis=1).astype(BF16)
    n_small = FOX_HEADS + GLA_RANK
    w_small = jnp.concatenate(
        [w_in[:, o_ff:], w_in[:, o_ga:o_fq], jnp.zeros((d, LANES - n_small), F32)], axis=1).astype(BF16)
    wa_pad = jnp.zeros((LANES, kw), F32).at[FOX_HEADS:n_small].set(gla_w_a_up[0]).astype(BF16)
    fox_bias = jnp.zeros((1, LANES), F32).at[0, :FOX_HEADS].set(fox_b_f[0])

    m = batch * t
    proj, small = _norm_proj(x2d, norm_w[0, 0], [(w_main, BF16), (w_small, F32)])
    out_a = _gla(proj, small, wa_pad, gla_b_a[0], gla_norm_w[0], batch, t)
    cum = _fox_cum(small, fox_bias, batch, t)
    out_b = _fox(proj, cum, batch, t, o_ga)
    x2d = _mix_mlp(out_a, out_b, x2d, w_out_even[0], norm_w[0, 1:4], w_mlp_up, w_mlp_down, 0)

    w_in = w_in_odd[0]
    w_main = jnp.concatenate([w_in[:, :group] * q_scale, w_in[:, group:]], axis=1).astype(BF16)
    (proj,) = _norm_proj(x2d, norm_w[1, 0], [(w_main, BF16)])
    out_c = _chunk_attn(proj, _chunk_attn_table(rel_bias[0]), batch, t)
    out_d = _lru(proj.reshape(batch, t, w_main.shape[1]), conv_w[0], conv_b[0],
                 _block_diag_halves(lru_w_a[0]), lru_b_a[0],
                 _block_diag_halves(lru_w_x[0]), lru_b_x[0], lru_lambda[0], 3 * group)
    x2d = _mix_mlp(out_c, out_d.reshape(m, group), x2d, w_out_odd[0], norm_w[1, 1:4],
                   w_mlp_up, w_mlp_down, 1)
    return x2d.reshape(batch, t, d)
```

```python
import functools

import jax
import jax.numpy as jnp
from jax import lax
from jax.experimental import pallas as pl
from jax.experimental.pallas import tpu as pltpu

F32 = jnp.float32
BF16 = jnp.bfloat16

NORM_EPS = 1e-6
CHUNK = 64
GLA_HEADS = 4
GLA_DK = 64
GLA_DV = 128
GLA_RANK = 16
GLA_GATE_TAU = 16.0
FOX_HEADS = 8
HEAD_DIM = 64
CA_LEFT_CHUNKS = 8
REL_CLIP = 128
CONV_WIDTH = 4
LRU_BLOCKS = 8
LRU_C = 8.0

LANES = 128
MXU_DIM = 256
MASK_VALUE = -1e30
VMEM_LIMIT_BYTES = 56 * 1024 * 1024

ROW_TILE = 1024
PROJ_ROW_TILE = 1024
STAGE_ROWS = 512
STAGE_COLS = 1024
FOX_BLOCK = 512
FOX_KBLOCK = 512
ONES_ROWS = 16
CA_QBLOCK = 4 * CHUNK
CA_WINDOW = CA_QBLOCK + CA_LEFT_CHUNKS * CHUNK
CA_PAD = CA_LEFT_CHUNKS * CHUNK
CA_LINE = 1024
LOG2E = 1.4426950408889634
LN2 = 0.6931471805599453
CUM_TERMS = 3
GLA_GROUP = 4
LRU_ROWS = 256
LRU_FRAMES = 256
LRU_PITCH = LRU_FRAMES + 8
CONV_HIST = 8
SCAN_UNROLL = 8


def _params(n_axes):
    return pltpu.CompilerParams(
        dimension_semantics=("arbitrary",) * n_axes,
        vmem_limit_bytes=VMEM_LIMIT_BYTES,
    )


def _const_spec(shape):
    nd = len(shape)
    return pl.BlockSpec(shape, lambda *_: (0,) * nd, pipeline_mode=pl.Buffered(1))


def _rmsnorm(x, w):
    y = x * lax.rsqrt(jnp.mean(x * x, axis=-1, keepdims=True) + NORM_EPS)
    return y * w


def _log_sigmoid(z):
    return jnp.minimum(z, 0.0) - LN2 * jnp.log2(1.0 + jnp.exp2(-LOG2E * jnp.abs(z)))


def _sigmoid(z):
    return 0.5 * jnp.tanh(0.5 * z) + 0.5


def _dot(a, b):
    return jnp.dot(a, b, preferred_element_type=F32)


def _dot_nt(a, b):
    return lax.dot_general(a, b, (((1,), (1,)), ((), ())), preferred_element_type=F32)


def _dot_tn(a, b):
    return lax.dot_general(a, b, (((0,), (0,)), ((), ())), preferred_element_type=F32)


def _half_mask(lane, j):
    return lane < HEAD_DIM if j == 0 else lane >= HEAD_DIM


def _split3(x):
    h1 = x.astype(BF16)
    r1 = x - h1.astype(F32)
    h2 = r1.astype(BF16)
    h3 = (r1 - h2.astype(F32)).astype(BF16)
    return h1, h2, h3


def _row_major(i):
    return (i, 0)


def _norm_proj_kernel(x_ref, nw_ref, *refs, n_chunk):
    n_out = len(refs) // 2
    tm = x_ref.shape[0]
    for rows in (slice(0, tm // 2), slice(tm // 2, tm)):
        h = _rmsnorm(x_ref[rows, :], nw_ref[...]).astype(BF16)
        for w_ref, o_ref in zip(refs[:n_out], refs[n_out:]):
            n_total = o_ref.shape[1]
            for n0 in range(0, n_total, n_chunk):
                n1 = min(n0 + n_chunk, n_total)
                o_ref[rows, n0:n1] = _dot(h, w_ref[:, n0:n1]).astype(o_ref.dtype)


def _norm_proj(x2d, nw, outputs):
    m, d = x2d.shape
    in_specs = [pl.BlockSpec((PROJ_ROW_TILE, d), _row_major), _const_spec((1, d))]
    in_specs += [_const_spec(w.shape) for w, _ in outputs]
    return pl.pallas_call(
        functools.partial(_norm_proj_kernel, n_chunk=2 * MXU_DIM),
        grid=(m // PROJ_ROW_TILE,),
        in_specs=in_specs,
        out_specs=[pl.BlockSpec((PROJ_ROW_TILE, w.shape[1]), _row_major) for w, _ in outputs],
        out_shape=[jax.ShapeDtypeStruct((m, w.shape[1]), dt) for w, dt in outputs],
        compiler_params=_params(1),
        name="norm_proj",
    )(x2d, nw.reshape(1, d), *[w for w, _ in outputs])


def _stage_bf16_weights(pairs, stage_ref, sem_ref):
    pieces = []
    for src, dst in pairs:
        rows, cols = dst.shape
        for r0 in range(0, rows, STAGE_ROWS):
            for c0 in range(0, cols, STAGE_COLS):
                pieces.append((src, dst, r0, c0))

    def copy(k):
        src, _, r0, c0 = pieces[k]
        return pltpu.make_async_copy(
            src.at[pl.ds(r0, STAGE_ROWS), pl.ds(c0, STAGE_COLS)], stage_ref.at[k % 2], sem_ref.at[k % 2])

    copy(0).start()
    for k, (_, dst, r0, c0) in enumerate(pieces):
        if k + 1 < len(pieces):
            copy(k + 1).start()
        copy(k).wait()
        dst[r0:r0 + STAGE_ROWS, c0:c0 + STAGE_COLS] = stage_ref[k % 2].astype(BF16)


def _mix_mlp_kernel(a_ref, b_ref, x_ref, nw_ref, wo_hbm, wu_hbm, wd_hbm, o_ref,
                    wo_ref, wu_ref, wd_ref, u_ref, y_ref, stage_ref, sem_ref, *, chunk, layer):
    @pl.when(pl.program_id(0) == 0)
    def _():
        _stage_bf16_weights([(wo_hbm, wo_ref), (wu_hbm.at[layer], wu_ref), (wd_hbm.at[layer], wd_ref)],
                            stage_ref, sem_ref)

    d_ff = wu_ref.shape[1]
    d = wd_ref.shape[1]
    tm = x_ref.shape[0]
    halves = [slice(0, tm // 2), slice(tm // 2, tm)]

    def out_proj(rows):
        mix = jnp.concatenate([a_ref[rows, :], b_ref[rows, :]], axis=1)
        for n0 in range(0, d, chunk):
            y_ref[rows, n0:n0 + chunk] = _dot(mix, wo_ref[:, n0:n0 + chunk])

    def residual_and_norm(rows):
        x1 = x_ref[rows, :] + _rmsnorm(y_ref[rows, :], nw_ref[0:1, :])
        o_ref[rows, :] = x1
        return _rmsnorm(x1, nw_ref[1:2, :]).astype(BF16)

    def up_proj(rows, h):
        for f0 in range(0, d_ff, chunk):
            u = jnp.maximum(_dot(h, wu_ref[:, f0:f0 + chunk]), 0.0)
            u_ref[rows, f0:f0 + chunk] = (u * u).astype(BF16)

    def down_proj(rows):
        for n0 in range(0, d, chunk):
            y_ref[rows, n0:n0 + chunk] = _dot(u_ref[rows, :], wd_ref[:, n0:n0 + chunk])

    def finish(rows):
        o_ref[rows, :] = o_ref[rows, :] + _rmsnorm(y_ref[rows, :], nw_ref[2:3, :])

    first, second = halves
    out_proj(first)
    out_proj(second)
    h_first = residual_and_norm(first)
    up_proj(first, h_first)
    h_second = residual_and_norm(second)
    up_proj(second, h_second)
    down_proj(first)
    down_proj(second)
    finish(first)
    finish(second)


def _mix_mlp(mix_a, mix_b, x2d, w_out, nw3, w_up_all, w_down_all, layer):
    m, d = x2d.shape
    g = mix_a.shape[1]
    d_ff = w_up_all.shape[2]
    return pl.pallas_call(
        functools.partial(_mix_mlp_kernel, chunk=2 * MXU_DIM, layer=layer),
        grid=(m // ROW_TILE,),
        in_specs=[
            pl.BlockSpec((ROW_TILE, g), _row_major),
            pl.BlockSpec((ROW_TILE, g), _row_major),
            pl.BlockSpec((ROW_TILE, d), _row_major),
            _const_spec((3, d)),
            pl.BlockSpec(memory_space=pl.ANY),
            pl.BlockSpec(memory_space=pl.ANY),
            pl.BlockSpec(memory_space=pl.ANY),
        ],
        out_specs=pl.BlockSpec((ROW_TILE, d), _row_major),
        out_shape=jax.ShapeDtypeStruct((m, d), F32),
        scratch_shapes=[
            pltpu.VMEM((2 * g, d), BF16),
            pltpu.VMEM((d, d_ff), BF16),
            pltpu.VMEM((d_ff, d), BF16),
            pltpu.VMEM((ROW_TILE, d_ff), BF16),
            pltpu.VMEM((ROW_TILE, d), F32),
            pltpu.VMEM((2, STAGE_ROWS, STAGE_COLS), F32),
            pltpu.SemaphoreType.DMA((2,)),
        ],
        compiler_params=_params(1),
        name="mix_mlp",
    )(mix_a, mix_b, x2d, nw3, w_out, w_up_all, w_down_all)


def _gla_kernel(q_ref, k_ref, v_ref, r_ref, sm_ref, wa_ref, ba_ref, nw_ref, o_ref,
                la_ref, st_ref, dec_ref):
    t = q_ref.shape[0]
    nc = t // CHUNK
    n_pairs = GLA_HEADS // 2

    z = _dot(sm_ref[...].astype(BF16), wa_ref[...]) + ba_ref[...]
    la_ref[...] = _log_sigmoid(z) * (1.0 / GLA_GATE_TAU)

    grp = GLA_GROUP * CHUNK
    row = lax.broadcasted_iota(jnp.int32, (grp, grp), 0)
    col = lax.broadcasted_iota(jnp.int32, (grp, grp), 1)
    tri = jnp.where((row >= col) & (row // CHUNK == col // CHUNK), 1.0, 0.0).astype(BF16)
    erow = lax.broadcasted_iota(jnp.int32, (grp, GLA_GROUP * LANES), 0)
    ecol = lax.broadcasted_iota(jnp.int32, (grp, GLA_GROUP * LANES), 1)
    chunk_ones = jnp.where(erow // CHUNK == ecol // LANES, 1.0, 0.0).astype(BF16)

    def increments(g, carry):
        r0 = pl.multiple_of(g * grp, grp)
        la = la_ref[pl.ds(r0, grp), :]
        hi = la.astype(BF16)
        lo = (la - hi.astype(F32)).astype(BF16)
        cum = _dot(tri, hi) + _dot(tri, lo)
        dec = jnp.exp(_dot_tn(hi, chunk_ones) + _dot_tn(lo, chunk_ones))
        k_g = k_ref[pl.ds(r0, grp), :].astype(F32)
        v_g = v_ref[pl.ds(r0, grp), :]
        k_dec = []
        for c in range(GLA_GROUP):
            rows = slice(c * CHUNK, (c + 1) * CHUNK)
            total = cum[(c + 1) * CHUNK - 1:(c + 1) * CHUNK, :]
            k_dec.append((k_g[rows] * jnp.exp(total - cum[rows])).astype(BF16))
        for c in range(GLA_GROUP):
            rows = slice(c * CHUNK, (c + 1) * CHUNK)
            dec_ref[g * GLA_GROUP + c] = dec[:, c * LANES:(c + 1) * LANES]
            for h in range(GLA_HEADS):
                p, j = divmod(h, 2)
                full = _dot_tn(k_dec[c][:, p * LANES:(p + 1) * LANES], v_g[rows, h * GLA_DV:(h + 1) * GLA_DV])
                st_ref[g * GLA_GROUP + c, p, j * GLA_DK:(j + 1) * GLA_DK, :] = full[j * GLA_DK:(j + 1) * GLA_DK, :]
        return carry

    lax.fori_loop(0, nc // GLA_GROUP, increments, 0)

    def scan(c, carry):
        for p in range(n_pairs):
            st_ref[c, p] = st_ref[c, p] + dec_ref[c, p * LANES:(p + 1) * LANES, :] * st_ref[c - 1, p]
        return carry

    lax.fori_loop(1, nc, scan, 0)

    lane = lax.broadcasted_iota(jnp.int32, (CHUNK, LANES), 1)
    scale = GLA_DK ** -0.5

    def outputs(g, carry):
        r0 = pl.multiple_of(g * grp, grp)
        q_g = q_ref[pl.ds(r0, grp), :] * jnp.asarray(scale, BF16)
        raw = []
        for c in range(GLA_GROUP):
            rows = slice(c * CHUNK, (c + 1) * CHUNK)
            for h in range(GLA_HEADS):
                p, j = divmod(h, 2)
                q_pair = q_g[rows, p * LANES:(p + 1) * LANES]
                q_h = jnp.where(_half_mask(lane, j), q_pair, jnp.zeros_like(q_pair))
                raw.append(_dot(q_h, st_ref[g * GLA_GROUP + c, p].astype(BF16)))
        for c in range(GLA_GROUP):
            for h in range(GLA_HEADS):
                o = raw[c * GLA_HEADS + h]
                o = o * lax.rsqrt(jnp.mean(o * o, axis=-1, keepdims=True) + NORM_EPS)
                o = o * nw_ref[:, h * GLA_DV:(h + 1) * GLA_DV]
                rows = pl.ds(r0 + c * CHUNK, CHUNK)
                r_h = r_ref[rows, h * GLA_DV:(h + 1) * GLA_DV].astype(F32)
                o_ref[rows, h * GLA_DV:(h + 1) * GLA_DV] = (o * (r_h * _sigmoid(r_h))).astype(BF16)
        return carry

    lax.fori_loop(0, nc // GLA_GROUP, outputs, 0)


def _gla(proj, small, wa_pad, b_a, norm_w, batch, t):
    kw = GLA_HEADS * GLA_DK
    gw = GLA_HEADS * GLA_DV
    nc = t // CHUNK
    return pl.pallas_call(
        _gla_kernel,
        grid=(batch,),
        in_specs=[
            pl.BlockSpec((t, kw), lambda b: (b, 0)),
            pl.BlockSpec((t, kw), lambda b: (b, 1)),
            pl.BlockSpec((t, gw), lambda b: (b, 1)),
            pl.BlockSpec((t, gw), lambda b: (b, 2)),
            pl.BlockSpec((t, LANES), lambda b: (b, 0)),
            _const_spec((LANES, kw)),
            _const_spec((1, kw)),
            _const_spec((1, gw)),
        ],
        out_specs=pl.BlockSpec((t, gw), lambda b: (b, 0)),
        out_shape=jax.ShapeDtypeStruct((batch * t, gw), BF16),
        scratch_shapes=[
            pltpu.VMEM((t, kw), F32),
            pltpu.VMEM((nc, GLA_HEADS // 2, LANES, GLA_DV), F32),
            pltpu.VMEM((nc, kw, LANES), F32),
        ],
        compiler_params=_params(1),
        name="gla",
    )(proj, proj, proj, proj, small, wa_pad, b_a.reshape(1, kw), norm_w.reshape(1, gw))


def _fox_cum_kernel(sm_ref, bias_ref, o_ref):
    t = sm_ref.shape[0]
    row = lax.broadcasted_iota(jnp.int32, (LANES, LANES), 0)
    col = lax.broadcasted_iota(jnp.int32, (LANES, LANES), 1)
    tri = jnp.where(row >= col, 1.0, 0.0).astype(BF16)
    carry = jnp.zeros((1, LANES), F32)
    for blk in range(t // LANES):
        rows = slice(blk * LANES, (blk + 1) * LANES)
        ls = _log_sigmoid(sm_ref[rows, :] + bias_ref[...])
        h1, h2, h3 = _split3(ls)
        cb = _dot(tri, h1) + _dot(tri, h2) + _dot(tri, h3) + carry
        o_ref[rows, :] = cb
        carry = cb[LANES - 1:LANES, :]


def _fox_cum(small, bias_row, batch, t):
    return pl.pallas_call(
        _fox_cum_kernel,
        grid=(batch,),
        in_specs=[pl.BlockSpec((t, LANES), lambda b: (b, 0)), _const_spec((1, LANES))],
        out_specs=pl.BlockSpec((t, LANES), lambda b: (b, 0)),
        out_shape=jax.ShapeDtypeStruct((batch * t, LANES), F32),
        compiler_params=_params(1),
        name="fox_cum",
    )(small, bias_row)


def _fox_kernel(q_ref, k_ref, v_ref, c_ref, o_ref, ka_ref, vt_ref, s_ref):
    t = k_ref.shape[0]
    blk = FOX_BLOCK
    kbs = FOX_KBLOCK
    pair = pl.program_id(1)

    sel_r = lax.broadcasted_iota(jnp.int32, (LANES, LANES), 0)
    sel_c = lax.broadcasted_iota(jnp.int32, (LANES, LANES), 1)
    sels = []
    for term in range(CUM_TERMS):
        hit = (((sel_r == 2 * pair) & (sel_c == term))
               | ((sel_r == 2 * pair + 1) & (sel_c == CUM_TERMS + term)))
        sels.append(jnp.where(hit, 1.0, 0.0).astype(BF16))
    for kb in range(t // kbs):
        rows = slice(kb * kbs, (kb + 1) * kbs)
        vt_ref[kb] = v_ref[rows, :].astype(F32).T.astype(BF16)
        parts = _split3(c_ref[rows, :] * (-LOG2E))
        extra = _dot(parts[0], sels[0]) + _dot(parts[1], sels[1]) + _dot(parts[2], sels[2])
        ka_ref[kb, :, 0:LANES] = k_ref[rows, :]
        ka_ref[kb, :, LANES:2 * LANES] = extra.astype(BF16)

    lane = lax.broadcasted_iota(jnp.int32, (blk, LANES), 1)
    krow = lax.broadcasted_iota(jnp.int32, (kbs, blk), 0)
    qcol = lax.broadcasted_iota(jnp.int32, (kbs, blk), 1)
    ahead = qcol - krow

    def augmented_queries(qi):
        q = q_ref[qi * blk:(qi + 1) * blk, :]
        q_aug = []
        for j in range(2):
            q_h = jnp.where(_half_mask(lane, j), q, jnp.zeros_like(q))
            ones = jnp.where((lane >= CUM_TERMS * j) & (lane < CUM_TERMS * (j + 1)), 1.0, 0.0).astype(BF16)
            q_aug.append(jnp.concatenate([q_h, ones], axis=1).astype(F32).T.astype(BF16))
        return q_aug

    def scores(q_aug, qi, kb, slot):
        k_blk = ka_ref[kb]
        offset = kb * kbs - qi * blk
        for j in range(2):
            s = _dot(k_blk, q_aug[j])
            if offset + kbs - 1 > 0:
                s = jnp.where(ahead >= offset, s, MASK_VALUE)
            s_ref[slot, j] = s

    def accumulate(kb, slot, state):
        new = []
        for j in range(2):
            m, l, acc = state[j]
            m_new = jnp.maximum(m, jnp.max(s_ref[slot, j], axis=0, keepdims=True))
            alpha = jnp.exp2(m - m_new)
            p = jnp.exp2(s_ref[slot, j] - m_new)
            l = l * alpha + jnp.sum(p, axis=0, keepdims=True)
            pv = _dot(vt_ref[kb, j * HEAD_DIM:(j + 1) * HEAD_DIM, :], p.astype(BF16))
            new.append((m_new, l, acc * alpha + pv))
        return new

    tasks = [(qi, kb) for qi in range(t // blk) for kb in range((qi + 1) * blk // kbs)]
    fresh = [(jnp.full((1, blk), MASK_VALUE, F32), jnp.zeros((1, blk), F32),
              jnp.zeros((HEAD_DIM, blk), F32))] * 2
    q_aug = augmented_queries(0)
    scores(q_aug, 0, 0, 0)
    state = fresh
    for n, (qi, kb) in enumerate(tasks):
        if n + 1 < len(tasks):
            qi_next, kb_next = tasks[n + 1]
            if qi_next != qi:
                q_aug = augmented_queries(qi_next)
            scores(q_aug, qi_next, kb_next, (n + 1) % 2)
        state = accumulate(kb, n % 2, state)
        if n + 1 == len(tasks) or tasks[n + 1][0] != qi:
            outs = [acc * (1.0 / l) for _, l, acc in state]
            o_ref[qi * blk:(qi + 1) * blk, :] = jnp.concatenate(outs, axis=0).T.astype(BF16)
            state = fresh


def _fox(proj, cum, batch, t, col0):
    nq = t // FOX_BLOCK
    n_pairs = FOX_HEADS // 2
    qc, kc, vc = col0 // LANES, col0 // LANES + n_pairs, col0 // LANES + 2 * n_pairs
    return pl.pallas_call(
        _fox_kernel,
        grid=(batch, n_pairs),
        in_specs=[
            pl.BlockSpec((t, LANES), lambda b, p: (b, qc + p)),
            pl.BlockSpec((t, LANES), lambda b, p: (b, kc + p)),
            pl.BlockSpec((t, LANES), lambda b, p: (b, vc + p)),
            pl.BlockSpec((t, LANES), lambda b, p: (b, 0)),
        ],
        out_specs=pl.BlockSpec((t, LANES), lambda b, p: (b, p)),
        out_shape=jax.ShapeDtypeStruct((batch * t, FOX_HEADS * HEAD_DIM), BF16),
        scratch_shapes=[
            pltpu.VMEM((t // FOX_KBLOCK, FOX_KBLOCK, 2 * LANES), BF16),
            pltpu.VMEM((t // FOX_KBLOCK, LANES, FOX_KBLOCK), BF16),
            pltpu.VMEM((2, 2, FOX_KBLOCK, FOX_BLOCK), F32),
        ],
        compiler_params=_params(2),
        name="fox",
    )(proj, proj, proj, cum)


def _chunk_attn_kernel(q_ref, k_ref, v_ref, tab_ref, o_ref, kp_ref, vt_ref, s_ref):
    t, g = q_ref.shape
    qb = CA_QBLOCK
    n_pad = CA_PAD // qb
    n_win = CA_WINDOW // qb

    for kb in range(n_pad):
        kp_ref[kb] = jnp.zeros((qb, g), BF16)
        vt_ref[kb] = jnp.zeros(vt_ref.shape[1:], BF16)
    for kb in range(t // qb):
        rows = slice(kb * qb, (kb + 1) * qb)
        kp_ref[n_pad + kb] = k_ref[rows, :]
        v_t = v_ref[rows, :].T
        for h in range(g // HEAD_DIM):
            vt_ref[n_pad + kb, h, 0:HEAD_DIM, :] = v_t[h * HEAD_DIM:(h + 1) * HEAD_DIM, :]
            vt_ref[n_pad + kb, h, HEAD_DIM:HEAD_DIM + ONES_ROWS, :] = jnp.ones((ONES_ROWS, qb), BF16)

    lane = lax.broadcasted_iota(jnp.int32, (qb, LANES), 1)
    n_heads = g // HEAD_DIM
    lax.fori_loop(0, t // qb, functools.partial(
        _chunk_attn_block, q_ref=q_ref, tab_ref=tab_ref, o_ref=o_ref, kp_ref=kp_ref, vt_ref=vt_ref,
        s_ref=s_ref, lane=lane, n_heads=n_heads, n_pad=n_pad, n_win=n_win), 0)


def _chunk_attn_block(i, carry, *, q_ref, tab_ref, o_ref, kp_ref, vt_ref, s_ref, lane, n_heads,
                      n_pad, n_win):
    qb = CA_QBLOCK
    q_rows = pl.ds(pl.multiple_of(i * qb, qb), qb)

    def scores(h):
        pair, j = divmod(h, 2)
        cols = slice(pair * LANES, (pair + 1) * LANES)
        q_pair = q_ref[q_rows, cols]
        q_h = jnp.where(_half_mask(lane, j), q_pair, jnp.zeros_like(q_pair))
        q_ht = q_h.T
        for w in range(n_win):
            tab_blk = jnp.where(i + w >= n_pad, w, n_win)
            s_ref[h % 2, w] = _dot(kp_ref[i + w, :, cols], q_ht) + tab_ref[h, tab_blk]

    def weighted_values(h):
        m = s_ref[h % 2, 0].max(axis=0, keepdims=True)
        for w in range(1, n_win):
            m = jnp.maximum(m, s_ref[h % 2, w].max(axis=0, keepdims=True))
        acc = jnp.zeros((HEAD_DIM + ONES_ROWS, qb), F32)
        for w in range(n_win):
            p = jnp.exp2((s_ref[h % 2, w] - m).astype(BF16))
            acc = acc + _dot(vt_ref[i + w, h], p)
        return acc[0:HEAD_DIM] * (1.0 / acc[HEAD_DIM:HEAD_DIM + 1])

    outs = []
    scores(0)
    for h in range(n_heads):
        if h + 1 < n_heads:
            scores(h + 1)
        outs.append(weighted_values(h))
    o_ref[q_rows, :] = jnp.concatenate(outs, axis=0).T.astype(BF16)
    return carry


def _chunk_attn(proj, table, batch, t):
    g = table.shape[0] * HEAD_DIM
    n_blocks = (CA_PAD + t) // CA_QBLOCK
    return pl.pallas_call(
        _chunk_attn_kernel,
        grid=(batch,),
        in_specs=[
            pl.BlockSpec((t, g), lambda b: (b, 0)),
            pl.BlockSpec((t, g), lambda b: (b, 1)),
            pl.BlockSpec((t, g), lambda b: (b, 2)),
            _const_spec(table.shape),
        ],
        out_specs=pl.BlockSpec((t, g), lambda b: (b, 0)),
        out_shape=jax.ShapeDtypeStruct((batch * t, g), BF16),
        scratch_shapes=[
            pltpu.VMEM((n_blocks, CA_QBLOCK, g), BF16),
            pltpu.VMEM((n_blocks, g // HEAD_DIM, HEAD_DIM + ONES_ROWS, CA_QBLOCK), BF16),
            pltpu.VMEM((2, CA_WINDOW // CA_QBLOCK, CA_QBLOCK, CA_QBLOCK), F32),
        ],
        compiler_params=_params(1),
        name="chunk_attn",
    )(proj, proj, proj, table)


def _ca_table_kernel(line_ref, o_ref):
    n_win = CA_WINDOW // CA_QBLOCK
    rows = 8
    x = jnp.broadcast_to(line_ref[0] * LOG2E, (rows, CA_LINE))
    qc = lax.broadcasted_iota(jnp.int32, (rows, CA_QBLOCK), 1) // CHUNK
    for grp in range(CA_WINDOW // rows):
        r0 = grp * rows
        y = pltpu.roll(x, (CA_LINE - CA_WINDOW + 1 + r0) % CA_LINE, axis=1, stride=1, stride_axis=0)
        kc = r0 // CHUNK
        allowed = (qc <= kc) & (qc >= kc - CA_LEFT_CHUNKS)
        w, r = divmod(r0, CA_QBLOCK)
        o_ref[0, w, r:r + rows, :] = jnp.where(allowed, y[:, 0:CA_QBLOCK], MASK_VALUE)
    o_ref[0, n_win] = jnp.full((CA_QBLOCK, CA_QBLOCK), MASK_VALUE, F32)


def _chunk_attn_table(rel_bias):
    heads = rel_bias.shape[0]
    n_win = CA_WINDOW // CA_QBLOCK
    left = CA_WINDOW - 1 - CA_PAD - REL_CLIP
    line = jnp.pad(rel_bias.astype(F32), ((0, 0), (left, CA_LINE - left - rel_bias.shape[1])), mode="edge")
    return pl.pallas_call(
        _ca_table_kernel,
        grid=(heads,),
        in_specs=[pl.BlockSpec((1, 1, CA_LINE), lambda h: (h, 0, 0))],
        out_specs=pl.BlockSpec((1, n_win + 1, CA_QBLOCK, CA_QBLOCK), lambda h: (h, 0, 0, 0)),
        out_shape=jax.ShapeDtypeStruct((heads, n_win + 1, CA_QBLOCK, CA_QBLOCK), F32),
        compiler_params=_params(1),
        name="ca_table",
    )(line.reshape(heads, 1, CA_LINE))


GELU_C0 = 0.7978845608028654
GELU_C1 = GELU_C0 * 0.044715


def _gelu(x):
    inner = x * (GELU_C0 + GELU_C1 * (x * x))
    return (0.5 * x) * (1.0 + jnp.tanh(inner))


def _lru_kernel(g_ref, x_ref, cw_ref, cb_ref, wa_ref, ba_ref, wx_ref, bx_ref, lam_ref, o_ref,
                xin_ref, xf_ref, a_ref, b_ref, hout_ref, h_ref):
    nb, frames, w = x_ref.shape
    n_slab = w // LANES
    sub = LRU_ROWS // nb
    i = pl.program_id(0)

    @pl.when(i == 0)
    def _():
        xf_ref[0:CONV_HIST] = jnp.zeros((CONV_HIST, nb, w), F32)
        h_ref[...] = jnp.zeros((nb, w), F32)

    for b in range(nb):
        xb = x_ref[b].astype(F32)
        for s in range(n_slab):
            xin_ref[s, b * LRU_PITCH:b * LRU_PITCH + frames, :] = xb[:, s * LANES:(s + 1) * LANES]

    def gather(t, carry):
        for s in range(n_slab):
            xf_ref[CONV_HIST + t, :, s * LANES:(s + 1) * LANES] = (
                xin_ref[s, pl.ds(t, nb, stride=LRU_PITCH), :])
        return carry

    lax.fori_loop(0, frames, gather, 0, unroll=SCAN_UNROLL)

    log2_base = (LRU_C * LOG2E) * _log_sigmoid(lam_ref[...])
    half = w // 2
    for blk in range(frames // sub):
        t0 = blk * sub
        xc = cb_ref[...]
        for tap in range(CONV_WIDTH):
            lo = CONV_HIST + t0 - tap
            xc = xc + (cw_ref[CONV_WIDTH - 1 - tap:CONV_WIDTH - tap, :]
                       * xf_ref[lo:lo + sub].reshape(LRU_ROWS, w))
        xcb = xc.astype(BF16)
        gr, gi = [], []
        for hb in range(2):
            cols = slice(hb * half, (hb + 1) * half)
            gr.append(_dot(xcb[:, cols], wa_ref[hb]))
            gi.append(_dot(xcb[:, cols], wx_ref[hb]))
        r = _sigmoid(jnp.concatenate(gr, axis=1) + ba_ref[...])
        gate_i = _sigmoid(jnp.concatenate(gi, axis=1) + bx_ref[...])
        a = jnp.exp2(r * log2_base)
        a_ref[t0:t0 + sub] = a.reshape(sub, nb, w)
        y = 1.0 - a * a
        root = jnp.where(y > 0.0, y * lax.rsqrt(y), 0.0)
        b_ref[t0:t0 + sub] = (root * (gate_i * xc)).reshape(sub, nb, w)

    xf_ref[0:CONV_HIST] = xf_ref[frames:frames + CONV_HIST]

    def scan(t, h):
        h = a_ref[t] * h + b_ref[t]
        for s in range(n_slab):
            hout_ref[s, pl.ds(t, nb, stride=LRU_PITCH), :] = h[:, s * LANES:(s + 1) * LANES]
        return h

    h_ref[...] = lax.fori_loop(0, frames, scan, h_ref[...], unroll=SCAN_UNROLL)

    for b in range(nb):
        rows = slice(b * LRU_PITCH, b * LRU_PITCH + frames)
        hv = jnp.concatenate([hout_ref[s, rows, :] for s in range(n_slab)], axis=1)
        o_ref[b] = (hv * _gelu(g_ref[b].astype(F32))).astype(BF16)


def _lru(proj3, conv_w, conv_b, wa_bd, b_a, wx_bd, b_x, lam, col0):
    nb, t, _ = proj3.shape
    w = conv_w.shape[1]
    gc = col0 // w
    n_slab = w // LANES
    return pl.pallas_call(
        _lru_kernel,
        grid=(t // LRU_FRAMES,),
        in_specs=[
            pl.BlockSpec((nb, LRU_FRAMES, w), lambda i: (0, i, gc)),
            pl.BlockSpec((nb, LRU_FRAMES, w), lambda i: (0, i, gc + 1)),
            _const_spec((CONV_WIDTH, w)),
            _const_spec((1, w)),
            _const_spec(wa_bd.shape),
            _const_spec((1, w)),
            _const_spec(wx_bd.shape),
            _const_spec((1, w)),
            _const_spec((1, w)),
        ],
        out_specs=pl.BlockSpec((nb, LRU_FRAMES, w), lambda i: (0, i, 0)),
        out_shape=jax.ShapeDtypeStruct((nb, t, w), BF16),
        scratch_shapes=[
            pltpu.VMEM((n_slab, nb * LRU_PITCH, LANES), F32),
            pltpu.VMEM((CONV_HIST + LRU_FRAMES, nb, w), F32),
            pltpu.VMEM((LRU_FRAMES, nb, w), F32),
            pltpu.VMEM((LRU_FRAMES, nb, w), F32),
            pltpu.VMEM((n_slab, nb * LRU_PITCH, LANES), F32),
            pltpu.VMEM((nb, w), F32),
        ],
        compiler_params=_params(1),
        name="rglru",
    )(proj3, proj3, conv_w, conv_b.reshape(1, w), wa_bd, b_a.reshape(1, w), wx_bd,
      b_x.reshape(1, w), lam.reshape(1, w))


def _block_diag_halves(wblk):
    nb, d, _ = wblk.shape
    per = nb // 2
    eye = jnp.eye(per, dtype=wblk.dtype)
    halves = [jnp.einsum("nde,nm->ndme", wblk[h * per:(h + 1) * per], eye).reshape(per * d, per * d)
              for h in range(2)]
    return jnp.stack(halves).astype(BF16)


def kernel(x, norm_w, w_in_even, gla_w_a_up, gla_b_a, gla_norm_w, fox_b_f, w_out_even,
           w_in_odd, rel_bias, conv_w, conv_b, lru_w_a, lru_b_a, lru_w_x, lru_b_x,
           lru_lambda, w_out_odd, w_mlp_up, w_mlp_down):
    batch, t, d = x.shape
    x2d = x.reshape(batch * t, d)
    group = d // 2
    kw = GLA_HEADS * GLA_DK

    w_in = w_in_even[0]
    o_ga = 2 * kw + 2 * group
    o_fq = o_ga + GLA_RANK
    o_ff = o_fq + 3 * group
    q_scale = LOG2E * HEAD_DIM ** -0.5
    w_main = jnp.concatenate(
        [w_in[:, :o_ga], w_in[:, o_fq:o_fq + group] * q_scale, w_in[:, o_fq + group:o_ff]],
        axis=1).astype(BF16)
    n_small = FOX_HEADS + GLA_RANK
    w_small = jnp.concatenate(
        [w_in[:, o_ff:], w_in[:, o_ga:o_fq], jnp.zeros((d, LANES - n_small), F32)], axis=1).astype(BF16)
    wa_pad = jnp.zeros((LANES, kw), F32).at[FOX_HEADS:n_small].set(gla_w_a_up[0]).astype(BF16)
    fox_bias = jnp.zeros((1, LANES), F32).at[0, :FOX_HEADS].set(fox_b_f[0])

    m = batch * t
    proj, small = _norm_proj(x2d, norm_w[0, 0], [(w_main, BF16), (w_small, F32)])
    out_a = _gla(proj, small, wa_pad, gla_b_a[0], gla_norm_w[0], batch, t)
    cum = _fox_cum(small, fox_bias, batch, t)
    out_b = _fox(proj, cum, batch, t, o_ga)
    x2d = _mix_mlp(out_a, out_b, x2d, w_out_even[0], norm_w[0, 1:4], w_mlp_up, w_mlp_down, 0)

    w_in = w_in_odd[0]
    w_main = jnp.concatenate([w_in[:, :group] * q_scale, w_in[:, group:]], axis=1).astype(BF16)
    (proj,) = _norm_proj(x2d, norm_w[1, 0], [(w_main, BF16)])
    out_c = _chunk_attn(proj, _chunk_attn_table(rel_bias[0]), batch, t)
    out_d = _lru(proj.reshape(batch, t, w_main.shape[1]), conv_w[0], conv_b[0],
                 _block_diag_halves(lru_w_a[0]), lru_b_a[0],
                 _block_diag_halves(lru_w_x[0]), lru_b_x[0], lru_lambda[0], 3 * group)
    x2d = _mix_mlp(out_c, out_d.reshape(m, group), x2d, w_out_odd[0], norm_w[1, 1:4],
                   w_mlp_up, w_mlp_down, 1)
    return x2d.reshape(batch, t, d)
```

```python
import functools

import jax
import jax.numpy as jnp
from jax import lax
from jax.experimental import pallas as pl
from jax.experimental.pallas import tpu as pltpu

F32 = jnp.float32
BF16 = jnp.bfloat16

NORM_EPS = 1e-6
CHUNK = 64
GLA_HEADS = 4
GLA_DK = 64
GLA_DV = 128
GLA_RANK = 16
GLA_GATE_TAU = 16.0
FOX_HEADS = 8
HEAD_DIM = 64
CA_LEFT_CHUNKS = 8
REL_CLIP = 128
CONV_WIDTH = 4
LRU_BLOCKS = 8
LRU_C = 8.0

LANES = 128
MXU_DIM = 256
MASK_VALUE = -1e30
VMEM_LIMIT_BYTES = 56 * 1024 * 1024

ROW_TILE = 1024
PROJ_ROW_TILE = 1024
STAGE_ROWS = 512
STAGE_COLS = 1024
FOX_BLOCK = 512
FOX_KBLOCK = 512
ONES_ROWS = 16
CA_QBLOCK = 4 * CHUNK
CA_WINDOW = CA_QBLOCK + CA_LEFT_CHUNKS * CHUNK
CA_PAD = CA_LEFT_CHUNKS * CHUNK
CA_LINE = 1024
LOG2E = 1.4426950408889634
LN2 = 0.6931471805599453
CUM_TERMS = 3
GLA_GROUP = 4
LRU_ROWS = 256
LRU_FRAMES = 256
LRU_PITCH = LRU_FRAMES + 8
CONV_HIST = 8
SCAN_UNROLL = 8


def _params(n_axes):
    return pltpu.CompilerParams(
        dimension_semantics=("arbitrary",) * n_axes,
        vmem_limit_bytes=VMEM_LIMIT_BYTES,
    )


def _const_spec(shape):
    nd = len(shape)
    return pl.BlockSpec(shape, lambda *_: (0,) * nd, pipeline_mode=pl.Buffered(1))


def _rmsnorm(x, w):
    y = x * lax.rsqrt(jnp.mean(x * x, axis=-1, keepdims=True) + NORM_EPS)
    return y * w


def _log_sigmoid(z):
    return jnp.minimum(z, 0.0) - LN2 * jnp.log2(1.0 + jnp.exp2(-LOG2E * jnp.abs(z)))


def _sigmoid(z):
    return 0.5 * jnp.tanh(0.5 * z) + 0.5


def _dot(a, b):
    return jnp.dot(a, b, preferred_element_type=F32)


def _dot_nt(a, b):
    return lax.dot_general(a, b, (((1,), (1,)), ((), ())), preferred_element_type=F32)


def _dot_tn(a, b):
    return lax.dot_general(a, b, (((0,), (0,)), ((), ())), preferred_element_type=F32)


def _half_mask(lane, j):
    return lane < HEAD_DIM if j == 0 else lane >= HEAD_DIM


def _split3(x):
    h1 = x.astype(BF16)
    r1 = x - h1.astype(F32)
    h2 = r1.astype(BF16)
    h3 = (r1 - h2.astype(F32)).astype(BF16)
    return h1, h2, h3


def _row_major(i):
    return (i, 0)


def _norm_proj_kernel(x_ref, nw_ref, w_ref, *refs, splits, n_chunk):
    with_small = len(refs) == len(splits) + 2
    o_refs = refs[1:1 + len(splits)] if with_small else refs
    tm = x_ref.shape[0]
    for rows in (slice(0, tm // 2), slice(tm // 2, tm)):
        h = _rmsnorm(x_ref[rows, :], nw_ref[...]).astype(BF16)
        for (c0, width, by_pair), o_ref in zip(splits, o_refs):
            for n0 in range(0, width, n_chunk):
                n1 = min(n0 + n_chunk, width)
                res = _dot(h, w_ref[:, c0 + n0:c0 + n1]).astype(o_ref.dtype)
                if by_pair:
                    for p in range(n0 // LANES, n1 // LANES):
                        o_ref[p, rows, :] = res[:, p * LANES - n0:(p + 1) * LANES - n0]
                else:
                    o_ref[rows, n0:n1] = res
        if with_small:
            refs[-1][rows, :] = _dot(h, refs[0][...])


def _norm_proj(x2d, nw, w_main, splits, w_small=None):
    m, d = x2d.shape
    in_specs = [pl.BlockSpec((PROJ_ROW_TILE, d), _row_major), _const_spec((1, d)),
                _const_spec(w_main.shape)]
    args = [x2d, nw.reshape(1, d), w_main]
    out_specs, out_shape = [], []
    for _, width, by_pair in splits:
        if by_pair:
            out_specs.append(pl.BlockSpec((width // LANES, PROJ_ROW_TILE, LANES), lambda i: (0, i, 0)))
            out_shape.append(jax.ShapeDtypeStruct((width // LANES, m, LANES), BF16))
        else:
            out_specs.append(pl.BlockSpec((PROJ_ROW_TILE, width), _row_major))
            out_shape.append(jax.ShapeDtypeStruct((m, width), BF16))
    if w_small is not None:
        in_specs.append(_const_spec(w_small.shape))
        args.append(w_small)
        out_specs.append(pl.BlockSpec((PROJ_ROW_TILE, w_small.shape[1]), _row_major))
        out_shape.append(jax.ShapeDtypeStruct((m, w_small.shape[1]), F32))
    return pl.pallas_call(
        functools.partial(_norm_proj_kernel, splits=tuple(splits), n_chunk=2 * MXU_DIM),
        grid=(m // PROJ_ROW_TILE,),
        in_specs=in_specs,
        out_specs=out_specs,
        out_shape=out_shape,
        compiler_params=_params(1),
        name="norm_proj",
    )(*args)


def _stage_bf16_weights(pairs, stage_ref, sem_ref):
    pieces = []
    for src, dst in pairs:
        rows, cols = dst.shape
        for r0 in range(0, rows, STAGE_ROWS):
            for c0 in range(0, cols, STAGE_COLS):
                pieces.append((src, dst, r0, c0))

    def copy(k):
        src, _, r0, c0 = pieces[k]
        return pltpu.make_async_copy(
            src.at[pl.ds(r0, STAGE_ROWS), pl.ds(c0, STAGE_COLS)], stage_ref.at[k % 2], sem_ref.at[k % 2])

    copy(0).start()
    for k, (_, dst, r0, c0) in enumerate(pieces):
        if k + 1 < len(pieces):
            copy(k + 1).start()
        copy(k).wait()
        dst[r0:r0 + STAGE_ROWS, c0:c0 + STAGE_COLS] = stage_ref[k % 2].astype(BF16)


def _mix_mlp_kernel(a_ref, b_ref, x_ref, nw_ref, wo_hbm, wu_hbm, wd_hbm, o_ref,
                    wo_ref, wu_ref, wd_ref, u_ref, y_ref, stage_ref, sem_ref, *, chunk, layer):
    @pl.when(pl.program_id(0) == 0)
    def _():
        _stage_bf16_weights([(wo_hbm, wo_ref), (wu_hbm.at[layer], wu_ref), (wd_hbm.at[layer], wd_ref)],
                            stage_ref, sem_ref)

    d_ff = wu_ref.shape[1]
    d = wd_ref.shape[1]
    tm = x_ref.shape[0]
    halves = [slice(0, tm // 2), slice(tm // 2, tm)]

    def out_proj(rows):
        if len(b_ref.shape) == 3:
            b = jnp.concatenate([b_ref[p, rows, :] for p in range(b_ref.shape[0])], axis=1)
        else:
            b = b_ref[rows, :]
        mix = jnp.concatenate([a_ref[rows, :], b], axis=1)
        for n0 in range(0, d, chunk):
            y_ref[rows, n0:n0 + chunk] = _dot(mix, wo_ref[:, n0:n0 + chunk])

    def residual_and_norm(rows):
        x1 = x_ref[rows, :] + _rmsnorm(y_ref[rows, :], nw_ref[0:1, :])
        o_ref[rows, :] = x1
        return _rmsnorm(x1, nw_ref[1:2, :]).astype(BF16)

    def up_proj(rows, h):
        for f0 in range(0, d_ff, chunk):
            u = jnp.maximum(_dot(h, wu_ref[:, f0:f0 + chunk]), 0.0)
            u_ref[rows, f0:f0 + chunk] = (u * u).astype(BF16)

    def down_proj(rows):
        for n0 in range(0, d, chunk):
            y_ref[rows, n0:n0 + chunk] = _dot(u_ref[rows, :], wd_ref[:, n0:n0 + chunk])

    def finish(rows):
        o_ref[rows, :] = o_ref[rows, :] + _rmsnorm(y_ref[rows, :], nw_ref[2:3, :])

    first, second = halves
    out_proj(first)
    out_proj(second)
    h_first = residual_and_norm(first)
    up_proj(first, h_first)
    h_second = residual_and_norm(second)
    up_proj(second, h_second)
    down_proj(first)
    down_proj(second)
    finish(first)
    finish(second)


def _mix_mlp(mix_a, mix_b, x2d, w_out, nw3, w_up_all, w_down_all, layer):
    m, d = x2d.shape
    g = mix_a.shape[1]
    d_ff = w_up_all.shape[2]
    return pl.pallas_call(
        functools.partial(_mix_mlp_kernel, chunk=2 * MXU_DIM, layer=layer),
        grid=(m // ROW_TILE,),
        in_specs=[
            pl.BlockSpec((ROW_TILE, g), _row_major),
            (pl.BlockSpec((mix_b.shape[0], ROW_TILE, LANES), lambda i: (0, i, 0)) if mix_b.ndim == 3
             else pl.BlockSpec((ROW_TILE, g), _row_major)),
            pl.BlockSpec((ROW_TILE, d), _row_major),
            _const_spec((3, d)),
            pl.BlockSpec(memory_space=pl.ANY),
            pl.BlockSpec(memory_space=pl.ANY),
            pl.BlockSpec(memory_space=pl.ANY),
        ],
        out_specs=pl.BlockSpec((ROW_TILE, d), _row_major),
        out_shape=jax.ShapeDtypeStruct((m, d), F32),
        scratch_shapes=[
            pltpu.VMEM((2 * g, d), BF16),
            pltpu.VMEM((d, d_ff), BF16),
            pltpu.VMEM((d_ff, d), BF16),
            pltpu.VMEM((ROW_TILE, d_ff), BF16),
            pltpu.VMEM((ROW_TILE, d), F32),
            pltpu.VMEM((2, STAGE_ROWS, STAGE_COLS), F32),
            pltpu.SemaphoreType.DMA((2,)),
        ],
        compiler_params=_params(1),
        name="mix_mlp",
    )(mix_a, mix_b, x2d, nw3, w_out, w_up_all, w_down_all)


def _gla_kernel(q_ref, k_ref, v_ref, r_ref, sm_ref, wa_ref, ba_ref, nw_ref, o_ref,
                la_ref, st_ref, dec_ref):
    t = q_ref.shape[0]
    nc = t // CHUNK
    n_pairs = GLA_HEADS // 2

    z = _dot(sm_ref[...].astype(BF16), wa_ref[...]) + ba_ref[...]
    la_ref[...] = _log_sigmoid(z) * (1.0 / GLA_GATE_TAU)

    grp = GLA_GROUP * CHUNK
    row = lax.broadcasted_iota(jnp.int32, (grp, grp), 0)
    col = lax.broadcasted_iota(jnp.int32, (grp, grp), 1)
    tri = jnp.where((row >= col) & (row // CHUNK == col // CHUNK), 1.0, 0.0).astype(BF16)
    erow = lax.broadcasted_iota(jnp.int32, (grp, GLA_GROUP * LANES), 0)
    ecol = lax.broadcasted_iota(jnp.int32, (grp, GLA_GROUP * LANES), 1)
    chunk_ones = jnp.where(erow // CHUNK == ecol // LANES, 1.0, 0.0).astype(BF16)

    def increments(g, carry):
        r0 = pl.multiple_of(g * grp, grp)
        la = la_ref[pl.ds(r0, grp), :]
        hi = la.astype(BF16)
        lo = (la - hi.astype(F32)).astype(BF16)
        cum = _dot(tri, hi) + _dot(tri, lo)
        dec = jnp.exp(_dot_tn(hi, chunk_ones) + _dot_tn(lo, chunk_ones))
        k_g = k_ref[pl.ds(r0, grp), :].astype(F32)
        v_g = v_ref[pl.ds(r0, grp), :]
        k_dec = []
        for c in range(GLA_GROUP):
            rows = slice(c * CHUNK, (c + 1) * CHUNK)
            total = cum[(c + 1) * CHUNK - 1:(c + 1) * CHUNK, :]
            k_dec.append((k_g[rows] * jnp.exp(total - cum[rows])).astype(BF16))
        for c in range(GLA_GROUP):
            rows = slice(c * CHUNK, (c + 1) * CHUNK)
            dec_ref[g * GLA_GROUP + c] = dec[:, c * LANES:(c + 1) * LANES]
            for h in range(GLA_HEADS):
                p, j = divmod(h, 2)
                full = _dot_tn(k_dec[c][:, p * LANES:(p + 1) * LANES], v_g[rows, h * GLA_DV:(h + 1) * GLA_DV])
                st_ref[g * GLA_GROUP + c, p, j * GLA_DK:(j + 1) * GLA_DK, :] = full[j * GLA_DK:(j + 1) * GLA_DK, :]
        return carry

    lax.fori_loop(0, nc // GLA_GROUP, increments, 0)

    def scan(c, carry):
        for p in range(n_pairs):
            st_ref[c, p] = st_ref[c, p] + dec_ref[c, p * LANES:(p + 1) * LANES, :] * st_ref[c - 1, p]
        return carry

    lax.fori_loop(1, nc, scan, 0)

    lane = lax.broadcasted_iota(jnp.int32, (CHUNK, LANES), 1)
    scale = GLA_DK ** -0.5

    def outputs(g, carry):
        r0 = pl.multiple_of(g * grp, grp)
        q_g = q_ref[pl.ds(r0, grp), :] * jnp.asarray(scale, BF16)
        raw = []
        for c in range(GLA_GROUP):
            rows = slice(c * CHUNK, (c + 1) * CHUNK)
            for h in range(GLA_HEADS):
                p, j = divmod(h, 2)
                q_pair = q_g[rows, p * LANES:(p + 1) * LANES]
                q_h = jnp.where(_half_mask(lane, j), q_pair, jnp.zeros_like(q_pair))
                raw.append(_dot(q_h, st_ref[g * GLA_GROUP + c, p].astype(BF16)))
        for c in range(GLA_GROUP):
            for h in range(GLA_HEADS):
                o = raw[c * GLA_HEADS + h]
                o = o * lax.rsqrt(jnp.mean(o * o, axis=-1, keepdims=True) + NORM_EPS)
                o = o * nw_ref[:, h * GLA_DV:(h + 1) * GLA_DV]
                rows = pl.ds(r0 + c * CHUNK, CHUNK)
                r_h = r_ref[rows, h * GLA_DV:(h + 1) * GLA_DV].astype(F32)
                o_ref[rows, h * GLA_DV:(h + 1) * GLA_DV] = (o * (r_h * _sigmoid(r_h))).astype(BF16)
        return carry

    lax.fori_loop(0, nc // GLA_GROUP, outputs, 0)


def _gla(q, k, v, r, small, wa_pad, b_a, norm_w, batch, t):
    kw = GLA_HEADS * GLA_DK
    gw = GLA_HEADS * GLA_DV
    nc = t // CHUNK
    return pl.pallas_call(
        _gla_kernel,
        grid=(batch,),
        in_specs=[
            pl.BlockSpec((t, kw), lambda b: (b, 0)),
            pl.BlockSpec((t, kw), lambda b: (b, 0)),
            pl.BlockSpec((t, gw), lambda b: (b, 0)),
            pl.BlockSpec((t, gw), lambda b: (b, 0)),
            pl.BlockSpec((t, LANES), lambda b: (b, 0)),
            _const_spec((LANES, kw)),
            _const_spec((1, kw)),
            _const_spec((1, gw)),
        ],
        out_specs=pl.BlockSpec((t, gw), lambda b: (b, 0)),
        out_shape=jax.ShapeDtypeStruct((batch * t, gw), BF16),
        scratch_shapes=[
            pltpu.VMEM((t, kw), F32),
            pltpu.VMEM((nc, GLA_HEADS // 2, LANES, GLA_DV), F32),
            pltpu.VMEM((nc, kw, LANES), F32),
        ],
        compiler_params=_params(1),
        name="gla",
    )(q, k, v, r, small, wa_pad, b_a.reshape(1, kw), norm_w.reshape(1, gw))


def _fox_cum_kernel(sm_ref, bias_ref, o_ref):
    t = sm_ref.shape[0]
    row = lax.broadcasted_iota(jnp.int32, (LANES, LANES), 0)
    col = lax.broadcasted_iota(jnp.int32, (LANES, LANES), 1)
    tri = jnp.where(row >= col, 1.0, 0.0).astype(BF16)
    carry = jnp.zeros((1, LANES), F32)
    for blk in range(t // LANES):
        rows = slice(blk * LANES, (blk + 1) * LANES)
        ls = _log_sigmoid(sm_ref[rows, :] + bias_ref[...])
        h1, h2, h3 = _split3(ls)
        cb = _dot(tri, h1) + _dot(tri, h2) + _dot(tri, h3) + carry
        o_ref[rows, :] = cb
        carry = cb[LANES - 1:LANES, :]


def _fox_cum(small, bias_row, batch, t):
    return pl.pallas_call(
        _fox_cum_kernel,
        grid=(batch,),
        in_specs=[pl.BlockSpec((t, LANES), lambda b: (b, 0)), _const_spec((1, LANES))],
        out_specs=pl.BlockSpec((t, LANES), lambda b: (b, 0)),
        out_shape=jax.ShapeDtypeStruct((batch * t, LANES), F32),
        compiler_params=_params(1),
        name="fox_cum",
    )(small, bias_row)


def _fox_kernel(q_ref, k_ref, v_ref, c_ref, o_ref, ka_ref, vt_ref, s_ref):
    t = k_ref.shape[0]
    blk = FOX_BLOCK
    kbs = FOX_KBLOCK
    pair = pl.program_id(1)

    sel_r = lax.broadcasted_iota(jnp.int32, (LANES, LANES), 0)
    sel_c = lax.broadcasted_iota(jnp.int32, (LANES, LANES), 1)
    sels = []
    for term in range(CUM_TERMS):
        hit = (((sel_r == 2 * pair) & (sel_c == term))
               | ((sel_r == 2 * pair + 1) & (sel_c == CUM_TERMS + term)))
        sels.append(jnp.where(hit, 1.0, 0.0).astype(BF16))
    for kb in range(t // kbs):
        rows = slice(kb * kbs, (kb + 1) * kbs)
        vt_ref[kb] = v_ref[rows, :].astype(F32).T.astype(BF16)
        parts = _split3(c_ref[rows, :] * (-LOG2E))
        extra = _dot(parts[0], sels[0]) + _dot(parts[1], sels[1]) + _dot(parts[2], sels[2])
        ka_ref[kb, :, 0:LANES] = k_ref[rows, :]
        ka_ref[kb, :, LANES:2 * LANES] = extra.astype(BF16)

    lane = lax.broadcasted_iota(jnp.int32, (blk, LANES), 1)
    krow = lax.broadcasted_iota(jnp.int32, (kbs, blk), 0)
    qcol = lax.broadcasted_iota(jnp.int32, (kbs, blk), 1)
    ahead = qcol - krow

    def augmented_queries(qi):
        q = q_ref[qi * blk:(qi + 1) * blk, :]
        q_aug = []
        for j in range(2):
            q_h = jnp.where(_half_mask(lane, j), q, jnp.zeros_like(q))
            ones = jnp.where((lane >= CUM_TERMS * j) & (lane < CUM_TERMS * (j + 1)), 1.0, 0.0).astype(BF16)
            q_aug.append(jnp.concatenate([q_h, ones], axis=1).astype(F32).T.astype(BF16))
        return q_aug

    def scores(q_aug, qi, kb, slot):
        k_blk = ka_ref[kb]
        offset = kb * kbs - qi * blk
        for j in range(2):
            s = _dot(k_blk, q_aug[j])
            if offset + kbs - 1 > 0:
                s = jnp.where(ahead >= offset, s, MASK_VALUE)
            s_ref[slot, j] = s

    def accumulate(kb, slot, state):
        new = []
        for j in range(2):
            m, l, acc = state[j]
            m_new = jnp.maximum(m, jnp.max(s_ref[slot, j], axis=0, keepdims=True))
            alpha = jnp.exp2(m - m_new)
            p = jnp.exp2(s_ref[slot, j] - m_new)
            l = l * alpha + jnp.sum(p, axis=0, keepdims=True)
            pv = _dot(vt_ref[kb, j * HEAD_DIM:(j + 1) * HEAD_DIM, :], p.astype(BF16))
            new.append((m_new, l, acc * alpha + pv))
        return new

    tasks = [(qi, kb) for qi in range(t // blk) for kb in range((qi + 1) * blk // kbs)]
    fresh = [(jnp.full((1, blk), MASK_VALUE, F32), jnp.zeros((1, blk), F32),
              jnp.zeros((HEAD_DIM, blk), F32))] * 2
    q_aug = augmented_queries(0)
    scores(q_aug, 0, 0, 0)
    state = fresh
    for n, (qi, kb) in enumerate(tasks):
        if n + 1 < len(tasks):
            qi_next, kb_next = tasks[n + 1]
            if qi_next != qi:
                q_aug = augmented_queries(qi_next)
            scores(q_aug, qi_next, kb_next, (n + 1) % 2)
        state = accumulate(kb, n % 2, state)
        if n + 1 == len(tasks) or tasks[n + 1][0] != qi:
            outs = [acc * (1.0 / l) for _, l, acc in state]
            o_ref[qi * blk:(qi + 1) * blk, :] = jnp.concatenate(outs, axis=0).T.astype(BF16)
            state = fresh


def _fox(q, k, v, cum, batch, t):
    n_pairs = FOX_HEADS // 2
    pair_block = pl.BlockSpec((None, t, LANES), lambda b, p: (p, b, 0))
    return pl.pallas_call(
        _fox_kernel,
        grid=(batch, n_pairs),
        in_specs=[
            pair_block,
            pair_block,
            pair_block,
            pl.BlockSpec((t, LANES), lambda b, p: (b, 0)),
        ],
        out_specs=pair_block,
        out_shape=jax.ShapeDtypeStruct((n_pairs, batch * t, LANES), BF16),
        scratch_shapes=[
            pltpu.VMEM((t // FOX_KBLOCK, FOX_KBLOCK, 2 * LANES), BF16),
            pltpu.VMEM((t // FOX_KBLOCK, LANES, FOX_KBLOCK), BF16),
            pltpu.VMEM((2, 2, FOX_KBLOCK, FOX_BLOCK), F32),
        ],
        compiler_params=_params(2),
        name="fox",
    )(q, k, v, cum)


def _chunk_attn_kernel(q_ref, k_ref, v_ref, tab_ref, o_ref, kp_ref, vt_ref, s_ref):
    t, g = q_ref.shape
    qb = CA_QBLOCK
    n_pad = CA_PAD // qb
    n_win = CA_WINDOW // qb

    for kb in range(n_pad):
        kp_ref[kb] = jnp.zeros((qb, g), BF16)
        vt_ref[kb] = jnp.zeros(vt_ref.shape[1:], BF16)
    for kb in range(t // qb):
        rows = slice(kb * qb, (kb + 1) * qb)
        kp_ref[n_pad + kb] = k_ref[rows, :]
        v_t = v_ref[rows, :].T
        for h in range(g // HEAD_DIM):
            vt_ref[n_pad + kb, h, 0:HEAD_DIM, :] = v_t[h * HEAD_DIM:(h + 1) * HEAD_DIM, :]
            vt_ref[n_pad + kb, h, HEAD_DIM:HEAD_DIM + ONES_ROWS, :] = jnp.ones((ONES_ROWS, qb), BF16)

    lane = lax.broadcasted_iota(jnp.int32, (qb, LANES), 1)
    n_heads = g // HEAD_DIM
    lax.fori_loop(0, t // qb, functools.partial(
        _chunk_attn_block, q_ref=q_ref, tab_ref=tab_ref, o_ref=o_ref, kp_ref=kp_ref, vt_ref=vt_ref,
        s_ref=s_ref, lane=lane, n_heads=n_heads, n_pad=n_pad, n_win=n_win), 0)


def _chunk_attn_block(i, carry, *, q_ref, tab_ref, o_ref, kp_ref, vt_ref, s_ref, lane, n_heads,
                      n_pad, n_win):
    qb = CA_QBLOCK
    q_rows = pl.ds(pl.multiple_of(i * qb, qb), qb)

    def scores(h):
        pair, j = divmod(h, 2)
        cols = slice(pair * LANES, (pair + 1) * LANES)
        q_pair = q_ref[q_rows, cols]
        q_h = jnp.where(_half_mask(lane, j), q_pair, jnp.zeros_like(q_pair))
        q_ht = q_h.T
        for w in range(n_win):
            tab_blk = jnp.where(i + w >= n_pad, w, n_win)
            s_ref[h % 2, w] = _dot(kp_ref[i + w, :, cols], q_ht) + tab_ref[h, tab_blk]

    def weighted_values(h):
        m = s_ref[h % 2, 0].max(axis=0, keepdims=True)
        for w in range(1, n_win):
            m = jnp.maximum(m, s_ref[h % 2, w].max(axis=0, keepdims=True))
        acc = jnp.zeros((HEAD_DIM + ONES_ROWS, qb), F32)
        for w in range(n_win):
            p = jnp.exp2((s_ref[h % 2, w] - m).astype(BF16))
            acc = acc + _dot(vt_ref[i + w, h], p)
        return acc[0:HEAD_DIM] * (1.0 / acc[HEAD_DIM:HEAD_DIM + 1])

    outs = []
    scores(0)
    for h in range(n_heads):
        if h + 1 < n_heads:
            scores(h + 1)
        outs.append(weighted_values(h))
    o_ref[q_rows, :] = jnp.concatenate(outs, axis=0).T.astype(BF16)
    return carry


def _chunk_attn(q, k, v, table, batch, t):
    g = table.shape[0] * HEAD_DIM
    n_blocks = (CA_PAD + t) // CA_QBLOCK
    return pl.pallas_call(
        _chunk_attn_kernel,
        grid=(batch,),
        in_specs=[
            pl.BlockSpec((t, g), lambda b: (b, 0)),
            pl.BlockSpec((t, g), lambda b: (b, 0)),
            pl.BlockSpec((t, g), lambda b: (b, 0)),
            _const_spec(table.shape),
        ],
        out_specs=pl.BlockSpec((t, g), lambda b: (b, 0)),
        out_shape=jax.ShapeDtypeStruct((batch * t, g), BF16),
        scratch_shapes=[
            pltpu.VMEM((n_blocks, CA_QBLOCK, g), BF16),
            pltpu.VMEM((n_blocks, g // HEAD_DIM, HEAD_DIM + ONES_ROWS, CA_QBLOCK), BF16),
            pltpu.VMEM((2, CA_WINDOW // CA_QBLOCK, CA_QBLOCK, CA_QBLOCK), F32),
        ],
        compiler_params=_params(1),
        name="chunk_attn",
    )(q, k, v, table)


def _ca_table_kernel(line_ref, o_ref):
    n_win = CA_WINDOW // CA_QBLOCK
    rows = 8
    x = jnp.broadcast_to(line_ref[0] * LOG2E, (rows, CA_LINE))
    qc = lax.broadcasted_iota(jnp.int32, (rows, CA_QBLOCK), 1) // CHUNK
    for grp in range(CA_WINDOW // rows):
        r0 = grp * rows
        y = pltpu.roll(x, (CA_LINE - CA_WINDOW + 1 + r0) % CA_LINE, axis=1, stride=1, stride_axis=0)
        kc = r0 // CHUNK
        allowed = (qc <= kc) & (qc >= kc - CA_LEFT_CHUNKS)
        w, r = divmod(r0, CA_QBLOCK)
        o_ref[0, w, r:r + rows, :] = jnp.where(allowed, y[:, 0:CA_QBLOCK], MASK_VALUE)
    o_ref[0, n_win] = jnp.full((CA_QBLOCK, CA_QBLOCK), MASK_VALUE, F32)


def _chunk_attn_table(rel_bias):
    heads = rel_bias.shape[0]
    n_win = CA_WINDOW // CA_QBLOCK
    left = CA_WINDOW - 1 - CA_PAD - REL_CLIP
    line = jnp.pad(rel_bias.astype(F32), ((0, 0), (left, CA_LINE - left - rel_bias.shape[1])), mode="edge")
    return pl.pallas_call(
        _ca_table_kernel,
        grid=(heads,),
        in_specs=[pl.BlockSpec((1, 1, CA_LINE), lambda h: (h, 0, 0))],
        out_specs=pl.BlockSpec((1, n_win + 1, CA_QBLOCK, CA_QBLOCK), lambda h: (h, 0, 0, 0)),
        out_shape=jax.ShapeDtypeStruct((heads, n_win + 1, CA_QBLOCK, CA_QBLOCK), F32),
        compiler_params=_params(1),
        name="ca_table",
    )(line.reshape(heads, 1, CA_LINE))


GELU_C0 = 0.7978845608028654
GELU_C1 = GELU_C0 * 0.044715


def _gelu(x):
    inner = x * (GELU_C0 + GELU_C1 * (x * x))
    return (0.5 * x) * (1.0 + jnp.tanh(inner))


def _lru_kernel(g_ref, x_ref, cw_ref, cb_ref, wa_ref, ba_ref, wx_ref, bx_ref, lam_ref, o_ref,
                xin_ref, xf_ref, a_ref, b_ref, hout_ref, h_ref):
    nb, frames, w = x_ref.shape
    n_slab = w // LANES
    sub = LRU_ROWS // nb
    i = pl.program_id(0)

    @pl.when(i == 0)
    def _():
        xf_ref[0:CONV_HIST] = jnp.zeros((CONV_HIST, nb, w), F32)
        h_ref[...] = jnp.zeros((nb, w), F32)

    for b in range(nb):
        xb = x_ref[b].astype(F32)
        for s in range(n_slab):
            xin_ref[s, b * LRU_PITCH:b * LRU_PITCH + frames, :] = xb[:, s * LANES:(s + 1) * LANES]

    def gather(t, carry):
        for s in range(n_slab):
            xf_ref[CONV_HIST + t, :, s * LANES:(s + 1) * LANES] = (
                xin_ref[s, pl.ds(t, nb, stride=LRU_PITCH), :])
        return carry

    lax.fori_loop(0, frames, gather, 0, unroll=SCAN_UNROLL)

    log2_base = (LRU_C * LOG2E) * _log_sigmoid(lam_ref[...])
    half = w // 2
    for blk in range(frames // sub):
        t0 = blk * sub
        xc = cb_ref[...]
        for tap in range(CONV_WIDTH):
            lo = CONV_HIST + t0 - tap
            xc = xc + (cw_ref[CONV_WIDTH - 1 - tap:CONV_WIDTH - tap, :]
                       * xf_ref[lo:lo + sub].reshape(LRU_ROWS, w))
        xcb = xc.astype(BF16)
        gr, gi = [], []
        for hb in range(2):
            cols = slice(hb * half, (hb + 1) * half)
            gr.append(_dot(xcb[:, cols], wa_ref[hb]))
            gi.append(_dot(xcb[:, cols], wx_ref[hb]))
        r = _sigmoid(jnp.concatenate(gr, axis=1) + ba_ref[...])
        gate_i = _sigmoid(jnp.concatenate(gi, axis=1) + bx_ref[...])
        a = jnp.exp2(r * log2_base)
        a_ref[t0:t0 + sub] = a.reshape(sub, nb, w)
        y = 1.0 - a * a
        root = jnp.where(y > 0.0, y * lax.rsqrt(y), 0.0)
        b_ref[t0:t0 + sub] = (root * (gate_i * xc)).reshape(sub, nb, w)

    xf_ref[0:CONV_HIST] = xf_ref[frames:frames + CONV_HIST]

    def scan(t, h):
        h = a_ref[t] * h + b_ref[t]
        for s in range(n_slab):
            hout_ref[s, pl.ds(t, nb, stride=LRU_PITCH), :] = h[:, s * LANES:(s + 1) * LANES]
        return h

    h_ref[...] = lax.fori_loop(0, frames, scan, h_ref[...], unroll=SCAN_UNROLL)

    for b in range(nb):
        rows = slice(b * LRU_PITCH, b * LRU_PITCH + frames)
        hv = jnp.concatenate([hout_ref[s, rows, :] for s in range(n_slab)], axis=1)
        o_ref[b] = (hv * _gelu(g_ref[b].astype(F32))).astype(BF16)


def _lru(gate3, x3, conv_w, conv_b, wa_bd, b_a, wx_bd, b_x, lam):
    nb, t, w = x3.shape
    n_slab = w // LANES
    return pl.pallas_call(
        _lru_kernel,
        grid=(t // LRU_FRAMES,),
        in_specs=[
            pl.BlockSpec((nb, LRU_FRAMES, w), lambda i: (0, i, 0)),
            pl.BlockSpec((nb, LRU_FRAMES, w), lambda i: (0, i, 0)),
            _const_spec((CONV_WIDTH, w)),
            _const_spec((1, w)),
            _const_spec(wa_bd.shape),
            _const_spec((1, w)),
            _const_spec(wx_bd.shape),
            _const_spec((1, w)),
            _const_spec((1, w)),
        ],
        out_specs=pl.BlockSpec((nb, LRU_FRAMES, w), lambda i: (0, i, 0)),
        out_shape=jax.ShapeDtypeStruct((nb, t, w), BF16),
        scratch_shapes=[
            pltpu.VMEM((n_slab, nb * LRU_PITCH, LANES), F32),
            pltpu.VMEM((CONV_HIST + LRU_FRAMES, nb, w), F32),
            pltpu.VMEM((LRU_FRAMES, nb, w), F32),
            pltpu.VMEM((LRU_FRAMES, nb, w), F32),
            pltpu.VMEM((n_slab, nb * LRU_PITCH, LANES), F32),
            pltpu.VMEM((nb, w), F32),
        ],
        compiler_params=_params(1),
        name="rglru",
    )(gate3, x3, conv_w, conv_b.reshape(1, w), wa_bd, b_a.reshape(1, w), wx_bd,
      b_x.reshape(1, w), lam.reshape(1, w))


def _block_diag_halves(wblk):
    nb, d, _ = wblk.shape
    per = nb // 2
    eye = jnp.eye(per, dtype=wblk.dtype)
    halves = [jnp.einsum("nde,nm->ndme", wblk[h * per:(h + 1) * per], eye).reshape(per * d, per * d)
              for h in range(2)]
    return jnp.stack(halves).astype(BF16)


def kernel(x, norm_w, w_in_even, gla_w_a_up, gla_b_a, gla_norm_w, fox_b_f, w_out_even,
           w_in_odd, rel_bias, conv_w, conv_b, lru_w_a, lru_b_a, lru_w_x, lru_b_x,
           lru_lambda, w_out_odd, w_mlp_up, w_mlp_down):
    batch, t, d = x.shape
    x2d = x.reshape(batch * t, d)
    group = d // 2
    kw = GLA_HEADS * GLA_DK

    w_in = w_in_even[0]
    o_ga = 2 * kw + 2 * group
    o_fq = o_ga + GLA_RANK
    o_ff = o_fq + 3 * group
    q_scale = LOG2E * HEAD_DIM ** -0.5
    w_main = jnp.concatenate(
        [w_in[:, :o_ga], w_in[:, o_fq:o_fq + group] * q_scale, w_in[:, o_fq + group:o_ff]],
        axis=1).astype(BF16)
    n_small = FOX_HEADS + GLA_RANK
    w_small = jnp.concatenate(
        [w_in[:, o_ff:], w_in[:, o_ga:o_fq], jnp.zeros((d, LANES - n_small), F32)], axis=1).astype(BF16)
    wa_pad = jnp.zeros((LANES, kw), F32).at[FOX_HEADS:n_small].set(gla_w_a_up[0]).astype(BF16)
    fox_bias = jnp.zeros((1, LANES), F32).at[0, :FOX_HEADS].set(fox_b_f[0])

    m = batch * t
    splits = [(0, kw, False), (kw, kw, False), (2 * kw, group, False), (2 * kw + group, group, False),
              (o_ga, group, True), (o_ga + group, group, True), (o_ga + 2 * group, group, True)]
    g_q, g_k, g_v, g_r, f_q, f_k, f_v, small = _norm_proj(x2d, norm_w[0, 0], w_main, splits, w_small)
    out_a = _gla(g_q, g_k, g_v, g_r, small, wa_pad, gla_b_a[0], gla_norm_w[0], batch, t)
    cum = _fox_cum(small, fox_bias, batch, t)
    out_b = _fox(f_q, f_k, f_v, cum, batch, t)
    x2d = _mix_mlp(out_a, out_b, x2d, w_out_even[0], norm_w[0, 1:4], w_mlp_up, w_mlp_down, 0)

    w_in = w_in_odd[0]
    w_main = jnp.concatenate([w_in[:, :group] * q_scale, w_in[:, group:]], axis=1).astype(BF16)
    splits = [(c * group, group, False) for c in range(5)]
    c_q, c_k, c_v, d_gate, d_in = _norm_proj(x2d, norm_w[1, 0], w_main, splits)
    out_c = _chunk_attn(c_q, c_k, c_v, _chunk_attn_table(rel_bias[0]), batch, t)
    out_d = _lru(d_gate.reshape(batch, t, group), d_in.reshape(batch, t, group), conv_w[0], conv_b[0],
                 _block_diag_halves(lru_w_a[0]), lru_b_a[0],
                 _block_diag_halves(lru_w_x[0]), lru_b_x[0], lru_lambda[0])
    x2d = _mix_mlp(out_c, out_d.reshape(m, group), x2d, w_out_odd[0], norm_w[1, 1:4],
                   w_mlp_up, w_mlp_down, 1)
    return x2d.reshape(batch, t, d)
```

```python
import functools

import jax
import jax.numpy as jnp
from jax import lax
from jax.experimental import pallas as pl
from jax.experimental.pallas import tpu as pltpu

F32 = jnp.float32
BF16 = jnp.bfloat16

NORM_EPS = 1e-6
CHUNK = 64
GLA_HEADS = 4
GLA_DK = 64
GLA_DV = 128
GLA_RANK = 16
GLA_GATE_TAU = 16.0
FOX_HEADS = 8
HEAD_DIM = 64
CA_LEFT_CHUNKS = 8
REL_CLIP = 128
CONV_WIDTH = 4
LRU_C = 8.0

LANES = 128
MXU_DIM = 256
MASK_VALUE = -1e30
VMEM_LIMIT_BYTES = 56 * 1024 * 1024

ROW_TILE = 1024
PROJ_ROW_TILE = 1024
STAGE_ROWS = 512
STAGE_COLS = 1024
FOX_BLOCK = 512
FOX_KBLOCK = 512
ONES_ROWS = 16
CA_QBLOCK = 4 * CHUNK
CA_WINDOW = CA_QBLOCK + CA_LEFT_CHUNKS * CHUNK
CA_PAD = CA_LEFT_CHUNKS * CHUNK
CA_LINE = 1024
LOG2E = 1.4426950408889634
LN2 = 0.6931471805599453
CUM_TERMS = 3
GLA_GROUP = 4
LRU_ROWS = 256
LRU_FRAMES = 256
LRU_PITCH = LRU_FRAMES + 8
CONV_HIST = 8
SCAN_UNROLL = 8


def _params(n_axes):
    return pltpu.CompilerParams(
        dimension_semantics=("arbitrary",) * n_axes,
        vmem_limit_bytes=VMEM_LIMIT_BYTES,
    )


def _const_spec(shape):
    nd = len(shape)
    return pl.BlockSpec(shape, lambda *_: (0,) * nd, pipeline_mode=pl.Buffered(1))


def _rmsnorm(x, w):
    y = x * lax.rsqrt(jnp.mean(x * x, axis=-1, keepdims=True) + NORM_EPS)
    return y * w


def _log_sigmoid(z):
    return jnp.minimum(z, 0.0) - LN2 * jnp.log2(1.0 + jnp.exp2(-LOG2E * jnp.abs(z)))


def _sigmoid(z):
    return 0.5 * jnp.tanh(0.5 * z) + 0.5


def _dot(a, b):
    return jnp.dot(a, b, preferred_element_type=F32)


def _dot_tn(a, b):
    return lax.dot_general(a, b, (((0,), (0,)), ((), ())), preferred_element_type=F32)


def _half_mask(lane, j):
    return lane < HEAD_DIM if j == 0 else lane >= HEAD_DIM


def _split3(x):
    h1 = x.astype(BF16)
    r1 = x - h1.astype(F32)
    h2 = r1.astype(BF16)
    h3 = (r1 - h2.astype(F32)).astype(BF16)
    return h1, h2, h3


def _row_major(i):
    return (i, 0)


def _norm_proj_kernel(x_ref, nw_ref, w_ref, *refs, splits, n_chunk):
    with_small = len(refs) == len(splits) + 2
    o_refs = refs[1:1 + len(splits)] if with_small else refs
    tm = x_ref.shape[0]
    for rows in (slice(0, tm // 2), slice(tm // 2, tm)):
        h = _rmsnorm(x_ref[rows, :], nw_ref[...]).astype(BF16)
        for (c0, width, by_pair), o_ref in zip(splits, o_refs):
            for n0 in range(0, width, n_chunk):
                n1 = min(n0 + n_chunk, width)
                res = _dot(h, w_ref[:, c0 + n0:c0 + n1]).astype(o_ref.dtype)
                if by_pair:
                    for p in range(n0 // LANES, n1 // LANES):
                        o_ref[p, rows, :] = res[:, p * LANES - n0:(p + 1) * LANES - n0]
                else:
                    o_ref[rows, n0:n1] = res
        if with_small:
            refs[-1][rows, :] = _dot(h, refs[0][...])


def _norm_proj(x2d, nw, w_main, splits, w_small=None):
    m, d = x2d.shape
    in_specs = [pl.BlockSpec((PROJ_ROW_TILE, d), _row_major), _const_spec((1, d)),
                _const_spec(w_main.shape)]
    args = [x2d, nw.reshape(1, d), w_main]
    out_specs, out_shape = [], []
    for _, width, by_pair in splits:
        if by_pair:
            out_specs.append(pl.BlockSpec((width // LANES, PROJ_ROW_TILE, LANES), lambda i: (0, i, 0)))
            out_shape.append(jax.ShapeDtypeStruct((width // LANES, m, LANES), BF16))
        else:
            out_specs.append(pl.BlockSpec((PROJ_ROW_TILE, width), _row_major))
            out_shape.append(jax.ShapeDtypeStruct((m, width), BF16))
    if w_small is not None:
        in_specs.append(_const_spec(w_small.shape))
        args.append(w_small)
        out_specs.append(pl.BlockSpec((PROJ_ROW_TILE, w_small.shape[1]), _row_major))
        out_shape.append(jax.ShapeDtypeStruct((m, w_small.shape[1]), F32))
    return pl.pallas_call(
        functools.partial(_norm_proj_kernel, splits=tuple(splits), n_chunk=2 * MXU_DIM),
        grid=(m // PROJ_ROW_TILE,),
        in_specs=in_specs,
        out_specs=out_specs,
        out_shape=out_shape,
        compiler_params=_params(1),
        name="norm_proj",
    )(*args)


def _stage_bf16_weights(pairs, stage_ref, sem_ref):
    pieces = []
    for src, dst in pairs:
        rows, cols = dst.shape
        for r0 in range(0, rows, STAGE_ROWS):
            for c0 in range(0, cols, STAGE_COLS):
                pieces.append((src, dst, r0, c0))

    def copy(k):
        src, _, r0, c0 = pieces[k]
        return pltpu.make_async_copy(
            src.at[pl.ds(r0, STAGE_ROWS), pl.ds(c0, STAGE_COLS)], stage_ref.at[k % 2], sem_ref.at[k % 2])

    copy(0).start()
    for k, (_, dst, r0, c0) in enumerate(pieces):
        if k + 1 < len(pieces):
            copy(k + 1).start()
        copy(k).wait()
        dst[r0:r0 + STAGE_ROWS, c0:c0 + STAGE_COLS] = stage_ref[k % 2].astype(BF16)


def _mix_mlp_kernel(a_ref, b_ref, x_ref, nw_ref, wo_hbm, wu_hbm, wd_hbm, o_ref,
                    wo_ref, wu_ref, wd_ref, u_ref, y_ref, stage_ref, sem_ref, *, chunk, layer):
    @pl.when(pl.program_id(0) == 0)
    def _():
        _stage_bf16_weights([(wo_hbm, wo_ref), (wu_hbm.at[layer], wu_ref), (wd_hbm.at[layer], wd_ref)],
                            stage_ref, sem_ref)

    d_ff = wu_ref.shape[1]
    d = wd_ref.shape[1]
    tm = x_ref.shape[0]
    halves = [slice(0, tm // 2), slice(tm // 2, tm)]

    def out_proj(rows):
        if len(b_ref.shape) == 3:
            b = jnp.concatenate([b_ref[p, rows, :] for p in range(b_ref.shape[0])], axis=1)
        else:
            b = b_ref[rows, :]
        mix = jnp.concatenate([a_ref[rows, :], b], axis=1)
        for n0 in range(0, d, chunk):
            y_ref[rows, n0:n0 + chunk] = _dot(mix, wo_ref[:, n0:n0 + chunk])

    def residual_and_norm(rows):
        x1 = x_ref[rows, :] + _rmsnorm(y_ref[rows, :], nw_ref[0:1, :])
        o_ref[rows, :] = x1
        return _rmsnorm(x1, nw_ref[1:2, :]).astype(BF16)

    def up_proj(rows, h):
        for f0 in range(0, d_ff, chunk):
            u = jnp.maximum(_dot(h, wu_ref[:, f0:f0 + chunk]), 0.0)
            u_ref[rows, f0:f0 + chunk] = (u * u).astype(BF16)

    def down_proj(rows):
        for n0 in range(0, d, chunk):
            y_ref[rows, n0:n0 + chunk] = _dot(u_ref[rows, :], wd_ref[:, n0:n0 + chunk])

    def finish(rows):
        o_ref[rows, :] = o_ref[rows, :] + _rmsnorm(y_ref[rows, :], nw_ref[2:3, :])

    first, second = halves
    out_proj(first)
    out_proj(second)
    h_first = residual_and_norm(first)
    up_proj(first, h_first)
    h_second = residual_and_norm(second)
    up_proj(second, h_second)
    down_proj(first)
    down_proj(second)
    finish(first)
    finish(second)


def _mix_mlp(mix_a, mix_b, x2d, w_out, nw3, w_up_all, w_down_all, layer):
    m, d = x2d.shape
    g = mix_a.shape[1]
    d_ff = w_up_all.shape[2]
    return pl.pallas_call(
        functools.partial(_mix_mlp_kernel, chunk=2 * MXU_DIM, layer=layer),
        grid=(m // ROW_TILE,),
        in_specs=[
            pl.BlockSpec((ROW_TILE, g), _row_major),
            (pl.BlockSpec((mix_b.shape[0], ROW_TILE, LANES), lambda i: (0, i, 0)) if mix_b.ndim == 3
             else pl.BlockSpec((ROW_TILE, g), _row_major)),
            pl.BlockSpec((ROW_TILE, d), _row_major),
            _const_spec((3, d)),
            pl.BlockSpec(memory_space=pl.ANY),
            pl.BlockSpec(memory_space=pl.ANY),
            pl.BlockSpec(memory_space=pl.ANY),
        ],
        out_specs=pl.BlockSpec((ROW_TILE, d), _row_major),
        out_shape=jax.ShapeDtypeStruct((m, d), F32),
        scratch_shapes=[
            pltpu.VMEM((2 * g, d), BF16),
            pltpu.VMEM((d, d_ff), BF16),
            pltpu.VMEM((d_ff, d), BF16),
            pltpu.VMEM((ROW_TILE, d_ff), BF16),
            pltpu.VMEM((ROW_TILE, d), F32),
            pltpu.VMEM((2, STAGE_ROWS, STAGE_COLS), F32),
            pltpu.SemaphoreType.DMA((2,)),
        ],
        compiler_params=_params(1),
        name="mix_mlp",
    )(mix_a, mix_b, x2d, nw3, w_out, w_up_all, w_down_all)


def _gla_kernel(q_ref, k_ref, v_ref, r_ref, sm_ref, wa_ref, ba_ref, nw_ref, o_ref,
                la_ref, st_ref, dec_ref):
    t = q_ref.shape[0]
    nc = t // CHUNK
    n_pairs = GLA_HEADS // 2

    z = _dot(sm_ref[...].astype(BF16), wa_ref[...]) + ba_ref[...]
    la_ref[...] = _log_sigmoid(z) * (1.0 / GLA_GATE_TAU)

    grp = GLA_GROUP * CHUNK
    row = lax.broadcasted_iota(jnp.int32, (grp, grp), 0)
    col = lax.broadcasted_iota(jnp.int32, (grp, grp), 1)
    tri = jnp.where((row >= col) & (row // CHUNK == col // CHUNK), 1.0, 0.0).astype(BF16)
    erow = lax.broadcasted_iota(jnp.int32, (grp, GLA_GROUP * LANES), 0)
    ecol = lax.broadcasted_iota(jnp.int32, (grp, GLA_GROUP * LANES), 1)
    chunk_ones = jnp.where(erow // CHUNK == ecol // LANES, 1.0, 0.0).astype(BF16)

    def increments(g, carry):
        r0 = pl.multiple_of(g * grp, grp)
        la = la_ref[pl.ds(r0, grp), :]
        hi = la.astype(BF16)
        lo = (la - hi.astype(F32)).astype(BF16)
        cum = _dot(tri, hi) + _dot(tri, lo)
        dec = jnp.exp(_dot_tn(hi, chunk_ones) + _dot_tn(lo, chunk_ones))
        k_g = k_ref[pl.ds(r0, grp), :].astype(F32)
        v_g = v_ref[pl.ds(r0, grp), :]
        k_dec = []
        for c in range(GLA_GROUP):
            rows = slice(c * CHUNK, (c + 1) * CHUNK)
            total = cum[(c + 1) * CHUNK - 1:(c + 1) * CHUNK, :]
            k_dec.append((k_g[rows] * jnp.exp(total - cum[rows])).astype(BF16))
        for c in range(GLA_GROUP):
            rows = slice(c * CHUNK, (c + 1) * CHUNK)
            dec_ref[g * GLA_GROUP + c] = dec[:, c * LANES:(c + 1) * LANES]
            for h in range(GLA_HEADS):
                p, j = divmod(h, 2)
                full = _dot_tn(k_dec[c][:, p * LANES:(p + 1) * LANES], v_g[rows, h * GLA_DV:(h + 1) * GLA_DV])
                st_ref[g * GLA_GROUP + c, p, j * GLA_DK:(j + 1) * GLA_DK, :] = full[j * GLA_DK:(j + 1) * GLA_DK, :]
        return carry

    lax.fori_loop(0, nc // GLA_GROUP, increments, 0)

    def scan(c, carry):
        for p in range(n_pairs):
            st_ref[c, p] = st_ref[c, p] + dec_ref[c, p * LANES:(p + 1) * LANES, :] * st_ref[c - 1, p]
        return carry

    lax.fori_loop(1, nc, scan, 0)

    lane = lax.broadcasted_iota(jnp.int32, (CHUNK, LANES), 1)
    scale = GLA_DK ** -0.5

    def outputs(g, carry):
        r0 = pl.multiple_of(g * grp, grp)
        q_g = q_ref[pl.ds(r0, grp), :] * jnp.asarray(scale, BF16)
        raw = []
        for c in range(GLA_GROUP):
            rows = slice(c * CHUNK, (c + 1) * CHUNK)
            for h in range(GLA_HEADS):
                p, j = divmod(h, 2)
                q_pair = q_g[rows, p * LANES:(p + 1) * LANES]
                q_h = jnp.where(_half_mask(lane, j), q_pair, jnp.zeros_like(q_pair))
                raw.append(_dot(q_h, st_ref[g * GLA_GROUP + c, p].astype(BF16)))
        for c in range(GLA_GROUP):
            for h in range(GLA_HEADS):
                o = raw[c * GLA_HEADS + h]
                o = o * lax.rsqrt(jnp.mean(o * o, axis=-1, keepdims=True) + NORM_EPS)
                o = o * nw_ref[:, h * GLA_DV:(h + 1) * GLA_DV]
                rows = pl.ds(r0 + c * CHUNK, CHUNK)
                r_h = r_ref[rows, h * GLA_DV:(h + 1) * GLA_DV].astype(F32)
                o_ref[rows, h * GLA_DV:(h + 1) * GLA_DV] = (o * (r_h * _sigmoid(r_h))).astype(BF16)
        return carry

    lax.fori_loop(0, nc // GLA_GROUP, outputs, 0)


def _gla(q, k, v, r, small, wa_pad, b_a, norm_w, batch, t):
    kw = GLA_HEADS * GLA_DK
    gw = GLA_HEADS * GLA_DV
    nc = t // CHUNK
    return pl.pallas_call(
        _gla_kernel,
        grid=(batch,),
        in_specs=[
            pl.BlockSpec((t, kw), lambda b: (b, 0)),
            pl.BlockSpec((t, kw), lambda b: (b, 0)),
            pl.BlockSpec((t, gw), lambda b: (b, 0)),
            pl.BlockSpec((t, gw), lambda b: (b, 0)),
            pl.BlockSpec((t, LANES), lambda b: (b, 0)),
            _const_spec((LANES, kw)),
            _const_spec((1, kw)),
            _const_spec((1, gw)),
        ],
        out_specs=pl.BlockSpec((t, gw), lambda b: (b, 0)),
        out_shape=jax.ShapeDtypeStruct((batch * t, gw), BF16),
        scratch_shapes=[
            pltpu.VMEM((t, kw), F32),
            pltpu.VMEM((nc, GLA_HEADS // 2, LANES, GLA_DV), F32),
            pltpu.VMEM((nc, kw, LANES), F32),
        ],
        compiler_params=_params(1),
        name="gla",
    )(q, k, v, r, small, wa_pad, b_a.reshape(1, kw), norm_w.reshape(1, gw))


def _fox_cum_kernel(sm_ref, bias_ref, o_ref):
    t = sm_ref.shape[0]
    row = lax.broadcasted_iota(jnp.int32, (LANES, LANES), 0)
    col = lax.broadcasted_iota(jnp.int32, (LANES, LANES), 1)
    tri = jnp.where(row >= col, 1.0, 0.0).astype(BF16)
    carry = jnp.zeros((1, LANES), F32)
    for blk in range(t // LANES):
        rows = slice(blk * LANES, (blk + 1) * LANES)
        ls = _log_sigmoid(sm_ref[rows, :] + bias_ref[...])
        h1, h2, h3 = _split3(ls)
        cb = _dot(tri, h1) + _dot(tri, h2) + _dot(tri, h3) + carry
        o_ref[rows, :] = cb
        carry = cb[LANES - 1:LANES, :]


def _fox_cum(small, bias_row, batch, t):
    return pl.pallas_call(
        _fox_cum_kernel,
        grid=(batch,),
        in_specs=[pl.BlockSpec((t, LANES), lambda b: (b, 0)), _const_spec((1, LANES))],
        out_specs=pl.BlockSpec((t, LANES), lambda b: (b, 0)),
        out_shape=jax.ShapeDtypeStruct((batch * t, LANES), F32),
        compiler_params=_params(1),
        name="fox_cum",
    )(small, bias_row)


def _fox_kernel(q_ref, k_ref, v_ref, c_ref, o_ref, ka_ref, vt_ref, s_ref):
    t = k_ref.shape[0]
    blk = FOX_BLOCK
    kbs = FOX_KBLOCK
    pair = pl.program_id(1)

    sel_r = lax.broadcasted_iota(jnp.int32, (LANES, LANES), 0)
    sel_c = lax.broadcasted_iota(jnp.int32, (LANES, LANES), 1)
    sels = []
    for term in range(CUM_TERMS):
        hit = (((sel_r == 2 * pair) & (sel_c == term))
               | ((sel_r == 2 * pair + 1) & (sel_c == CUM_TERMS + term)))
        sels.append(jnp.where(hit, 1.0, 0.0).astype(BF16))
    for kb in range(t // kbs):
        rows = slice(kb * kbs, (kb + 1) * kbs)
        vt_ref[kb] = v_ref[rows, :].astype(F32).T.astype(BF16)
        parts = _split3(c_ref[rows, :] * (-LOG2E))
        extra = _dot(parts[0], sels[0]) + _dot(parts[1], sels[1]) + _dot(parts[2], sels[2])
        ka_ref[kb, :, 0:LANES] = k_ref[rows, :]
        ka_ref[kb, :, LANES:2 * LANES] = extra.astype(BF16)

    lane = lax.broadcasted_iota(jnp.int32, (blk, LANES), 1)
    krow = lax.broadcasted_iota(jnp.int32, (kbs, blk), 0)
    qcol = lax.broadcasted_iota(jnp.int32, (kbs, blk), 1)
    ahead = qcol - krow

    def augmented_queries(qi):
        q = q_ref[qi * blk:(qi + 1) * blk, :]
        q_aug = []
        for j in range(2):
            q_h = jnp.where(_half_mask(lane, j), q, jnp.zeros_like(q))
            ones = jnp.where((lane >= CUM_TERMS * j) & (lane < CUM_TERMS * (j + 1)), 1.0, 0.0).astype(BF16)
            q_aug.append(jnp.concatenate([q_h, ones], axis=1).astype(F32).T.astype(BF16))
        return q_aug

    def scores(q_aug, qi, kb, slot):
        k_blk = ka_ref[kb]
        offset = kb * kbs - qi * blk
        for j in range(2):
            s = _dot(k_blk, q_aug[j])
            if offset + kbs - 1 > 0:
                s = jnp.where(ahead >= offset, s, MASK_VALUE)
            s_ref[slot, j] = s

    def accumulate(kb, slot, state):
        new = []
        for j in range(2):
            m, l, acc = state[j]
            m_new = jnp.maximum(m, jnp.max(s_ref[slot, j], axis=0, keepdims=True))
            alpha = jnp.exp2(m - m_new)
            p = jnp.exp2(s_ref[slot, j] - m_new)
            l = l * alpha + jnp.sum(p, axis=0, keepdims=True)
            pv = _dot(vt_ref[kb, j * HEAD_DIM:(j + 1) * HEAD_DIM, :], p.astype(BF16))
            new.append((m_new, l, acc * alpha + pv))
        return new

    tasks = [(qi, kb) for qi in range(t // blk) for kb in range((qi + 1) * blk // kbs)]
    fresh = [(jnp.full((1, blk), MASK_VALUE, F32), jnp.zeros((1, blk), F32),
              jnp.zeros((HEAD_DIM, blk), F32))] * 2
    q_aug = augmented_queries(0)
    scores(q_aug, 0, 0, 0)
    state = fresh
    for n, (qi, kb) in enumerate(tasks):
        if n + 1 < len(tasks):
            qi_next, kb_next = tasks[n + 1]
            if qi_next != qi:
                q_aug = augmented_queries(qi_next)
            scores(q_aug, qi_next, kb_next, (n + 1) % 2)
        state = accumulate(kb, n % 2, state)
        if n + 1 == len(tasks) or tasks[n + 1][0] != qi:
            outs = [acc * (1.0 / l) for _, l, acc in state]
            o_ref[qi * blk:(qi + 1) * blk, :] = jnp.concatenate(outs, axis=0).T.astype(BF16)
            state = fresh


def _fox(q, k, v, cum, batch, t):
    n_pairs = FOX_HEADS // 2
    pair_block = pl.BlockSpec((None, t, LANES), lambda b, p: (p, b, 0))
    return pl.pallas_call(
        _fox_kernel,
        grid=(batch, n_pairs),
        in_specs=[
            pair_block,
            pair_block,
            pair_block,
            pl.BlockSpec((t, LANES), lambda b, p: (b, 0)),
        ],
        out_specs=pair_block,
        out_shape=jax.ShapeDtypeStruct((n_pairs, batch * t, LANES), BF16),
        scratch_shapes=[
            pltpu.VMEM((t // FOX_KBLOCK, FOX_KBLOCK, 2 * LANES), BF16),
            pltpu.VMEM((t // FOX_KBLOCK, LANES, FOX_KBLOCK), BF16),
            pltpu.VMEM((2, 2, FOX_KBLOCK, FOX_BLOCK), F32),
        ],
        compiler_params=_params(2),
        name="fox",
    )(q, k, v, cum)


def _chunk_attn_kernel(q_ref, k_ref, v_ref, tab_ref, o_ref, kp_ref, vt_ref, s_ref):
    t, g = q_ref.shape
    qb = CA_QBLOCK
    n_pad = CA_PAD // qb
    n_win = CA_WINDOW // qb

    for kb in range(n_pad):
        kp_ref[kb] = jnp.zeros((qb, g), BF16)
        vt_ref[kb] = jnp.zeros(vt_ref.shape[1:], BF16)
    for kb in range(t // qb):
        rows = slice(kb * qb, (kb + 1) * qb)
        kp_ref[n_pad + kb] = k_ref[rows, :]
        v_t = v_ref[rows, :].T
        for h in range(g // HEAD_DIM):
            vt_ref[n_pad + kb, h, 0:HEAD_DIM, :] = v_t[h * HEAD_DIM:(h + 1) * HEAD_DIM, :]
            vt_ref[n_pad + kb, h, HEAD_DIM:HEAD_DIM + ONES_ROWS, :] = jnp.ones((ONES_ROWS, qb), BF16)

    lane = lax.broadcasted_iota(jnp.int32, (qb, LANES), 1)
    n_heads = g // HEAD_DIM
    lax.fori_loop(0, t // qb, functools.partial(
        _chunk_attn_block, q_ref=q_ref, tab_ref=tab_ref, o_ref=o_ref, kp_ref=kp_ref, vt_ref=vt_ref,
        s_ref=s_ref, lane=lane, n_heads=n_heads, n_pad=n_pad, n_win=n_win), 0)


def _chunk_attn_block(i, carry, *, q_ref, tab_ref, o_ref, kp_ref, vt_ref, s_ref, lane, n_heads,
                      n_pad, n_win):
    qb = CA_QBLOCK
    q_rows = pl.ds(pl.multiple_of(i * qb, qb), qb)

    def scores(h):
        pair, j = divmod(h, 2)
        cols = slice(pair * LANES, (pair + 1) * LANES)
        q_pair = q_ref[q_rows, cols]
        q_h = jnp.where(_half_mask(lane, j), q_pair, jnp.zeros_like(q_pair))
        q_ht = q_h.T
        for w in range(n_win):
            tab_blk = jnp.where(i + w >= n_pad, w, n_win)
            s_ref[h % 2, w] = _dot(kp_ref[i + w, :, cols], q_ht) + tab_ref[h, tab_blk]

    def weighted_values(h):
        m = s_ref[h % 2, 0].max(axis=0, keepdims=True)
        for w in range(1, n_win):
            m = jnp.maximum(m, s_ref[h % 2, w].max(axis=0, keepdims=True))
        acc = jnp.zeros((HEAD_DIM + ONES_ROWS, qb), F32)
        for w in range(n_win):
            p = jnp.exp2((s_ref[h % 2, w] - m).astype(BF16))
            acc = acc + _dot(vt_ref[i + w, h], p)
        return acc[0:HEAD_DIM] * (1.0 / acc[HEAD_DIM:HEAD_DIM + 1])

    outs = []
    scores(0)
    for h in range(n_heads):
        if h + 1 < n_heads:
            scores(h + 1)
        outs.append(weighted_values(h))
    o_ref[q_rows, :] = jnp.concatenate(outs, axis=0).T.astype(BF16)
    return carry


def _chunk_attn(q, k, v, table, batch, t):
    g = table.shape[0] * HEAD_DIM
    n_blocks = (CA_PAD + t) // CA_QBLOCK
    return pl.pallas_call(
        _chunk_attn_kernel,
        grid=(batch,),
        in_specs=[
            pl.BlockSpec((t, g), lambda b: (b, 0)),
            pl.BlockSpec((t, g), lambda b: (b, 0)),
            pl.BlockSpec((t, g), lambda b: (b, 0)),
            _const_spec(table.shape),
        ],
        out_specs=pl.BlockSpec((t, g), lambda b: (b, 0)),
        out_shape=jax.ShapeDtypeStruct((batch * t, g), BF16),
        scratch_shapes=[
            pltpu.VMEM((n_blocks, CA_QBLOCK, g), BF16),
            pltpu.VMEM((n_blocks, g // HEAD_DIM, HEAD_DIM + ONES_ROWS, CA_QBLOCK), BF16),
            pltpu.VMEM((2, CA_WINDOW // CA_QBLOCK, CA_QBLOCK, CA_QBLOCK), F32),
        ],
        compiler_params=_params(1),
        name="chunk_attn",
    )(q, k, v, table)


def _ca_table_kernel(line_ref, o_ref):
    n_win = CA_WINDOW // CA_QBLOCK
    rows = 8
    x = jnp.broadcast_to(line_ref[0] * LOG2E, (rows, CA_LINE))
    qc = lax.broadcasted_iota(jnp.int32, (rows, CA_QBLOCK), 1) // CHUNK
    for grp in range(CA_WINDOW // rows):
        r0 = grp * rows
        y = pltpu.roll(x, (CA_LINE - CA_WINDOW + 1 + r0) % CA_LINE, axis=1, stride=1, stride_axis=0)
        kc = r0 // CHUNK
        allowed = (qc <= kc) & (qc >= kc - CA_LEFT_CHUNKS)
        w, r = divmod(r0, CA_QBLOCK)
        o_ref[0, w, r:r + rows, :] = jnp.where(allowed, y[:, 0:CA_QBLOCK], MASK_VALUE)
    o_ref[0, n_win] = jnp.full((CA_QBLOCK, CA_QBLOCK), MASK_VALUE, F32)


def _chunk_attn_table(rel_bias):
    heads = rel_bias.shape[0]
    n_win = CA_WINDOW // CA_QBLOCK
    left = CA_WINDOW - 1 - CA_PAD - REL_CLIP
    line = jnp.pad(rel_bias.astype(F32), ((0, 0), (left, CA_LINE - left - rel_bias.shape[1])), mode="edge")
    return pl.pallas_call(
        _ca_table_kernel,
        grid=(heads,),
        in_specs=[pl.BlockSpec((1, 1, CA_LINE), lambda h: (h, 0, 0))],
        out_specs=pl.BlockSpec((1, n_win + 1, CA_QBLOCK, CA_QBLOCK), lambda h: (h, 0, 0, 0)),
        out_shape=jax.ShapeDtypeStruct((heads, n_win + 1, CA_QBLOCK, CA_QBLOCK), F32),
        compiler_params=_params(1),
        name="ca_table",
    )(line.reshape(heads, 1, CA_LINE))


GELU_C0 = 0.7978845608028654
GELU_C1 = GELU_C0 * 0.044715


def _gelu(x):
    inner = x * (GELU_C0 + GELU_C1 * (x * x))
    return (0.5 * x) * (1.0 + jnp.tanh(inner))


def _lru_kernel(g_ref, x_ref, cw_ref, cb_ref, wa_ref, ba_ref, wx_ref, bx_ref, lam_ref, o_ref,
                xin_ref, xf_ref, a_ref, b_ref, hout_ref, h_ref):
    nb, frames, w = x_ref.shape
    n_slab = w // LANES
    sub = LRU_ROWS // nb
    i = pl.program_id(0)

    @pl.when(i == 0)
    def _():
        xf_ref[0:CONV_HIST] = jnp.zeros((CONV_HIST, nb, w), F32)
        h_ref[...] = jnp.zeros((nb, w), F32)

    for b in range(nb):
        xb = x_ref[b].astype(F32)
        for s in range(n_slab):
            xin_ref[s, b * LRU_PITCH:b * LRU_PITCH + frames, :] = xb[:, s * LANES:(s + 1) * LANES]

    def gather(t, carry):
        for s in range(n_slab):
            xf_ref[CONV_HIST + t, :, s * LANES:(s + 1) * LANES] = (
                xin_ref[s, pl.ds(t, nb, stride=LRU_PITCH), :])
        return carry

    lax.fori_loop(0, frames, gather, 0, unroll=SCAN_UNROLL)

    half_log2_base = (0.5 * LRU_C * LOG2E) * _log_sigmoid(lam_ref[...])
    half = w // 2
    for blk in range(frames // sub):
        t0 = blk * sub
        xc = cb_ref[...]
        for tap in range(CONV_WIDTH):
            lo = CONV_HIST + t0 - tap
            xc = xc + (cw_ref[CONV_WIDTH - 1 - tap:CONV_WIDTH - tap, :]
                       * xf_ref[lo:lo + sub].reshape(LRU_ROWS, w))
        xcb = xc.astype(BF16)
        gr, gi = [], []
        for hb in range(2):
            cols = slice(hb * half, (hb + 1) * half)
            gr.append(_dot(xcb[:, cols], wa_ref[hb]))
            gi.append(_dot(xcb[:, cols], wx_ref[hb]))
        t_r = jnp.tanh(jnp.concatenate(gr, axis=1) + ba_ref[...])
        gate_i = 0.5 * jnp.tanh(jnp.concatenate(gi, axis=1) + bx_ref[...]) + 0.5
        a = jnp.exp2(t_r * half_log2_base + half_log2_base)
        a_ref[t0:t0 + sub] = a.reshape(sub, nb, w)
        y = 1.0 - a * a
        root = jnp.where(y > 0.0, y * lax.rsqrt(y), 0.0)
        b_ref[t0:t0 + sub] = (root * (gate_i * xc)).reshape(sub, nb, w)

    xf_ref[0:CONV_HIST] = xf_ref[frames:frames + CONV_HIST]

    def scan(t, h):
        h = a_ref[t] * h + b_ref[t]
        for s in range(n_slab):
            hout_ref[s, pl.ds(t, nb, stride=LRU_PITCH), :] = h[:, s * LANES:(s + 1) * LANES]
        return h

    h_ref[...] = lax.fori_loop(0, frames, scan, h_ref[...], unroll=SCAN_UNROLL)

    for b in range(nb):
        rows = slice(b * LRU_PITCH, b * LRU_PITCH + frames)
        hv = jnp.concatenate([hout_ref[s, rows, :] for s in range(n_slab)], axis=1)
        o_ref[b] = (hv * _gelu(g_ref[b].astype(F32))).astype(BF16)


def _lru(gate3, x3, conv_w, conv_b, wa_bd, b_a, wx_bd, b_x, lam):
    nb, t, w = x3.shape
    n_slab = w // LANES
    return pl.pallas_call(
        _lru_kernel,
        grid=(t // LRU_FRAMES,),
        in_specs=[
            pl.BlockSpec((nb, LRU_FRAMES, w), lambda i: (0, i, 0)),
            pl.BlockSpec((nb, LRU_FRAMES, w), lambda i: (0, i, 0)),
            _const_spec((CONV_WIDTH, w)),
            _const_spec((1, w)),
            _const_spec(wa_bd.shape),
            _const_spec((1, w)),
            _const_spec(wx_bd.shape),
            _const_spec((1, w)),
            _const_spec((1, w)),
        ],
        out_specs=pl.BlockSpec((nb, LRU_FRAMES, w), lambda i: (0, i, 0)),
        out_shape=jax.ShapeDtypeStruct((nb, t, w), BF16),
        scratch_shapes=[
            pltpu.VMEM((n_slab, nb * LRU_PITCH, LANES), F32),
            pltpu.VMEM((CONV_HIST + LRU_FRAMES, nb, w), F32),
            pltpu.VMEM((LRU_FRAMES, nb, w), F32),
            pltpu.VMEM((LRU_FRAMES, nb, w), F32),
            pltpu.VMEM((n_slab, nb * LRU_PITCH, LANES), F32),
            pltpu.VMEM((nb, w), F32),
        ],
        compiler_params=_params(1),
        name="rglru",
    )(gate3, x3, conv_w, conv_b.reshape(1, w), wa_bd, b_a.reshape(1, w), wx_bd,
      b_x.reshape(1, w), lam.reshape(1, w))


def _block_diag_halves(wblk):
    nb, d, _ = wblk.shape
    per = nb // 2
    eye = jnp.eye(per, dtype=wblk.dtype)
    halves = [jnp.einsum("nde,nm->ndme", wblk[h * per:(h + 1) * per], eye).reshape(per * d, per * d)
              for h in range(2)]
    return jnp.stack(halves).astype(BF16)


def kernel(x, norm_w, w_in_even, gla_w_a_up, gla_b_a, gla_norm_w, fox_b_f, w_out_even,
           w_in_odd, rel_bias, conv_w, conv_b, lru_w_a, lru_b_a, lru_w_x, lru_b_x,
           lru_lambda, w_out_odd, w_mlp_up, w_mlp_down):
    batch, t, d = x.shape
    x2d = x.reshape(batch * t, d)
    group = d // 2
    kw = GLA_HEADS * GLA_DK

    w_in = w_in_even[0]
    o_ga = 2 * kw + 2 * group
    o_fq = o_ga + GLA_RANK
    o_ff = o_fq + 3 * group
    q_scale = LOG2E * HEAD_DIM ** -0.5
    w_main = jnp.concatenate(
        [w_in[:, :o_ga], w_in[:, o_fq:o_fq + group] * q_scale, w_in[:, o_fq + group:o_ff]],
        axis=1).astype(BF16)
    n_small = FOX_HEADS + GLA_RANK
    w_small = jnp.concatenate(
        [w_in[:, o_ff:], w_in[:, o_ga:o_fq], jnp.zeros((d, LANES - n_small), F32)], axis=1).astype(BF16)
    wa_pad = jnp.zeros((LANES, kw), F32).at[FOX_HEADS:n_small].set(gla_w_a_up[0]).astype(BF16)
    fox_bias = jnp.zeros((1, LANES), F32).at[0, :FOX_HEADS].set(fox_b_f[0])

    m = batch * t
    splits = [(0, kw, False), (kw, kw, False), (2 * kw, group, False), (2 * kw + group, group, False),
              (o_ga, group, True), (o_ga + group, group, True), (o_ga + 2 * group, group, True)]
    g_q, g_k, g_v, g_r, f_q, f_k, f_v, small = _norm_proj(x2d, norm_w[0, 0], w_main, splits, w_small)
    out_a = _gla(g_q, g_k, g_v, g_r, small, wa_pad, gla_b_a[0], gla_norm_w[0], batch, t)
    cum = _fox_cum(small, fox_bias, batch, t)
    out_b = _fox(f_q, f_k, f_v, cum, batch, t)
    x2d = _mix_mlp(out_a, out_b, x2d, w_out_even[0], norm_w[0, 1:4], w_mlp_up, w_mlp_down, 0)

    w_in = w_in_odd[0]
    w_main = jnp.concatenate([w_in[:, :group] * q_scale, w_in[:, group:]], axis=1).astype(BF16)
    splits = [(c * group, group, False) for c in range(5)]
    c_q, c_k, c_v, d_gate, d_in = _norm_proj(x2d, norm_w[1, 0], w_main, splits)
    out_c = _chunk_attn(c_q, c_k, c_v, _chunk_attn_table(rel_bias[0]), batch, t)
    out_d = _lru(d_gate.reshape(batch, t, group), d_in.reshape(batch, t, group), conv_w[0], conv_b[0],
                 _block_diag_halves(0.5 * lru_w_a[0]), 0.5 * lru_b_a[0],
                 _block_diag_halves(0.5 * lru_w_x[0]), 0.5 * lru_b_x[0], lru_lambda[0])
    x2d = _mix_mlp(out_c, out_d.reshape(m, group), x2d, w_out_odd[0], norm_w[1, 1:4],
                   w_mlp_up, w_mlp_down, 1)
    return x2d.reshape(batch, t, d)
```

```python
import functools

import jax
import jax.numpy as jnp
from jax import lax
from jax.experimental import pallas as pl
from jax.experimental.pallas import tpu as pltpu

F32 = jnp.float32
BF16 = jnp.bfloat16

NORM_EPS = 1e-6
CHUNK = 64
GLA_HEADS = 4
GLA_DK = 64
GLA_DV = 128
GLA_RANK = 16
GLA_GATE_TAU = 16.0
FOX_HEADS = 8
HEAD_DIM = 64
CA_LEFT_CHUNKS = 8
REL_CLIP = 128
CONV_WIDTH = 4
LRU_C = 8.0

LANES = 128
MXU_DIM = 256
MASK_VALUE = -1e30
VMEM_LIMIT_BYTES = 56 * 1024 * 1024

ROW_TILE = 1024
PROJ_ROW_TILE = 1024
STAGE_ROWS = 512
STAGE_COLS = 1024
FOX_BLOCK = 512
FOX_KBLOCK = 512
ONES_ROWS = 16
CA_QBLOCK = 4 * CHUNK
CA_WINDOW = CA_QBLOCK + CA_LEFT_CHUNKS * CHUNK
CA_PAD = CA_LEFT_CHUNKS * CHUNK
CA_LINE = 1024
LOG2E = 1.4426950408889634
LN2 = 0.6931471805599453
CUM_TERMS = 3
GLA_GROUP = 4
LRU_ROWS = 256
LRU_FRAMES = 256
LRU_PITCH = LRU_FRAMES + 8
CONV_HIST = 8
SCAN_UNROLL = 8


def _params(n_axes, vmem_mib):
    assert vmem_mib * 1024 * 1024 <= VMEM_LIMIT_BYTES
    return pltpu.CompilerParams(
        dimension_semantics=("arbitrary",) * n_axes,
        vmem_limit_bytes=vmem_mib * 1024 * 1024,
    )


def _const_spec(shape):
    nd = len(shape)
    return pl.BlockSpec(shape, lambda *_: (0,) * nd, pipeline_mode=pl.Buffered(1))


def _rmsnorm(x, w):
    y = x * lax.rsqrt(jnp.mean(x * x, axis=-1, keepdims=True) + NORM_EPS)
    return y * w


def _log_sigmoid(z):
    return jnp.minimum(z, 0.0) - LN2 * jnp.log2(1.0 + jnp.exp2(-LOG2E * jnp.abs(z)))


def _sigmoid(z):
    return 0.5 * jnp.tanh(0.5 * z) + 0.5


def _dot(a, b):
    return jnp.dot(a, b, preferred_element_type=F32)


def _dot_tn(a, b):
    return lax.dot_general(a, b, (((0,), (0,)), ((), ())), preferred_element_type=F32)


def _half_mask(lane, j):
    return lane < HEAD_DIM if j == 0 else lane >= HEAD_DIM


def _split3(x):
    h1 = x.astype(BF16)
    r1 = x - h1.astype(F32)
    h2 = r1.astype(BF16)
    h3 = (r1 - h2.astype(F32)).astype(BF16)
    return h1, h2, h3


def _row_major(i):
    return (i, 0)


def _norm_proj_kernel(x_ref, nw_ref, w_ref, *refs, splits, n_chunk):
    with_small = len(refs) == len(splits) + 2
    o_refs = refs[1:1 + len(splits)] if with_small else refs
    tm = x_ref.shape[0]
    for rows in (slice(0, tm // 2), slice(tm // 2, tm)):
        h = _rmsnorm(x_ref[rows, :], nw_ref[...]).astype(BF16)
        for (c0, width, by_pair), o_ref in zip(splits, o_refs):
            for n0 in range(0, width, n_chunk):
                n1 = min(n0 + n_chunk, width)
                res = _dot(h, w_ref[:, c0 + n0:c0 + n1]).astype(o_ref.dtype)
                if by_pair:
                    for p in range(n0 // LANES, n1 // LANES):
                        o_ref[p, rows, :] = res[:, p * LANES - n0:(p + 1) * LANES - n0]
                else:
                    o_ref[rows, n0:n1] = res
        if with_small:
            refs[-1][rows, :] = _dot(h, refs[0][...])


def _norm_proj(x2d, nw, w_main, splits, w_small=None):
    m, d = x2d.shape
    in_specs = [pl.BlockSpec((PROJ_ROW_TILE, d), _row_major), _const_spec((1, d)),
                _const_spec(w_main.shape)]
    args = [x2d, nw.reshape(1, d), w_main]
    out_specs, out_shape = [], []
    for _, width, by_pair in splits:
        if by_pair:
            out_specs.append(pl.BlockSpec((width // LANES, PROJ_ROW_TILE, LANES), lambda i: (0, i, 0)))
            out_shape.append(jax.ShapeDtypeStruct((width // LANES, m, LANES), BF16))
        else:
            out_specs.append(pl.BlockSpec((PROJ_ROW_TILE, width), _row_major))
            out_shape.append(jax.ShapeDtypeStruct((m, width), BF16))
    if w_small is not None:
        in_specs.append(_const_spec(w_small.shape))
        args.append(w_small)
        out_specs.append(pl.BlockSpec((PROJ_ROW_TILE, w_small.shape[1]), _row_major))
        out_shape.append(jax.ShapeDtypeStruct((m, w_small.shape[1]), F32))
    return pl.pallas_call(
        functools.partial(_norm_proj_kernel, splits=tuple(splits), n_chunk=2 * MXU_DIM),
        grid=(m // PROJ_ROW_TILE,),
        in_specs=in_specs,
        out_specs=out_specs,
        out_shape=out_shape,
        compiler_params=_params(1, 36),
        name="norm_proj",
    )(*args)


def _stage_bf16_weights(pairs, stage_ref, sem_ref):
    pieces = []
    for src, dst in pairs:
        rows, cols = dst.shape
        for r0 in range(0, rows, STAGE_ROWS):
            for c0 in range(0, cols, STAGE_COLS):
                pieces.append((src, dst, r0, c0))

    def copy(k):
        src, _, r0, c0 = pieces[k]
        return pltpu.make_async_copy(
            src.at[pl.ds(r0, STAGE_ROWS), pl.ds(c0, STAGE_COLS)], stage_ref.at[k % 2], sem_ref.at[k % 2])

    copy(0).start()
    for k, (_, dst, r0, c0) in enumerate(pieces):
        if k + 1 < len(pieces):
            copy(k + 1).start()
        copy(k).wait()
        dst[r0:r0 + STAGE_ROWS, c0:c0 + STAGE_COLS] = stage_ref[k % 2].astype(BF16)


def _mix_mlp_kernel(a_ref, b_ref, x_ref, nw_ref, wo_hbm, wu_hbm, wd_hbm, o_ref,
                    wo_ref, wu_ref, wd_ref, u_ref, y_ref, stage_ref, sem_ref, *, chunk, layer):
    @pl.when(pl.program_id(0) == 0)
    def _():
        _stage_bf16_weights([(wo_hbm, wo_ref), (wu_hbm.at[layer], wu_ref), (wd_hbm.at[layer], wd_ref)],
                            stage_ref, sem_ref)

    d_ff = wu_ref.shape[1]
    d = wd_ref.shape[1]
    tm = x_ref.shape[0]
    halves = [slice(0, tm // 2), slice(tm // 2, tm)]

    def out_proj(rows):
        if len(b_ref.shape) == 3:
            b = jnp.concatenate([b_ref[p, rows, :] for p in range(b_ref.shape[0])], axis=1)
        else:
            b = b_ref[rows, :]
        mix = jnp.concatenate([a_ref[rows, :], b], axis=1)
        for n0 in range(0, d, chunk):
            y_ref[rows, n0:n0 + chunk] = _dot(mix, wo_ref[:, n0:n0 + chunk])

    def residual_and_norm(rows):
        x1 = x_ref[rows, :] + _rmsnorm(y_ref[rows, :], nw_ref[0:1, :])
        o_ref[rows, :] = x1
        return _rmsnorm(x1, nw_ref[1:2, :]).astype(BF16)

    def up_proj(rows, h):
        for f0 in range(0, d_ff, chunk):
            u = jnp.maximum(_dot(h, wu_ref[:, f0:f0 + chunk]), 0.0)
            u_ref[rows, f0:f0 + chunk] = (u * u).astype(BF16)

    def down_proj(rows):
        for n0 in range(0, d, chunk):
            y_ref[rows, n0:n0 + chunk] = _dot(u_ref[rows, :], wd_ref[:, n0:n0 + chunk])

    def finish(rows):
        o_ref[rows, :] = o_ref[rows, :] + _rmsnorm(y_ref[rows, :], nw_ref[2:3, :])

    first, second = halves
    out_proj(first)
    out_proj(second)
    h_first = residual_and_norm(first)
    up_proj(first, h_first)
    h_second = residual_and_norm(second)
    up_proj(second, h_second)
    down_proj(first)
    down_proj(second)
    finish(first)
    finish(second)


def _mix_mlp(mix_a, mix_b, x2d, w_out, nw3, w_up_all, w_down_all, layer):
    m, d = x2d.shape
    g = mix_a.shape[1]
    d_ff = w_up_all.shape[2]
    return pl.pallas_call(
        functools.partial(_mix_mlp_kernel, chunk=2 * MXU_DIM, layer=layer),
        grid=(m // ROW_TILE,),
        in_specs=[
            pl.BlockSpec((ROW_TILE, g), _row_major),
            (pl.BlockSpec((mix_b.shape[0], ROW_TILE, LANES), lambda i: (0, i, 0)) if mix_b.ndim == 3
             else pl.BlockSpec((ROW_TILE, g), _row_major)),
            pl.BlockSpec((ROW_TILE, d), _row_major),
            _const_spec((3, d)),
            pl.BlockSpec(memory_space=pl.ANY),
            pl.BlockSpec(memory_space=pl.ANY),
            pl.BlockSpec(memory_space=pl.ANY),
        ],
        out_specs=pl.BlockSpec((ROW_TILE, d), _row_major),
        out_shape=jax.ShapeDtypeStruct((m, d), F32),
        scratch_shapes=[
            pltpu.VMEM((2 * g, d), BF16),
            pltpu.VMEM((d, d_ff), BF16),
            pltpu.VMEM((d_ff, d), BF16),
            pltpu.VMEM((ROW_TILE, d_ff), BF16),
            pltpu.VMEM((ROW_TILE, d), F32),
            pltpu.VMEM((2, STAGE_ROWS, STAGE_COLS), F32),
            pltpu.SemaphoreType.DMA((2,)),
        ],
        compiler_params=_params(1, 56),
        name="mix_mlp",
    )(mix_a, mix_b, x2d, nw3, w_out, w_up_all, w_down_all)


def _gla_kernel(q_ref, k_ref, v_ref, r_ref, sm_ref, wa_ref, ba_ref, nw_ref, o_ref,
                la_ref, st_ref, dec_ref):
    t = q_ref.shape[0]
    nc = t // CHUNK
    n_pairs = GLA_HEADS // 2

    z = _dot(sm_ref[...].astype(BF16), wa_ref[...]) + ba_ref[...]
    la_ref[...] = _log_sigmoid(z) * (1.0 / GLA_GATE_TAU)

    grp = GLA_GROUP * CHUNK
    row = lax.broadcasted_iota(jnp.int32, (grp, grp), 0)
    col = lax.broadcasted_iota(jnp.int32, (grp, grp), 1)
    tri = jnp.where((row >= col) & (row // CHUNK == col // CHUNK), 1.0, 0.0).astype(BF16)
    erow = lax.broadcasted_iota(jnp.int32, (grp, GLA_GROUP * LANES), 0)
    ecol = lax.broadcasted_iota(jnp.int32, (grp, GLA_GROUP * LANES), 1)
    chunk_ones = jnp.where(erow // CHUNK == ecol // LANES, 1.0, 0.0).astype(BF16)

    def increments(g, carry):
        r0 = pl.multiple_of(g * grp, grp)
        la = la_ref[pl.ds(r0, grp), :]
        hi = la.astype(BF16)
        lo = (la - hi.astype(F32)).astype(BF16)
        cum = _dot(tri, hi) + _dot(tri, lo)
        dec = jnp.exp(_dot_tn(hi, chunk_ones) + _dot_tn(lo, chunk_ones))
        k_g = k_ref[pl.ds(r0, grp), :].astype(F32)
        v_g = v_ref[pl.ds(r0, grp), :]
        k_dec = []
        for c in range(GLA_GROUP):
            rows = slice(c * CHUNK, (c + 1) * CHUNK)
            total = cum[(c + 1) * CHUNK - 1:(c + 1) * CHUNK, :]
            k_dec.append((k_g[rows] * jnp.exp(total - cum[rows])).astype(BF16))
        for c in range(GLA_GROUP):
            rows = slice(c * CHUNK, (c + 1) * CHUNK)
            dec_ref[g * GLA_GROUP + c] = dec[:, c * LANES:(c + 1) * LANES]
            for h in range(GLA_HEADS):
                p, j = divmod(h, 2)
                full = _dot_tn(k_dec[c][:, p * LANES:(p + 1) * LANES], v_g[rows, h * GLA_DV:(h + 1) * GLA_DV])
                st_ref[g * GLA_GROUP + c, p, j * GLA_DK:(j + 1) * GLA_DK, :] = full[j * GLA_DK:(j + 1) * GLA_DK, :]
        return carry

    lax.fori_loop(0, nc // GLA_GROUP, increments, 0)

    def scan(c, carry):
        for p in range(n_pairs):
            st_ref[c, p] = st_ref[c, p] + dec_ref[c, p * LANES:(p + 1) * LANES, :] * st_ref[c - 1, p]
        return carry

    lax.fori_loop(1, nc, scan, 0)

    lane = lax.broadcasted_iota(jnp.int32, (CHUNK, LANES), 1)
    scale = GLA_DK ** -0.5

    def outputs(g, carry):
        r0 = pl.multiple_of(g * grp, grp)
        q_g = q_ref[pl.ds(r0, grp), :] * jnp.asarray(scale, BF16)
        raw = []
        for c in range(GLA_GROUP):
            rows = slice(c * CHUNK, (c + 1) * CHUNK)
            for h in range(GLA_HEADS):
                p, j = divmod(h, 2)
                q_pair = q_g[rows, p * LANES:(p + 1) * LANES]
                q_h = jnp.where(_half_mask(lane, j), q_pair, jnp.zeros_like(q_pair))
                raw.append(_dot(q_h, st_ref[g * GLA_GROUP + c, p].astype(BF16)))
        for c in range(GLA_GROUP):
            for h in range(GLA_HEADS):
                o = raw[c * GLA_HEADS + h]
                o = o * lax.rsqrt(jnp.mean(o * o, axis=-1, keepdims=True) + NORM_EPS)
                o = o * nw_ref[:, h * GLA_DV:(h + 1) * GLA_DV]
                rows = pl.ds(r0 + c * CHUNK, CHUNK)
                r_h = r_ref[rows, h * GLA_DV:(h + 1) * GLA_DV].astype(F32)
                o_ref[rows, h * GLA_DV:(h + 1) * GLA_DV] = (o * (r_h * _sigmoid(r_h))).astype(BF16)
        return carry

    lax.fori_loop(0, nc // GLA_GROUP, outputs, 0)


def _gla(q, k, v, r, small, wa_pad, b_a, norm_w, batch, t):
    kw = GLA_HEADS * GLA_DK
    gw = GLA_HEADS * GLA_DV
    nc = t // CHUNK
    return pl.pallas_call(
        _gla_kernel,
        grid=(batch,),
        in_specs=[
            pl.BlockSpec((t, kw), lambda b: (b, 0)),
            pl.BlockSpec((t, kw), lambda b: (b, 0)),
            pl.BlockSpec((t, gw), lambda b: (b, 0)),
            pl.BlockSpec((t, gw), lambda b: (b, 0)),
            pl.BlockSpec((t, LANES), lambda b: (b, 0)),
            _const_spec((LANES, kw)),
            _const_spec((1, kw)),
            _const_spec((1, gw)),
        ],
        out_specs=pl.BlockSpec((t, gw), lambda b: (b, 0)),
        out_shape=jax.ShapeDtypeStruct((batch * t, gw), BF16),
        scratch_shapes=[
            pltpu.VMEM((t, kw), F32),
            pltpu.VMEM((nc, GLA_HEADS // 2, LANES, GLA_DV), F32),
            pltpu.VMEM((nc, kw, LANES), F32),
        ],
        compiler_params=_params(1, 36),
        name="gla",
    )(q, k, v, r, small, wa_pad, b_a.reshape(1, kw), norm_w.reshape(1, gw))


def _fox_cum_kernel(sm_ref, bias_ref, o_ref):
    t = sm_ref.shape[0]
    row = lax.broadcasted_iota(jnp.int32, (LANES, LANES), 0)
    col = lax.broadcasted_iota(jnp.int32, (LANES, LANES), 1)
    tri = jnp.where(row >= col, 1.0, 0.0).astype(BF16)
    carry = jnp.zeros((1, LANES), F32)
    for blk in range(t // LANES):
        rows = slice(blk * LANES, (blk + 1) * LANES)
        ls = _log_sigmoid(sm_ref[rows, :] + bias_ref[...])
        h1, h2, h3 = _split3(ls)
        cb = _dot(tri, h1) + _dot(tri, h2) + _dot(tri, h3) + carry
        o_ref[rows, :] = cb
        carry = cb[LANES - 1:LANES, :]


def _fox_cum(small, bias_row, batch, t):
    return pl.pallas_call(
        _fox_cum_kernel,
        grid=(batch,),
        in_specs=[pl.BlockSpec((t, LANES), lambda b: (b, 0)), _const_spec((1, LANES))],
        out_specs=pl.BlockSpec((t, LANES), lambda b: (b, 0)),
        out_shape=jax.ShapeDtypeStruct((batch * t, LANES), F32),
        compiler_params=_params(1, 8),
        name="fox_cum",
    )(small, bias_row)


def _fox_kernel(q_ref, k_ref, v_ref, c_ref, o_ref, ka_ref, vt_ref, s_ref):
    t = k_ref.shape[0]
    blk = FOX_BLOCK
    kbs = FOX_KBLOCK
    pair = pl.program_id(1)

    sel_r = lax.broadcasted_iota(jnp.int32, (LANES, LANES), 0)
    sel_c = lax.broadcasted_iota(jnp.int32, (LANES, LANES), 1)
    sels = []
    for term in range(CUM_TERMS):
        hit = (((sel_r == 2 * pair) & (sel_c == term))
               | ((sel_r == 2 * pair + 1) & (sel_c == CUM_TERMS + term)))
        sels.append(jnp.where(hit, 1.0, 0.0).astype(BF16))
    for kb in range(t // kbs):
        rows = slice(kb * kbs, (kb + 1) * kbs)
        vt_ref[kb] = v_ref[rows, :].astype(F32).T.astype(BF16)
        parts = _split3(c_ref[rows, :] * (-LOG2E))
        extra = _dot(parts[0], sels[0]) + _dot(parts[1], sels[1]) + _dot(parts[2], sels[2])
        ka_ref[kb, :, 0:LANES] = k_ref[rows, :]
        ka_ref[kb, :, LANES:2 * LANES] = extra.astype(BF16)

    lane = lax.broadcasted_iota(jnp.int32, (blk, LANES), 1)
    krow = lax.broadcasted_iota(jnp.int32, (kbs, blk), 0)
    qcol = lax.broadcasted_iota(jnp.int32, (kbs, blk), 1)
    ahead = qcol - krow

    def augmented_queries(qi):
        q = q_ref[qi * blk:(qi + 1) * blk, :]
        q_aug = []
        for j in range(2):
            q_h = jnp.where(_half_mask(lane, j), q, jnp.zeros_like(q))
            ones = jnp.where((lane >= CUM_TERMS * j) & (lane < CUM_TERMS * (j + 1)), 1.0, 0.0).astype(BF16)
            q_aug.append(jnp.concatenate([q_h, ones], axis=1).astype(F32).T.astype(BF16))
        return q_aug

    def scores(q_aug, qi, kb, slot):
        k_blk = ka_ref[kb]
        offset = kb * kbs - qi * blk
        for j in range(2):
            s = _dot(k_blk, q_aug[j])
            if offset + kbs - 1 > 0:
                s = jnp.where(ahead >= offset, s, MASK_VALUE)
            s_ref[slot, j] = s

    def accumulate(kb, slot, state):
        new = []
        for j in range(2):
            m, l, acc = state[j]
            m_new = jnp.maximum(m, jnp.max(s_ref[slot, j], axis=0, keepdims=True))
            alpha = jnp.exp2(m - m_new)
            p = jnp.exp2(s_ref[slot, j] - m_new)
            l = l * alpha + jnp.sum(p, axis=0, keepdims=True)
            pv = _dot(vt_ref[kb, j * HEAD_DIM:(j + 1) * HEAD_DIM, :], p.astype(BF16))
            new.append((m_new, l, acc * alpha + pv))
        return new

    tasks = [(qi, kb) for qi in range(t // blk) for kb in range((qi + 1) * blk // kbs)]
    fresh = [(jnp.full((1, blk), MASK_VALUE, F32), jnp.zeros((1, blk), F32),
              jnp.zeros((HEAD_DIM, blk), F32))] * 2
    q_aug = augmented_queries(0)
    scores(q_aug, 0, 0, 0)
    state = fresh
    for n, (qi, kb) in enumerate(tasks):
        if n + 1 < len(tasks):
            qi_next, kb_next = tasks[n + 1]
            if qi_next != qi:
                q_aug = augmented_queries(qi_next)
            scores(q_aug, qi_next, kb_next, (n + 1) % 2)
        state = accumulate(kb, n % 2, state)
        if n + 1 == len(tasks) or tasks[n + 1][0] != qi:
            outs = [acc * (1.0 / l) for _, l, acc in state]
            o_ref[qi * blk:(qi + 1) * blk, :] = jnp.concatenate(outs, axis=0).T.astype(BF16)
            state = fresh


def _fox(q, k, v, cum, batch, t):
    n_pairs = FOX_HEADS // 2
    pair_block = pl.BlockSpec((None, t, LANES), lambda b, p: (p, b, 0))
    return pl.pallas_call(
        _fox_kernel,
        grid=(batch, n_pairs),
        in_specs=[
            pair_block,
            pair_block,
            pair_block,
            pl.BlockSpec((t, LANES), lambda b, p: (b, 0)),
        ],
        out_specs=pair_block,
        out_shape=jax.ShapeDtypeStruct((n_pairs, batch * t, LANES), BF16),
        scratch_shapes=[
            pltpu.VMEM((t // FOX_KBLOCK, FOX_KBLOCK, 2 * LANES), BF16),
            pltpu.VMEM((t // FOX_KBLOCK, LANES, FOX_KBLOCK), BF16),
            pltpu.VMEM((2, 2, FOX_KBLOCK, FOX_BLOCK), F32),
        ],
        compiler_params=_params(2, 20),
        name="fox",
    )(q, k, v, cum)


def _chunk_attn_kernel(q_ref, k_ref, v_ref, tab_ref, o_ref, kp_ref, vt_ref, s_ref):
    t, g = q_ref.shape
    qb = CA_QBLOCK
    n_pad = CA_PAD // qb
    n_win = CA_WINDOW // qb

    for kb in range(n_pad):
        kp_ref[kb] = jnp.zeros((qb, g), BF16)
        vt_ref[kb] = jnp.zeros(vt_ref.shape[1:], BF16)
    for kb in range(t // qb):
        rows = slice(kb * qb, (kb + 1) * qb)
        kp_ref[n_pad + kb] = k_ref[rows, :]
        v_t = v_ref[rows, :].T
        for h in range(g // HEAD_DIM):
            vt_ref[n_pad + kb, h, 0:HEAD_DIM, :] = v_t[h * HEAD_DIM:(h + 1) * HEAD_DIM, :]
            vt_ref[n_pad + kb, h, HEAD_DIM:HEAD_DIM + ONES_ROWS, :] = jnp.ones((ONES_ROWS, qb), BF16)

    lane = lax.broadcasted_iota(jnp.int32, (qb, LANES), 1)
    n_heads = g // HEAD_DIM
    lax.fori_loop(0, t // qb, functools.partial(
        _chunk_attn_block, q_ref=q_ref, tab_ref=tab_ref, o_ref=o_ref, kp_ref=kp_ref, vt_ref=vt_ref,
        s_ref=s_ref, lane=lane, n_heads=n_heads, n_pad=n_pad, n_win=n_win), 0)


def _chunk_attn_block(i, carry, *, q_ref, tab_ref, o_ref, kp_ref, vt_ref, s_ref, lane, n_heads,
                      n_pad, n_win):
    qb = CA_QBLOCK
    q_rows = pl.ds(pl.multiple_of(i * qb, qb), qb)

    def scores(h):
        pair, j = divmod(h, 2)
        cols = slice(pair * LANES, (pair + 1) * LANES)
        q_pair = q_ref[q_rows, cols]
        q_h = jnp.where(_half_mask(lane, j), q_pair, jnp.zeros_like(q_pair))
        q_ht = q_h.T
        for w in range(n_win):
            tab_blk = jnp.where(i + w >= n_pad, w, n_win)
            s_ref[h % 2, w] = _dot(kp_ref[i + w, :, cols], q_ht) + tab_ref[h, tab_blk]

    def weighted_values(h):
        m = s_ref[h % 2, 0].max(axis=0, keepdims=True)
        for w in range(1, n_win):
            m = jnp.maximum(m, s_ref[h % 2, w].max(axis=0, keepdims=True))
        acc = jnp.zeros((HEAD_DIM + ONES_ROWS, qb), F32)
        for w in range(n_win):
            p = jnp.exp2((s_ref[h % 2, w] - m).astype(BF16))
            acc = acc + _dot(vt_ref[i + w, h], p)
        return acc[0:HEAD_DIM] * (1.0 / acc[HEAD_DIM:HEAD_DIM + 1])

    outs = []
    scores(0)
    for h in range(n_heads):
        if h + 1 < n_heads:
            scores(h + 1)
        outs.append(weighted_values(h))
    o_ref[q_rows, :] = jnp.concatenate(outs, axis=0).T.astype(BF16)
    return carry


def _chunk_attn(q, k, v, table, batch, t):
    g = table.shape[0] * HEAD_DIM
    n_blocks = (CA_PAD + t) // CA_QBLOCK
    return pl.pallas_call(
        _chunk_attn_kernel,
        grid=(batch,),
        in_specs=[
            pl.BlockSpec((t, g), lambda b: (b, 0)),
            pl.BlockSpec((t, g), lambda b: (b, 0)),
            pl.BlockSpec((t, g), lambda b: (b, 0)),
            _const_spec(table.shape),
        ],
        out_specs=pl.BlockSpec((t, g), lambda b: (b, 0)),
        out_shape=jax.ShapeDtypeStruct((batch * t, g), BF16),
        scratch_shapes=[
            pltpu.VMEM((n_blocks, CA_QBLOCK, g), BF16),
            pltpu.VMEM((n_blocks, g // HEAD_DIM, HEAD_DIM + ONES_ROWS, CA_QBLOCK), BF16),
            pltpu.VMEM((2, CA_WINDOW // CA_QBLOCK, CA_QBLOCK, CA_QBLOCK), F32),
        ],
        compiler_params=_params(1, 40),
        name="chunk_attn",
    )(q, k, v, table)


def _ca_table_kernel(line_ref, o_ref):
    n_win = CA_WINDOW // CA_QBLOCK
    rows = 8
    x = jnp.broadcast_to(line_ref[0] * LOG2E, (rows, CA_LINE))
    qc = lax.broadcasted_iota(jnp.int32, (rows, CA_QBLOCK), 1) // CHUNK
    for grp in range(CA_WINDOW // rows):
        r0 = grp * rows
        y = pltpu.roll(x, (CA_LINE - CA_WINDOW + 1 + r0) % CA_LINE, axis=1, stride=1, stride_axis=0)
        kc = r0 // CHUNK
        allowed = (qc <= kc) & (qc >= kc - CA_LEFT_CHUNKS)
        w, r = divmod(r0, CA_QBLOCK)
        o_ref[0, w, r:r + rows, :] = jnp.where(allowed, y[:, 0:CA_QBLOCK], MASK_VALUE)
    o_ref[0, n_win] = jnp.full((CA_QBLOCK, CA_QBLOCK), MASK_VALUE, F32)


def _chunk_attn_table(rel_bias):
    heads = rel_bias.shape[0]
    n_win = CA_WINDOW // CA_QBLOCK
    left = CA_WINDOW - 1 - CA_PAD - REL_CLIP
    line = jnp.pad(rel_bias.astype(F32), ((0, 0), (left, CA_LINE - left - rel_bias.shape[1])), mode="edge")
    return pl.pallas_call(
        _ca_table_kernel,
        grid=(heads,),
        in_specs=[pl.BlockSpec((1, 1, CA_LINE), lambda h: (h, 0, 0))],
        out_specs=pl.BlockSpec((1, n_win + 1, CA_QBLOCK, CA_QBLOCK), lambda h: (h, 0, 0, 0)),
        out_shape=jax.ShapeDtypeStruct((heads, n_win + 1, CA_QBLOCK, CA_QBLOCK), F32),
        compiler_params=_params(1, 8),
        name="ca_table",
    )(line.reshape(heads, 1, CA_LINE))


GELU_C0 = 0.7978845608028654
GELU_C1 = GELU_C0 * 0.044715


def _gelu(x):
    inner = x * (GELU_C0 + GELU_C1 * (x * x))
    return (0.5 * x) * (1.0 + jnp.tanh(inner))


def _lru_kernel(g_ref, x_ref, cw_ref, cb_ref, wa_ref, ba_ref, wx_ref, bx_ref, lam_ref, o_ref,
                xin_ref, xf_ref, a_ref, b_ref, hout_ref, h_ref):
    nb, frames, w = x_ref.shape
    n_slab = w // LANES
    sub = LRU_ROWS // nb
    i = pl.program_id(0)

    @pl.when(i == 0)
    def _():
        xf_ref[0:CONV_HIST] = jnp.zeros((CONV_HIST, nb, w), F32)
        h_ref[...] = jnp.zeros((nb, w), F32)

    for b in range(nb):
        xb = x_ref[b].astype(F32)
        for s in range(n_slab):
            xin_ref[s, b * LRU_PITCH:b * LRU_PITCH + frames, :] = xb[:, s * LANES:(s + 1) * LANES]

    def gather(t, carry):
        for s in range(n_slab):
            xf_ref[CONV_HIST + t, :, s * LANES:(s + 1) * LANES] = (
                xin_ref[s, pl.ds(t, nb, stride=LRU_PITCH), :])
        return carry

    lax.fori_loop(0, frames, gather, 0, unroll=SCAN_UNROLL)

    half_log2_base = (0.5 * LRU_C * LOG2E) * _log_sigmoid(lam_ref[...])
    half = w // 2
    for blk in range(frames // sub):
        t0 = blk * sub
        xc = cb_ref[...]
        for tap in range(CONV_WIDTH):
            lo = CONV_HIST + t0 - tap
            xc = xc + (cw_ref[CONV_WIDTH - 1 - tap:CONV_WIDTH - tap, :]
                       * xf_ref[lo:lo + sub].reshape(LRU_ROWS, w))
        xcb = xc.astype(BF16)
        gr, gi = [], []
        for hb in range(2):
            cols = slice(hb * half, (hb + 1) * half)
            gr.append(_dot(xcb[:, cols], wa_ref[hb]))
            gi.append(_dot(xcb[:, cols], wx_ref[hb]))
        t_r = jnp.tanh(jnp.concatenate(gr, axis=1) + ba_ref[...])
        gate_i = 0.5 * jnp.tanh(jnp.concatenate(gi, axis=1) + bx_ref[...]) + 0.5
        a = jnp.exp2(t_r * half_log2_base + half_log2_base)
        a_ref[t0:t0 + sub] = a.reshape(sub, nb, w)
        y = 1.0 - a * a
        root = jnp.where(y > 0.0, y * lax.rsqrt(y), 0.0)
        b_ref[t0:t0 + sub] = (root * (gate_i * xc)).reshape(sub, nb, w)

    xf_ref[0:CONV_HIST] = xf_ref[frames:frames + CONV_HIST]

    def scan(t, h):
        h = a_ref[t] * h + b_ref[t]
        for s in range(n_slab):
            hout_ref[s, pl.ds(t, nb, stride=LRU_PITCH), :] = h[:, s * LANES:(s + 1) * LANES]
        return h

    h_ref[...] = lax.fori_loop(0, frames, scan, h_ref[...], unroll=SCAN_UNROLL)

    for b in range(nb):
        rows = slice(b * LRU_PITCH, b * LRU_PITCH + frames)
        hv = jnp.concatenate([hout_ref[s, rows, :] for s in range(n_slab)], axis=1)
        o_ref[b] = (hv * _gelu(g_ref[b].astype(F32))).astype(BF16)


def _lru(gate3, x3, conv_w, conv_b, wa_bd, b_a, wx_bd, b_x, lam):
    nb, t, w = x3.shape
    n_slab = w // LANES
    return pl.pallas_call(
        _lru_kernel,
        grid=(t // LRU_FRAMES,),
        in_specs=[
            pl.BlockSpec((nb, LRU_FRAMES, w), lambda i: (0, i, 0)),
            pl.BlockSpec((nb, LRU_FRAMES, w), lambda i: (0, i, 0)),
            _const_spec((CONV_WIDTH, w)),
            _const_spec((1, w)),
            _const_spec(wa_bd.shape),
            _const_spec((1, w)),
            _const_spec(wx_bd.shape),
            _const_spec((1, w)),
            _const_spec((1, w)),
        ],
        out_specs=pl.BlockSpec((nb, LRU_FRAMES, w), lambda i: (0, i, 0)),
        out_shape=jax.ShapeDtypeStruct((nb, t, w), BF16),
        scratch_shapes=[
            pltpu.VMEM((n_slab, nb * LRU_PITCH, LANES), F32),
            pltpu.VMEM((CONV_HIST + LRU_FRAMES, nb, w), F32),
            pltpu.VMEM((LRU_FRAMES, nb, w), F32),
            pltpu.VMEM((LRU_FRAMES, nb, w), F32),
            pltpu.VMEM((n_slab, nb * LRU_PITCH, LANES), F32),
            pltpu.VMEM((nb, w), F32),
        ],
        compiler_params=_params(1, 42),
        name="rglru",
    )(gate3, x3, conv_w, conv_b.reshape(1, w), wa_bd, b_a.reshape(1, w), wx_bd,
      b_x.reshape(1, w), lam.reshape(1, w))


def _block_diag_halves(wblk):
    nb, d, _ = wblk.shape
    per = nb // 2
    eye = jnp.eye(per, dtype=wblk.dtype)
    halves = [jnp.einsum("nde,nm->ndme", wblk[h * per:(h + 1) * per], eye).reshape(per * d, per * d)
              for h in range(2)]
    return jnp.stack(halves).astype(BF16)


def kernel(x, norm_w, w_in_even, gla_w_a_up, gla_b_a, gla_norm_w, fox_b_f, w_out_even,
           w_in_odd, rel_bias, conv_w, conv_b, lru_w_a, lru_b_a, lru_w_x, lru_b_x,
           lru_lambda, w_out_odd, w_mlp_up, w_mlp_down):
    batch, t, d = x.shape
    x2d = x.reshape(batch * t, d)
    group = d // 2
    kw = GLA_HEADS * GLA_DK

    w_in = w_in_even[0]
    o_ga = 2 * kw + 2 * group
    o_fq = o_ga + GLA_RANK
    o_ff = o_fq + 3 * group
    q_scale = LOG2E * HEAD_DIM ** -0.5
    w_main = jnp.concatenate(
        [w_in[:, :o_ga], w_in[:, o_fq:o_fq + group] * q_scale, w_in[:, o_fq + group:o_ff]],
        axis=1).astype(BF16)
    n_small = FOX_HEADS + GLA_RANK
    w_small = jnp.concatenate(
        [w_in[:, o_ff:], w_in[:, o_ga:o_fq], jnp.zeros((d, LANES - n_small), F32)], axis=1).astype(BF16)
    wa_pad = jnp.zeros((LANES, kw), F32).at[FOX_HEADS:n_small].set(gla_w_a_up[0]).astype(BF16)
    fox_bias = jnp.zeros((1, LANES), F32).at[0, :FOX_HEADS].set(fox_b_f[0])

    m = batch * t
    splits = [(0, kw, False), (kw, kw, False), (2 * kw, group, False), (2 * kw + group, group, False),
              (o_ga, group, True), (o_ga + group, group, True), (o_ga + 2 * group, group, True)]
    g_q, g_k, g_v, g_r, f_q, f_k, f_v, small = _norm_proj(x2d, norm_w[0, 0], w_main, splits, w_small)
    out_a = _gla(g_q, g_k, g_v, g_r, small, wa_pad, gla_b_a[0], gla_norm_w[0], batch, t)
    cum = _fox_cum(small, fox_bias, batch, t)
    out_b = _fox(f_q, f_k, f_v, cum, batch, t)
    x2d = _mix_mlp(out_a, out_b, x2d, w_out_even[0], norm_w[0, 1:4], w_mlp_up, w_mlp_down, 0)

    w_in = w_in_odd[0]
    w_main = jnp.concatenate([w_in[:, :group] * q_scale, w_in[:, group:]], axis=1).astype(BF16)
    splits = [(c * group, group, False) for c in range(5)]
    c_q, c_k, c_v, d_gate, d_in = _norm_proj(x2d, norm_w[1, 0], w_main, splits)
    out_c = _chunk_attn(c_q, c_k, c_v, _chunk_attn_table(rel_bias[0]), batch, t)
    out_d = _lru(d_gate.reshape(batch, t, group), d_in.reshape(batch, t, group), conv_w[0], conv_b[0],
                 _block_diag_halves(0.5 * lru_w_a[0]), 0.5 * lru_b_a[0],
                 _block_diag_halves(0.5 * lru_w_x[0]), 0.5 * lru_b_x[0], lru_lambda[0])
    x2d = _mix_mlp(out_c, out_d.reshape(m, group), x2d, w_out_odd[0], norm_w[1, 1:4],
                   w_mlp_up, w_mlp_down, 1)
    return x2d.reshape(batch, t, d)
```

```python
import functools

import jax
import jax.numpy as jnp
from jax import lax
from jax.experimental import pallas as pl
from jax.experimental.pallas import tpu as pltpu

F32 = jnp.float32
BF16 = jnp.bfloat16

NORM_EPS = 1e-6
CHUNK = 64
GLA_HEADS = 4
GLA_DK = 64
GLA_DV = 128
GLA_RANK = 16
GLA_GATE_TAU = 16.0
FOX_HEADS = 8
HEAD_DIM = 64
CA_LEFT_CHUNKS = 8
REL_CLIP = 128
CONV_WIDTH = 4
LRU_C = 8.0

LANES = 128
MXU_DIM = 256
MASK_VALUE = -1e30
VMEM_LIMIT_BYTES = 56 * 1024 * 1024

ROW_TILE = 1024
PROJ_ROW_TILE = 1024
STAGE_ROWS = 512
STAGE_COLS = 1024
FOX_BLOCK = 512
FOX_KBLOCK = 512
ONES_ROWS = 16
CA_QBLOCK = 4 * CHUNK
CA_WINDOW = CA_QBLOCK + CA_LEFT_CHUNKS * CHUNK
CA_PAD = CA_LEFT_CHUNKS * CHUNK
CA_LINE = 1024
LOG2E = 1.4426950408889634
LN2 = 0.6931471805599453
CUM_TERMS = 3
GLA_GROUP = 4
LRU_ROWS = 256
LRU_FRAMES = 256
LRU_PITCH = LRU_FRAMES + 8
CONV_HIST = 8
SCAN_UNROLL = 8


def _params(n_axes, vmem_mib):
    assert vmem_mib * 1024 * 1024 <= VMEM_LIMIT_BYTES
    return pltpu.CompilerParams(
        dimension_semantics=("arbitrary",) * n_axes,
        vmem_limit_bytes=vmem_mib * 1024 * 1024,
    )


def _const_spec(shape):
    nd = len(shape)
    return pl.BlockSpec(shape, lambda *_: (0,) * nd, pipeline_mode=pl.Buffered(1))


def _rmsnorm(x, w):
    y = x * lax.rsqrt(jnp.mean(x * x, axis=-1, keepdims=True) + NORM_EPS)
    return y * w


def _log_sigmoid(z):
    return jnp.minimum(z, 0.0) - LN2 * jnp.log2(1.0 + jnp.exp2(-LOG2E * jnp.abs(z)))


def _sigmoid(z):
    return 0.5 * jnp.tanh(0.5 * z) + 0.5


def _dot(a, b):
    return jnp.dot(a, b, preferred_element_type=F32)


def _dot_tn(a, b):
    return lax.dot_general(a, b, (((0,), (0,)), ((), ())), preferred_element_type=F32)


def _half_mask(lane, j):
    return lane < HEAD_DIM if j == 0 else lane >= HEAD_DIM


def _split3(x):
    h1 = x.astype(BF16)
    r1 = x - h1.astype(F32)
    h2 = r1.astype(BF16)
    h3 = (r1 - h2.astype(F32)).astype(BF16)
    return h1, h2, h3


def _row_major(i):
    return (i, 0)


def _norm_proj_kernel(x_ref, nw_ref, w_ref, *refs, splits, n_chunk):
    with_small = len(refs) == len(splits) + 2
    o_refs = refs[1:1 + len(splits)] if with_small else refs
    tm = x_ref.shape[0]
    for rows in (slice(0, tm // 2), slice(tm // 2, tm)):
        h = _rmsnorm(x_ref[rows, :], nw_ref[...]).astype(BF16)
        for (c0, width, by_pair), o_ref in zip(splits, o_refs):
            for n0 in range(0, width, n_chunk):
                n1 = min(n0 + n_chunk, width)
                res = _dot(h, w_ref[:, c0 + n0:c0 + n1]).astype(o_ref.dtype)
                if by_pair:
                    for p in range(n0 // LANES, n1 // LANES):
                        o_ref[p, rows, :] = res[:, p * LANES - n0:(p + 1) * LANES - n0]
                else:
                    o_ref[rows, n0:n1] = res
        if with_small:
            refs[-1][rows, :] = _dot(h, refs[0][...])


def _norm_proj(x2d, nw, w_main, splits, w_small=None):
    m, d = x2d.shape
    in_specs = [pl.BlockSpec((PROJ_ROW_TILE, d), _row_major), _const_spec((1, d)),
                _const_spec(w_main.shape)]
    args = [x2d, nw.reshape(1, d), w_main]
    out_specs, out_shape = [], []
    for _, width, by_pair in splits:
        if by_pair:
            out_specs.append(pl.BlockSpec((width // LANES, PROJ_ROW_TILE, LANES), lambda i: (0, i, 0)))
            out_shape.append(jax.ShapeDtypeStruct((width // LANES, m, LANES), BF16))
        else:
            out_specs.append(pl.BlockSpec((PROJ_ROW_TILE, width), _row_major))
            out_shape.append(jax.ShapeDtypeStruct((m, width), BF16))
    if w_small is not None:
        in_specs.append(_const_spec(w_small.shape))
        args.append(w_small)
        out_specs.append(pl.BlockSpec((PROJ_ROW_TILE, w_small.shape[1]), _row_major))
        out_shape.append(jax.ShapeDtypeStruct((m, w_small.shape[1]), F32))
    return pl.pallas_call(
        functools.partial(_norm_proj_kernel, splits=tuple(splits), n_chunk=2 * MXU_DIM),
        grid=(m // PROJ_ROW_TILE,),
        in_specs=in_specs,
        out_specs=out_specs,
        out_shape=out_shape,
        compiler_params=_params(1, 36),
        name="norm_proj",
    )(*args)


def _stage_bf16_weights(pairs, stage_ref, sem_ref):
    pieces = []
    for src, dst in pairs:
        rows, cols = dst.shape
        for r0 in range(0, rows, STAGE_ROWS):
            for c0 in range(0, cols, STAGE_COLS):
                pieces.append((src, dst, r0, c0))

    def copy(k):
        src, _, r0, c0 = pieces[k]
        return pltpu.make_async_copy(
            src.at[pl.ds(r0, STAGE_ROWS), pl.ds(c0, STAGE_COLS)], stage_ref.at[k % 2], sem_ref.at[k % 2])

    copy(0).start()
    for k, (_, dst, r0, c0) in enumerate(pieces):
        if k + 1 < len(pieces):
            copy(k + 1).start()
        copy(k).wait()
        dst[r0:r0 + STAGE_ROWS, c0:c0 + STAGE_COLS] = stage_ref[k % 2].astype(BF16)


def _mix_mlp_kernel(a_ref, b_ref, x_ref, nw_ref, wo_hbm, wu_hbm, wd_hbm, o_ref,
                    wo_ref, wu_ref, wd_ref, u_ref, y_ref, stage_ref, sem_ref, *, chunk, layer):
    @pl.when(pl.program_id(0) == 0)
    def _():
        _stage_bf16_weights([(wo_hbm, wo_ref), (wu_hbm.at[layer], wu_ref), (wd_hbm.at[layer], wd_ref)],
                            stage_ref, sem_ref)

    d_ff = wu_ref.shape[1]
    d = wd_ref.shape[1]
    tm = x_ref.shape[0]
    halves = [slice(0, tm // 2), slice(tm // 2, tm)]

    def out_proj(rows):
        if len(b_ref.shape) == 3:
            b = jnp.concatenate([b_ref[p, rows, :] for p in range(b_ref.shape[0])], axis=1)
        else:
            b = b_ref[rows, :]
        mix = jnp.concatenate([a_ref[rows, :], b], axis=1)
        for n0 in range(0, d, chunk):
            y_ref[rows, n0:n0 + chunk] = _dot(mix, wo_ref[:, n0:n0 + chunk])

    def residual_and_norm(rows):
        x1 = x_ref[rows, :] + _rmsnorm(y_ref[rows, :], nw_ref[0:1, :])
        o_ref[rows, :] = x1
        return _rmsnorm(x1, nw_ref[1:2, :]).astype(BF16)

    def up_proj(rows, h):
        for f0 in range(0, d_ff, chunk):
            u = jnp.maximum(_dot(h, wu_ref[:, f0:f0 + chunk]), 0.0)
            u_ref[rows, f0:f0 + chunk] = (u * u).astype(BF16)

    def down_proj(rows):
        for n0 in range(0, d, chunk):
            y_ref[rows, n0:n0 + chunk] = _dot(u_ref[rows, :], wd_ref[:, n0:n0 + chunk])

    def finish(rows):
        o_ref[rows, :] = o_ref[rows, :] + _rmsnorm(y_ref[rows, :], nw_ref[2:3, :])

    first, second = halves
    out_proj(first)
    out_proj(second)
    h_first = residual_and_norm(first)
    up_proj(first, h_first)
    h_second = residual_and_norm(second)
    up_proj(second, h_second)
    down_proj(first)
    down_proj(second)
    finish(first)
    finish(second)


def _mix_mlp(mix_a, mix_b, x2d, w_out, nw3, w_up_all, w_down_all, layer):
    m, d = x2d.shape
    g = mix_a.shape[1]
    d_ff = w_up_all.shape[2]
    return pl.pallas_call(
        functools.partial(_mix_mlp_kernel, chunk=2 * MXU_DIM, layer=layer),
        grid=(m // ROW_TILE,),
        in_specs=[
            pl.BlockSpec((ROW_TILE, g), _row_major),
            (pl.BlockSpec((mix_b.shape[0], ROW_TILE, LANES), lambda i: (0, i, 0)) if mix_b.ndim == 3
             else pl.BlockSpec((ROW_TILE, g), _row_major)),
            pl.BlockSpec((ROW_TILE, d), _row_major),
            _const_spec((3, d)),
            pl.BlockSpec(memory_space=pl.ANY),
            pl.BlockSpec(memory_space=pl.ANY),
            pl.BlockSpec(memory_space=pl.ANY),
        ],
        out_specs=pl.BlockSpec((ROW_TILE, d), _row_major),
        out_shape=jax.ShapeDtypeStruct((m, d), F32),
        scratch_shapes=[
            pltpu.VMEM((2 * g, d), BF16),
            pltpu.VMEM((d, d_ff), BF16),
            pltpu.VMEM((d_ff, d), BF16),
            pltpu.VMEM((ROW_TILE, d_ff), BF16),
            pltpu.VMEM((ROW_TILE, d), F32),
            pltpu.VMEM((2, STAGE_ROWS, STAGE_COLS), F32),
            pltpu.SemaphoreType.DMA((2,)),
        ],
        compiler_params=_params(1, 56),
        name="mix_mlp",
    )(mix_a, mix_b, x2d, nw3, w_out, w_up_all, w_down_all)


def _gla_kernel(q_ref, k_ref, v_ref, r_ref, sm_ref, wa_ref, ba_ref, nw_ref, o_ref,
                la_ref, st_ref, dec_ref):
    t = q_ref.shape[0]
    nc = t // CHUNK
    n_pairs = GLA_HEADS // 2

    z = _dot(sm_ref[...].astype(BF16), wa_ref[...]) + ba_ref[...]
    la_ref[...] = _log_sigmoid(z) * (1.0 / GLA_GATE_TAU)

    grp = GLA_GROUP * CHUNK
    row = lax.broadcasted_iota(jnp.int32, (grp, grp), 0)
    col = lax.broadcasted_iota(jnp.int32, (grp, grp), 1)
    tri = jnp.where((row >= col) & (row // CHUNK == col // CHUNK), 1.0, 0.0).astype(BF16)
    erow = lax.broadcasted_iota(jnp.int32, (grp, GLA_GROUP * LANES), 0)
    ecol = lax.broadcasted_iota(jnp.int32, (grp, GLA_GROUP * LANES), 1)
    chunk_ones = jnp.where(erow // CHUNK == ecol // LANES, 1.0, 0.0).astype(BF16)

    def increments(g, carry):
        r0 = pl.multiple_of(g * grp, grp)
        la = la_ref[pl.ds(r0, grp), :]
        hi = la.astype(BF16)
        lo = (la - hi.astype(F32)).astype(BF16)
        cum = _dot(tri, hi) + _dot(tri, lo)
        dec = jnp.exp(_dot_tn(hi, chunk_ones) + _dot_tn(lo, chunk_ones))
        k_g = k_ref[pl.ds(r0, grp), :].astype(F32)
        v_g = v_ref[pl.ds(r0, grp), :]
        k_dec = []
        for c in range(GLA_GROUP):
            rows = slice(c * CHUNK, (c + 1) * CHUNK)
            total = cum[(c + 1) * CHUNK - 1:(c + 1) * CHUNK, :]
            k_dec.append((k_g[rows] * jnp.exp(total - cum[rows])).astype(BF16))
        for c in range(GLA_GROUP):
            rows = slice(c * CHUNK, (c + 1) * CHUNK)
            dec_ref[g * GLA_GROUP + c] = dec[:, c * LANES:(c + 1) * LANES]
            for h in range(GLA_HEADS):
                p, j = divmod(h, 2)
                full = _dot_tn(k_dec[c][:, p * LANES:(p + 1) * LANES], v_g[rows, h * GLA_DV:(h + 1) * GLA_DV])
                st_ref[g * GLA_GROUP + c, p, j * GLA_DK:(j + 1) * GLA_DK, :] = full[j * GLA_DK:(j + 1) * GLA_DK, :]
        return carry

    lax.fori_loop(0, nc // GLA_GROUP, increments, 0)

    def scan(c, carry):
        for p in range(n_pairs):
            st_ref[c, p] = st_ref[c, p] + dec_ref[c, p * LANES:(p + 1) * LANES, :] * st_ref[c - 1, p]
        return carry

    lax.fori_loop(1, nc, scan, 0)

    lane = lax.broadcasted_iota(jnp.int32, (CHUNK, LANES), 1)
    scale = GLA_DK ** -0.5

    def outputs(g, carry):
        r0 = pl.multiple_of(g * grp, grp)
        q_g = q_ref[pl.ds(r0, grp), :] * jnp.asarray(scale, BF16)
        raw = []
        for c in range(GLA_GROUP):
            rows = slice(c * CHUNK, (c + 1) * CHUNK)
            for h in range(GLA_HEADS):
                p, j = divmod(h, 2)
                q_pair = q_g[rows, p * LANES:(p + 1) * LANES]
                q_h = jnp.where(_half_mask(lane, j), q_pair, jnp.zeros_like(q_pair))
                raw.append(_dot(q_h, st_ref[g * GLA_GROUP + c, p].astype(BF16)))
        for c in range(GLA_GROUP):
            for h in range(GLA_HEADS):
                o = raw[c * GLA_HEADS + h]
                o = o * lax.rsqrt(jnp.mean(o * o, axis=-1, keepdims=True) + NORM_EPS)
                o = o * nw_ref[:, h * GLA_DV:(h + 1) * GLA_DV]
                rows = pl.ds(r0 + c * CHUNK, CHUNK)
                r_h = r_ref[rows, h * GLA_DV:(h + 1) * GLA_DV].astype(F32)
                o_ref[rows, h * GLA_DV:(h + 1) * GLA_DV] = (o * (r_h * _sigmoid(r_h))).astype(BF16)
        return carry

    lax.fori_loop(0, nc // GLA_GROUP, outputs, 0)


def _gla(q, k, v, r, small, wa_pad, b_a, norm_w, batch, t):
    kw = GLA_HEADS * GLA_DK
    gw = GLA_HEADS * GLA_DV
    nc = t // CHUNK
    return pl.pallas_call(
        _gla_kernel,
        grid=(batch,),
        in_specs=[
            pl.BlockSpec((t, kw), lambda b: (b, 0)),
            pl.BlockSpec((t, kw), lambda b: (b, 0)),
            pl.BlockSpec((t, gw), lambda b: (b, 0)),
            pl.BlockSpec((t, gw), lambda b: (b, 0)),
            pl.BlockSpec((t, LANES), lambda b: (b, 0)),
            _const_spec((LANES, kw)),
            _const_spec((1, kw)),
            _const_spec((1, gw)),
        ],
        out_specs=pl.BlockSpec((t, gw), lambda b: (b, 0)),
        out_shape=jax.ShapeDtypeStruct((batch * t, gw), BF16),
        scratch_shapes=[
            pltpu.VMEM((t, kw), F32),
            pltpu.VMEM((nc, GLA_HEADS // 2, LANES, GLA_DV), F32),
            pltpu.VMEM((nc, kw, LANES), F32),
        ],
        compiler_params=_params(1, 56),
        name="gla",
    )(q, k, v, r, small, wa_pad, b_a.reshape(1, kw), norm_w.reshape(1, gw))


def _fox_cum_kernel(sm_ref, bias_ref, o_ref):
    t = sm_ref.shape[0]
    row = lax.broadcasted_iota(jnp.int32, (LANES, LANES), 0)
    col = lax.broadcasted_iota(jnp.int32, (LANES, LANES), 1)
    tri = jnp.where(row >= col, 1.0, 0.0).astype(BF16)
    carry = jnp.zeros((1, LANES), F32)
    for blk in range(t // LANES):
        rows = slice(blk * LANES, (blk + 1) * LANES)
        ls = _log_sigmoid(sm_ref[rows, :] + bias_ref[...])
        h1, h2, h3 = _split3(ls)
        cb = _dot(tri, h1) + _dot(tri, h2) + _dot(tri, h3) + carry
        o_ref[rows, :] = cb
        carry = cb[LANES - 1:LANES, :]


def _fox_cum(small, bias_row, batch, t):
    return pl.pallas_call(
        _fox_cum_kernel,
        grid=(batch,),
        in_specs=[pl.BlockSpec((t, LANES), lambda b: (b, 0)), _const_spec((1, LANES))],
        out_specs=pl.BlockSpec((t, LANES), lambda b: (b, 0)),
        out_shape=jax.ShapeDtypeStruct((batch * t, LANES), F32),
        compiler_params=_params(1, 8),
        name="fox_cum",
    )(small, bias_row)


def _fox_kernel(q_ref, k_ref, v_ref, c_ref, o_ref, ka_ref, vt_ref, s_ref):
    t = k_ref.shape[0]
    blk = FOX_BLOCK
    kbs = FOX_KBLOCK
    pair = pl.program_id(1)

    sel_r = lax.broadcasted_iota(jnp.int32, (LANES, LANES), 0)
    sel_c = lax.broadcasted_iota(jnp.int32, (LANES, LANES), 1)
    sels = []
    for term in range(CUM_TERMS):
        hit = (((sel_r == 2 * pair) & (sel_c == term))
               | ((sel_r == 2 * pair + 1) & (sel_c == CUM_TERMS + term)))
        sels.append(jnp.where(hit, 1.0, 0.0).astype(BF16))
    for kb in range(t // kbs):
        rows = slice(kb * kbs, (kb + 1) * kbs)
        vt_ref[kb] = v_ref[rows, :].astype(F32).T.astype(BF16)
        parts = _split3(c_ref[rows, :] * (-LOG2E))
        extra = _dot(parts[0], sels[0]) + _dot(parts[1], sels[1]) + _dot(parts[2], sels[2])
        ka_ref[kb, :, 0:LANES] = k_ref[rows, :]
        ka_ref[kb, :, LANES:2 * LANES] = extra.astype(BF16)

    lane = lax.broadcasted_iota(jnp.int32, (blk, LANES), 1)
    krow = lax.broadcasted_iota(jnp.int32, (kbs, blk), 0)
    qcol = lax.broadcasted_iota(jnp.int32, (kbs, blk), 1)
    ahead = qcol - krow

    def augmented_queries(qi):
        q = q_ref[qi * blk:(qi + 1) * blk, :]
        q_aug = []
        for j in range(2):
            q_h = jnp.where(_half_mask(lane, j), q, jnp.zeros_like(q))
            ones = jnp.where((lane >= CUM_TERMS * j) & (lane < CUM_TERMS * (j + 1)), 1.0, 0.0).astype(BF16)
            q_aug.append(jnp.concatenate([q_h, ones], axis=1).astype(F32).T.astype(BF16))
        return q_aug

    def scores(q_aug, qi, kb, slot):
        k_blk = ka_ref[kb]
        offset = kb * kbs - qi * blk
        for j in range(2):
            s = _dot(k_blk, q_aug[j])
            if offset + kbs - 1 > 0:
                s = jnp.where(ahead >= offset, s, MASK_VALUE)
            s_ref[slot, j] = s

    def accumulate(kb, slot, state):
        new = []
        for j in range(2):
            m, l, acc = state[j]
            m_new = jnp.maximum(m, jnp.max(s_ref[slot, j], axis=0, keepdims=True))
            alpha = jnp.exp2(m - m_new)
            p = jnp.exp2(s_ref[slot, j] - m_new)
            l = l * alpha + jnp.sum(p, axis=0, keepdims=True)
            pv = _dot(vt_ref[kb, j * HEAD_DIM:(j + 1) * HEAD_DIM, :], p.astype(BF16))
            new.append((m_new, l, acc * alpha + pv))
        return new

    tasks = [(qi, kb) for qi in range(t // blk) for kb in range((qi + 1) * blk // kbs)]
    fresh = [(jnp.full((1, blk), MASK_VALUE, F32), jnp.zeros((1, blk), F32),
              jnp.zeros((HEAD_DIM, blk), F32))] * 2
    q_aug = augmented_queries(0)
    scores(q_aug, 0, 0, 0)
    state = fresh
    for n, (qi, kb) in enumerate(tasks):
        if n + 1 < len(tasks):
            qi_next, kb_next = tasks[n + 1]
            if qi_next != qi:
                q_aug = augmented_queries(qi_next)
            scores(q_aug, qi_next, kb_next, (n + 1) % 2)
        state = accumulate(kb, n % 2, state)
        if n + 1 == len(tasks) or tasks[n + 1][0] != qi:
            outs = [acc * (1.0 / l) for _, l, acc in state]
            o_ref[qi * blk:(qi + 1) * blk, :] = jnp.concatenate(outs, axis=0).T.astype(BF16)
            state = fresh


def _fox(q, k, v, cum, batch, t):
    n_pairs = FOX_HEADS // 2
    pair_block = pl.BlockSpec((None, t, LANES), lambda b, p: (p, b, 0))
    return pl.pallas_call(
        _fox_kernel,
        grid=(batch, n_pairs),
        in_specs=[
            pair_block,
            pair_block,
            pair_block,
            pl.BlockSpec((t, LANES), lambda b, p: (b, 0)),
        ],
        out_specs=pair_block,
        out_shape=jax.ShapeDtypeStruct((n_pairs, batch * t, LANES), BF16),
        scratch_shapes=[
            pltpu.VMEM((t // FOX_KBLOCK, FOX_KBLOCK, 2 * LANES), BF16),
            pltpu.VMEM((t // FOX_KBLOCK, LANES, FOX_KBLOCK), BF16),
            pltpu.VMEM((2, 2, FOX_KBLOCK, FOX_BLOCK), F32),
        ],
        compiler_params=_params(2, 56),
        name="fox",
    )(q, k, v, cum)


def _chunk_attn_kernel(q_ref, k_ref, v_ref, tab_ref, o_ref, kp_ref, vt_ref, s_ref):
    t, g = q_ref.shape
    qb = CA_QBLOCK
    n_pad = CA_PAD // qb
    n_win = CA_WINDOW // qb

    for kb in range(n_pad):
        kp_ref[kb] = jnp.zeros((qb, g), BF16)
        vt_ref[kb] = jnp.zeros(vt_ref.shape[1:], BF16)
    for kb in range(t // qb):
        rows = slice(kb * qb, (kb + 1) * qb)
        kp_ref[n_pad + kb] = k_ref[rows, :]
        v_t = v_ref[rows, :].T
        for h in range(g // HEAD_DIM):
            vt_ref[n_pad + kb, h, 0:HEAD_DIM, :] = v_t[h * HEAD_DIM:(h + 1) * HEAD_DIM, :]
            vt_ref[n_pad + kb, h, HEAD_DIM:HEAD_DIM + ONES_ROWS, :] = jnp.ones((ONES_ROWS, qb), BF16)

    lane = lax.broadcasted_iota(jnp.int32, (qb, LANES), 1)
    n_heads = g // HEAD_DIM
    lax.fori_loop(0, t // qb, functools.partial(
        _chunk_attn_block, q_ref=q_ref, tab_ref=tab_ref, o_ref=o_ref, kp_ref=kp_ref, vt_ref=vt_ref,
        s_ref=s_ref, lane=lane, n_heads=n_heads, n_pad=n_pad, n_win=n_win), 0)


def _chunk_attn_block(i, carry, *, q_ref, tab_ref, o_ref, kp_ref, vt_ref, s_ref, lane, n_heads,
                      n_pad, n_win):
    qb = CA_QBLOCK
    q_rows = pl.ds(pl.multiple_of(i * qb, qb), qb)

    def scores(h):
        pair, j = divmod(h, 2)
        cols = slice(pair * LANES, (pair + 1) * LANES)
        q_pair = q_ref[q_rows, cols]
        q_h = jnp.where(_half_mask(lane, j), q_pair, jnp.zeros_like(q_pair))
        q_ht = q_h.T
        for w in range(n_win):
            tab_blk = jnp.where(i + w >= n_pad, w, n_win)
            s_ref[h % 2, w] = _dot(kp_ref[i + w, :, cols], q_ht) + tab_ref[h, tab_blk]

    def weighted_values(h):
        m = s_ref[h % 2, 0].max(axis=0, keepdims=True)
        for w in range(1, n_win):
            m = jnp.maximum(m, s_ref[h % 2, w].max(axis=0, keepdims=True))
        acc = jnp.zeros((HEAD_DIM + ONES_ROWS, qb), F32)
        for w in range(n_win):
            p = jnp.exp2((s_ref[h % 2, w] - m).astype(BF16))
            acc = acc + _dot(vt_ref[i + w, h], p)
        return acc[0:HEAD_DIM] * (1.0 / acc[HEAD_DIM:HEAD_DIM + 1])

    outs = []
    scores(0)
    for h in range(n_heads):
        if h + 1 < n_heads:
            scores(h + 1)
        outs.append(weighted_values(h))
    o_ref[q_rows, :] = jnp.concatenate(outs, axis=0).T.astype(BF16)
    return carry


def _chunk_attn(q, k, v, table, batch, t):
    g = table.shape[0] * HEAD_DIM
    n_blocks = (CA_PAD + t) // CA_QBLOCK
    return pl.pallas_call(
        _chunk_attn_kernel,
        grid=(batch,),
        in_specs=[
            pl.BlockSpec((t, g), lambda b: (b, 0)),
            pl.BlockSpec((t, g), lambda b: (b, 0)),
            pl.BlockSpec((t, g), lambda b: (b, 0)),
            _const_spec(table.shape),
        ],
        out_specs=pl.BlockSpec((t, g), lambda b: (b, 0)),
        out_shape=jax.ShapeDtypeStruct((batch * t, g), BF16),
        scratch_shapes=[
            pltpu.VMEM((n_blocks, CA_QBLOCK, g), BF16),
            pltpu.VMEM((n_blocks, g // HEAD_DIM, HEAD_DIM + ONES_ROWS, CA_QBLOCK), BF16),
            pltpu.VMEM((2, CA_WINDOW // CA_QBLOCK, CA_QBLOCK, CA_QBLOCK), F32),
        ],
        compiler_params=_params(1, 40),
        name="chunk_attn",
    )(q, k, v, table)


def _ca_table_kernel(line_ref, o_ref):
    n_win = CA_WINDOW // CA_QBLOCK
    rows = 8
    x = jnp.broadcast_to(line_ref[0] * LOG2E, (rows, CA_LINE))
    qc = lax.broadcasted_iota(jnp.int32, (rows, CA_QBLOCK), 1) // CHUNK
    for grp in range(CA_WINDOW // rows):
        r0 = grp * rows
        y = pltpu.roll(x, (CA_LINE - CA_WINDOW + 1 + r0) % CA_LINE, axis=1, stride=1, stride_axis=0)
        kc = r0 // CHUNK
        allowed = (qc <= kc) & (qc >= kc - CA_LEFT_CHUNKS)
        w, r = divmod(r0, CA_QBLOCK)
        o_ref[0, w, r:r + rows, :] = jnp.where(allowed, y[:, 0:CA_QBLOCK], MASK_VALUE)
    o_ref[0, n_win] = jnp.full((CA_QBLOCK, CA_QBLOCK), MASK_VALUE, F32)


def _chunk_attn_table(rel_bias):
    heads = rel_bias.shape[0]
    n_win = CA_WINDOW // CA_QBLOCK
    left = CA_WINDOW - 1 - CA_PAD - REL_CLIP
    line = jnp.pad(rel_bias.astype(F32), ((0, 0), (left, CA_LINE - left - rel_bias.shape[1])), mode="edge")
    return pl.pallas_call(
        _ca_table_kernel,
        grid=(heads,),
        in_specs=[pl.BlockSpec((1, 1, CA_LINE), lambda h: (h, 0, 0))],
        out_specs=pl.BlockSpec((1, n_win + 1, CA_QBLOCK, CA_QBLOCK), lambda h: (h, 0, 0, 0)),
        out_shape=jax.ShapeDtypeStruct((heads, n_win + 1, CA_QBLOCK, CA_QBLOCK), F32),
        compiler_params=_params(1, 8),
        name="ca_table",
    )(line.reshape(heads, 1, CA_LINE))


GELU_C0 = 0.7978845608028654
GELU_C1 = GELU_C0 * 0.044715


def _gelu(x):
    inner = x * (GELU_C0 + GELU_C1 * (x * x))
    return (0.5 * x) * (1.0 + jnp.tanh(inner))


def _lru_kernel(g_ref, x_ref, cw_ref, cb_ref, wa_ref, ba_ref, wx_ref, bx_ref, lam_ref, o_ref,
                xin_ref, xf_ref, a_ref, b_ref, hout_ref, h_ref):
    nb, frames, w = x_ref.shape
    n_slab = w // LANES
    sub = LRU_ROWS // nb
    i = pl.program_id(0)

    @pl.when(i == 0)
    def _():
        xf_ref[0:CONV_HIST] = jnp.zeros((CONV_HIST, nb, w), F32)
        h_ref[...] = jnp.zeros((nb, w), F32)

    for b in range(nb):
        xb = x_ref[b].astype(F32)
        for s in range(n_slab):
            xin_ref[s, b * LRU_PITCH:b * LRU_PITCH + frames, :] = xb[:, s * LANES:(s + 1) * LANES]

    def gather(t, carry):
        for s in range(n_slab):
            xf_ref[CONV_HIST + t, :, s * LANES:(s + 1) * LANES] = (
                xin_ref[s, pl.ds(t, nb, stride=LRU_PITCH), :])
        return carry

    lax.fori_loop(0, frames, gather, 0, unroll=SCAN_UNROLL)

    half_log2_base = (0.5 * LRU_C * LOG2E) * _log_sigmoid(lam_ref[...])
    half = w // 2
    for blk in range(frames // sub):
        t0 = blk * sub
        xc = cb_ref[...]
        for tap in range(CONV_WIDTH):
            lo = CONV_HIST + t0 - tap
            xc = xc + (cw_ref[CONV_WIDTH - 1 - tap:CONV_WIDTH - tap, :]
                       * xf_ref[lo:lo + sub].reshape(LRU_ROWS, w))
        xcb = xc.astype(BF16)
        gr, gi = [], []
        for hb in range(2):
            cols = slice(hb * half, (hb + 1) * half)
            gr.append(_dot(xcb[:, cols], wa_ref[hb]))
            gi.append(_dot(xcb[:, cols], wx_ref[hb]))
        t_r = jnp.tanh(jnp.concatenate(gr, axis=1) + ba_ref[...])
        gate_i = 0.5 * jnp.tanh(jnp.concatenate(gi, axis=1) + bx_ref[...]) + 0.5
        a = jnp.exp2(t_r * half_log2_base + half_log2_base)
        a_ref[t0:t0 + sub] = a.reshape(sub, nb, w)
        y = 1.0 - a * a
        root = jnp.where(y > 0.0, y * lax.rsqrt(y), 0.0)
        b_ref[t0:t0 + sub] = (root * (gate_i * xc)).reshape(sub, nb, w)

    xf_ref[0:CONV_HIST] = xf_ref[frames:frames + CONV_HIST]

    def scan(t, h):
        h = a_ref[t] * h + b_ref[t]
        for s in range(n_slab):
            hout_ref[s, pl.ds(t, nb, stride=LRU_PITCH), :] = h[:, s * LANES:(s + 1) * LANES]
        return h

    h_ref[...] = lax.fori_loop(0, frames, scan, h_ref[...], unroll=SCAN_UNROLL)

    for b in range(nb):
        rows = slice(b * LRU_PITCH, b * LRU_PITCH + frames)
        hv = jnp.concatenate([hout_ref[s, rows, :] for s in range(n_slab)], axis=1)
        o_ref[b] = (hv * _gelu(g_ref[b].astype(F32))).astype(BF16)


def _lru(gate3, x3, conv_w, conv_b, wa_bd, b_a, wx_bd, b_x, lam):
    nb, t, w = x3.shape
    n_slab = w // LANES
    return pl.pallas_call(
        _lru_kernel,
        grid=(t // LRU_FRAMES,),
        in_specs=[
            pl.BlockSpec((nb, LRU_FRAMES, w), lambda i: (0, i, 0)),
            pl.BlockSpec((nb, LRU_FRAMES, w), lambda i: (0, i, 0)),
            _const_spec((CONV_WIDTH, w)),
            _const_spec((1, w)),
            _const_spec(wa_bd.shape),
            _const_spec((1, w)),
            _const_spec(wx_bd.shape),
            _const_spec((1, w)),
            _const_spec((1, w)),
        ],
        out_specs=pl.BlockSpec((nb, LRU_FRAMES, w), lambda i: (0, i, 0)),
        out_shape=jax.ShapeDtypeStruct((nb, t, w), BF16),
        scratch_shapes=[
            pltpu.VMEM((n_slab, nb * LRU_PITCH, LANES), F32),
            pltpu.VMEM((CONV_HIST + LRU_FRAMES, nb, w), F32),
            pltpu.VMEM((LRU_FRAMES, nb, w), F32),
            pltpu.VMEM((LRU_FRAMES, nb, w), F32),
            pltpu.VMEM((n_slab, nb * LRU_PITCH, LANES), F32),
            pltpu.VMEM((nb, w), F32),
        ],
        compiler_params=_params(1, 42),
        name="rglru",
    )(gate3, x3, conv_w, conv_b.reshape(1, w), wa_bd, b_a.reshape(1, w), wx_bd,
      b_x.reshape(1, w), lam.reshape(1, w))


def _block_diag_halves(wblk):
    nb, d, _ = wblk.shape
    per = nb // 2
    eye = jnp.eye(per, dtype=wblk.dtype)
    halves = [jnp.einsum("nde,nm->ndme", wblk[h * per:(h + 1) * per], eye).reshape(per * d, per * d)
              for h in range(2)]
    return jnp.stack(halves).astype(BF16)


def kernel(x, norm_w, w_in_even, gla_w_a_up, gla_b_a, gla_norm_w, fox_b_f, w_out_even,
           w_in_odd, rel_bias, conv_w, conv_b, lru_w_a, lru_b_a, lru_w_x, lru_b_x,
           lru_lambda, w_out_odd, w_mlp_up, w_mlp_down):
    batch, t, d = x.shape
    x2d = x.reshape(batch * t, d)
    group = d // 2
    kw = GLA_HEADS * GLA_DK

    w_in = w_in_even[0]
    o_ga = 2 * kw + 2 * group
    o_fq = o_ga + GLA_RANK
    o_ff = o_fq + 3 * group
    q_scale = LOG2E * HEAD_DIM ** -0.5
    w_main = jnp.concatenate(
        [w_in[:, :o_ga], w_in[:, o_fq:o_fq + group] * q_scale, w_in[:, o_fq + group:o_ff]],
        axis=1).astype(BF16)
    n_small = FOX_HEADS + GLA_RANK
    w_small = jnp.concatenate(
        [w_in[:, o_ff:], w_in[:, o_ga:o_fq], jnp.zeros((d, LANES - n_small), F32)], axis=1).astype(BF16)
    wa_pad = jnp.zeros((LANES, kw), F32).at[FOX_HEADS:n_small].set(gla_w_a_up[0]).astype(BF16)
    fox_bias = jnp.zeros((1, LANES), F32).at[0, :FOX_HEADS].set(fox_b_f[0])

    m = batch * t
    splits = [(0, kw, False), (kw, kw, False), (2 * kw, group, False), (2 * kw + group, group, False),
              (o_ga, group, True), (o_ga + group, group, True), (o_ga + 2 * group, group, True)]
    g_q, g_k, g_v, g_r, f_q, f_k, f_v, small = _norm_proj(x2d, norm_w[0, 0], w_main, splits, w_small)
    out_a = _gla(g_q, g_k, g_v, g_r, small, wa_pad, gla_b_a[0], gla_norm_w[0], batch, t)
    cum = _fox_cum(small, fox_bias, batch, t)
    out_b = _fox(f_q, f_k, f_v, cum, batch, t)
    x2d = _mix_mlp(out_a, out_b, x2d, w_out_even[0], norm_w[0, 1:4], w_mlp_up, w_mlp_down, 0)

    w_in = w_in_odd[0]
    w_main = jnp.concatenate([w_in[:, :group] * q_scale, w_in[:, group:]], axis=1).astype(BF16)
    splits = [(c * group, group, False) for c in range(5)]
    c_q, c_k, c_v, d_gate, d_in = _norm_proj(x2d, norm_w[1, 0], w_main, splits)
    out_c = _chunk_attn(c_q, c_k, c_v, _chunk_attn_table(rel_bias[0]), batch, t)
    out_d = _lru(d_gate.reshape(batch, t, group), d_in.reshape(batch, t, group), conv_w[0], conv_b[0],
                 _block_diag_halves(0.5 * lru_w_a[0]), 0.5 * lru_b_a[0],
                 _block_diag_halves(0.5 * lru_w_x[0]), 0.5 * lru_b_x[0], lru_lambda[0])
    x2d = _mix_mlp(out_c, out_d.reshape(m, group), x2d, w_out_odd[0], norm_w[1, 1:4],
                   w_mlp_up, w_mlp_down, 1)
    return x2d.reshape(batch, t, d)
```

```python
import functools

import jax
import jax.numpy as jnp
from jax import lax
from jax.experimental import pallas as pl
from jax.experimental.pallas import tpu as pltpu

F32 = jnp.float32
BF16 = jnp.bfloat16

NORM_EPS = 1e-6
CHUNK = 64
GLA_HEADS = 4
GLA_DK = 64
GLA_DV = 128
GLA_RANK = 16
GLA_GATE_TAU = 16.0
FOX_HEADS = 8
HEAD_DIM = 64
CA_LEFT_CHUNKS = 8
REL_CLIP = 128
CONV_WIDTH = 4
LRU_C = 8.0

LANES = 128
MXU_DIM = 256
MASK_VALUE = -1e30
VMEM_LIMIT_BYTES = 56 * 1024 * 1024

ROW_TILE = 1024
PROJ_ROW_TILE = 1024
STAGE_ROWS = 512
STAGE_COLS = 1024
STAGE_W_ROWS = 128
FOX_BLOCK = 512
FOX_KBLOCK = 512
ONES_ROWS = 16
CA_QBLOCK = 4 * CHUNK
CA_WINDOW = CA_QBLOCK + CA_LEFT_CHUNKS * CHUNK
CA_PAD = CA_LEFT_CHUNKS * CHUNK
CA_LINE = 1024
LOG2E = 1.4426950408889634
LN2 = 0.6931471805599453
CUM_TERMS = 3
GLA_GROUP = 4
LRU_ROWS = 256
LRU_FRAMES = 256
LRU_PITCH = LRU_FRAMES + 8
CONV_HIST = 8
SCAN_UNROLL = 8


def _params(n_axes, vmem_mib):
    assert vmem_mib * 1024 * 1024 <= VMEM_LIMIT_BYTES
    return pltpu.CompilerParams(
        dimension_semantics=("arbitrary",) * n_axes,
        vmem_limit_bytes=vmem_mib * 1024 * 1024,
    )


def _const_spec(shape):
    nd = len(shape)
    return pl.BlockSpec(shape, lambda *_: (0,) * nd, pipeline_mode=pl.Buffered(1))


def _rmsnorm(x, w):
    y = x * lax.rsqrt(jnp.mean(x * x, axis=-1, keepdims=True) + NORM_EPS)
    return y * w


def _log_sigmoid(z):
    return jnp.minimum(z, 0.0) - LN2 * jnp.log2(1.0 + jnp.exp2(-LOG2E * jnp.abs(z)))


def _sigmoid(z):
    return 0.5 * jnp.tanh(0.5 * z) + 0.5


def _dot(a, b):
    return jnp.dot(a, b, preferred_element_type=F32)


def _dot_tn(a, b):
    return lax.dot_general(a, b, (((0,), (0,)), ((), ())), preferred_element_type=F32)


def _half_mask(lane, j):
    return lane < HEAD_DIM if j == 0 else lane >= HEAD_DIM


def _split3(x):
    h1 = x.astype(BF16)
    r1 = x - h1.astype(F32)
    h2 = r1.astype(BF16)
    h3 = (r1 - h2.astype(F32)).astype(BF16)
    return h1, h2, h3


def _row_major(i):
    return (i, 0)


def _stage_projection_weights(w_hbm, w_ref, ws_ref, stage_ref, sem_ref, segments, small_segments):
    d = w_ref.shape[0]
    pieces = d // STAGE_W_ROWS

    def copy(k):
        return pltpu.make_async_copy(w_hbm.at[pl.ds(k * STAGE_W_ROWS, STAGE_W_ROWS), :],
                                     stage_ref.at[k % 2], sem_ref.at[k % 2])

    def segment(k, src, width, scale):
        a0 = (src // LANES) * LANES
        a1 = min(-(-(src + width) // LANES) * LANES, stage_ref.shape[2])
        seg = stage_ref[k % 2, :, a0:a1][:, src - a0:src - a0 + width]
        return (seg * scale if scale != 1.0 else seg).astype(BF16)

    copy(0).start()
    for k in range(pieces):
        if k + 1 < pieces:
            copy(k + 1).start()
        copy(k).wait()
        rows = slice(k * STAGE_W_ROWS, (k + 1) * STAGE_W_ROWS)
        for src, width, dst, scale in segments:
            w_ref[rows, dst:dst + width] = segment(k, src, width, scale)
        if ws_ref is not None:
            ws_ref[rows, :] = jnp.zeros((STAGE_W_ROWS, ws_ref.shape[1]), BF16)
            for src, width, dst, scale in small_segments:
                ws_ref[rows, dst:dst + width] = segment(k, src, width, scale)


def _norm_proj_kernel(x_ref, nw_ref, w_hbm, *refs, splits, segments, small_segments, n_chunk):
    with_small = small_segments is not None
    n_out = len(splits) + (1 if with_small else 0)
    o_refs = refs[:len(splits)]
    w_ref = refs[n_out]
    ws_ref = refs[n_out + 1] if with_small else None
    stage_ref, sem_ref = refs[-2:]

    @pl.when(pl.program_id(0) == 0)
    def _():
        _stage_projection_weights(w_hbm, w_ref, ws_ref, stage_ref, sem_ref, segments, small_segments)

    tm = x_ref.shape[0]
    for rows in (slice(0, tm // 2), slice(tm // 2, tm)):
        h = _rmsnorm(x_ref[rows, :], nw_ref[...]).astype(BF16)
        for (c0, width, by_pair), o_ref in zip(splits, o_refs):
            for n0 in range(0, width, n_chunk):
                n1 = min(n0 + n_chunk, width)
                res = _dot(h, w_ref[:, c0 + n0:c0 + n1]).astype(o_ref.dtype)
                if by_pair:
                    for p in range(n0 // LANES, n1 // LANES):
                        o_ref[p, rows, :] = res[:, p * LANES - n0:(p + 1) * LANES - n0]
                else:
                    o_ref[rows, n0:n1] = res
        if with_small:
            refs[len(splits)][rows, :] = _dot(h, ws_ref[...])


def _norm_proj(x2d, nw, w_raw, segments, splits, small_segments=None):
    m, d = x2d.shape
    n_main = max(dst + width for _, width, dst, _ in segments)
    in_specs = [pl.BlockSpec((PROJ_ROW_TILE, d), _row_major), _const_spec((1, d)),
                pl.BlockSpec(memory_space=pl.ANY)]
    args = [x2d, nw.reshape(1, d), w_raw]
    scratch = [pltpu.VMEM((d, n_main), BF16)]
    out_specs, out_shape = [], []
    for _, width, by_pair in splits:
        if by_pair:
            out_specs.append(pl.BlockSpec((width // LANES, PROJ_ROW_TILE, LANES), lambda i: (0, i, 0)))
            out_shape.append(jax.ShapeDtypeStruct((width // LANES, m, LANES), BF16))
        else:
            out_specs.append(pl.BlockSpec((PROJ_ROW_TILE, width), _row_major))
            out_shape.append(jax.ShapeDtypeStruct((m, width), BF16))
    if small_segments is not None:
        out_specs.append(pl.BlockSpec((PROJ_ROW_TILE, LANES), _row_major))
        out_shape.append(jax.ShapeDtypeStruct((m, LANES), F32))
        scratch.append(pltpu.VMEM((d, LANES), BF16))
    scratch += [pltpu.VMEM((2, STAGE_W_ROWS, w_raw.shape[1]), F32), pltpu.SemaphoreType.DMA((2,))]
    return pl.pallas_call(
        functools.partial(_norm_proj_kernel, splits=tuple(splits), segments=tuple(segments),
                          small_segments=None if small_segments is None else tuple(small_segments),
                          n_chunk=2 * MXU_DIM),
        grid=(m // PROJ_ROW_TILE,),
        in_specs=in_specs,
        out_specs=out_specs,
        out_shape=out_shape,
        scratch_shapes=scratch,
        compiler_params=_params(1, 40),
        name="norm_proj",
    )(*args)


def _stage_bf16_weights(pairs, stage_ref, sem_ref):
    pieces = []
    for src, dst in pairs:
        rows, cols = dst.shape
        for r0 in range(0, rows, STAGE_ROWS):
            for c0 in range(0, cols, STAGE_COLS):
                pieces.append((src, dst, r0, c0))

    def copy(k):
        src, _, r0, c0 = pieces[k]
        return pltpu.make_async_copy(
            src.at[pl.ds(r0, STAGE_ROWS), pl.ds(c0, STAGE_COLS)], stage_ref.at[k % 2], sem_ref.at[k % 2])

    copy(0).start()
    for k, (_, dst, r0, c0) in enumerate(pieces):
        if k + 1 < len(pieces):
            copy(k + 1).start()
        copy(k).wait()
        dst[r0:r0 + STAGE_ROWS, c0:c0 + STAGE_COLS] = stage_ref[k % 2].astype(BF16)


def _mix_mlp_kernel(a_ref, b_ref, x_ref, nw_ref, wo_hbm, wu_hbm, wd_hbm, o_ref,
                    wo_ref, wu_ref, wd_ref, u_ref, y_ref, stage_ref, sem_ref, *, chunk, layer):
    @pl.when(pl.program_id(0) == 0)
    def _():
        _stage_bf16_weights([(wo_hbm, wo_ref), (wu_hbm.at[layer], wu_ref), (wd_hbm.at[layer], wd_ref)],
                            stage_ref, sem_ref)

    d_ff = wu_ref.shape[1]
    d = wd_ref.shape[1]
    tm = x_ref.shape[0]
    halves = [slice(0, tm // 2), slice(tm // 2, tm)]

    def out_proj(rows):
        if len(b_ref.shape) == 3:
            b = jnp.concatenate([b_ref[p, rows, :] for p in range(b_ref.shape[0])], axis=1)
        else:
            b = b_ref[rows, :]
        mix = jnp.concatenate([a_ref[rows, :], b], axis=1)
        for n0 in range(0, d, chunk):
            y_ref[rows, n0:n0 + chunk] = _dot(mix, wo_ref[:, n0:n0 + chunk])

    def residual_and_norm(rows):
        x1 = x_ref[rows, :] + _rmsnorm(y_ref[rows, :], nw_ref[0:1, :])
        o_ref[rows, :] = x1
        return _rmsnorm(x1, nw_ref[1:2, :]).astype(BF16)

    def up_proj(rows, h):
        for f0 in range(0, d_ff, chunk):
            u = jnp.maximum(_dot(h, wu_ref[:, f0:f0 + chunk]), 0.0)
            u_ref[rows, f0:f0 + chunk] = (u * u).astype(BF16)

    def down_proj(rows):
        for n0 in range(0, d, chunk):
            y_ref[rows, n0:n0 + chunk] = _dot(u_ref[rows, :], wd_ref[:, n0:n0 + chunk])

    def finish(rows):
        o_ref[rows, :] = o_ref[rows, :] + _rmsnorm(y_ref[rows, :], nw_ref[2:3, :])

    first, second = halves
    out_proj(first)
    out_proj(second)
    h_first = residual_and_norm(first)
    up_proj(first, h_first)
    h_second = residual_and_norm(second)
    up_proj(second, h_second)
    down_proj(first)
    down_proj(second)
    finish(first)
    finish(second)


def _mix_mlp(mix_a, mix_b, x2d, w_out, nw3, w_up_all, w_down_all, layer):
    m, d = x2d.shape
    g = mix_a.shape[1]
    d_ff = w_up_all.shape[2]
    return pl.pallas_call(
        functools.partial(_mix_mlp_kernel, chunk=2 * MXU_DIM, layer=layer),
        grid=(m // ROW_TILE,),
        in_specs=[
            pl.BlockSpec((ROW_TILE, g), _row_major),
            (pl.BlockSpec((mix_b.shape[0], ROW_TILE, LANES), lambda i: (0, i, 0)) if mix_b.ndim == 3
             else pl.BlockSpec((ROW_TILE, g), _row_major)),
            pl.BlockSpec((ROW_TILE, d), _row_major),
            _const_spec((3, d)),
            pl.BlockSpec(memory_space=pl.ANY),
            pl.BlockSpec(memory_space=pl.ANY),
            pl.BlockSpec(memory_space=pl.ANY),
        ],
        out_specs=pl.BlockSpec((ROW_TILE, d), _row_major),
        out_shape=jax.ShapeDtypeStruct((m, d), F32),
        scratch_shapes=[
            pltpu.VMEM((2 * g, d), BF16),
            pltpu.VMEM((d, d_ff), BF16),
            pltpu.VMEM((d_ff, d), BF16),
            pltpu.VMEM((ROW_TILE, d_ff), BF16),
            pltpu.VMEM((ROW_TILE, d), F32),
            pltpu.VMEM((2, STAGE_ROWS, STAGE_COLS), F32),
            pltpu.SemaphoreType.DMA((2,)),
        ],
        compiler_params=_params(1, 56),
        name="mix_mlp",
    )(mix_a, mix_b, x2d, nw3, w_out, w_up_all, w_down_all)


def _gla_kernel(q_ref, k_ref, v_ref, r_ref, sm_ref, wa_ref, ba_ref, nw_ref, o_ref,
                la_ref, st_ref, dec_ref):
    t = q_ref.shape[0]
    nc = t // CHUNK
    n_pairs = GLA_HEADS // 2

    z = _dot(sm_ref[...].astype(BF16), wa_ref[...]) + ba_ref[...]
    la_ref[...] = _log_sigmoid(z) * (1.0 / GLA_GATE_TAU)

    grp = GLA_GROUP * CHUNK
    row = lax.broadcasted_iota(jnp.int32, (grp, grp), 0)
    col = lax.broadcasted_iota(jnp.int32, (grp, grp), 1)
    tri = jnp.where((row >= col) & (row // CHUNK == col // CHUNK), 1.0, 0.0).astype(BF16)
    erow = lax.broadcasted_iota(jnp.int32, (grp, GLA_GROUP * LANES), 0)
    ecol = lax.broadcasted_iota(jnp.int32, (grp, GLA_GROUP * LANES), 1)
    chunk_ones = jnp.where(erow // CHUNK == ecol // LANES, 1.0, 0.0).astype(BF16)

    def increments(g, carry):
        r0 = pl.multiple_of(g * grp, grp)
        la = la_ref[pl.ds(r0, grp), :]
        hi = la.astype(BF16)
        lo = (la - hi.astype(F32)).astype(BF16)
        cum = _dot(tri, hi) + _dot(tri, lo)
        dec = jnp.exp(_dot_tn(hi, chunk_ones) + _dot_tn(lo, chunk_ones))
        k_g = k_ref[pl.ds(r0, grp), :].astype(F32)
        v_g = v_ref[pl.ds(r0, grp), :]
        k_dec = []
        for c in range(GLA_GROUP):
            rows = slice(c * CHUNK, (c + 1) * CHUNK)
            total = cum[(c + 1) * CHUNK - 1:(c + 1) * CHUNK, :]
            k_dec.append((k_g[rows] * jnp.exp(total - cum[rows])).astype(BF16))
        for c in range(GLA_GROUP):
            rows = slice(c * CHUNK, (c + 1) * CHUNK)
            dec_ref[g * GLA_GROUP + c] = dec[:, c * LANES:(c + 1) * LANES]
            for h in range(GLA_HEADS):
                p, j = divmod(h, 2)
                full = _dot_tn(k_dec[c][:, p * LANES:(p + 1) * LANES], v_g[rows, h * GLA_DV:(h + 1) * GLA_DV])
                st_ref[g * GLA_GROUP + c, p, j * GLA_DK:(j + 1) * GLA_DK, :] = full[j * GLA_DK:(j + 1) * GLA_DK, :]
        return carry

    lax.fori_loop(0, nc // GLA_GROUP, increments, 0)

    def scan(c, carry):
        for p in range(n_pairs):
            st_ref[c, p] = st_ref[c, p] + dec_ref[c, p * LANES:(p + 1) * LANES, :] * st_ref[c - 1, p]
        return carry

    lax.fori_loop(1, nc, scan, 0)

    lane = lax.broadcasted_iota(jnp.int32, (CHUNK, LANES), 1)
    scale = GLA_DK ** -0.5

    def outputs(g, carry):
        r0 = pl.multiple_of(g * grp, grp)
        q_g = q_ref[pl.ds(r0, grp), :] * jnp.asarray(scale, BF16)
        raw = []
        for c in range(GLA_GROUP):
            rows = slice(c * CHUNK, (c + 1) * CHUNK)
            for h in range(GLA_HEADS):
                p, j = divmod(h, 2)
                q_pair = q_g[rows, p * LANES:(p + 1) * LANES]
                q_h = jnp.where(_half_mask(lane, j), q_pair, jnp.zeros_like(q_pair))
                raw.append(_dot(q_h, st_ref[g * GLA_GROUP + c, p].astype(BF16)))
        for c in range(GLA_GROUP):
            for h in range(GLA_HEADS):
                o = raw[c * GLA_HEADS + h]
                o = o * lax.rsqrt(jnp.mean(o * o, axis=-1, keepdims=True) + NORM_EPS)
                o = o * nw_ref[:, h * GLA_DV:(h + 1) * GLA_DV]
                rows = pl.ds(r0 + c * CHUNK, CHUNK)
                r_h = r_ref[rows, h * GLA_DV:(h + 1) * GLA_DV].astype(F32)
                o_ref[rows, h * GLA_DV:(h + 1) * GLA_DV] = (o * (r_h * _sigmoid(r_h))).astype(BF16)
        return carry

    lax.fori_loop(0, nc // GLA_GROUP, outputs, 0)


def _gla(q, k, v, r, small, wa_pad, b_a, norm_w, batch, t):
    kw = GLA_HEADS * GLA_DK
    gw = GLA_HEADS * GLA_DV
    nc = t // CHUNK
    return pl.pallas_call(
        _gla_kernel,
        grid=(batch,),
        in_specs=[
            pl.BlockSpec((t, kw), lambda b: (b, 0)),
            pl.BlockSpec((t, kw), lambda b: (b, 0)),
            pl.BlockSpec((t, gw), lambda b: (b, 0)),
            pl.BlockSpec((t, gw), lambda b: (b, 0)),
            pl.BlockSpec((t, LANES), lambda b: (b, 0)),
            _const_spec((LANES, kw)),
            _const_spec((1, kw)),
            _const_spec((1, gw)),
        ],
        out_specs=pl.BlockSpec((t, gw), lambda b: (b, 0)),
        out_shape=jax.ShapeDtypeStruct((batch * t, gw), BF16),
        scratch_shapes=[
            pltpu.VMEM((t, kw), F32),
            pltpu.VMEM((nc, GLA_HEADS // 2, LANES, GLA_DV), F32),
            pltpu.VMEM((nc, kw, LANES), F32),
        ],
        compiler_params=_params(1, 56),
        name="gla",
    )(q, k, v, r, small, wa_pad, b_a.reshape(1, kw), norm_w.reshape(1, gw))


def _fox_cum_kernel(sm_ref, bias_ref, o_ref):
    t = sm_ref.shape[0]
    row = lax.broadcasted_iota(jnp.int32, (LANES, LANES), 0)
    col = lax.broadcasted_iota(jnp.int32, (LANES, LANES), 1)
    tri = jnp.where(row >= col, 1.0, 0.0).astype(BF16)
    carry = jnp.zeros((1, LANES), F32)
    for blk in range(t // LANES):
        rows = slice(blk * LANES, (blk + 1) * LANES)
        ls = _log_sigmoid(sm_ref[rows, :] + bias_ref[...])
        h1, h2, h3 = _split3(ls)
        cb = _dot(tri, h1) + _dot(tri, h2) + _dot(tri, h3) + carry
        o_ref[rows, :] = cb
        carry = cb[LANES - 1:LANES, :]


def _fox_cum(small, bias_row, batch, t):
    return pl.pallas_call(
        _fox_cum_kernel,
        grid=(batch,),
        in_specs=[pl.BlockSpec((t, LANES), lambda b: (b, 0)), _const_spec((1, LANES))],
        out_specs=pl.BlockSpec((t, LANES), lambda b: (b, 0)),
        out_shape=jax.ShapeDtypeStruct((batch * t, LANES), F32),
        compiler_params=_params(1, 8),
        name="fox_cum",
    )(small, bias_row)


def _fox_kernel(q_ref, k_ref, v_ref, c_ref, o_ref, ka_ref, vt_ref, s_ref):
    t = k_ref.shape[0]
    blk = FOX_BLOCK
    kbs = FOX_KBLOCK
    pair = pl.program_id(1)

    sel_r = lax.broadcasted_iota(jnp.int32, (LANES, LANES), 0)
    sel_c = lax.broadcasted_iota(jnp.int32, (LANES, LANES), 1)
    sels = []
    for term in range(CUM_TERMS):
        hit = (((sel_r == 2 * pair) & (sel_c == term))
               | ((sel_r == 2 * pair + 1) & (sel_c == CUM_TERMS + term)))
        sels.append(jnp.where(hit, 1.0, 0.0).astype(BF16))
    for kb in range(t // kbs):
        rows = slice(kb * kbs, (kb + 1) * kbs)
        vt_ref[kb] = v_ref[rows, :].astype(F32).T.astype(BF16)
        parts = _split3(c_ref[rows, :] * (-LOG2E))
        extra = _dot(parts[0], sels[0]) + _dot(parts[1], sels[1]) + _dot(parts[2], sels[2])
        ka_ref[kb, :, 0:LANES] = k_ref[rows, :]
        ka_ref[kb, :, LANES:2 * LANES] = extra.astype(BF16)

    lane = lax.broadcasted_iota(jnp.int32, (blk, LANES), 1)
    krow = lax.broadcasted_iota(jnp.int32, (kbs, blk), 0)
    qcol = lax.broadcasted_iota(jnp.int32, (kbs, blk), 1)
    ahead = qcol - krow

    def augmented_queries(qi):
        q = q_ref[qi * blk:(qi + 1) * blk, :]
        q_aug = []
        for j in range(2):
            q_h = jnp.where(_half_mask(lane, j), q, jnp.zeros_like(q))
            ones = jnp.where((lane >= CUM_TERMS * j) & (lane < CUM_TERMS * (j + 1)), 1.0, 0.0).astype(BF16)
            q_aug.append(jnp.concatenate([q_h, ones], axis=1).astype(F32).T.astype(BF16))
        return q_aug

    def scores(q_aug, qi, kb, slot):
        k_blk = ka_ref[kb]
        offset = kb * kbs - qi * blk
        for j in range(2):
            s = _dot(k_blk, q_aug[j])
            if offset + kbs - 1 > 0:
                s = jnp.where(ahead >= offset, s, MASK_VALUE)
            s_ref[slot, j] = s

    def accumulate(kb, slot, state):
        new = []
        for j in range(2):
            m, l, acc = state[j]
            m_new = jnp.maximum(m, jnp.max(s_ref[slot, j], axis=0, keepdims=True))
            alpha = jnp.exp2(m - m_new)
            p = jnp.exp2(s_ref[slot, j] - m_new)
            l = l * alpha + jnp.sum(p, axis=0, keepdims=True)
            pv = _dot(vt_ref[kb, j * HEAD_DIM:(j + 1) * HEAD_DIM, :], p.astype(BF16))
            new.append((m_new, l, acc * alpha + pv))
        return new

    tasks = [(qi, kb) for qi in range(t // blk) for kb in range((qi + 1) * blk // kbs)]
    fresh = [(jnp.full((1, blk), MASK_VALUE, F32), jnp.zeros((1, blk), F32),
              jnp.zeros((HEAD_DIM, blk), F32))] * 2
    q_aug = augmented_queries(0)
    scores(q_aug, 0, 0, 0)
    state = fresh
    for n, (qi, kb) in enumerate(tasks):
        if n + 1 < len(tasks):
            qi_next, kb_next = tasks[n + 1]
            if qi_next != qi:
                q_aug = augmented_queries(qi_next)
            scores(q_aug, qi_next, kb_next, (n + 1) % 2)
        state = accumulate(kb, n % 2, state)
        if n + 1 == len(tasks) or tasks[n + 1][0] != qi:
            outs = [acc * (1.0 / l) for _, l, acc in state]
            o_ref[qi * blk:(qi + 1) * blk, :] = jnp.concatenate(outs, axis=0).T.astype(BF16)
            state = fresh


def _fox(q, k, v, cum, batch, t):
    n_pairs = FOX_HEADS // 2
    pair_block = pl.BlockSpec((None, t, LANES), lambda b, p: (p, b, 0))
    return pl.pallas_call(
        _fox_kernel,
        grid=(batch, n_pairs),
        in_specs=[
            pair_block,
            pair_block,
            pair_block,
            pl.BlockSpec((t, LANES), lambda b, p: (b, 0)),
        ],
        out_specs=pair_block,
        out_shape=jax.ShapeDtypeStruct((n_pairs, batch * t, LANES), BF16),
        scratch_shapes=[
            pltpu.VMEM((t // FOX_KBLOCK, FOX_KBLOCK, 2 * LANES), BF16),
            pltpu.VMEM((t // FOX_KBLOCK, LANES, FOX_KBLOCK), BF16),
            pltpu.VMEM((2, 2, FOX_KBLOCK, FOX_BLOCK), F32),
        ],
        compiler_params=_params(2, 56),
        name="fox",
    )(q, k, v, cum)


def _chunk_attn_kernel(q_ref, k_ref, v_ref, tab_ref, o_ref, kp_ref, vt_ref, s_ref):
    t, g = q_ref.shape
    qb = CA_QBLOCK
    n_pad = CA_PAD // qb
    n_win = CA_WINDOW // qb

    for kb in range(n_pad):
        kp_ref[kb] = jnp.zeros((qb, g), BF16)
        vt_ref[kb] = jnp.zeros(vt_ref.shape[1:], BF16)
    for kb in range(t // qb):
        rows = slice(kb * qb, (kb + 1) * qb)
        kp_ref[n_pad + kb] = k_ref[rows, :]
        v_t = v_ref[rows, :].T
        for h in range(g // HEAD_DIM):
            vt_ref[n_pad + kb, h, 0:HEAD_DIM, :] = v_t[h * HEAD_DIM:(h + 1) * HEAD_DIM, :]
            vt_ref[n_pad + kb, h, HEAD_DIM:HEAD_DIM + ONES_ROWS, :] = jnp.ones((ONES_ROWS, qb), BF16)

    lane = lax.broadcasted_iota(jnp.int32, (qb, LANES), 1)
    n_heads = g // HEAD_DIM
    lax.fori_loop(0, t // qb, functools.partial(
        _chunk_attn_block, q_ref=q_ref, tab_ref=tab_ref, o_ref=o_ref, kp_ref=kp_ref, vt_ref=vt_ref,
        s_ref=s_ref, lane=lane, n_heads=n_heads, n_pad=n_pad, n_win=n_win), 0)


def _chunk_attn_block(i, carry, *, q_ref, tab_ref, o_ref, kp_ref, vt_ref, s_ref, lane, n_heads,
                      n_pad, n_win):
    qb = CA_QBLOCK
    q_rows = pl.ds(pl.multiple_of(i * qb, qb), qb)

    def scores(h):
        pair, j = divmod(h, 2)
        cols = slice(pair * LANES, (pair + 1) * LANES)
        q_pair = q_ref[q_rows, cols]
        q_h = jnp.where(_half_mask(lane, j), q_pair, jnp.zeros_like(q_pair))
        q_ht = q_h.T
        for w in range(n_win):
            tab_blk = jnp.where(i + w >= n_pad, w, n_win)
            s_ref[h % 2, w] = _dot(kp_ref[i + w, :, cols], q_ht) + tab_ref[h, tab_blk]

    def weighted_values(h):
        m = s_ref[h % 2, 0].max(axis=0, keepdims=True)
        for w in range(1, n_win):
            m = jnp.maximum(m, s_ref[h % 2, w].max(axis=0, keepdims=True))
        acc = jnp.zeros((HEAD_DIM + ONES_ROWS, qb), F32)
        for w in range(n_win):
            p = jnp.exp2((s_ref[h % 2, w] - m).astype(BF16))
            acc = acc + _dot(vt_ref[i + w, h], p)
        return acc[0:HEAD_DIM] * (1.0 / acc[HEAD_DIM:HEAD_DIM + 1])

    outs = []
    scores(0)
    for h in range(n_heads):
        if h + 1 < n_heads:
            scores(h + 1)
        outs.append(weighted_values(h))
    o_ref[q_rows, :] = jnp.concatenate(outs, axis=0).T.astype(BF16)
    return carry


def _chunk_attn(q, k, v, table, batch, t):
    g = table.shape[0] * HEAD_DIM
    n_blocks = (CA_PAD + t) // CA_QBLOCK
    return pl.pallas_call(
        _chunk_attn_kernel,
        grid=(batch,),
        in_specs=[
            pl.BlockSpec((t, g), lambda b: (b, 0)),
            pl.BlockSpec((t, g), lambda b: (b, 0)),
            pl.BlockSpec((t, g), lambda b: (b, 0)),
            _const_spec(table.shape),
        ],
        out_specs=pl.BlockSpec((t, g), lambda b: (b, 0)),
        out_shape=jax.ShapeDtypeStruct((batch * t, g), BF16),
        scratch_shapes=[
            pltpu.VMEM((n_blocks, CA_QBLOCK, g), BF16),
            pltpu.VMEM((n_blocks, g // HEAD_DIM, HEAD_DIM + ONES_ROWS, CA_QBLOCK), BF16),
            pltpu.VMEM((2, CA_WINDOW // CA_QBLOCK, CA_QBLOCK, CA_QBLOCK), F32),
        ],
        compiler_params=_params(1, 40),
        name="chunk_attn",
    )(q, k, v, table)


def _ca_table_kernel(line_ref, o_ref):
    n_win = CA_WINDOW // CA_QBLOCK
    rows = 8
    x = jnp.broadcast_to(line_ref[0] * LOG2E, (rows, CA_LINE))
    qc = lax.broadcasted_iota(jnp.int32, (rows, CA_QBLOCK), 1) // CHUNK
    for grp in range(CA_WINDOW // rows):
        r0 = grp * rows
        y = pltpu.roll(x, (CA_LINE - CA_WINDOW + 1 + r0) % CA_LINE, axis=1, stride=1, stride_axis=0)
        kc = r0 // CHUNK
        allowed = (qc <= kc) & (qc >= kc - CA_LEFT_CHUNKS)
        w, r = divmod(r0, CA_QBLOCK)
        o_ref[0, w, r:r + rows, :] = jnp.where(allowed, y[:, 0:CA_QBLOCK], MASK_VALUE)
    o_ref[0, n_win] = jnp.full((CA_QBLOCK, CA_QBLOCK), MASK_VALUE, F32)


def _chunk_attn_table(rel_bias):
    heads = rel_bias.shape[0]
    n_win = CA_WINDOW // CA_QBLOCK
    left = CA_WINDOW - 1 - CA_PAD - REL_CLIP
    line = jnp.pad(rel_bias.astype(F32), ((0, 0), (left, CA_LINE - left - rel_bias.shape[1])), mode="edge")
    return pl.pallas_call(
        _ca_table_kernel,
        grid=(heads,),
        in_specs=[pl.BlockSpec((1, 1, CA_LINE), lambda h: (h, 0, 0))],
        out_specs=pl.BlockSpec((1, n_win + 1, CA_QBLOCK, CA_QBLOCK), lambda h: (h, 0, 0, 0)),
        out_shape=jax.ShapeDtypeStruct((heads, n_win + 1, CA_QBLOCK, CA_QBLOCK), F32),
        compiler_params=_params(1, 8),
        name="ca_table",
    )(line.reshape(heads, 1, CA_LINE))


GELU_C0 = 0.7978845608028654
GELU_C1 = GELU_C0 * 0.044715


def _gelu(x):
    inner = x * (GELU_C0 + GELU_C1 * (x * x))
    return (0.5 * x) * (1.0 + jnp.tanh(inner))


def _lru_kernel(g_ref, x_ref, cw_ref, cb_ref, wa_ref, ba_ref, wx_ref, bx_ref, lam_ref, o_ref,
                xin_ref, xf_ref, a_ref, b_ref, hout_ref, h_ref):
    nb, frames, w = x_ref.shape
    n_slab = w // LANES
    sub = LRU_ROWS // nb
    i = pl.program_id(0)

    @pl.when(i == 0)
    def _():
        xf_ref[0:CONV_HIST] = jnp.zeros((CONV_HIST, nb, w), F32)
        h_ref[...] = jnp.zeros((nb, w), F32)

    for b in range(nb):
        xb = x_ref[b].astype(F32)
        for s in range(n_slab):
            xin_ref[s, b * LRU_PITCH:b * LRU_PITCH + frames, :] = xb[:, s * LANES:(s + 1) * LANES]

    def gather(t, carry):
        for s in range(n_slab):
            xf_ref[CONV_HIST + t, :, s * LANES:(s + 1) * LANES] = (
                xin_ref[s, pl.ds(t, nb, stride=LRU_PITCH), :])
        return carry

    lax.fori_loop(0, frames, gather, 0, unroll=SCAN_UNROLL)

    half_log2_base = (0.5 * LRU_C * LOG2E) * _log_sigmoid(lam_ref[...])
    half = w // 2
    for blk in range(frames // sub):
        t0 = blk * sub
        xc = cb_ref[...]
        for tap in range(CONV_WIDTH):
            lo = CONV_HIST + t0 - tap
            xc = xc + (cw_ref[CONV_WIDTH - 1 - tap:CONV_WIDTH - tap, :]
                       * xf_ref[lo:lo + sub].reshape(LRU_ROWS, w))
        xcb = xc.astype(BF16)
        gr, gi = [], []
        for hb in range(2):
            cols = slice(hb * half, (hb + 1) * half)
            gr.append(_dot(xcb[:, cols], wa_ref[hb]))
            gi.append(_dot(xcb[:, cols], wx_ref[hb]))
        t_r = jnp.tanh(jnp.concatenate(gr, axis=1) + ba_ref[...])
        gate_i = 0.5 * jnp.tanh(jnp.concatenate(gi, axis=1) + bx_ref[...]) + 0.5
        a = jnp.exp2(t_r * half_log2_base + half_log2_base)
        a_ref[t0:t0 + sub] = a.reshape(sub, nb, w)
        y = 1.0 - a * a
        root = jnp.where(y > 0.0, y * lax.rsqrt(y), 0.0)
        b_ref[t0:t0 + sub] = (root * (gate_i * xc)).reshape(sub, nb, w)

    xf_ref[0:CONV_HIST] = xf_ref[frames:frames + CONV_HIST]

    def scan(t, h):
        h = a_ref[t] * h + b_ref[t]
        for s in range(n_slab):
            hout_ref[s, pl.ds(t, nb, stride=LRU_PITCH), :] = h[:, s * LANES:(s + 1) * LANES]
        return h

    h_ref[...] = lax.fori_loop(0, frames, scan, h_ref[...], unroll=SCAN_UNROLL)

    for b in range(nb):
        rows = slice(b * LRU_PITCH, b * LRU_PITCH + frames)
        hv = jnp.concatenate([hout_ref[s, rows, :] for s in range(n_slab)], axis=1)
        o_ref[b] = (hv * _gelu(g_ref[b].astype(F32))).astype(BF16)


def _lru(gate3, x3, conv_w, conv_b, wa_bd, b_a, wx_bd, b_x, lam):
    nb, t, w = x3.shape
    n_slab = w // LANES
    return pl.pallas_call(
        _lru_kernel,
        grid=(t // LRU_FRAMES,),
        in_specs=[
            pl.BlockSpec((nb, LRU_FRAMES, w), lambda i: (0, i, 0)),
            pl.BlockSpec((nb, LRU_FRAMES, w), lambda i: (0, i, 0)),
            _const_spec((CONV_WIDTH, w)),
            _const_spec((1, w)),
            _const_spec(wa_bd.shape),
            _const_spec((1, w)),
            _const_spec(wx_bd.shape),
            _const_spec((1, w)),
            _const_spec((1, w)),
        ],
        out_specs=pl.BlockSpec((nb, LRU_FRAMES, w), lambda i: (0, i, 0)),
        out_shape=jax.ShapeDtypeStruct((nb, t, w), BF16),
        scratch_shapes=[
            pltpu.VMEM((n_slab, nb * LRU_PITCH, LANES), F32),
            pltpu.VMEM((CONV_HIST + LRU_FRAMES, nb, w), F32),
            pltpu.VMEM((LRU_FRAMES, nb, w), F32),
            pltpu.VMEM((LRU_FRAMES, nb, w), F32),
            pltpu.VMEM((n_slab, nb * LRU_PITCH, LANES), F32),
            pltpu.VMEM((nb, w), F32),
        ],
        compiler_params=_params(1, 42),
        name="rglru",
    )(gate3, x3, conv_w, conv_b.reshape(1, w), wa_bd, b_a.reshape(1, w), wx_bd,
      b_x.reshape(1, w), lam.reshape(1, w))


def _block_diag_halves(wblk):
    nb, d, _ = wblk.shape
    per = nb // 2
    eye = jnp.eye(per, dtype=wblk.dtype)
    halves = [jnp.einsum("nde,nm->ndme", wblk[h * per:(h + 1) * per], eye).reshape(per * d, per * d)
              for h in range(2)]
    return jnp.stack(halves).astype(BF16)


def kernel(x, norm_w, w_in_even, gla_w_a_up, gla_b_a, gla_norm_w, fox_b_f, w_out_even,
           w_in_odd, rel_bias, conv_w, conv_b, lru_w_a, lru_b_a, lru_w_x, lru_b_x,
           lru_lambda, w_out_odd, w_mlp_up, w_mlp_down):
    batch, t, d = x.shape
    x2d = x.reshape(batch * t, d)
    group = d // 2
    kw = GLA_HEADS * GLA_DK

    w_in = w_in_even[0]
    o_ga = 2 * kw + 2 * group
    o_fq = o_ga + GLA_RANK
    o_ff = o_fq + 3 * group
    q_scale = LOG2E * HEAD_DIM ** -0.5
    segments = [(0, o_ga, 0, 1.0), (o_fq, group, o_ga, q_scale), (o_fq + group, 2 * group, o_ga + group, 1.0)]
    n_small = FOX_HEADS + GLA_RANK
    small_segments = [(o_ff, FOX_HEADS, 0, 1.0), (o_ga, GLA_RANK, FOX_HEADS, 1.0)]
    wa_pad = jnp.zeros((LANES, kw), F32).at[FOX_HEADS:n_small].set(gla_w_a_up[0]).astype(BF16)
    fox_bias = jnp.zeros((1, LANES), F32).at[0, :FOX_HEADS].set(fox_b_f[0])

    m = batch * t
    splits = [(0, kw, False), (kw, kw, False), (2 * kw, group, False), (2 * kw + group, group, False),
              (o_ga, group, True), (o_ga + group, group, True), (o_ga + 2 * group, group, True)]
    g_q, g_k, g_v, g_r, f_q, f_k, f_v, small = _norm_proj(
        x2d, norm_w[0, 0], w_in, segments, splits, small_segments)
    out_a = _gla(g_q, g_k, g_v, g_r, small, wa_pad, gla_b_a[0], gla_norm_w[0], batch, t)
    cum = _fox_cum(small, fox_bias, batch, t)
    out_b = _fox(f_q, f_k, f_v, cum, batch, t)
    x2d = _mix_mlp(out_a, out_b, x2d, w_out_even[0], norm_w[0, 1:4], w_mlp_up, w_mlp_down, 0)

    w_in = w_in_odd[0]
    segments = [(0, group, 0, q_scale), (group, 4 * group, group, 1.0)]
    splits = [(c * group, group, False) for c in range(5)]
    c_q, c_k, c_v, d_gate, d_in = _norm_proj(x2d, norm_w[1, 0], w_in, segments, splits)
    out_c = _chunk_attn(c_q, c_k, c_v, _chunk_attn_table(rel_bias[0]), batch, t)
    out_d = _lru(d_gate.reshape(batch, t, group), d_in.reshape(batch, t, group), conv_w[0], conv_b[0],
                 _block_diag_halves(0.5 * lru_w_a[0]), 0.5 * lru_b_a[0],
                 _block_diag_halves(0.5 * lru_w_x[0]), 0.5 * lru_b_x[0], lru_lambda[0])
    x2d = _mix_mlp(out_c, out_d.reshape(m, group), x2d, w_out_odd[0], norm_w[1, 1:4],
                   w_mlp_up, w_mlp_down, 1)
    return x2d.reshape(batch, t, d)
```

```python
import functools

import jax
import jax.numpy as jnp
from jax import lax
from jax.experimental import pallas as pl
from jax.experimental.pallas import tpu as pltpu

F32 = jnp.float32
BF16 = jnp.bfloat16

NORM_EPS = 1e-6
CHUNK = 64
GLA_HEADS = 4
GLA_DK = 64
GLA_DV = 128
GLA_RANK = 16
GLA_GATE_TAU = 16.0
FOX_HEADS = 8
HEAD_DIM = 64
CA_LEFT_CHUNKS = 8
REL_CLIP = 128
CONV_WIDTH = 4
LRU_C = 8.0

LANES = 128
MXU_DIM = 256
MASK_VALUE = -1e30
VMEM_LIMIT_BYTES = 58 * 1024 * 1024

ROW_TILE = 1024
PROJ_ROW_TILE = 1024
STAGE_ROWS = 512
STAGE_COLS = 1024
STAGE_W_ROWS = 128
FOX_BLOCK = 512
FOX_KBLOCK = 512
FOX_PAIRS_PER_STEP = 2
ONES_ROWS = 16
CA_QBLOCK = 4 * CHUNK
CA_WINDOW = CA_QBLOCK + CA_LEFT_CHUNKS * CHUNK
CA_PAD = CA_LEFT_CHUNKS * CHUNK
CA_LINE = 1024
LOG2E = 1.4426950408889634
LN2 = 0.6931471805599453
CUM_TERMS = 3
GLA_GROUP = 4
LRU_ROWS = 256
LRU_FRAMES = 256
LRU_PITCH = LRU_FRAMES + 8
CONV_HIST = 8
SCAN_UNROLL = 8


def _params(n_axes, vmem_mib):
    assert vmem_mib * 1024 * 1024 <= VMEM_LIMIT_BYTES
    return pltpu.CompilerParams(
        dimension_semantics=("arbitrary",) * n_axes,
        vmem_limit_bytes=vmem_mib * 1024 * 1024,
    )


def _const_spec(shape):
    nd = len(shape)
    return pl.BlockSpec(shape, lambda *_: (0,) * nd, pipeline_mode=pl.Buffered(1))


def _rmsnorm(x, w):
    y = x * lax.rsqrt(jnp.mean(x * x, axis=-1, keepdims=True) + NORM_EPS)
    return y * w


def _log_sigmoid(z):
    return jnp.minimum(z, 0.0) - LN2 * jnp.log2(1.0 + jnp.exp2(-LOG2E * jnp.abs(z)))


def _sigmoid(z):
    return 0.5 * jnp.tanh(0.5 * z) + 0.5


def _dot(a, b):
    return jnp.dot(a, b, preferred_element_type=F32)


def _dot_tn(a, b):
    return lax.dot_general(a, b, (((0,), (0,)), ((), ())), preferred_element_type=F32)


def _half_mask(lane, j):
    return lane < HEAD_DIM if j == 0 else lane >= HEAD_DIM


def _split3(x):
    h1 = x.astype(BF16)
    r1 = x - h1.astype(F32)
    h2 = r1.astype(BF16)
    h3 = (r1 - h2.astype(F32)).astype(BF16)
    return h1, h2, h3


def _row_major(i):
    return (i, 0)


def _stage_projection_weights(w_hbm, w_ref, ws_ref, stage_ref, sem_ref, segments, small_segments):
    d = w_ref.shape[0]
    pieces = d // STAGE_W_ROWS

    def copy(k):
        return pltpu.make_async_copy(w_hbm.at[0, pl.ds(k * STAGE_W_ROWS, STAGE_W_ROWS), :],
                                     stage_ref.at[k % 2], sem_ref.at[k % 2])

    def segment(k, src, width, scale):
        a0 = (src // LANES) * LANES
        a1 = min(-(-(src + width) // LANES) * LANES, stage_ref.shape[2])
        seg = stage_ref[k % 2, :, a0:a1][:, src - a0:src - a0 + width]
        return (seg * scale if scale != 1.0 else seg).astype(BF16)

    copy(0).start()
    for k in range(pieces):
        if k + 1 < pieces:
            copy(k + 1).start()
        copy(k).wait()
        rows = slice(k * STAGE_W_ROWS, (k + 1) * STAGE_W_ROWS)
        for src, width, dst, scale in segments:
            w_ref[rows, dst:dst + width] = segment(k, src, width, scale)
        if ws_ref is not None:
            ws_ref[rows, :] = jnp.zeros((STAGE_W_ROWS, ws_ref.shape[1]), BF16)
            for src, width, dst, scale in small_segments:
                ws_ref[rows, dst:dst + width] = segment(k, src, width, scale)


def _norm_proj_kernel(x_ref, nw_ref, w_hbm, *refs, splits, segments, small_segments, n_chunk):
    with_small = small_segments is not None
    n_out = len(splits) + (1 if with_small else 0)
    o_refs = refs[:len(splits)]
    w_ref = refs[n_out]
    ws_ref = refs[n_out + 1] if with_small else None
    stage_ref, sem_ref = refs[-2:]

    @pl.when(pl.program_id(0) == 0)
    def _():
        _stage_projection_weights(w_hbm, w_ref, ws_ref, stage_ref, sem_ref, segments, small_segments)

    tm = x_ref.shape[0]
    for rows in (slice(0, tm // 2), slice(tm // 2, tm)):
        h = _rmsnorm(x_ref[rows, :], nw_ref[...]).astype(BF16)
        for (c0, width, by_pair), o_ref in zip(splits, o_refs):
            for n0 in range(0, width, n_chunk):
                n1 = min(n0 + n_chunk, width)
                res = _dot(h, w_ref[:, c0 + n0:c0 + n1]).astype(o_ref.dtype)
                if by_pair:
                    for p in range(n0 // LANES, n1 // LANES):
                        o_ref[p, rows, :] = res[:, p * LANES - n0:(p + 1) * LANES - n0]
                else:
                    o_ref[rows, n0:n1] = res
        if with_small:
            refs[len(splits)][rows, :] = _dot(h, ws_ref[...])


def _norm_proj(x2d, nw, w_raw, segments, splits, small_segments=None):
    m, d = x2d.shape
    n_main = max(dst + width for _, width, dst, _ in segments)
    in_specs = [pl.BlockSpec((PROJ_ROW_TILE, d), _row_major), _const_spec((1, d)),
                pl.BlockSpec(memory_space=pl.ANY)]
    args = [x2d, nw.reshape(1, d), w_raw]
    scratch = [pltpu.VMEM((d, n_main), BF16)]
    out_specs, out_shape = [], []
    for _, width, by_pair in splits:
        if by_pair:
            out_specs.append(pl.BlockSpec((width // LANES, PROJ_ROW_TILE, LANES), lambda i: (0, i, 0)))
            out_shape.append(jax.ShapeDtypeStruct((width // LANES, m, LANES), BF16))
        else:
            out_specs.append(pl.BlockSpec((PROJ_ROW_TILE, width), _row_major))
            out_shape.append(jax.ShapeDtypeStruct((m, width), BF16))
    if small_segments is not None:
        out_specs.append(pl.BlockSpec((PROJ_ROW_TILE, LANES), _row_major))
        out_shape.append(jax.ShapeDtypeStruct((m, LANES), F32))
        scratch.append(pltpu.VMEM((d, LANES), BF16))
    scratch += [pltpu.VMEM((2, STAGE_W_ROWS, w_raw.shape[2]), F32), pltpu.SemaphoreType.DMA((2,))]
    return pl.pallas_call(
        functools.partial(_norm_proj_kernel, splits=tuple(splits), segments=tuple(segments),
                          small_segments=None if small_segments is None else tuple(small_segments),
                          n_chunk=2 * MXU_DIM),
        grid=(m // PROJ_ROW_TILE,),
        in_specs=in_specs,
        out_specs=out_specs,
        out_shape=out_shape,
        scratch_shapes=scratch,
        compiler_params=_params(1, 58),
        name="norm_proj",
    )(*args)


def _stage_bf16_weights(pairs, stage_ref, sem_ref):
    pieces = []
    for src, dst in pairs:
        rows, cols = dst.shape
        for r0 in range(0, rows, STAGE_ROWS):
            for c0 in range(0, cols, STAGE_COLS):
                pieces.append((src, dst, r0, c0))

    def copy(k):
        src, _, r0, c0 = pieces[k]
        return pltpu.make_async_copy(
            src.at[pl.ds(r0, STAGE_ROWS), pl.ds(c0, STAGE_COLS)], stage_ref.at[k % 2], sem_ref.at[k % 2])

    copy(0).start()
    for k, (_, dst, r0, c0) in enumerate(pieces):
        if k + 1 < len(pieces):
            copy(k + 1).start()
        copy(k).wait()
        dst[r0:r0 + STAGE_ROWS, c0:c0 + STAGE_COLS] = stage_ref[k % 2].astype(BF16)


def _mix_mlp_kernel(a_ref, b_ref, x_ref, nw_ref, wo_hbm, wu_hbm, wd_hbm, o_ref,
                    wo_ref, wu_ref, wd_ref, u_ref, y_ref, stage_ref, sem_ref, *, chunk, layer):
    @pl.when(pl.program_id(0) == 0)
    def _():
        _stage_bf16_weights([(wo_hbm, wo_ref), (wu_hbm.at[layer], wu_ref), (wd_hbm.at[layer], wd_ref)],
                            stage_ref, sem_ref)

    d_ff = wu_ref.shape[1]
    d = wd_ref.shape[1]
    tm = x_ref.shape[0]
    halves = [slice(0, tm // 2), slice(tm // 2, tm)]

    def out_proj(rows):
        if len(b_ref.shape) == 3:
            b = jnp.concatenate([b_ref[p, rows, :] for p in range(b_ref.shape[0])], axis=1)
        else:
            b = b_ref[rows, :]
        mix = jnp.concatenate([a_ref[rows, :], b], axis=1)
        for n0 in range(0, d, chunk):
            y_ref[rows, n0:n0 + chunk] = _dot(mix, wo_ref[:, n0:n0 + chunk])

    def residual_and_norm(rows):
        x1 = x_ref[rows, :] + _rmsnorm(y_ref[rows, :], nw_ref[0:1, :])
        o_ref[rows, :] = x1
        return _rmsnorm(x1, nw_ref[1:2, :]).astype(BF16)

    def up_proj(rows, h):
        for f0 in range(0, d_ff, chunk):
            u = jnp.maximum(_dot(h, wu_ref[:, f0:f0 + chunk]), 0.0)
            u_ref[rows, f0:f0 + chunk] = (u * u).astype(BF16)

    def down_proj(rows):
        for n0 in range(0, d, chunk):
            y_ref[rows, n0:n0 + chunk] = _dot(u_ref[rows, :], wd_ref[:, n0:n0 + chunk])

    def finish(rows):
        o_ref[rows, :] = o_ref[rows, :] + _rmsnorm(y_ref[rows, :], nw_ref[2:3, :])

    first, second = halves
    out_proj(first)
    out_proj(second)
    h_first = residual_and_norm(first)
    up_proj(first, h_first)
    h_second = residual_and_norm(second)
    up_proj(second, h_second)
    down_proj(first)
    down_proj(second)
    finish(first)
    finish(second)


def _mix_mlp(mix_a, mix_b, x2d, w_out, nw3, w_up_all, w_down_all, layer):
    m, d = x2d.shape
    g = mix_a.shape[1]
    d_ff = w_up_all.shape[2]
    return pl.pallas_call(
        functools.partial(_mix_mlp_kernel, chunk=2 * MXU_DIM, layer=layer),
        grid=(m // ROW_TILE,),
        in_specs=[
            pl.BlockSpec((ROW_TILE, g), _row_major),
            (pl.BlockSpec((mix_b.shape[0], ROW_TILE, LANES), lambda i: (0, i, 0)) if mix_b.ndim == 3
             else pl.BlockSpec((ROW_TILE, g), _row_major)),
            pl.BlockSpec((ROW_TILE, d), _row_major),
            _const_spec((3, d)),
            pl.BlockSpec(memory_space=pl.ANY),
            pl.BlockSpec(memory_space=pl.ANY),
            pl.BlockSpec(memory_space=pl.ANY),
        ],
        out_specs=pl.BlockSpec((ROW_TILE, d), _row_major),
        out_shape=jax.ShapeDtypeStruct((m, d), F32),
        scratch_shapes=[
            pltpu.VMEM((2 * g, d), BF16),
            pltpu.VMEM((d, d_ff), BF16),
            pltpu.VMEM((d_ff, d), BF16),
            pltpu.VMEM((ROW_TILE, d_ff), BF16),
            pltpu.VMEM((ROW_TILE, d), F32),
            pltpu.VMEM((2, STAGE_ROWS, STAGE_COLS), F32),
            pltpu.SemaphoreType.DMA((2,)),
        ],
        compiler_params=_params(1, 56),
        name="mix_mlp",
    )(mix_a, mix_b, x2d, nw3, w_out, w_up_all, w_down_all)


def _cumulative_log_gate(sm_ref, bias_ref, o_ref):
    t = sm_ref.shape[0]
    row = lax.broadcasted_iota(jnp.int32, (LANES, LANES), 0)
    col = lax.broadcasted_iota(jnp.int32, (LANES, LANES), 1)
    tri = jnp.where(row >= col, 1.0, 0.0).astype(BF16)
    carry = jnp.zeros((1, LANES), F32)
    for blk in range(t // LANES):
        rows = slice(blk * LANES, (blk + 1) * LANES)
        ls = _log_sigmoid(sm_ref[rows, :] + bias_ref[...])
        h1, h2, h3 = _split3(ls)
        cb = _dot(tri, h1) + _dot(tri, h2) + _dot(tri, h3) + carry
        o_ref[rows, :] = cb
        carry = cb[LANES - 1:LANES, :]


def _gla_kernel(q_ref, k_ref, v_ref, r_ref, sm_ref, wa_ref, ba_ref, nw_ref, fb_ref, o_ref, cum_ref,
                la_ref, st_ref, dec_ref):
    t = q_ref.shape[0]
    nc = t // CHUNK
    n_pairs = GLA_HEADS // 2

    z = _dot(sm_ref[...].astype(BF16), wa_ref[...]) + ba_ref[...]
    la_ref[...] = _log_sigmoid(z) * (1.0 / GLA_GATE_TAU)
    _cumulative_log_gate(sm_ref, fb_ref, cum_ref)

    grp = GLA_GROUP * CHUNK
    row = lax.broadcasted_iota(jnp.int32, (grp, grp), 0)
    col = lax.broadcasted_iota(jnp.int32, (grp, grp), 1)
    tri = jnp.where((row >= col) & (row // CHUNK == col // CHUNK), 1.0, 0.0).astype(BF16)
    erow = lax.broadcasted_iota(jnp.int32, (grp, GLA_GROUP * LANES), 0)
    ecol = lax.broadcasted_iota(jnp.int32, (grp, GLA_GROUP * LANES), 1)
    chunk_ones = jnp.where(erow // CHUNK == ecol // LANES, 1.0, 0.0).astype(BF16)

    def increments(g, carry):
        r0 = pl.multiple_of(g * grp, grp)
        la = la_ref[pl.ds(r0, grp), :]
        hi = la.astype(BF16)
        lo = (la - hi.astype(F32)).astype(BF16)
        cum = _dot(tri, hi) + _dot(tri, lo)
        dec = jnp.exp(_dot_tn(hi, chunk_ones) + _dot_tn(lo, chunk_ones))
        k_g = k_ref[pl.ds(r0, grp), :].astype(F32)
        v_g = v_ref[pl.ds(r0, grp), :]
        k_dec = []
        for c in range(GLA_GROUP):
            rows = slice(c * CHUNK, (c + 1) * CHUNK)
            total = cum[(c + 1) * CHUNK - 1:(c + 1) * CHUNK, :]
            k_dec.append((k_g[rows] * jnp.exp(total - cum[rows])).astype(BF16))
        for c in range(GLA_GROUP):
            rows = slice(c * CHUNK, (c + 1) * CHUNK)
            dec_ref[g * GLA_GROUP + c] = dec[:, c * LANES:(c + 1) * LANES]
            for h in range(GLA_HEADS):
                p, j = divmod(h, 2)
                full = _dot_tn(k_dec[c][:, p * LANES:(p + 1) * LANES], v_g[rows, h * GLA_DV:(h + 1) * GLA_DV])
                st_ref[g * GLA_GROUP + c, p, j * GLA_DK:(j + 1) * GLA_DK, :] = full[j * GLA_DK:(j + 1) * GLA_DK, :]
        return carry

    lax.fori_loop(0, nc // GLA_GROUP, increments, 0)

    def scan(c, carry):
        for p in range(n_pairs):
            st_ref[c, p] = st_ref[c, p] + dec_ref[c, p * LANES:(p + 1) * LANES, :] * st_ref[c - 1, p]
        return carry

    lax.fori_loop(1, nc, scan, 0)

    lane = lax.broadcasted_iota(jnp.int32, (CHUNK, LANES), 1)
    scale = GLA_DK ** -0.5

    def outputs(g, carry):
        r0 = pl.multiple_of(g * grp, grp)
        q_g = q_ref[pl.ds(r0, grp), :] * jnp.asarray(scale, BF16)
        raw = []
        for c in range(GLA_GROUP):
            rows = slice(c * CHUNK, (c + 1) * CHUNK)
            for h in range(GLA_HEADS):
                p, j = divmod(h, 2)
                q_pair = q_g[rows, p * LANES:(p + 1) * LANES]
                q_h = jnp.where(_half_mask(lane, j), q_pair, jnp.zeros_like(q_pair))
                raw.append(_dot(q_h, st_ref[g * GLA_GROUP + c, p].astype(BF16)))
        for c in range(GLA_GROUP):
            for h in range(GLA_HEADS):
                o = raw[c * GLA_HEADS + h]
                o = o * lax.rsqrt(jnp.mean(o * o, axis=-1, keepdims=True) + NORM_EPS)
                o = o * nw_ref[:, h * GLA_DV:(h + 1) * GLA_DV]
                rows = pl.ds(r0 + c * CHUNK, CHUNK)
                r_h = r_ref[rows, h * GLA_DV:(h + 1) * GLA_DV].astype(F32)
                o_ref[rows, h * GLA_DV:(h + 1) * GLA_DV] = (o * (r_h * _sigmoid(r_h))).astype(BF16)
        return carry

    lax.fori_loop(0, nc // GLA_GROUP, outputs, 0)


def _gla(q, k, v, r, small, wa_pad, b_a, norm_w, fox_bias, batch, t):
    kw = GLA_HEADS * GLA_DK
    gw = GLA_HEADS * GLA_DV
    nc = t // CHUNK
    return pl.pallas_call(
        _gla_kernel,
        grid=(batch,),
        in_specs=[
            pl.BlockSpec((t, kw), lambda b: (b, 0)),
            pl.BlockSpec((t, kw), lambda b: (b, 0)),
            pl.BlockSpec((t, gw), lambda b: (b, 0)),
            pl.BlockSpec((t, gw), lambda b: (b, 0)),
            pl.BlockSpec((t, LANES), lambda b: (b, 0)),
            _const_spec((LANES, kw)),
            _const_spec((1, kw)),
            _const_spec((1, gw)),
            _const_spec((1, LANES)),
        ],
        out_specs=[pl.BlockSpec((t, gw), lambda b: (b, 0)), pl.BlockSpec((t, LANES), lambda b: (b, 0))],
        out_shape=[jax.ShapeDtypeStruct((batch * t, gw), BF16),
                   jax.ShapeDtypeStruct((batch * t, LANES), F32)],
        scratch_shapes=[
            pltpu.VMEM((t, kw), F32),
            pltpu.VMEM((nc, GLA_HEADS // 2, LANES, GLA_DV), F32),
            pltpu.VMEM((nc, kw, LANES), F32),
        ],
        compiler_params=_params(1, 56),
        name="gla",
    )(q, k, v, r, small, wa_pad, b_a.reshape(1, kw), norm_w.reshape(1, gw), fox_bias)


def _fox_kernel(q_ref, k_ref, v_ref, c_ref, o_ref, ka_ref, vt_ref, s_ref):
    t = k_ref.shape[1]
    blk = FOX_BLOCK
    kbs = FOX_KBLOCK
    n_pp = q_ref.shape[0]
    n_heads = 2 * n_pp

    sel_r = lax.broadcasted_iota(jnp.int32, (LANES, LANES), 0)
    sel_c = lax.broadcasted_iota(jnp.int32, (LANES, LANES), 1)
    for pp in range(n_pp):
        pair = n_pp * pl.program_id(1) + pp
        sels = []
        for term in range(CUM_TERMS):
            hit = (((sel_r == 2 * pair) & (sel_c == term))
                   | ((sel_r == 2 * pair + 1) & (sel_c == CUM_TERMS + term)))
            sels.append(jnp.where(hit, 1.0, 0.0).astype(BF16))
        for kb in range(t // kbs):
            rows = slice(kb * kbs, (kb + 1) * kbs)
            vt_ref[pp, kb] = v_ref[pp, rows, :].astype(F32).T.astype(BF16)
            parts = _split3(c_ref[rows, :] * (-LOG2E))
            extra = _dot(parts[0], sels[0]) + _dot(parts[1], sels[1]) + _dot(parts[2], sels[2])
            ka_ref[pp, kb, :, 0:LANES] = k_ref[pp, rows, :]
            ka_ref[pp, kb, :, LANES:2 * LANES] = extra.astype(BF16)

    lane = lax.broadcasted_iota(jnp.int32, (blk, LANES), 1)
    krow = lax.broadcasted_iota(jnp.int32, (kbs, blk), 0)
    qcol = lax.broadcasted_iota(jnp.int32, (kbs, blk), 1)
    ahead = qcol - krow

    def augmented_queries(qi):
        q_aug = []
        for g in range(n_heads):
            pp, j = divmod(g, 2)
            q = q_ref[pp, qi * blk:(qi + 1) * blk, :]
            q_h = jnp.where(_half_mask(lane, j), q, jnp.zeros_like(q))
            ones = jnp.where((lane >= CUM_TERMS * j) & (lane < CUM_TERMS * (j + 1)), 1.0, 0.0).astype(BF16)
            q_aug.append(jnp.concatenate([q_h, ones], axis=1).astype(F32).T.astype(BF16))
        return q_aug

    def scores(q_aug, qi, kb, slot):
        offset = kb * kbs - qi * blk
        for g in range(n_heads):
            s = _dot(ka_ref[g // 2, kb], q_aug[g])
            if offset + kbs - 1 > 0:
                s = jnp.where(ahead >= offset, s, MASK_VALUE)
            s_ref[slot, g] = s

    def accumulate(kb, slot, state):
        new = []
        for g in range(n_heads):
            pp, j = divmod(g, 2)
            m, l, acc = state[g]
            m_new = jnp.maximum(m, jnp.max(s_ref[slot, g], axis=0, keepdims=True))
            alpha = jnp.exp2(m - m_new)
            p = jnp.exp2(s_ref[slot, g] - m_new)
            l = l * alpha + jnp.sum(p, axis=0, keepdims=True)
            pv = _dot(vt_ref[pp, kb, j * HEAD_DIM:(j + 1) * HEAD_DIM, :], p.astype(BF16))
            new.append((m_new, l, acc * alpha + pv))
        return new

    tasks = [(qi, kb) for qi in range(t // blk) for kb in range((qi + 1) * blk // kbs)]
    fresh = [(jnp.full((1, blk), MASK_VALUE, F32), jnp.zeros((1, blk), F32),
              jnp.zeros((HEAD_DIM, blk), F32))] * n_heads
    q_aug = augmented_queries(0)
    scores(q_aug, 0, 0, 0)
    state = fresh
    for n, (qi, kb) in enumerate(tasks):
        if n + 1 < len(tasks):
            qi_next, kb_next = tasks[n + 1]
            if qi_next != qi:
                q_aug = augmented_queries(qi_next)
            scores(q_aug, qi_next, kb_next, (n + 1) % 2)
        state = accumulate(kb, n % 2, state)
        if n + 1 == len(tasks) or tasks[n + 1][0] != qi:
            outs = [acc * (1.0 / l) for _, l, acc in state]
            for pp in range(n_pp):
                o_ref[pp, qi * blk:(qi + 1) * blk, :] = (
                    jnp.concatenate(outs[2 * pp:2 * pp + 2], axis=0).T.astype(BF16))
            state = fresh


def _fox(q, k, v, cum, batch, t):
    n_pairs = FOX_HEADS // 2
    pair_block = pl.BlockSpec((FOX_PAIRS_PER_STEP, t, LANES), lambda b, p: (p, b, 0))
    return pl.pallas_call(
        _fox_kernel,
        grid=(batch, n_pairs // FOX_PAIRS_PER_STEP),
        in_specs=[
            pair_block,
            pair_block,
            pair_block,
            pl.BlockSpec((t, LANES), lambda b, p: (b, 0)),
        ],
        out_specs=pair_block,
        out_shape=jax.ShapeDtypeStruct((n_pairs, batch * t, LANES), BF16),
        scratch_shapes=[
            pltpu.VMEM((FOX_PAIRS_PER_STEP, t // FOX_KBLOCK, FOX_KBLOCK, 2 * LANES), BF16),
            pltpu.VMEM((FOX_PAIRS_PER_STEP, t // FOX_KBLOCK, LANES, FOX_KBLOCK), BF16),
            pltpu.VMEM((2, 2 * FOX_PAIRS_PER_STEP, FOX_KBLOCK, FOX_BLOCK), F32),
        ],
        compiler_params=_params(2, 56),
        name="fox",
    )(q, k, v, cum)


def _chunk_attn_kernel(q_ref, k_ref, v_ref, tab_ref, o_ref, kp_ref, vt_ref, s_ref):
    t, g = q_ref.shape
    qb = CA_QBLOCK
    n_pad = CA_PAD // qb
    n_win = CA_WINDOW // qb

    for kb in range(n_pad):
        kp_ref[kb] = jnp.zeros((qb, g), BF16)
        vt_ref[kb] = jnp.zeros(vt_ref.shape[1:], BF16)
    for kb in range(t // qb):
        rows = slice(kb * qb, (kb + 1) * qb)
        kp_ref[n_pad + kb] = k_ref[rows, :]
        v_t = v_ref[rows, :].T
        for h in range(g // HEAD_DIM):
            vt_ref[n_pad + kb, h, 0:HEAD_DIM, :] = v_t[h * HEAD_DIM:(h + 1) * HEAD_DIM, :]
            vt_ref[n_pad + kb, h, HEAD_DIM:HEAD_DIM + ONES_ROWS, :] = jnp.ones((ONES_ROWS, qb), BF16)

    lane = lax.broadcasted_iota(jnp.int32, (qb, LANES), 1)
    n_heads = g // HEAD_DIM
    lax.fori_loop(0, t // qb, functools.partial(
        _chunk_attn_block, q_ref=q_ref, tab_ref=tab_ref, o_ref=o_ref, kp_ref=kp_ref, vt_ref=vt_ref,
        s_ref=s_ref, lane=lane, n_heads=n_heads, n_pad=n_pad, n_win=n_win), 0)


def _chunk_attn_block(i, carry, *, q_ref, tab_ref, o_ref, kp_ref, vt_ref, s_ref, lane, n_heads,
                      n_pad, n_win):
    qb = CA_QBLOCK
    q_rows = pl.ds(pl.multiple_of(i * qb, qb), qb)

    def scores(h):
        pair, j = divmod(h, 2)
        cols = slice(pair * LANES, (pair + 1) * LANES)
        q_pair = q_ref[q_rows, cols]
        q_h = jnp.where(_half_mask(lane, j), q_pair, jnp.zeros_like(q_pair))
        q_ht = q_h.T
        for w in range(n_win):
            tab_blk = jnp.where(i + w >= n_pad, w, n_win)
            s_ref[h % 2, w] = _dot(kp_ref[i + w, :, cols], q_ht) + tab_ref[h, tab_blk]

    def weighted_values(h):
        m = s_ref[h % 2, 0].max(axis=0, keepdims=True)
        for w in range(1, n_win):
            m = jnp.maximum(m, s_ref[h % 2, w].max(axis=0, keepdims=True))
        acc = jnp.zeros((HEAD_DIM + ONES_ROWS, qb), F32)
        for w in range(n_win):
            p = jnp.exp2((s_ref[h % 2, w] - m).astype(BF16))
            acc = acc + _dot(vt_ref[i + w, h], p)
        return acc[0:HEAD_DIM] * (1.0 / acc[HEAD_DIM:HEAD_DIM + 1])

    outs = []
    scores(0)
    for h in range(n_heads):
        if h + 1 < n_heads:
            scores(h + 1)
        outs.append(weighted_values(h))
    o_ref[q_rows, :] = jnp.concatenate(outs, axis=0).T.astype(BF16)
    return carry


def _chunk_attn(q, k, v, table, batch, t):
    g = table.shape[0] * HEAD_DIM
    n_blocks = (CA_PAD + t) // CA_QBLOCK
    return pl.pallas_call(
        _chunk_attn_kernel,
        grid=(batch,),
        in_specs=[
            pl.BlockSpec((t, g), lambda b: (b, 0)),
            pl.BlockSpec((t, g), lambda b: (b, 0)),
            pl.BlockSpec((t, g), lambda b: (b, 0)),
            _const_spec(table.shape),
        ],
        out_specs=pl.BlockSpec((t, g), lambda b: (b, 0)),
        out_shape=jax.ShapeDtypeStruct((batch * t, g), BF16),
        scratch_shapes=[
            pltpu.VMEM((n_blocks, CA_QBLOCK, g), BF16),
            pltpu.VMEM((n_blocks, g // HEAD_DIM, HEAD_DIM + ONES_ROWS, CA_QBLOCK), BF16),
            pltpu.VMEM((2, CA_WINDOW // CA_QBLOCK, CA_QBLOCK, CA_QBLOCK), F32),
        ],
        compiler_params=_params(1, 40),
        name="chunk_attn",
    )(q, k, v, table)


def _ca_table_kernel(line_ref, o_ref):
    n_win = CA_WINDOW // CA_QBLOCK
    rows = 8
    x = jnp.broadcast_to(line_ref[0] * LOG2E, (rows, CA_LINE))
    qc = lax.broadcasted_iota(jnp.int32, (rows, CA_QBLOCK), 1) // CHUNK
    for grp in range(CA_WINDOW // rows):
        r0 = grp * rows
        y = pltpu.roll(x, (CA_LINE - CA_WINDOW + 1 + r0) % CA_LINE, axis=1, stride=1, stride_axis=0)
        kc = r0 // CHUNK
        allowed = (qc <= kc) & (qc >= kc - CA_LEFT_CHUNKS)
        w, r = divmod(r0, CA_QBLOCK)
        o_ref[0, w, r:r + rows, :] = jnp.where(allowed, y[:, 0:CA_QBLOCK], MASK_VALUE)
    o_ref[0, n_win] = jnp.full((CA_QBLOCK, CA_QBLOCK), MASK_VALUE, F32)


def _chunk_attn_table(rel_bias):
    heads = rel_bias.shape[0]
    n_win = CA_WINDOW // CA_QBLOCK
    left = CA_WINDOW - 1 - CA_PAD - REL_CLIP
    line = jnp.pad(rel_bias.astype(F32), ((0, 0), (left, CA_LINE - left - rel_bias.shape[1])), mode="edge")
    return pl.pallas_call(
        _ca_table_kernel,
        grid=(heads,),
        in_specs=[pl.BlockSpec((1, 1, CA_LINE), lambda h: (h, 0, 0))],
        out_specs=pl.BlockSpec((1, n_win + 1, CA_QBLOCK, CA_QBLOCK), lambda h: (h, 0, 0, 0)),
        out_shape=jax.ShapeDtypeStruct((heads, n_win + 1, CA_QBLOCK, CA_QBLOCK), F32),
        compiler_params=_params(1, 8),
        name="ca_table",
    )(line.reshape(heads, 1, CA_LINE))


GELU_C0 = 0.7978845608028654
GELU_C1 = GELU_C0 * 0.044715


def _gelu(x):
    inner = x * (GELU_C0 + GELU_C1 * (x * x))
    return (0.5 * x) * (1.0 + jnp.tanh(inner))


def _lru_kernel(g_ref, x_ref, cw_ref, cb_ref, wa_ref, ba_ref, wx_ref, bx_ref, lam_ref, o_ref,
                xin_ref, xf_ref, a_ref, b_ref, hout_ref, h_ref):
    nb, frames, w = x_ref.shape
    n_slab = w // LANES
    sub = LRU_ROWS // nb
    i = pl.program_id(0)

    @pl.when(i == 0)
    def _():
        xf_ref[0:CONV_HIST] = jnp.zeros((CONV_HIST, nb, w), F32)
        h_ref[...] = jnp.zeros((nb, w), F32)

    for b in range(nb):
        xb = x_ref[b].astype(F32)
        for s in range(n_slab):
            xin_ref[s, b * LRU_PITCH:b * LRU_PITCH + frames, :] = xb[:, s * LANES:(s + 1) * LANES]

    def gather(t, carry):
        for s in range(n_slab):
            xf_ref[CONV_HIST + t, :, s * LANES:(s + 1) * LANES] = (
                xin_ref[s, pl.ds(t, nb, stride=LRU_PITCH), :])
        return carry

    lax.fori_loop(0, frames, gather, 0, unroll=SCAN_UNROLL)

    half_log2_base = (0.5 * LRU_C * LOG2E) * _log_sigmoid(lam_ref[...])
    half = w // 2
    for blk in range(frames // sub):
        t0 = blk * sub
        xc = cb_ref[...]
        for tap in range(CONV_WIDTH):
            lo = CONV_HIST + t0 - tap
            xc = xc + (cw_ref[CONV_WIDTH - 1 - tap:CONV_WIDTH - tap, :]
                       * xf_ref[lo:lo + sub].reshape(LRU_ROWS, w))
        xcb = xc.astype(BF16)
        gr, gi = [], []
        for hb in range(2):
            cols = slice(hb * half, (hb + 1) * half)
            gr.append(_dot(xcb[:, cols], wa_ref[hb]))
            gi.append(_dot(xcb[:, cols], wx_ref[hb]))
        t_r = jnp.tanh(jnp.concatenate(gr, axis=1) + ba_ref[...])
        gate_i = 0.5 * jnp.tanh(jnp.concatenate(gi, axis=1) + bx_ref[...]) + 0.5
        a = jnp.exp2(t_r * half_log2_base + half_log2_base)
        a_ref[t0:t0 + sub] = a.reshape(sub, nb, w)
        y = 1.0 - a * a
        root = jnp.where(y > 0.0, y * lax.rsqrt(y), 0.0)
        b_ref[t0:t0 + sub] = (root * (gate_i * xc)).reshape(sub, nb, w)

    xf_ref[0:CONV_HIST] = xf_ref[frames:frames + CONV_HIST]

    def scan(t, h):
        h = a_ref[t] * h + b_ref[t]
        for s in range(n_slab):
            hout_ref[s, pl.ds(t, nb, stride=LRU_PITCH), :] = h[:, s * LANES:(s + 1) * LANES]
        return h

    h_ref[...] = lax.fori_loop(0, frames, scan, h_ref[...], unroll=SCAN_UNROLL)

    for b in range(nb):
        rows = slice(b * LRU_PITCH, b * LRU_PITCH + frames)
        hv = jnp.concatenate([hout_ref[s, rows, :] for s in range(n_slab)], axis=1)
        o_ref[b] = (hv * _gelu(g_ref[b].astype(F32))).astype(BF16)


def _lru(gate3, x3, conv_w, conv_b, wa_bd, b_a, wx_bd, b_x, lam):
    nb, t, w = x3.shape
    n_slab = w // LANES
    return pl.pallas_call(
        _lru_kernel,
        grid=(t // LRU_FRAMES,),
        in_specs=[
            pl.BlockSpec((nb, LRU_FRAMES, w), lambda i: (0, i, 0)),
            pl.BlockSpec((nb, LRU_FRAMES, w), lambda i: (0, i, 0)),
            _const_spec((CONV_WIDTH, w)),
            _const_spec((1, w)),
            _const_spec(wa_bd.shape),
            _const_spec((1, w)),
            _const_spec(wx_bd.shape),
            _const_spec((1, w)),
            _const_spec((1, w)),
        ],
        out_specs=pl.BlockSpec((nb, LRU_FRAMES, w), lambda i: (0, i, 0)),
        out_shape=jax.ShapeDtypeStruct((nb, t, w), BF16),
        scratch_shapes=[
            pltpu.VMEM((n_slab, nb * LRU_PITCH, LANES), F32),
            pltpu.VMEM((CONV_HIST + LRU_FRAMES, nb, w), F32),
            pltpu.VMEM((LRU_FRAMES, nb, w), F32),
            pltpu.VMEM((LRU_FRAMES, nb, w), F32),
            pltpu.VMEM((n_slab, nb * LRU_PITCH, LANES), F32),
            pltpu.VMEM((nb, w), F32),
        ],
        compiler_params=_params(1, 42),
        name="rglru",
    )(gate3, x3, conv_w, conv_b.reshape(1, w), wa_bd, b_a.reshape(1, w), wx_bd,
      b_x.reshape(1, w), lam.reshape(1, w))


def _block_diag_halves(wblk):
    nb, d, _ = wblk.shape
    per = nb // 2
    eye = jnp.eye(per, dtype=wblk.dtype)
    halves = [jnp.einsum("nde,nm->ndme", wblk[h * per:(h + 1) * per], eye).reshape(per * d, per * d)
              for h in range(2)]
    return jnp.stack(halves).astype(BF16)


def kernel(x, norm_w, w_in_even, gla_w_a_up, gla_b_a, gla_norm_w, fox_b_f, w_out_even,
           w_in_odd, rel_bias, conv_w, conv_b, lru_w_a, lru_b_a, lru_w_x, lru_b_x,
           lru_lambda, w_out_odd, w_mlp_up, w_mlp_down):
    batch, t, d = x.shape
    x2d = x.reshape(batch * t, d)
    group = d // 2
    kw = GLA_HEADS * GLA_DK

    o_ga = 2 * kw + 2 * group
    o_fq = o_ga + GLA_RANK
    o_ff = o_fq + 3 * group
    q_scale = LOG2E * HEAD_DIM ** -0.5
    segments = [(0, o_ga, 0, 1.0), (o_fq, group, o_ga, q_scale), (o_fq + group, 2 * group, o_ga + group, 1.0)]
    n_small = FOX_HEADS + GLA_RANK
    small_segments = [(o_ff, FOX_HEADS, 0, 1.0), (o_ga, GLA_RANK, FOX_HEADS, 1.0)]
    wa_pad = jnp.zeros((LANES, kw), F32).at[FOX_HEADS:n_small].set(gla_w_a_up[0]).astype(BF16)
    fox_bias = jnp.zeros((1, LANES), F32).at[0, :FOX_HEADS].set(fox_b_f[0])

    m = batch * t
    splits = [(0, kw, False), (kw, kw, False), (2 * kw, group, False), (2 * kw + group, group, False),
              (o_ga, group, True), (o_ga + group, group, True), (o_ga + 2 * group, group, True)]
    g_q, g_k, g_v, g_r, f_q, f_k, f_v, small = _norm_proj(
        x2d, norm_w[0, 0], w_in_even, segments, splits, small_segments)
    out_a, cum = _gla(g_q, g_k, g_v, g_r, small, wa_pad, gla_b_a[0], gla_norm_w[0], fox_bias, batch, t)
    out_b = _fox(f_q, f_k, f_v, cum, batch, t)
    x2d = _mix_mlp(out_a, out_b, x2d, w_out_even[0], norm_w[0, 1:4], w_mlp_up, w_mlp_down, 0)

    segments = [(0, group, 0, q_scale), (group, 4 * group, group, 1.0)]
    splits = [(c * group, group, False) for c in range(5)]
    c_q, c_k, c_v, d_gate, d_in = _norm_proj(x2d, norm_w[1, 0], w_in_odd, segments, splits)
    out_c = _chunk_attn(c_q, c_k, c_v, _chunk_attn_table(rel_bias[0]), batch, t)
    out_d = _lru(d_gate.reshape(batch, t, group), d_in.reshape(batch, t, group), conv_w[0], conv_b[0],
                 _block_diag_halves(0.5 * lru_w_a[0]), 0.5 * lru_b_a[0],
                 _block_diag_halves(0.5 * lru_w_x[0]), 0.5 * lru_b_x[0], lru_lambda[0])
    x2d = _mix_mlp(out_c, out_d.reshape(m, group), x2d, w_out_odd[0], norm_w[1, 1:4],
                   w_mlp_up, w_mlp_down, 1)
    return x2d.reshape(batch, t, d)
```

```python
import functools

import jax
import jax.numpy as jnp
from jax import lax
from jax.experimental import pallas as pl
from jax.experimental.pallas import tpu as pltpu

F32 = jnp.float32
BF16 = jnp.bfloat16

NORM_EPS = 1e-6
CHUNK = 64
GLA_HEADS = 4
GLA_DK = 64
GLA_DV = 128
GLA_RANK = 16
GLA_GATE_TAU = 16.0
FOX_HEADS = 8
HEAD_DIM = 64
CA_LEFT_CHUNKS = 8
REL_CLIP = 128
CONV_WIDTH = 4
LRU_C = 8.0

LANES = 128
MXU_DIM = 256
MASK_VALUE = -1e30
VMEM_LIMIT_BYTES = 58 * 1024 * 1024

ROW_TILE = 1024
PROJ_ROW_TILE = 1024
STAGE_ROWS = 512
STAGE_COLS = 1024
STAGE_W_ROWS = 256
FOX_BLOCK = 512
FOX_KBLOCK = 512
ONES_ROWS = 16
CA_QBLOCK = 4 * CHUNK
CA_WINDOW = CA_QBLOCK + CA_LEFT_CHUNKS * CHUNK
CA_PAD = CA_LEFT_CHUNKS * CHUNK
CA_LINE = 1024
LOG2E = 1.4426950408889634
LN2 = 0.6931471805599453
CUM_TERMS = 3
GLA_GROUP = 4
LRU_ROWS = 256
LRU_FRAMES = 256
LRU_PITCH = LRU_FRAMES + 8
CONV_HIST = 8
SCAN_UNROLL = 8


def _params(n_axes, vmem_mib):
    assert vmem_mib * 1024 * 1024 <= VMEM_LIMIT_BYTES
    return pltpu.CompilerParams(
        dimension_semantics=("arbitrary",) * n_axes,
        vmem_limit_bytes=vmem_mib * 1024 * 1024,
    )


def _const_spec(shape):
    nd = len(shape)
    return pl.BlockSpec(shape, lambda *_: (0,) * nd, pipeline_mode=pl.Buffered(1))


def _rmsnorm(x, w):
    y = x * lax.rsqrt(jnp.mean(x * x, axis=-1, keepdims=True) + NORM_EPS)
    return y * w


def _log_sigmoid(z):
    return jnp.minimum(z, 0.0) - LN2 * jnp.log2(1.0 + jnp.exp2(-LOG2E * jnp.abs(z)))


def _sigmoid(z):
    return 0.5 * jnp.tanh(0.5 * z) + 0.5


def _dot(a, b):
    return jnp.dot(a, b, preferred_element_type=F32)


def _dot_tn(a, b):
    return lax.dot_general(a, b, (((0,), (0,)), ((), ())), preferred_element_type=F32)


def _half_mask(lane, j):
    return lane < HEAD_DIM if j == 0 else lane >= HEAD_DIM


def _split3(x):
    h1 = x.astype(BF16)
    r1 = x - h1.astype(F32)
    h2 = r1.astype(BF16)
    h3 = (r1 - h2.astype(F32)).astype(BF16)
    return h1, h2, h3


def _row_major(i):
    return (i, 0)


def _stage_projection_weights(w_hbm, w_ref, ws_ref, stage_ref, sem_ref, segments, small_segments):
    d = w_ref.shape[0]
    pieces = d // STAGE_W_ROWS

    def copy(k):
        return pltpu.make_async_copy(w_hbm.at[0, pl.ds(k * STAGE_W_ROWS, STAGE_W_ROWS), :],
                                     stage_ref.at[k % 2], sem_ref.at[k % 2])

    def segment(k, src, width, scale):
        a0 = (src // LANES) * LANES
        a1 = min(-(-(src + width) // LANES) * LANES, stage_ref.shape[2])
        seg = stage_ref[k % 2, :, a0:a1][:, src - a0:src - a0 + width]
        return (seg * scale if scale != 1.0 else seg).astype(BF16)

    copy(0).start()
    for k in range(pieces):
        if k + 1 < pieces:
            copy(k + 1).start()
        copy(k).wait()
        rows = slice(k * STAGE_W_ROWS, (k + 1) * STAGE_W_ROWS)
        for src, width, dst, scale in segments:
            w_ref[rows, dst:dst + width] = segment(k, src, width, scale)
        if ws_ref is not None:
            ws_ref[rows, :] = jnp.zeros((STAGE_W_ROWS, ws_ref.shape[1]), BF16)
            for src, width, dst, scale in small_segments:
                ws_ref[rows, dst:dst + width] = segment(k, src, width, scale)


def _norm_proj_kernel(x_ref, nw_ref, w_hbm, *refs, splits, segments, small_segments, n_chunk):
    with_small = small_segments is not None
    n_out = len(splits) + (1 if with_small else 0)
    o_refs = refs[:len(splits)]
    w_ref = refs[n_out]
    ws_ref = refs[n_out + 1] if with_small else None
    stage_ref, sem_ref = refs[-2:]

    @pl.when(pl.program_id(0) == 0)
    def _():
        _stage_projection_weights(w_hbm, w_ref, ws_ref, stage_ref, sem_ref, segments, small_segments)

    tm = x_ref.shape[0]
    for rows in (slice(0, tm // 2), slice(tm // 2, tm)):
        h = _rmsnorm(x_ref[rows, :], nw_ref[...]).astype(BF16)
        for (c0, width, by_pair), o_ref in zip(splits, o_refs):
            for n0 in range(0, width, n_chunk):
                n1 = min(n0 + n_chunk, width)
                res = _dot(h, w_ref[:, c0 + n0:c0 + n1]).astype(o_ref.dtype)
                if by_pair:
                    for p in range(n0 // LANES, n1 // LANES):
                        o_ref[p, rows, :] = res[:, p * LANES - n0:(p + 1) * LANES - n0]
                else:
                    o_ref[rows, n0:n1] = res
        if with_small:
            refs[len(splits)][rows, :] = _dot(h, ws_ref[...])


def _norm_proj(x2d, nw, w_raw, segments, splits, small_segments=None):
    m, d = x2d.shape
    n_main = max(dst + width for _, width, dst, _ in segments)
    in_specs = [pl.BlockSpec((PROJ_ROW_TILE, d), _row_major), _const_spec((1, d)),
                pl.BlockSpec(memory_space=pl.ANY)]
    args = [x2d, nw.reshape(1, d), w_raw]
    scratch = [pltpu.VMEM((d, n_main), BF16)]
    out_specs, out_shape = [], []
    for _, width, by_pair in splits:
        if by_pair:
            out_specs.append(pl.BlockSpec((width // LANES, PROJ_ROW_TILE, LANES), lambda i: (0, i, 0)))
            out_shape.append(jax.ShapeDtypeStruct((width // LANES, m, LANES), BF16))
        else:
            out_specs.append(pl.BlockSpec((PROJ_ROW_TILE, width), _row_major))
            out_shape.append(jax.ShapeDtypeStruct((m, width), BF16))
    if small_segments is not None:
        out_specs.append(pl.BlockSpec((PROJ_ROW_TILE, LANES), _row_major))
        out_shape.append(jax.ShapeDtypeStruct((m, LANES), F32))
        scratch.append(pltpu.VMEM((d, LANES), BF16))
    scratch += [pltpu.VMEM((2, STAGE_W_ROWS, w_raw.shape[2]), F32), pltpu.SemaphoreType.DMA((2,))]
    return pl.pallas_call(
        functools.partial(_norm_proj_kernel, splits=tuple(splits), segments=tuple(segments),
                          small_segments=None if small_segments is None else tuple(small_segments),
                          n_chunk=2 * MXU_DIM),
        grid=(m // PROJ_ROW_TILE,),
        in_specs=in_specs,
        out_specs=out_specs,
        out_shape=out_shape,
        scratch_shapes=scratch,
        compiler_params=_params(1, 58),
        name="norm_proj",
    )(*args)


def _stage_bf16_weights(pairs, stage_ref, sem_ref):
    pieces = []
    for src, dst in pairs:
        rows, cols = dst.shape
        for r0 in range(0, rows, STAGE_ROWS):
            for c0 in range(0, cols, STAGE_COLS):
                pieces.append((src, dst, r0, c0))

    def copy(k):
        src, _, r0, c0 = pieces[k]
        return pltpu.make_async_copy(
            src.at[pl.ds(r0, STAGE_ROWS), pl.ds(c0, STAGE_COLS)], stage_ref.at[k % 2], sem_ref.at[k % 2])

    copy(0).start()
    for k, (_, dst, r0, c0) in enumerate(pieces):
        if k + 1 < len(pieces):
            copy(k + 1).start()
        copy(k).wait()
        dst[r0:r0 + STAGE_ROWS, c0:c0 + STAGE_COLS] = stage_ref[k % 2].astype(BF16)


def _mix_mlp_kernel(a_ref, b_ref, x_ref, nw_ref, wo_hbm, wu_hbm, wd_hbm, o_ref,
                    wo_ref, wu_ref, wd_ref, u_ref, y_ref, stage_ref, sem_ref, *, chunk, layer):
    @pl.when(pl.program_id(0) == 0)
    def _():
        _stage_bf16_weights([(wo_hbm, wo_ref), (wu_hbm.at[layer], wu_ref), (wd_hbm.at[layer], wd_ref)],
                            stage_ref, sem_ref)

    d_ff = wu_ref.shape[1]
    d = wd_ref.shape[1]
    tm = x_ref.shape[0]
    halves = [slice(0, tm // 2), slice(tm // 2, tm)]

    def out_proj(rows):
        if len(b_ref.shape) == 3:
            b = jnp.concatenate([b_ref[p, rows, :] for p in range(b_ref.shape[0])], axis=1)
        else:
            b = b_ref[rows, :]
        mix = jnp.concatenate([a_ref[rows, :], b], axis=1)
        for n0 in range(0, d, chunk):
            y_ref[rows, n0:n0 + chunk] = _dot(mix, wo_ref[:, n0:n0 + chunk])

    def residual_and_norm(rows):
        x1 = x_ref[rows, :] + _rmsnorm(y_ref[rows, :], nw_ref[0:1, :])
        o_ref[rows, :] = x1
        return _rmsnorm(x1, nw_ref[1:2, :]).astype(BF16)

    def up_proj(rows, h):
        for f0 in range(0, d_ff, chunk):
            u = jnp.maximum(_dot(h, wu_ref[:, f0:f0 + chunk]), 0.0)
            u_ref[rows, f0:f0 + chunk] = (u * u).astype(BF16)

    def down_proj(rows):
        for n0 in range(0, d, chunk):
            y_ref[rows, n0:n0 + chunk] = _dot(u_ref[rows, :], wd_ref[:, n0:n0 + chunk])

    def finish(rows):
        o_ref[rows, :] = o_ref[rows, :] + _rmsnorm(y_ref[rows, :], nw_ref[2:3, :])

    first, second = halves
    out_proj(first)
    out_proj(second)
    h_first = residual_and_norm(first)
    up_proj(first, h_first)
    h_second = residual_and_norm(second)
    up_proj(second, h_second)
    down_proj(first)
    down_proj(second)
    finish(first)
    finish(second)


def _mix_mlp(mix_a, mix_b, x2d, w_out, nw3, w_up_all, w_down_all, layer):
    m, d = x2d.shape
    g = mix_a.shape[1]
    d_ff = w_up_all.shape[2]
    return pl.pallas_call(
        functools.partial(_mix_mlp_kernel, chunk=2 * MXU_DIM, layer=layer),
        grid=(m // ROW_TILE,),
        in_specs=[
            pl.BlockSpec((ROW_TILE, g), _row_major),
            (pl.BlockSpec((mix_b.shape[0], ROW_TILE, LANES), lambda i: (0, i, 0)) if mix_b.ndim == 3
             else pl.BlockSpec((ROW_TILE, g), _row_major)),
            pl.BlockSpec((ROW_TILE, d), _row_major),
            _const_spec((3, d)),
            pl.BlockSpec(memory_space=pl.ANY),
            pl.BlockSpec(memory_space=pl.ANY),
            pl.BlockSpec(memory_space=pl.ANY),
        ],
        out_specs=pl.BlockSpec((ROW_TILE, d), _row_major),
        out_shape=jax.ShapeDtypeStruct((m, d), F32),
        scratch_shapes=[
            pltpu.VMEM((2 * g, d), BF16),
            pltpu.VMEM((d, d_ff), BF16),
            pltpu.VMEM((d_ff, d), BF16),
            pltpu.VMEM((ROW_TILE, d_ff), BF16),
            pltpu.VMEM((ROW_TILE, d), F32),
            pltpu.VMEM((2, STAGE_ROWS, STAGE_COLS), F32),
            pltpu.SemaphoreType.DMA((2,)),
        ],
        compiler_params=_params(1, 56),
        name="mix_mlp",
    )(mix_a, mix_b, x2d, nw3, w_out, w_up_all, w_down_all)


def _cumulative_log_gate(sm_ref, bias_ref, o_ref):
    t = sm_ref.shape[0]
    row = lax.broadcasted_iota(jnp.int32, (LANES, LANES), 0)
    col = lax.broadcasted_iota(jnp.int32, (LANES, LANES), 1)
    tri = jnp.where(row >= col, 1.0, 0.0).astype(BF16)
    carry = jnp.zeros((1, LANES), F32)
    for blk in range(t // LANES):
        rows = slice(blk * LANES, (blk + 1) * LANES)
        ls = _log_sigmoid(sm_ref[rows, :] + bias_ref[...])
        h1, h2, h3 = _split3(ls)
        cb = _dot(tri, h1) + _dot(tri, h2) + _dot(tri, h3) + carry
        o_ref[rows, :] = cb
        carry = cb[LANES - 1:LANES, :]


def _gla_kernel(q_ref, k_ref, v_ref, r_ref, sm_ref, wa_ref, ba_ref, nw_ref, fb_ref, o_ref, cum_ref,
                la_ref, st_ref, dec_ref):
    t = q_ref.shape[0]
    nc = t // CHUNK
    n_pairs = GLA_HEADS // 2

    z = _dot(sm_ref[...].astype(BF16), wa_ref[...]) + ba_ref[...]
    la_ref[...] = _log_sigmoid(z) * (1.0 / GLA_GATE_TAU)
    _cumulative_log_gate(sm_ref, fb_ref, cum_ref)

    grp = GLA_GROUP * CHUNK
    row = lax.broadcasted_iota(jnp.int32, (grp, grp), 0)
    col = lax.broadcasted_iota(jnp.int32, (grp, grp), 1)
    tri = jnp.where((row >= col) & (row // CHUNK == col // CHUNK), 1.0, 0.0).astype(BF16)
    erow = lax.broadcasted_iota(jnp.int32, (grp, GLA_GROUP * LANES), 0)
    ecol = lax.broadcasted_iota(jnp.int32, (grp, GLA_GROUP * LANES), 1)
    chunk_ones = jnp.where(erow // CHUNK == ecol // LANES, 1.0, 0.0).astype(BF16)

    def increments(g, carry):
        r0 = pl.multiple_of(g * grp, grp)
        la = la_ref[pl.ds(r0, grp), :]
        hi = la.astype(BF16)
        lo = (la - hi.astype(F32)).astype(BF16)
        cum = _dot(tri, hi) + _dot(tri, lo)
        dec = jnp.exp(_dot_tn(hi, chunk_ones) + _dot_tn(lo, chunk_ones))
        k_g = k_ref[pl.ds(r0, grp), :].astype(F32)
        v_g = v_ref[pl.ds(r0, grp), :]
        k_dec = []
        for c in range(GLA_GROUP):
            rows = slice(c * CHUNK, (c + 1) * CHUNK)
            total = cum[(c + 1) * CHUNK - 1:(c + 1) * CHUNK, :]
            k_dec.append((k_g[rows] * jnp.exp(total - cum[rows])).astype(BF16))
        for c in range(GLA_GROUP):
            rows = slice(c * CHUNK, (c + 1) * CHUNK)
            dec_ref[g * GLA_GROUP + c] = dec[:, c * LANES:(c + 1) * LANES]
            for h in range(GLA_HEADS):
                p, j = divmod(h, 2)
                full = _dot_tn(k_dec[c][:, p * LANES:(p + 1) * LANES], v_g[rows, h * GLA_DV:(h + 1) * GLA_DV])
                st_ref[g * GLA_GROUP + c, p, j * GLA_DK:(j + 1) * GLA_DK, :] = full[j * GLA_DK:(j + 1) * GLA_DK, :]
        return carry

    lax.fori_loop(0, nc // GLA_GROUP, increments, 0)

    def scan(c, carry):
        for p in range(n_pairs):
            st_ref[c, p] = st_ref[c, p] + dec_ref[c, p * LANES:(p + 1) * LANES, :] * st_ref[c - 1, p]
        return carry

    lax.fori_loop(1, nc, scan, 0)

    lane = lax.broadcasted_iota(jnp.int32, (CHUNK, LANES), 1)
    scale = GLA_DK ** -0.5

    def outputs(g, carry):
        r0 = pl.multiple_of(g * grp, grp)
        q_g = q_ref[pl.ds(r0, grp), :] * jnp.asarray(scale, BF16)
        raw = []
        for c in range(GLA_GROUP):
            rows = slice(c * CHUNK, (c + 1) * CHUNK)
            for h in range(GLA_HEADS):
                p, j = divmod(h, 2)
                q_pair = q_g[rows, p * LANES:(p + 1) * LANES]
                q_h = jnp.where(_half_mask(lane, j), q_pair, jnp.zeros_like(q_pair))
                raw.append(_dot(q_h, st_ref[g * GLA_GROUP + c, p].astype(BF16)))
        for c in range(GLA_GROUP):
            for h in range(GLA_HEADS):
                o = raw[c * GLA_HEADS + h]
                o = o * lax.rsqrt(jnp.mean(o * o, axis=-1, keepdims=True) + NORM_EPS)
                o = o * nw_ref[:, h * GLA_DV:(h + 1) * GLA_DV]
                rows = pl.ds(r0 + c * CHUNK, CHUNK)
                r_h = r_ref[rows, h * GLA_DV:(h + 1) * GLA_DV].astype(F32)
                o_ref[rows, h * GLA_DV:(h + 1) * GLA_DV] = (o * (r_h * _sigmoid(r_h))).astype(BF16)
        return carry

    lax.fori_loop(0, nc // GLA_GROUP, outputs, 0)


def _gla(q, k, v, r, small, wa_pad, b_a, norm_w, fox_bias, batch, t):
    kw = GLA_HEADS * GLA_DK
    gw = GLA_HEADS * GLA_DV
    nc = t // CHUNK
    return pl.pallas_call(
        _gla_kernel,
        grid=(batch,),
        in_specs=[
            pl.BlockSpec((t, kw), lambda b: (b, 0)),
            pl.BlockSpec((t, kw), lambda b: (b, 0)),
            pl.BlockSpec((t, gw), lambda b: (b, 0)),
            pl.BlockSpec((t, gw), lambda b: (b, 0)),
            pl.BlockSpec((t, LANES), lambda b: (b, 0)),
            _const_spec((LANES, kw)),
            _const_spec((1, kw)),
            _const_spec((1, gw)),
            _const_spec((1, LANES)),
        ],
        out_specs=[pl.BlockSpec((t, gw), lambda b: (b, 0)), pl.BlockSpec((t, LANES), lambda b: (b, 0))],
        out_shape=[jax.ShapeDtypeStruct((batch * t, gw), BF16),
                   jax.ShapeDtypeStruct((batch * t, LANES), F32)],
        scratch_shapes=[
            pltpu.VMEM((t, kw), F32),
            pltpu.VMEM((nc, GLA_HEADS // 2, LANES, GLA_DV), F32),
            pltpu.VMEM((nc, kw, LANES), F32),
        ],
        compiler_params=_params(1, 56),
        name="gla",
    )(q, k, v, r, small, wa_pad, b_a.reshape(1, kw), norm_w.reshape(1, gw), fox_bias)


def _fox_kernel(q_ref, k_ref, v_ref, c_ref, o_ref, ka_ref, vt_ref, s_ref):
    t = k_ref.shape[0]
    blk = FOX_BLOCK
    kbs = FOX_KBLOCK
    pair = pl.program_id(1)

    sel_r = lax.broadcasted_iota(jnp.int32, (LANES, LANES), 0)
    sel_c = lax.broadcasted_iota(jnp.int32, (LANES, LANES), 1)
    sels = []
    for term in range(CUM_TERMS):
        hit = (((sel_r == 2 * pair) & (sel_c == term))
               | ((sel_r == 2 * pair + 1) & (sel_c == CUM_TERMS + term)))
        sels.append(jnp.where(hit, 1.0, 0.0).astype(BF16))
    for kb in range(t // kbs):
        rows = slice(kb * kbs, (kb + 1) * kbs)
        vt_ref[kb] = v_ref[rows, :].astype(F32).T.astype(BF16)
        parts = _split3(c_ref[rows, :] * (-LOG2E))
        extra = _dot(parts[0], sels[0]) + _dot(parts[1], sels[1]) + _dot(parts[2], sels[2])
        ka_ref[kb, :, 0:LANES] = k_ref[rows, :]
        ka_ref[kb, :, LANES:2 * LANES] = extra.astype(BF16)

    lane = lax.broadcasted_iota(jnp.int32, (blk, LANES), 1)
    krow = lax.broadcasted_iota(jnp.int32, (kbs, blk), 0)
    qcol = lax.broadcasted_iota(jnp.int32, (kbs, blk), 1)
    ahead = qcol - krow

    def augmented_queries(qi):
        q = q_ref[qi * blk:(qi + 1) * blk, :]
        q_aug = []
        for j in range(2):
            q_h = jnp.where(_half_mask(lane, j), q, jnp.zeros_like(q))
            ones = jnp.where((lane >= CUM_TERMS * j) & (lane < CUM_TERMS * (j + 1)), 1.0, 0.0).astype(BF16)
            q_aug.append(jnp.concatenate([q_h, ones], axis=1).astype(F32).T.astype(BF16))
        return q_aug

    def scores(q_aug, qi, kb, slot):
        k_blk = ka_ref[kb]
        offset = kb * kbs - qi * blk
        for j in range(2):
            s = _dot(k_blk, q_aug[j])
            if offset + kbs - 1 > 0:
                s = jnp.where(ahead >= offset, s, MASK_VALUE)
            s_ref[slot, j] = s

    def accumulate(kb, slot, state):
        new = []
        for j in range(2):
            m, l, acc = state[j]
            m_new = jnp.maximum(m, jnp.max(s_ref[slot, j], axis=0, keepdims=True))
            alpha = jnp.exp2(m - m_new)
            p = jnp.exp2(s_ref[slot, j] - m_new)
            l = l * alpha + jnp.sum(p, axis=0, keepdims=True)
            pv = _dot(vt_ref[kb, j * HEAD_DIM:(j + 1) * HEAD_DIM, :], p.astype(BF16))
            new.append((m_new, l, acc * alpha + pv))
        return new

    tasks = [(qi, kb) for qi in range(t // blk) for kb in range((qi + 1) * blk // kbs)]
    fresh = [(jnp.full((1, blk), MASK_VALUE, F32), jnp.zeros((1, blk), F32),
              jnp.zeros((HEAD_DIM, blk), F32))] * 2
    q_aug = augmented_queries(0)
    scores(q_aug, 0, 0, 0)
    state = fresh
    for n, (qi, kb) in enumerate(tasks):
        if n + 1 < len(tasks):
            qi_next, kb_next = tasks[n + 1]
            if qi_next != qi:
                q_aug = augmented_queries(qi_next)
            scores(q_aug, qi_next, kb_next, (n + 1) % 2)
        state = accumulate(kb, n % 2, state)
        if n + 1 == len(tasks) or tasks[n + 1][0] != qi:
            outs = [acc * (1.0 / l) for _, l, acc in state]
            o_ref[qi * blk:(qi + 1) * blk, :] = jnp.concatenate(outs, axis=0).T.astype(BF16)
            state = fresh


def _fox(q, k, v, cum, batch, t):
    n_pairs = FOX_HEADS // 2
    pair_block = pl.BlockSpec((None, t, LANES), lambda b, p: (p, b, 0))
    return pl.pallas_call(
        _fox_kernel,
        grid=(batch, n_pairs),
        in_specs=[
            pair_block,
            pair_block,
            pair_block,
            pl.BlockSpec((t, LANES), lambda b, p: (b, 0)),
        ],
        out_specs=pair_block,
        out_shape=jax.ShapeDtypeStruct((n_pairs, batch * t, LANES), BF16),
        scratch_shapes=[
            pltpu.VMEM((t // FOX_KBLOCK, FOX_KBLOCK, 2 * LANES), BF16),
            pltpu.VMEM((t // FOX_KBLOCK, LANES, FOX_KBLOCK), BF16),
            pltpu.VMEM((2, 2, FOX_KBLOCK, FOX_BLOCK), F32),
        ],
        compiler_params=_params(2, 56),
        name="fox",
    )(q, k, v, cum)


def _chunk_attn_kernel(q_ref, k_ref, v_ref, tab_ref, o_ref, kp_ref, vt_ref, s_ref):
    t, g = q_ref.shape
    qb = CA_QBLOCK
    n_pad = CA_PAD // qb
    n_win = CA_WINDOW // qb

    for kb in range(n_pad):
        kp_ref[kb] = jnp.zeros((qb, g), BF16)
        vt_ref[kb] = jnp.zeros(vt_ref.shape[1:], BF16)
    for kb in range(t // qb):
        rows = slice(kb * qb, (kb + 1) * qb)
        kp_ref[n_pad + kb] = k_ref[rows, :]
        v_t = v_ref[rows, :].T
        for h in range(g // HEAD_DIM):
            vt_ref[n_pad + kb, h, 0:HEAD_DIM, :] = v_t[h * HEAD_DIM:(h + 1) * HEAD_DIM, :]
            vt_ref[n_pad + kb, h, HEAD_DIM:HEAD_DIM + ONES_ROWS, :] = jnp.ones((ONES_ROWS, qb), BF16)

    lane = lax.broadcasted_iota(jnp.int32, (qb, LANES), 1)
    n_heads = g // HEAD_DIM
    lax.fori_loop(0, t // qb, functools.partial(
        _chunk_attn_block, q_ref=q_ref, tab_ref=tab_ref, o_ref=o_ref, kp_ref=kp_ref, vt_ref=vt_ref,
        s_ref=s_ref, lane=lane, n_heads=n_heads, n_pad=n_pad, n_win=n_win), 0)


def _chunk_attn_block(i, carry, *, q_ref, tab_ref, o_ref, kp_ref, vt_ref, s_ref, lane, n_heads,
                      n_pad, n_win):
    qb = CA_QBLOCK
    q_rows = pl.ds(pl.multiple_of(i * qb, qb), qb)

    def scores(h):
        pair, j = divmod(h, 2)
        cols = slice(pair * LANES, (pair + 1) * LANES)
        q_pair = q_ref[q_rows, cols]
        q_h = jnp.where(_half_mask(lane, j), q_pair, jnp.zeros_like(q_pair))
        q_ht = q_h.T
        for w in range(n_win):
            tab_blk = jnp.where(i + w >= n_pad, w, n_win)
            s_ref[h % 2, w] = _dot(kp_ref[i + w, :, cols], q_ht) + tab_ref[h, tab_blk]

    def weighted_values(h):
        m = s_ref[h % 2, 0].max(axis=0, keepdims=True)
        for w in range(1, n_win):
            m = jnp.maximum(m, s_ref[h % 2, w].max(axis=0, keepdims=True))
        acc = jnp.zeros((HEAD_DIM + ONES_ROWS, qb), F32)
        for w in range(n_win):
            p = jnp.exp2((s_ref[h % 2, w] - m).astype(BF16))
            acc = acc + _dot(vt_ref[i + w, h], p)
        return acc[0:HEAD_DIM] * (1.0 / acc[HEAD_DIM:HEAD_DIM + 1])

    outs = []
    scores(0)
    for h in range(n_heads):
        if h + 1 < n_heads:
            scores(h + 1)
        outs.append(weighted_values(h))
    o_ref[q_rows, :] = jnp.concatenate(outs, axis=0).T.astype(BF16)
    return carry


def _chunk_attn(q, k, v, table, batch, t):
    g = table.shape[0] * HEAD_DIM
    n_blocks = (CA_PAD + t) // CA_QBLOCK
    return pl.pallas_call(
        _chunk_attn_kernel,
        grid=(batch,),
        in_specs=[
            pl.BlockSpec((t, g), lambda b: (b, 0)),
            pl.BlockSpec((t, g), lambda b: (b, 0)),
            pl.BlockSpec((t, g), lambda b: (b, 0)),
            _const_spec(table.shape),
        ],
        out_specs=pl.BlockSpec((t, g), lambda b: (b, 0)),
        out_shape=jax.ShapeDtypeStruct((batch * t, g), BF16),
        scratch_shapes=[
            pltpu.VMEM((n_blocks, CA_QBLOCK, g), BF16),
            pltpu.VMEM((n_blocks, g // HEAD_DIM, HEAD_DIM + ONES_ROWS, CA_QBLOCK), BF16),
            pltpu.VMEM((2, CA_WINDOW // CA_QBLOCK, CA_QBLOCK, CA_QBLOCK), F32),
        ],
        compiler_params=_params(1, 40),
        name="chunk_attn",
    )(q, k, v, table)


def _ca_table_kernel(line_ref, o_ref):
    n_win = CA_WINDOW // CA_QBLOCK
    rows = 8
    x = jnp.broadcast_to(line_ref[0] * LOG2E, (rows, CA_LINE))
    qc = lax.broadcasted_iota(jnp.int32, (rows, CA_QBLOCK), 1) // CHUNK
    for grp in range(CA_WINDOW // rows):
        r0 = grp * rows
        y = pltpu.roll(x, (CA_LINE - CA_WINDOW + 1 + r0) % CA_LINE, axis=1, stride=1, stride_axis=0)
        kc = r0 // CHUNK
        allowed = (qc <= kc) & (qc >= kc - CA_LEFT_CHUNKS)
        w, r = divmod(r0, CA_QBLOCK)
        o_ref[0, w, r:r + rows, :] = jnp.where(allowed, y[:, 0:CA_QBLOCK], MASK_VALUE)
    o_ref[0, n_win] = jnp.full((CA_QBLOCK, CA_QBLOCK), MASK_VALUE, F32)


def _chunk_attn_table(rel_bias):
    heads = rel_bias.shape[0]
    n_win = CA_WINDOW // CA_QBLOCK
    left = CA_WINDOW - 1 - CA_PAD - REL_CLIP
    line = jnp.pad(rel_bias.astype(F32), ((0, 0), (left, CA_LINE - left - rel_bias.shape[1])), mode="edge")
    return pl.pallas_call(
        _ca_table_kernel,
        grid=(heads,),
        in_specs=[pl.BlockSpec((1, 1, CA_LINE), lambda h: (h, 0, 0))],
        out_specs=pl.BlockSpec((1, n_win + 1, CA_QBLOCK, CA_QBLOCK), lambda h: (h, 0, 0, 0)),
        out_shape=jax.ShapeDtypeStruct((heads, n_win + 1, CA_QBLOCK, CA_QBLOCK), F32),
        compiler_params=_params(1, 8),
        name="ca_table",
    )(line.reshape(heads, 1, CA_LINE))


GELU_C0 = 0.7978845608028654
GELU_C1 = GELU_C0 * 0.044715


def _gelu(x):
    inner = x * (GELU_C0 + GELU_C1 * (x * x))
    return (0.5 * x) * (1.0 + jnp.tanh(inner))


def _lru_kernel(g_ref, x_ref, cw_ref, cb_ref, wa_ref, ba_ref, wx_ref, bx_ref, lam_ref, o_ref,
                xin_ref, xf_ref, a_ref, b_ref, hout_ref, h_ref):
    nb, frames, w = x_ref.shape
    n_slab = w // LANES
    sub = LRU_ROWS // nb
    i = pl.program_id(0)

    @pl.when(i == 0)
    def _():
        xf_ref[0:CONV_HIST] = jnp.zeros((CONV_HIST, nb, w), F32)
        h_ref[...] = jnp.zeros((nb, w), F32)

    for b in range(nb):
        xb = x_ref[b].astype(F32)
        for s in range(n_slab):
            xin_ref[s, b * LRU_PITCH:b * LRU_PITCH + frames, :] = xb[:, s * LANES:(s + 1) * LANES]

    def gather(t, carry):
        for s in range(n_slab):
            xf_ref[CONV_HIST + t, :, s * LANES:(s + 1) * LANES] = (
                xin_ref[s, pl.ds(t, nb, stride=LRU_PITCH), :])
        return carry

    lax.fori_loop(0, frames, gather, 0, unroll=SCAN_UNROLL)

    half_log2_base = (0.5 * LRU_C * LOG2E) * _log_sigmoid(lam_ref[...])
    half = w // 2
    for blk in range(frames // sub):
        t0 = blk * sub
        xc = cb_ref[...]
        for tap in range(CONV_WIDTH):
            lo = CONV_HIST + t0 - tap
            xc = xc + (cw_ref[CONV_WIDTH - 1 - tap:CONV_WIDTH - tap, :]
                       * xf_ref[lo:lo + sub].reshape(LRU_ROWS, w))
        xcb = xc.astype(BF16)
        gr, gi = [], []
        for hb in range(2):
            cols = slice(hb * half, (hb + 1) * half)
            gr.append(_dot(xcb[:, cols], wa_ref[hb]))
            gi.append(_dot(xcb[:, cols], wx_ref[hb]))
        t_r = jnp.tanh(jnp.concatenate(gr, axis=1) + ba_ref[...])
        gate_i = 0.5 * jnp.tanh(jnp.concatenate(gi, axis=1) + bx_ref[...]) + 0.5
        a = jnp.exp2(t_r * half_log2_base + half_log2_base)
        a_ref[t0:t0 + sub] = a.reshape(sub, nb, w)
        y = 1.0 - a * a
        root = jnp.where(y > 0.0, y * lax.rsqrt(y), 0.0)
        b_ref[t0:t0 + sub] = (root * (gate_i * xc)).reshape(sub, nb, w)

    xf_ref[0:CONV_HIST] = xf_ref[frames:frames + CONV_HIST]

    def scan(t, h):
        h = a_ref[t] * h + b_ref[t]
        for s in range(n_slab):
            hout_ref[s, pl.ds(t, nb, stride=LRU_PITCH), :] = h[:, s * LANES:(s + 1) * LANES]
        return h

    h_ref[...] = lax.fori_loop(0, frames, scan, h_ref[...], unroll=SCAN_UNROLL)

    for b in range(nb):
        rows = slice(b * LRU_PITCH, b * LRU_PITCH + frames)
        hv = jnp.concatenate([hout_ref[s, rows, :] for s in range(n_slab)], axis=1)
        o_ref[b] = (hv * _gelu(g_ref[b].astype(F32))).astype(BF16)


def _lru(gate3, x3, conv_w, conv_b, wa_bd, b_a, wx_bd, b_x, lam):
    nb, t, w = x3.shape
    n_slab = w // LANES
    return pl.pallas_call(
        _lru_kernel,
        grid=(t // LRU_FRAMES,),
        in_specs=[
            pl.BlockSpec((nb, LRU_FRAMES, w), lambda i: (0, i, 0)),
            pl.BlockSpec((nb, LRU_FRAMES, w), lambda i: (0, i, 0)),
            _const_spec((CONV_WIDTH, w)),
            _const_spec((1, w)),
            _const_spec(wa_bd.shape),
            _const_spec((1, w)),
            _const_spec(wx_bd.shape),
            _const_spec((1, w)),
            _const_spec((1, w)),
        ],
        out_specs=pl.BlockSpec((nb, LRU_FRAMES, w), lambda i: (0, i, 0)),
        out_shape=jax.ShapeDtypeStruct((nb, t, w), BF16),
        scratch_shapes=[
            pltpu.VMEM((n_slab, nb * LRU_PITCH, LANES), F32),
            pltpu.VMEM((CONV_HIST + LRU_FRAMES, nb, w), F32),
            pltpu.VMEM((LRU_FRAMES, nb, w), F32),
            pltpu.VMEM((LRU_FRAMES, nb, w), F32),
            pltpu.VMEM((n_slab, nb * LRU_PITCH, LANES), F32),
            pltpu.VMEM((nb, w), F32),
        ],
        compiler_params=_params(1, 42),
        name="rglru",
    )(gate3, x3, conv_w, conv_b.reshape(1, w), wa_bd, b_a.reshape(1, w), wx_bd,
      b_x.reshape(1, w), lam.reshape(1, w))


def _block_diag_halves(wblk):
    nb, d, _ = wblk.shape
    per = nb // 2
    eye = jnp.eye(per, dtype=wblk.dtype)
    halves = [jnp.einsum("nde,nm->ndme", wblk[h * per:(h + 1) * per], eye).reshape(per * d, per * d)
              for h in range(2)]
    return jnp.stack(halves).astype(BF16)


def kernel(x, norm_w, w_in_even, gla_w_a_up, gla_b_a, gla_norm_w, fox_b_f, w_out_even,
           w_in_odd, rel_bias, conv_w, conv_b, lru_w_a, lru_b_a, lru_w_x, lru_b_x,
           lru_lambda, w_out_odd, w_mlp_up, w_mlp_down):
    batch, t, d = x.shape
    x2d = x.reshape(batch * t, d)
    group = d // 2
    kw = GLA_HEADS * GLA_DK

    o_ga = 2 * kw + 2 * group
    o_fq = o_ga + GLA_RANK
    o_ff = o_fq + 3 * group
    q_scale = LOG2E * HEAD_DIM ** -0.5
    segments = [(0, o_ga, 0, 1.0), (o_fq, group, o_ga, q_scale), (o_fq + group, 2 * group, o_ga + group, 1.0)]
    n_small = FOX_HEADS + GLA_RANK
    small_segments = [(o_ff, FOX_HEADS, 0, 1.0), (o_ga, GLA_RANK, FOX_HEADS, 1.0)]
    wa_pad = jnp.zeros((LANES, kw), F32).at[FOX_HEADS:n_small].set(gla_w_a_up[0]).astype(BF16)
    fox_bias = jnp.zeros((1, LANES), F32).at[0, :FOX_HEADS].set(fox_b_f[0])

    m = batch * t
    splits = [(0, kw, False), (kw, kw, False), (2 * kw, group, False), (2 * kw + group, group, False),
              (o_ga, group, True), (o_ga + group, group, True), (o_ga + 2 * group, group, True)]
    g_q, g_k, g_v, g_r, f_q, f_k, f_v, small = _norm_proj(
        x2d, norm_w[0, 0], w_in_even, segments, splits, small_segments)
    out_a, cum = _gla(g_q, g_k, g_v, g_r, small, wa_pad, gla_b_a[0], gla_norm_w[0], fox_bias, batch, t)
    out_b = _fox(f_q, f_k, f_v, cum, batch, t)
    x2d = _mix_mlp(out_a, out_b, x2d, w_out_even[0], norm_w[0, 1:4], w_mlp_up, w_mlp_down, 0)

    segments = [(0, group, 0, q_scale), (group, 4 * group, group, 1.0)]
    splits = [(c * group, group, False) for c in range(5)]
    c_q, c_k, c_v, d_gate, d_in = _norm_proj(x2d, norm_w[1, 0], w_in_odd, segments, splits)
    out_c = _chunk_attn(c_q, c_k, c_v, _chunk_attn_table(rel_bias[0]), batch, t)
    out_d = _lru(d_gate.reshape(batch, t, group), d_in.reshape(batch, t, group), conv_w[0], conv_b[0],
                 _block_diag_halves(0.5 * lru_w_a[0]), 0.5 * lru_b_a[0],
                 _block_diag_halves(0.5 * lru_w_x[0]), 0.5 * lru_b_x[0], lru_lambda[0])
    x2d = _mix_mlp(out_c, out_d.reshape(m, group), x2d, w_out_odd[0], norm_w[1, 1:4],
                   w_mlp_up, w_mlp_down, 1)
    return x2d.reshape(batch, t, d)
```
